```python
import math
import jax
import jax.numpy as jnp
from jax import lax
import numpy as np

D_MODEL = 1024
BATCH = 2
SEQ = 8192
DEPTH = 1
DEC_BATCH = 32
DEC_SEQ = 4
PAST_LEN = 8192
PAGE_SIZE = 128

HEAD_DIM = 64
N_HEADS = D_MODEL // HEAD_DIM
H_MOBA = N_HEADS // 2
H_NSA = N_HEADS - H_MOBA
H_NSA_KV = 2
NSA_GROUP = H_NSA // H_NSA_KV
W_MOBA = H_MOBA * HEAD_DIM
W_NSA = H_NSA * HEAD_DIM
W_NSA_KV = H_NSA_KV * HEAD_DIM
D_IN = 4 * W_MOBA + 2 * W_NSA + 6 * W_NSA_KV + 3 * H_NSA
MOBA_BLOCK = 256
MOBA_TOPK = 3
CMP_LEN = 32
CMP_STRIDE = 16
CMP_HIDDEN = 2 * HEAD_DIM
SLC_BLOCK = 64
SLC_TOPN = 16
WINDOW = 512
N_BUCKETS = 32
MAX_DISTANCE = 128
Q_BLOCK = 64
RMS_EPS = 1e-6
NEG = -1e30
FORCE = 1e9
TINY = 1e-30

kernel_name = 'hymba_moba_nsa_decoder_step'


def proj_splits():
    sizes = [W_MOBA] * 4 + [W_NSA] + [W_NSA_KV] * 6 + [3 * H_NSA, W_NSA]
    return [int(s) for s in np.cumsum(sizes)[:-1]]


def rmsnorm(x, gain):
    xf = x.astype(jnp.float32)
    inv = lax.rsqrt(jnp.mean(xf * xf, axis=-1, keepdims=True) + RMS_EPS)
    return (xf * inv).astype(x.dtype) * gain


def masked_softmax(logits, mask):
    logits = jnp.where(mask, logits.astype(jnp.float32), NEG)
    p = jnp.exp(logits - jnp.max(logits, axis=-1, keepdims=True)) * mask
    return p / jnp.maximum(jnp.sum(p, axis=-1, keepdims=True), TINY)


def t5_bucket(rel):
    n = jnp.maximum(rel, 0)
    exact = N_BUCKETS // 2
    nf = jnp.maximum(n, 1).astype(jnp.float32)
    large = exact + (jnp.log(nf / exact) / math.log(MAX_DISTANCE / exact) * (N_BUCKETS - exact)).astype(jnp.int32)
    return jnp.where(n < exact, n, jnp.minimum(large, N_BUCKETS - 1))


def project(x, c, w_ada_l, b_ada_l, gain_l, w_in_l):
    B, T, _ = x.shape
    shift, scale, gate = jnp.split(c @ w_ada_l + b_ada_l, 3, axis=-1)
    h = rmsnorm(x, gain_l) * (1.0 + scale[:, None]) + shift[:, None]
    (q_m, k_m, v_m, z_m, q_n, kc, vc, ks, vs, kw, vw, g_n, z_n) = jnp.split(h @ w_in_l, proj_splits(), axis=-1)
    heads = lambda a: a.reshape(B, T, -1, HEAD_DIM)
    gates = jax.nn.sigmoid(g_n).reshape(B, T, H_NSA, 3)
    return (gate, z_m, z_n, heads(q_m), heads(k_m), heads(v_m), heads(q_n),
            heads(kc), heads(vc), heads(ks), heads(vs), heads(kw), heads(vw), gates)


def moba_keys(k, v):
    B, L, H, Dh = k.shape
    nb = -(-L // MOBA_BLOCK)
    pad = ((0, 0), (0, nb * MOBA_BLOCK - L), (0, 0), (0, 0))
    kb = jnp.pad(k, pad).reshape(B, nb, MOBA_BLOCK, H, Dh)
    vb = jnp.pad(v, pad).reshape(B, nb, MOBA_BLOCK, H, Dh)
    kmean = jnp.mean(kb.astype(jnp.float32), axis=2).astype(k.dtype)
    return kb, vb, kmean


def moba_attend(q, qpos, kb, vb, kmean, bias_tab):
    B, Q, H, Dh = q.shape
    nb = kb.shape[1]
    cur = qpos // MOBA_BLOCK
    s = jnp.einsum('bqhd,bnhd->bhqn', q, kmean).astype(jnp.float32)
    full = jnp.arange(nb, dtype=jnp.int32)[None, :] < cur[:, None]
    top = min(MOBA_TOPK, nb)
    _, idx = lax.top_k(jnp.where(full, s, NEG), top)
    own = jnp.broadcast_to(cur[None, None, :, None], (B, H, Q, 1))
    blocks = jnp.concatenate([idx, own], axis=-1)
    ok = jnp.concatenate([idx < cur[None, None, :, None], jnp.ones_like(own, dtype=bool)], axis=-1)
    bi = jnp.arange(B)[:, None, None, None]
    hi = jnp.arange(H)[None, :, None, None]
    kg = kb[bi, blocks, :, hi]
    vg = vb[bi, blocks, :, hi]
    n = (top + 1) * MOBA_BLOCK
    logits = jnp.einsum('bqhd,bhqnkd->bhqnk', q, kg) * (HEAD_DIM ** -0.5)
    rel = qpos[None, None, :, None, None] - (blocks[..., None] * MOBA_BLOCK + jnp.arange(MOBA_BLOCK, dtype=jnp.int32))
    mask = ok[..., None] & (rel >= 0)
    hh = jnp.arange(H)[None, :, None, None, None]
    logits = logits.astype(jnp.float32) + bias_tab[hh, t5_bucket(rel)].astype(jnp.float32)
    p = masked_softmax(logits.reshape(B, H, Q, n), mask.reshape(B, H, Q, n))
    o = jnp.einsum('bhqn,bhqnd->bqhd', p.astype(vg.dtype), vg.reshape(B, H, Q, n, Dh))
    return o.reshape(B, Q, H * Dh)


def nsa_compress(rows, pe, w1, w2):
    B, L, Hk, Dh = rows.shape
    n_cmp = (L - CMP_LEN) // CMP_STRIDE + 1
    idx = jnp.arange(n_cmp)[:, None] * CMP_STRIDE + jnp.arange(CMP_LEN)[None, :]
    blk = rows[:, idx] + pe[:, None, :]
    blk = jnp.moveaxis(blk, 3, 2).reshape(B, n_cmp, Hk, CMP_LEN * Dh)
    return jax.nn.silu(blk @ w1) @ w2


def nsa_keys(kc_r, vc_r, ks_r, vs_r, pe, kw1, kw2, vw1, vw2):
    B, L, Hk, Dh = ks_r.shape
    kc = nsa_compress(kc_r, pe[0], kw1, kw2)
    vc = nsa_compress(vc_r, pe[1], vw1, vw2)
    n_cmp = kc.shape[1]
    start = jnp.arange(n_cmp, dtype=jnp.int32) * CMP_STRIDE
    cmp_end = start + (CMP_LEN - 1)
    n_slc = -(-L // SLC_BLOCK)
    bstart = jnp.arange(n_slc, dtype=jnp.int32) * SLC_BLOCK
    ov = ((start[:, None] < bstart[None, :] + SLC_BLOCK) & (cmp_end[:, None] >= bstart[None, :])).astype(jnp.float32)
    pad = ((0, 0), (0, n_slc * SLC_BLOCK - L), (0, 0), (0, 0))
    ksb = jnp.pad(ks_r, pad).reshape(B, n_slc, SLC_BLOCK, Hk, Dh)
    vsb = jnp.pad(vs_r, pad).reshape(B, n_slc, SLC_BLOCK, Hk, Dh)
    return kc, vc, cmp_end, ov, ksb, vsb


def nsa_attend(q, gates, qpos, kc, vc, cmp_end, ov, ksb, vsb, kw, vw, kw_pos, bias_tab):
    B, Q = q.shape[:2]
    G = NSA_GROUP
    sc = HEAD_DIM ** -0.5
    qg = q.reshape(B, Q, H_NSA_KV, G, HEAD_DIM)
    bt = bias_tab.reshape(H_NSA_KV, G, N_BUCKETS)
    lc = jnp.einsum('bqkgd,bnkd->bkgqn', qg, kc) * sc
    pc = masked_softmax(lc, (cmp_end[None, :] <= qpos[:, None])[None, None, None])
    o_cmp = jnp.einsum('bkgqn,bnkd->bqkgd', pc.astype(vc.dtype), vc)
    n_slc = ov.shape[1]
    imp = jnp.einsum('bkgqn,ns->bkqs', pc, ov)
    cur = qpos // SLC_BLOCK
    j = jnp.arange(n_slc, dtype=jnp.int32)[None, :]
    avail = j <= cur[:, None]
    forced = (j == 0) | (j == cur[:, None]) | (j == cur[:, None] - 1)
    imp = jnp.where(avail, jnp.where(forced, FORCE, imp), NEG)
    top = min(SLC_TOPN, n_slc)
    _, idx = lax.top_k(imp, top)
    bi = jnp.arange(B)[:, None, None, None]
    ki = jnp.arange(H_NSA_KV)[None, :, None, None]
    ks = ksb[bi, idx, :, ki]
    vs = vsb[bi, idx, :, ki]
    ls = jnp.einsum('bqkgd,bkqnsd->bkgqns', qg, ks) * sc
    rel_s = qpos[None, None, :, None, None] - (idx[..., None] * SLC_BLOCK + jnp.arange(SLC_BLOCK, dtype=jnp.int32))
    mask_s = (idx <= cur[None, None, :, None])[..., None] & (rel_s >= 0)
    kk = jnp.arange(H_NSA_KV)[None, :, None, None, None, None]
    gg = jnp.arange(G)[None, None, :, None, None, None]
    ls = ls.astype(jnp.float32) + bt[kk, gg, t5_bucket(rel_s)[:, :, None]].astype(jnp.float32)
    n = top * SLC_BLOCK
    ps = masked_softmax(ls.reshape(B, H_NSA_KV, G, Q, n), mask_s.reshape(B, H_NSA_KV, 1, Q, n))
    o_slc = jnp.einsum('bkgqn,bkqnd->bqkgd', ps.astype(vs.dtype), vs.reshape(B, H_NSA_KV, Q, n, HEAD_DIM))
    lw = jnp.einsum('bqkgd,bwkd->bkgqw', qg, kw) * sc
    rel_w = qpos[:, None] - kw_pos[None, :]
    mask_w = (rel_w >= 0) & (rel_w < WINDOW) & (kw_pos[None, :] >= 0)
    lw = lw.astype(jnp.float32) + bt[:, :, t5_bucket(rel_w)].astype(jnp.float32)
    pw = masked_softmax(lw, mask_w)
    o_win = jnp.einsum('bkgqw,bwkd->bqkgd', pw.astype(vw.dtype), vw)
    g = gates.reshape(B, Q, H_NSA_KV, G, 3)
    o = g[..., 0:1] * o_cmp + g[..., 1:2] * o_slc + g[..., 2:3] * o_win
    return o.reshape(B, Q, W_NSA)


def mix_out(x, gate, o_m, z_m, o_n, z_n, w_out_l):
    mixed = jnp.concatenate([o_m * jax.nn.silu(z_m), o_n * jax.nn.silu(z_n)], axis=-1)
    return x + gate[:, None, :] * (mixed @ w_out_l)


def prompt_layer(x, c, lp, bias_m, bias_n):
    w_ada_l, b_ada_l, gain_l, w_in_l, pe, kw1, kw2, vw1, vw2, w_out_l = lp
    B, T, _ = x.shape
    (gate, z_m, z_n, q_m, k_m, v_m, q_n, kc_r, vc_r, ks_r, vs_r, kw_r, vw_r, gates) = project(x, c, w_ada_l, b_ada_l, gain_l, w_in_l)
    kb, vb, kmean = moba_keys(k_m, v_m)
    nk = nsa_keys(kc_r, vc_r, ks_r, vs_r, pe, kw1, kw2, vw1, vw2)
    wpad = ((0, 0), (WINDOW, 0), (0, 0), (0, 0))
    kw_pad = jnp.pad(kw_r, wpad)
    vw_pad = jnp.pad(vw_r, wpad)
    nc = T // Q_BLOCK
    chunk = lambda a: jnp.moveaxis(a.reshape(B, nc, Q_BLOCK, *a.shape[2:]), 1, 0)
    pos = jnp.arange(T, dtype=jnp.int32).reshape(nc, Q_BLOCK)

    def step(args):
        qm, qn, g, p = args
        s0 = p[0]
        kw = lax.dynamic_slice_in_dim(kw_pad, s0, WINDOW + Q_BLOCK, axis=1)
        vw = lax.dynamic_slice_in_dim(vw_pad, s0, WINDOW + Q_BLOCK, axis=1)
        kw_pos = s0 - WINDOW + jnp.arange(WINDOW + Q_BLOCK, dtype=jnp.int32)
        o_m = moba_attend(qm, p, kb, vb, kmean, bias_m)
        o_n = nsa_attend(qn, g, p, *nk, kw, vw, kw_pos, bias_n)
        return o_m, o_n

    o_m, o_n = lax.map(step, (chunk(q_m), chunk(q_n), chunk(gates), pos))
    unchunk = lambda o: jnp.moveaxis(o, 0, 1).reshape(B, T, -1)
    x_new = mix_out(x, gate, unchunk(o_m), z_m, unchunk(o_n), z_n, w_out_l)
    wb = min(WINDOW, T)
    return (x_new, jnp.stack([k_m, v_m], axis=2), jnp.stack([kc_r, vc_r, ks_r, vs_r], axis=2),
            jnp.stack([kw_r, vw_r], axis=2)[:, T - wb:])


def sample_layer(x, c, cache_m, cache_n, win_state, page_table, lp, bias_m, bias_n):
    w_ada_l, b_ada_l, gain_l, w_in_l, pe, kw1, kw2, vw1, vw2, w_out_l = lp
    B, S, _ = x.shape
    P = page_table.shape[1] * cache_m.shape[1]
    (gate, z_m, z_n, q_m, k_m, v_m, q_n, kc_r, vc_r, ks_r, vs_r, kw_r, vw_r, gates) = project(x, c, w_ada_l, b_ada_l, gain_l, w_in_l)
    past_m = cache_m[page_table].reshape(B, P, 2, H_MOBA, HEAD_DIM)
    past_n = cache_n[page_table].reshape(B, P, 4, H_NSA_KV, HEAD_DIM)
    cat = lambda past, new: jnp.concatenate([past, new], axis=1)
    qpos = P + jnp.arange(S, dtype=jnp.int32)
    o_m = moba_attend(q_m, qpos, *moba_keys(cat(past_m[:, :, 0], k_m), cat(past_m[:, :, 1], v_m)), bias_m)
    nk = nsa_keys(cat(past_n[:, :, 0], kc_r), cat(past_n[:, :, 1], vc_r), cat(past_n[:, :, 2], ks_r),
                  cat(past_n[:, :, 3], vs_r), pe, kw1, kw2, vw1, vw2)
    win_all = cat(win_state, jnp.stack([kw_r, vw_r], axis=2))
    wb = win_state.shape[1]
    kw_pos = P - wb + jnp.arange(wb + S, dtype=jnp.int32)
    o_n = nsa_attend(q_n, gates, qpos, *nk, win_all[:, :, 0], win_all[:, :, 1], kw_pos, bias_n)
    x_new = mix_out(x, gate, o_m, z_m, o_n, z_n, w_out_l)
    return (x_new, jnp.stack([k_m, v_m], axis=2), jnp.stack([kc_r, vc_r, ks_r, vs_r], axis=2), win_all[:, S:])


def setup_inputs(seed: int = 0) -> dict:
    key = jax.random.key(seed)
    ks = jax.random.split(key, 24)
    n_pages = PAST_LEN // PAGE_SIZE
    used = DEC_BATCH * n_pages
    n_phys = used + max(1, used // 4)
    wb = min(WINDOW, PAST_LEN)
    nrm = lambda k, shape, s=1.0: jax.random.normal(k, shape, jnp.float32) * s
    page_table = jax.random.permutation(ks[7], n_phys)[:used].reshape(DEC_BATCH, n_pages).astype(jnp.int32)
    return {
        'x_prompt': nrm(ks[0], (BATCH, SEQ, D_MODEL)),
        'x_sample': nrm(ks[1], (DEC_BATCH, DEC_SEQ, D_MODEL)),
        'c_prompt': nrm(ks[2], (BATCH, D_MODEL)),
        'c_sample': nrm(ks[3], (DEC_BATCH, D_MODEL)),
        'cache_moba_kv': nrm(ks[4], (DEPTH, n_phys, PAGE_SIZE, 2, H_MOBA, HEAD_DIM)),
        'cache_nsa_kv': nrm(ks[5], (DEPTH, n_phys, PAGE_SIZE, 4, H_NSA_KV, HEAD_DIM)),
        'state_nsa_win': nrm(ks[6], (DEPTH, DEC_BATCH, wb, 2, H_NSA_KV, HEAD_DIM)),
        'page_table': page_table,
        'w_ada': nrm(ks[8], (DEPTH, D_MODEL, 3 * D_MODEL), 0.3 * D_MODEL ** -0.5),
        'b_ada': nrm(ks[9], (DEPTH, 3 * D_MODEL), 0.01),
        'norm_gain': 1.0 + nrm(ks[10], (DEPTH, D_MODEL), 0.01),
        'w_in': nrm(ks[11], (DEPTH, D_MODEL, D_IN), D_MODEL ** -0.5),
        'cmp_pe': nrm(ks[12], (DEPTH, 2, CMP_LEN, HEAD_DIM), 0.5),
        'cmp_k_w1': nrm(ks[13], (DEPTH, CMP_LEN * HEAD_DIM, CMP_HIDDEN), (CMP_LEN * HEAD_DIM) ** -0.5),
        'cmp_k_w2': nrm(ks[14], (DEPTH, CMP_HIDDEN, HEAD_DIM), CMP_HIDDEN ** -0.5),
        'cmp_v_w1': nrm(ks[15], (DEPTH, CMP_LEN * HEAD_DIM, CMP_HIDDEN), (CMP_LEN * HEAD_DIM) ** -0.5),
        'cmp_v_w2': nrm(ks[16], (DEPTH, CMP_HIDDEN, HEAD_DIM), CMP_HIDDEN ** -0.5),
        'w_out': nrm(ks[17], (DEPTH, D_MODEL, D_MODEL), D_MODEL ** -0.5),
        'rel_bias': nrm(ks[18], (N_BUCKETS, N_HEADS), 0.5),
        'final_gain': 1.0 + nrm(ks[19], (D_MODEL,), 0.01),
    }


def reference(x_prompt, x_sample, c_prompt, c_sample, cache_moba_kv, cache_nsa_kv, state_nsa_win, page_table,
              w_ada, b_ada, norm_gain, w_in, cmp_pe, cmp_k_w1, cmp_k_w2, cmp_v_w1, cmp_v_w2, w_out,
              rel_bias, final_gain):
    bias_m = rel_bias[:, :H_MOBA].T
    bias_n = rel_bias[:, H_MOBA:].T
    xp, xs = x_prompt, x_sample
    mkv_p, mkv_s, nkv_p, nkv_s, win_p, win_s = [], [], [], [], [], []
    for l in range(DEPTH):
        lp = (w_ada[l], b_ada[l], norm_gain[l], w_in[l], cmp_pe[l], cmp_k_w1[l], cmp_k_w2[l],
              cmp_v_w1[l], cmp_v_w2[l], w_out[l])
        xp, a, b, cw = prompt_layer(xp, c_prompt, lp, bias_m, bias_n)
        mkv_p.append(a)
        nkv_p.append(b)
        win_p.append(cw)
        xs, a, b, cw = sample_layer(xs, c_sample, cache_moba_kv[l], cache_nsa_kv[l], state_nsa_win[l],
                                    page_table, lp, bias_m, bias_n)
        mkv_s.append(a)
        nkv_s.append(b)
        win_s.append(cw)
    y_prompt = rmsnorm(xp, final_gain)
    y_sample = rmsnorm(xs, final_gain)
    moba_kv_prompt = jnp.stack(mkv_p)
    moba_kv_sample = jnp.stack(mkv_s)
    nsa_kv_prompt = jnp.stack(nkv_p)
    nsa_kv_sample = jnp.stack(nkv_s)
    win_prompt = jnp.stack(win_p)
    win_sample = jnp.stack(win_s)
    return (y_prompt, y_sample, moba_kv_prompt, moba_kv_sample, nsa_kv_prompt, nsa_kv_sample, win_prompt, win_sample)
```

```python
import functools
import math

import jax
import jax.numpy as jnp
from jax import lax
from jax.experimental import pallas as pl
from jax.experimental.pallas import tpu as pltpu

F32 = jnp.float32
BF16 = jnp.bfloat16

HEAD_DIM = 64
H_MOBA = 8
H_NSA = 8
H_NSA_KV = 2
NSA_GROUP = 4
W_MOBA = H_MOBA * HEAD_DIM
W_NSA = H_NSA * HEAD_DIM
W_NSA_KV = H_NSA_KV * HEAD_DIM
MOBA_BLOCK = 256
MOBA_TOPK = 3
CMP_LEN = 32
CMP_STRIDE = 16
CMP_HIDDEN = 2 * HEAD_DIM
SLC_BLOCK = 64
SLC_TOPN = 16
WINDOW = 512
N_BUCKETS = 32
MAX_DISTANCE = 128
RMS_EPS = 1e-6
NEG = -1e30
FORCE = 1e9
TINY = 1e-30

LANE = 128
SUBLANE = 8
KEY_TILE = 256
NSA_TQ = 128
PROJ_TM = 256
PAGES_PER_STEP = 8
NEW_PAD = 8
VMEM_LIMIT = 56 * 1024 * 1024

C_QM, C_MKV, C_ZM, C_QN, C_NKV, C_WKV, C_ZN, C_G, C_END = 0, 512, 1536, 2048, 3072, 3584, 3840, 4352, 4608

_NT = (((1,), (1,)), ((), ()))


def _dot(a, b):
    return jnp.dot(a, b, preferred_element_type=F32)


def _dot_nt(a, b):
    return lax.dot_general(a, b, _NT, preferred_element_type=F32)


def _split_bf16(a):
    hi = a.astype(BF16)
    lo = (a - hi.astype(F32)).astype(BF16)
    return hi, lo


def _sigmoid(x):
    return 1.0 / (1.0 + jnp.exp(-x))


def _cparams(sem):
    return pltpu.CompilerParams(dimension_semantics=sem, vmem_limit_bytes=VMEM_LIMIT)


def _ada_kernel(c_ref, w_ref, b_ref, o_ref):
    ch, cl = _split_bf16(c_ref[...])
    wh, wl = _split_bf16(w_ref[...])
    o_ref[...] = _dot(ch, wh) + _dot(ch, wl) + _dot(cl, wh) + b_ref[...]


def _ada(c_all, w_ada, b_ada):
    m, d = c_all.shape
    n = w_ada.shape[1]
    tn = 512
    return pl.pallas_call(
        _ada_kernel,
        grid=(n // tn,),
        in_specs=[pl.BlockSpec((m, d), lambda j: (0, 0)),
                  pl.BlockSpec((d, tn), lambda j: (0, j)),
                  pl.BlockSpec((1, tn), lambda j: (0, j))],
        out_specs=pl.BlockSpec((m, tn), lambda j: (0, j)),
        out_shape=jax.ShapeDtypeStruct((m, n), F32),
        compiler_params=_cparams(("arbitrary",)),
        name="ada",
    )(c_all, w_ada, b_ada.reshape(1, n))


def _proj_kernel(x_ref, sc_ref, sh_ref, gain_ref, w_ref,
                 qm_ref, mkv32_ref, mkv16_ref, zm_ref, qn_ref, nkv32_ref, nkv16_ref, wkv32_ref, wkv16_ref,
                 zn_ref, g_ref, *km_refs):
    x = x_ref[...]
    inv = lax.rsqrt(jnp.mean(x * x, axis=-1, keepdims=True) + RMS_EPS)
    h = (x * inv) * gain_ref[...] * (1.0 + sc_ref[0]) + sh_ref[0]
    hb = h.astype(BF16)
    col = lambda a, b: _dot(hb, w_ref[:, a:b])
    qm_ref[...] = col(C_QM, C_MKV).astype(BF16)
    mkv = col(C_MKV, C_ZM)
    mkv32_ref[...] = mkv
    mkv16_ref[...] = mkv.astype(BF16)
    zm_ref[...] = col(C_ZM, C_QN).astype(BF16)
    qn_ref[...] = col(C_QN, C_NKV).astype(BF16)
    nkv = col(C_NKV, C_WKV)
    nkv32_ref[...] = nkv
    nkv16_ref[...] = nkv[:, 2 * W_NSA_KV:].astype(BF16)
    wkv = col(C_WKV, C_ZN)
    wkv32_ref[...] = wkv
    wkv16_ref[...] = wkv.astype(BF16)
    zn_ref[...] = col(C_ZN, C_G).astype(BF16)
    g_ref[...] = _sigmoid(col(C_G, C_END))
    if km_refs:
        km_refs[0][0] = jnp.mean(mkv[:, :W_MOBA], axis=0, keepdims=True)


def _proj(x2d, sc, sh, gain, w, tm, tiles_per_mod, with_kmean):
    r, d = x2d.shape
    nt = r // tm
    mrows = sc.shape[1]
    row = lambda width: pl.BlockSpec((tm, width), lambda i: (i, 0))
    widths = [(512, BF16), (1024, F32), (1024, BF16), (512, BF16), (1024, BF16), (512, F32), (256, BF16),
              (256, F32), (256, BF16), (512, BF16), (256, F32)]
    out_specs = [row(wd) for wd, _ in widths]
    out_shape = [jax.ShapeDtypeStruct((r, wd), dt) for wd, dt in widths]
    if with_kmean:
        out_specs.append(pl.BlockSpec((1, 1, W_MOBA), lambda i: (i, 0, 0)))
        out_shape.append(jax.ShapeDtypeStruct((nt, 1, W_MOBA), F32))
    return pl.pallas_call(
        _proj_kernel,
        grid=(nt,),
        in_specs=[row(d),
                  pl.BlockSpec((1, mrows, d), lambda i: (i // tiles_per_mod, 0, 0)),
                  pl.BlockSpec((1, mrows, d), lambda i: (i // tiles_per_mod, 0, 0)),
                  pl.BlockSpec((1, d), lambda i: (0, 0)),
                  pl.BlockSpec((d, C_END), lambda i: (0, 0))],
        out_specs=out_specs,
        out_shape=out_shape,
        compiler_params=_cparams(("arbitrary",)),
        name="proj",
    )(x2d, sc, sh, gain, w)


def _flash_step(s, v, m_ref, l_ref, acc_ref):
    m_old = m_ref[...]
    m_new = jnp.maximum(m_old, jnp.max(s, axis=1, keepdims=True))
    alpha = jnp.exp(m_old - m_new)
    p = jnp.exp(s - m_new)
    l_ref[...] = alpha * l_ref[...] + jnp.sum(p, axis=1, keepdims=True)
    acc_ref[...] = alpha * acc_ref[...] + _dot(p.astype(BF16), v)
    m_ref[...] = m_new


def _flash_init(m_ref, l_ref, acc_ref):
    m_ref[...] = jnp.full(m_ref.shape, NEG, F32)
    l_ref[...] = jnp.zeros(l_ref.shape, F32)
    acc_ref[...] = jnp.zeros(acc_ref.shape, F32)


def _topk_mask(vals, k):
    lanef = lax.broadcasted_iota(jnp.int32, vals.shape, 1).astype(F32)

    def body(_, taken):
        cur = jnp.where(taken > 0.0, -jnp.inf, vals)
        mx = jnp.max(cur, axis=1, keepdims=True)
        first = jnp.min(jnp.where(cur == mx, lanef, 1e9), axis=1, keepdims=True)
        return jnp.where(lanef == first, 1.0, taken)

    return lax.fori_loop(0, k, body, jnp.zeros(vals.shape, F32))


def _masked_softmax(logits, valid):
    lm = jnp.where(valid, logits, NEG)
    p = jnp.exp(lm - jnp.max(lm, axis=1, keepdims=True)) * valid.astype(F32)
    return p / jnp.maximum(jnp.sum(p, axis=1, keepdims=True), TINY)


def _moba_kernel(q_ref, k_ref, v_ref, km_ref, kp_ref, tab_ref, o_ref, qaug_ref, m_ref, l_ref, acc_ref):
    i = pl.program_id(2)
    tq = q_ref.shape[1]
    q = q_ref[0]
    lane = lax.broadcasted_iota(jnp.int32, q.shape, 1)
    zero = jnp.zeros_like(q)
    q2 = jnp.concatenate([jnp.where(lane < HEAD_DIM, q, zero), jnp.where(lane >= HEAD_DIM, q, zero)], axis=0)
    sc = _dot_nt(q2, km_ref[0].astype(BF16))
    blk = lax.broadcasted_iota(jnp.int32, sc.shape, 1)
    cand = blk < i
    taken = _topk_mask(jnp.where(cand, sc, NEG), MOBA_TOPK)
    sel = ((taken > 0.0) & cand) | (blk == i)
    qaug_ref[...] = jnp.concatenate([q2, jnp.where(sel, 0.0, NEG).astype(BF16)], axis=1)
    _flash_init(m_ref, l_ref, acc_ref)

    def tile_logits(n):
        off = pl.multiple_of(n * KEY_TILE, KEY_TILE)
        kaug = jnp.concatenate([k_ref[0, pl.ds(off, KEY_TILE), :], kp_ref[pl.ds(off, KEY_TILE), :]], axis=1)
        return _dot_nt(qaug_ref[...], kaug), v_ref[0, pl.ds(off, KEY_TILE), :]

    def far_body(n, carry):
        s, v = tile_logits(n)
        _flash_step(s, v, m_ref, l_ref, acc_ref)
        return carry

    lax.fori_loop(0, jnp.maximum(i - 1, 0), far_body, 0)

    @pl.when(i >= 1)
    def _():
        s, v = tile_logits(i - 1)
        _flash_step(s + tab_ref[0, :, 0:KEY_TILE], v, m_ref, l_ref, acc_ref)

    s, v = tile_logits(i)
    _flash_step(s + tab_ref[0, :, KEY_TILE:2 * KEY_TILE], v, m_ref, l_ref, acc_ref)
    o = acc_ref[...] / jnp.maximum(l_ref[...], TINY)
    o_ref[0] = jnp.where(lane < HEAD_DIM, o[:tq], o[tq:]).astype(BF16)


def _moba_prompt(qm, mkv16, kmean_pad, kp, tab):
    b, t, _ = qm.shape
    tq = MOBA_BLOCK
    nq = t // tq
    hp = H_MOBA // 2
    rows = 2 * tq
    return pl.pallas_call(
        _moba_kernel,
        grid=(b, hp, nq),
        in_specs=[pl.BlockSpec((1, tq, LANE), lambda bb, h, i: (bb, i, h)),
                  pl.BlockSpec((1, t, LANE), lambda bb, h, i: (bb, 0, h)),
                  pl.BlockSpec((1, t, LANE), lambda bb, h, i: (bb, 0, hp + h)),
                  pl.BlockSpec((1, LANE, LANE), lambda bb, h, i: (bb, 0, h)),
                  pl.BlockSpec((t, LANE), lambda bb, h, i: (0, 0)),
                  pl.BlockSpec((1, rows, 2 * KEY_TILE), lambda bb, h, i: (h, 0, 0))],
        out_specs=pl.BlockSpec((1, tq, LANE), lambda bb, h, i: (bb, i, h)),
        out_shape=jax.ShapeDtypeStruct((b, t, W_MOBA), BF16),
        scratch_shapes=[pltpu.VMEM((rows, 2 * LANE), BF16),
                        pltpu.VMEM((rows, 1), F32), pltpu.VMEM((rows, 1), F32), pltpu.VMEM((rows, LANE), F32)],
        compiler_params=_cparams(("arbitrary", "arbitrary", "arbitrary")),
        name="moba_prompt",
    )(qm, mkv16, mkv16, kmean_pad, kp, tab)


def _cmp_kernel(pt_ref, *refs, pps):
    del pt_ref
    pages_k, pages_v = refs[:pps], refs[pps:2 * pps]
    pe_ref, kwa_ref, kwb_ref, kw2_ref, vwa_ref, vwb_ref, vw2_ref, kc_ref, vc_ref, uk_ref, uv_ref = refs[2 * pps:]
    j = pl.program_id(1)
    groups = pages_k[0].shape[1] // CMP_STRIDE
    for u in range(pps):
        row0 = pl.multiple_of((j * pps + u) * groups, groups)
        for l in range(CMP_STRIDE):
            rows_l = pl.ds(l, groups, stride=CMP_STRIDE)
            uk_ref[pl.ds(row0, groups), l * LANE:(l + 1) * LANE] = pages_k[u][0, rows_l, :]
            uv_ref[pl.ds(row0, groups), l * LANE:(l + 1) * LANE] = pages_v[u][0, rows_l, :]

    @pl.when(j == pl.num_programs(1) - 1)
    def _():
        nc = uk_ref.shape[0]
        last = lax.broadcasted_iota(jnp.int32, (nc, LANE), 0) == nc - 1

        def compress(u_ref, pe, wa_ref, wb_ref, w2_ref, out_ref):
            u = u_ref[...]
            pa = _dot((u + pe[0]).astype(BF16), wa_ref[...])
            pb = _dot((u + pe[1]).astype(BF16), wb_ref[...])
            pre = pa + pltpu.roll(pb, nc - 1, 0)
            hid = pre * _sigmoid(pre)
            out = _dot(hid.astype(BF16), w2_ref[...])
            out_ref[0] = jnp.where(last, 0.0, out).astype(BF16)

        compress(uk_ref, pe_ref[0], kwa_ref, kwb_ref, kw2_ref, kc_ref)
        compress(uv_ref, pe_ref[1], vwa_ref, vwb_ref, vw2_ref, vc_ref)


def _compress(pages_arr, pt, pe, kwa, kwb, kw2, vwa, vwb, vw2):
    b, n_pg = pt.shape
    page = pages_arr.shape[1]
    pps = min(PAGES_PER_STEP, n_pg)
    nc = n_pg * page // CMP_STRIDE
    page_spec = lambda u, col: pl.BlockSpec((1, page, LANE), lambda bb, j, p: (p[bb, j * pps + u], 0, col))
    full = lambda a: pl.BlockSpec(a.shape, lambda bb, j, p: (0,) * a.ndim)
    consts = (pe, kwa, kwb, kw2, vwa, vwb, vw2)
    grid_spec = pltpu.PrefetchScalarGridSpec(
        num_scalar_prefetch=1,
        grid=(b, n_pg // pps),
        in_specs=[page_spec(u, col) for col in range(2) for u in range(pps)] + [full(a) for a in consts],
        out_specs=[pl.BlockSpec((1, nc, LANE), lambda bb, j, p: (bb, 0, 0))] * 2,
        scratch_shapes=[pltpu.VMEM((nc, CMP_STRIDE * LANE), F32)] * 2)
    return pl.pallas_call(
        functools.partial(_cmp_kernel, pps=pps),
        grid_spec=grid_spec,
        out_shape=[jax.ShapeDtypeStruct((b, nc, LANE), BF16)] * 2,
        compiler_params=_cparams(("arbitrary", "arbitrary")),
        name="nsa_compress",
    )(pt, *([pages_arr] * (2 * pps)), *consts)


def _nsa_kernel(q_ref, kc_ref, vc_ref, ov_ref, ks_ref, vs_ref, kw_ref, vw_ref, kp_ref, ts_ref, tw_ref, g_ref,
                o_ref, qaug_ref, m_ref, l_ref, acc_ref, ocmp_ref, oslc_ref):
    k = pl.program_id(1)
    i = pl.program_id(2)
    tq = q_ref.shape[1]
    rows = NSA_GROUP * tq
    nc = kc_ref.shape[1]
    q4 = jnp.concatenate([q_ref[0, :, g * LANE:(g + 1) * LANE] for g in range(NSA_GROUP)], axis=0)
    qpos1 = i * tq + lax.broadcasted_iota(jnp.int32, (tq, 1), 0)
    qpos = jnp.concatenate([qpos1] * NSA_GROUP, axis=0)

    lc = _dot_nt(q4, kc_ref[0])
    tok = lax.broadcasted_iota(jnp.int32, lc.shape, 1)
    pc = _masked_softmax(lc, tok * CMP_STRIDE + (CMP_LEN - 1) <= qpos)
    ocmp_ref[...] = _dot(pc.astype(BF16), vc_ref[0])

    pcs = pc[0:tq] + pc[tq:2 * tq] + pc[2 * tq:3 * tq] + pc[3 * tq:4 * tq]
    ph, plo = _split_bf16(pcs)
    imp = _dot(ph, ov_ref[...]) + _dot(plo, ov_ref[...])
    jb = lax.broadcasted_iota(jnp.int32, imp.shape, 1)
    cur = qpos1 >> int(math.log2(SLC_BLOCK))
    avail = jb <= cur
    forced = (jb == 0) | (jb == cur) | (jb == cur - 1)
    imp = jnp.where(avail, jnp.where(forced, FORCE, imp), NEG)
    sel = (_topk_mask(imp, SLC_TOPN) > 0.0) & avail
    selb = jnp.where(sel, 0.0, NEG).astype(BF16)
    qaug_ref[...] = jnp.concatenate([q4, jnp.concatenate([selb] * NSA_GROUP, axis=0)], axis=1)

    cd = (i * tq) // KEY_TILE
    a0 = pl.multiple_of((i * tq) % KEY_TILE, tq)

    def table(t_ref, c0):
        return jnp.concatenate([t_ref[0, pl.ds(g * KEY_TILE + a0, tq), c0:c0 + KEY_TILE] for g in range(NSA_GROUP)],
                               axis=0)

    _flash_init(m_ref, l_ref, acc_ref)

    def slc_logits(n):
        off = pl.multiple_of(n * KEY_TILE, KEY_TILE)
        kaug = jnp.concatenate([ks_ref[0, pl.ds(off, KEY_TILE), :], kp_ref[pl.ds(off, KEY_TILE), :]], axis=1)
        return _dot_nt(qaug_ref[...], kaug), vs_ref[0, pl.ds(off, KEY_TILE), :]

    def far_body(n, carry):
        s, v = slc_logits(n)
        _flash_step(s, v, m_ref, l_ref, acc_ref)
        return carry

    lax.fori_loop(0, jnp.maximum(cd - 1, 0), far_body, 0)

    @pl.when(cd >= 1)
    def _():
        s, v = slc_logits(cd - 1)
        _flash_step(s + table(ts_ref, 0), v, m_ref, l_ref, acc_ref)

    s, v = slc_logits(cd)
    _flash_step(s + table(ts_ref, KEY_TILE), v, m_ref, l_ref, acc_ref)
    oslc_ref[...] = acc_ref[...] / jnp.maximum(l_ref[...], TINY)

    _flash_init(m_ref, l_ref, acc_ref)
    n_win = WINDOW // KEY_TILE
    for w in range(n_win + 1):
        n = cd - n_win + w

        @pl.when(n >= 0)
        def _():
            off = pl.multiple_of(n * KEY_TILE, KEY_TILE)
            s = _dot_nt(q4, kw_ref[0, pl.ds(off, KEY_TILE), :]) + table(tw_ref, w * KEY_TILE)
            _flash_step(s, vw_ref[0, pl.ds(off, KEY_TILE), :], m_ref, l_ref, acc_ref)

    owin = acc_ref[...] / jnp.maximum(l_ref[...], TINY)

    ocmp = ocmp_ref[...]
    oslc = oslc_ref[...]
    gts = g_ref[0]
    lane = lax.broadcasted_iota(jnp.int32, (tq, LANE), 1)
    placed = []
    for g in range(NSA_GROUP):
        r = slice(g * tq, (g + 1) * tq)
        og = (gts[:, g:g + 1] * ocmp[r] + gts[:, NSA_GROUP + g:NSA_GROUP + g + 1] * oslc[r]
              + gts[:, 2 * NSA_GROUP + g:2 * NSA_GROUP + g + 1] * owin[r])
        placed.append(jnp.where(k == (g % 2), og, pltpu.roll(og, HEAD_DIM, 1)))
    for pair in range(NSA_GROUP // 2):
        o_ref[0, :, pair * LANE:(pair + 1) * LANE] = jnp.where(
            lane < HEAD_DIM, placed[2 * pair], placed[2 * pair + 1]).astype(BF16)


def _nsa_prompt(qn, kc, vc, ov, nkv16, wkv16, kps, ts, tw, gts):
    b, t, _ = qn.shape
    tq = min(NSA_TQ, t)
    nq = t // tq
    rows = NSA_GROUP * tq
    nc = kc.shape[1]
    seq = lambda col: pl.BlockSpec((1, t, LANE), lambda bb, k, i: (bb, 0, col))
    return pl.pallas_call(
        _nsa_kernel,
        grid=(b, H_NSA_KV, nq),
        in_specs=[pl.BlockSpec((1, tq, NSA_GROUP * LANE), lambda bb, k, i: (bb, i, k)),
                  pl.BlockSpec((1, nc, LANE), lambda bb, k, i: (bb, 0, 0)),
                  pl.BlockSpec((1, nc, LANE), lambda bb, k, i: (bb, 0, 0)),
                  pl.BlockSpec(ov.shape, lambda bb, k, i: (0, 0)),
                  seq(0), seq(1),
                  seq(0), seq(1),
                  pl.BlockSpec((t, LANE), lambda bb, k, i: (0, 0)),
                  pl.BlockSpec((1,) + ts.shape[1:], lambda bb, k, i: (k, 0, 0)),
                  pl.BlockSpec((1,) + tw.shape[1:], lambda bb, k, i: (k, 0, 0)),
                  pl.BlockSpec((1, tq, LANE), lambda bb, k, i: (bb, i, k))],
        out_specs=pl.BlockSpec((1, tq, NSA_GROUP * HEAD_DIM), lambda bb, k, i: (bb, i, k)),
        out_shape=jax.ShapeDtypeStruct((b, t, W_NSA), BF16),
        scratch_shapes=[pltpu.VMEM((rows, 2 * LANE), BF16),
                        pltpu.VMEM((rows, 1), F32), pltpu.VMEM((rows, 1), F32), pltpu.VMEM((rows, LANE), F32),
                        pltpu.VMEM((rows, LANE), F32), pltpu.VMEM((rows, LANE), F32)],
        compiler_params=_cparams(("arbitrary", "arbitrary", "arbitrary")),
        name="nsa_prompt",
    )(qn, kc, vc, ov, nkv16, nkv16, wkv16, wkv16, kps, ts, tw, gts)


def _moba_dec_kernel(pt_ref, qbd_ref, new_ref, td_ref, *refs, pps, nblk):
    del pt_ref
    pages = refs[:pps]
    o_ref, km_ref, mall_ref, lall_ref, oall_ref = refs[pps:]
    j = pl.program_id(1)
    qbd = qbd_ref[0]
    rows = qbd.shape[0]
    lane = lax.broadcasted_iota(jnp.int32, (rows, LANE), 1)
    ppb = MOBA_BLOCK // pages[0].shape[1]

    @pl.when(j == 0)
    def _():
        km_ref[...] = jnp.zeros(km_ref.shape, F32)
        mall_ref[...] = jnp.full(mall_ref.shape, NEG, F32)
        lall_ref[...] = jnp.zeros(lall_ref.shape, F32)

    for u in range(pps // ppb):
        blk = j * (pps // ppb) + u
        kf = jnp.concatenate([pages[u * ppb + w][0, :, :W_MOBA] for w in range(ppb)], axis=0)
        vf = jnp.concatenate([pages[u * ppb + w][0, :, W_MOBA:] for w in range(ppb)], axis=0)
        km_ref[pl.ds(blk, 1), :] = jnp.mean(kf, axis=0, keepdims=True)
        near = (blk == nblk - 1).astype(F32)
        s = _dot_nt(qbd, kf.astype(BF16)) + near * td_ref[:, 0:MOBA_BLOCK]
        mb = jnp.max(s, axis=1, keepdims=True)
        p = jnp.exp(s - mb)
        mall_ref[...] = jnp.where(lane == blk, mb, mall_ref[...])
        lall_ref[...] = jnp.where(lane == blk, jnp.sum(p, axis=1, keepdims=True), lall_ref[...])
        oall_ref[pl.ds(blk, 1)] = _dot(p.astype(BF16), vf.astype(BF16))[None]

    @pl.when(j == pl.num_programs(1) - 1)
    def _():
        kn = new_ref[0, :, :W_MOBA].astype(BF16)
        vn = new_ref[0, :, W_MOBA:].astype(BF16)
        s_own = _dot_nt(qbd, kn) + td_ref[:, MOBA_BLOCK:MOBA_BLOCK + NEW_PAD]
        m_own = jnp.max(s_own, axis=1, keepdims=True)
        p_own = jnp.exp(s_own - m_own)
        l_own = jnp.sum(p_own, axis=1, keepdims=True)
        o_own = _dot(p_own.astype(BF16), vn)
        sc = _dot_nt(qbd, km_ref[...].astype(BF16))
        cand = lane < nblk
        sel = (_topk_mask(jnp.where(cand, sc, NEG), MOBA_TOPK) > 0.0) & cand
        mall = mall_ref[...]
        m_fin = jnp.maximum(jnp.max(jnp.where(sel, mall, NEG), axis=1, keepdims=True), m_own)
        w = jnp.where(sel, jnp.exp(mall - m_fin), 0.0)
        w_own = jnp.exp(m_own - m_fin)
        l = jnp.sum(w * lall_ref[...], axis=1, keepdims=True) + w_own * l_own
        acc = w_own * o_own
        for n in range(nblk):
            acc = acc + w[:, n:n + 1] * oall_ref[n]
        o = acc / jnp.maximum(l, TINY)
        rowh = lax.broadcasted_iota(jnp.int32, o.shape, 0) & (H_MOBA - 1)
        laneh = lax.broadcasted_iota(jnp.int32, o.shape, 1) >> int(math.log2(HEAD_DIM))
        o = jnp.where(rowh == laneh, o, 0.0)
        o_ref[0] = jnp.sum(o.reshape(rows // H_MOBA, H_MOBA, W_MOBA), axis=1)


def _moba_decode(cache_m, pt, qbd, new_pad, td):
    b, n_pg = pt.shape
    page = cache_m.shape[1]
    pps = min(PAGES_PER_STEP, n_pg)
    nblk = n_pg * page // MOBA_BLOCK
    rows = qbd.shape[1]
    s_new = rows // H_MOBA
    page_spec = lambda u: pl.BlockSpec((1, page, 2 * W_MOBA), lambda bb, j, p: (p[bb, j * pps + u], 0, 0))
    grid_spec = pltpu.PrefetchScalarGridSpec(
        num_scalar_prefetch=1,
        grid=(b, n_pg // pps),
        in_specs=[pl.BlockSpec((1, rows, W_MOBA), lambda bb, j, p: (bb, 0, 0)),
                  pl.BlockSpec((1, NEW_PAD, 2 * W_MOBA), lambda bb, j, p: (bb, 0, 0)),
                  pl.BlockSpec(td.shape, lambda bb, j, p: (0, 0))] + [page_spec(u) for u in range(pps)],
        out_specs=pl.BlockSpec((1, s_new, W_MOBA), lambda bb, j, p: (bb, 0, 0)),
        scratch_shapes=[pltpu.VMEM((LANE, W_MOBA), F32), pltpu.VMEM((rows, LANE), F32), pltpu.VMEM((rows, LANE), F32),
                        pltpu.VMEM((nblk, rows, W_MOBA), F32)])
    return pl.pallas_call(
        functools.partial(_moba_dec_kernel, pps=pps, nblk=nblk),
        grid_spec=grid_spec,
        out_shape=jax.ShapeDtypeStruct((b, s_new, W_MOBA), F32),
        compiler_params=_cparams(("arbitrary", "arbitrary")),
        name="moba_decode",
    )(pt, qbd, new_pad, td, *([cache_m] * pps))


def _nsa_dec_kernel(pt_ref, qd_ref, kc_ref, vc_ref, ov_ref, kp_ref, new_ref, win_ref, neww_ref, g_ref,
                    ts_ref, tso_ref, tw_ref, two_ref, *refs, pps, past):
    del pt_ref
    pages = refs[:pps]
    o_ref, qaug_ref, m_ref, l_ref, acc_ref, ocmp_ref, selown_ref = refs[pps:]
    j = pl.program_id(1)
    nj = pl.num_programs(1)
    qd = qd_ref[0]
    rows = qd.shape[0]
    page = pages[0].shape[1]
    grp = rows // NSA_GROUP
    s_new = grp // H_NSA_KV

    @pl.when(j == 0)
    def _():
        lc = _dot_nt(qd, kc_ref[0])
        tok = lax.broadcasted_iota(jnp.int32, lc.shape, 1)
        qpos = past + (lax.broadcasted_iota(jnp.int32, (rows, 1), 0) & (s_new - 1))
        pc = _masked_softmax(lc, tok * CMP_STRIDE + (CMP_LEN - 1) <= qpos)
        ocmp_ref[...] = _dot(pc.astype(BF16), vc_ref[0])
        pcs = pc[0:grp] + pc[grp:2 * grp] + pc[2 * grp:3 * grp] + pc[3 * grp:4 * grp]
        ph, plo = _split_bf16(pcs)
        imp = _dot(ph, ov_ref[...]) + _dot(plo, ov_ref[...])
        jb = lax.broadcasted_iota(jnp.int32, imp.shape, 1)
        cur = qpos[0:grp] >> int(math.log2(SLC_BLOCK))
        avail = jb <= cur
        forced = (jb == 0) | (jb == cur) | (jb == cur - 1)
        imp = jnp.where(avail, jnp.where(forced, FORCE, imp), NEG)
        sel = (_topk_mask(imp, SLC_TOPN) > 0.0) & avail
        selb = jnp.concatenate([jnp.where(sel, 0.0, NEG)] * NSA_GROUP, axis=0)
        qaug_ref[...] = jnp.concatenate([qd, selb[:, :LANE].astype(BF16)], axis=1)
        selown_ref[...] = selb[:, LANE:]
        _flash_init(m_ref, l_ref, acc_ref)

    for u in range(pps):
        pg = j * pps + u
        off = pl.multiple_of(pg * page, page)
        kaug = jnp.concatenate([pages[u][0, :, :LANE].astype(BF16), kp_ref[pl.ds(off, page), :]], axis=1)
        near = (pg == nj * pps - 1).astype(F32)
        s = _dot_nt(qaug_ref[...], kaug) + near * ts_ref[...]
        _flash_step(s, pages[u][0, :, LANE:].astype(BF16), m_ref, l_ref, acc_ref)

    @pl.when(j == nj - 1)
    def _():
        new = new_ref[0]
        s_own = _dot_nt(qd, new[:, 2 * LANE:3 * LANE].astype(BF16)) + tso_ref[...] + selown_ref[:, 0:1]
        _flash_step(s_own, new[:, 3 * LANE:].astype(BF16), m_ref, l_ref, acc_ref)
        oslc = acc_ref[...] / jnp.maximum(l_ref[...], TINY)
        _flash_init(m_ref, l_ref, acc_ref)
        win = win_ref[0]
        _flash_step(_dot_nt(qd, win[:, :LANE].astype(BF16)) + tw_ref[...], win[:, LANE:].astype(BF16),
                    m_ref, l_ref, acc_ref)
        neww = neww_ref[0]
        _flash_step(_dot_nt(qd, neww[:, :LANE].astype(BF16)) + two_ref[...], neww[:, LANE:].astype(BF16),
                    m_ref, l_ref, acc_ref)
        owin = acc_ref[...] / jnp.maximum(l_ref[...], TINY)
        gts = g_ref[0]
        o_ref[0] = gts[:, 0:1] * ocmp_ref[...] + gts[:, 1:2] * oslc + gts[:, 2:3] * owin


def _nsa_decode(cache_n, pt, qd, kc, vc, ov, kps, new_pad, win, neww, gts, ts, tso, tw, two):
    b, n_pg = pt.shape
    page = cache_n.shape[1]
    pps = min(PAGES_PER_STEP, n_pg)
    rows = qd.shape[1]
    nc = kc.shape[1]
    full = lambda a: pl.BlockSpec(a.shape, lambda bb, j, p: (0,) * a.ndim)
    per_b = lambda a: pl.BlockSpec((1,) + a.shape[1:], lambda bb, j, p: (bb,) + (0,) * (a.ndim - 1))
    page_spec = lambda u: pl.BlockSpec((1, page, 2 * LANE), lambda bb, j, p: (p[bb, j * pps + u], 0, 1))
    grid_spec = pltpu.PrefetchScalarGridSpec(
        num_scalar_prefetch=1,
        grid=(b, n_pg // pps),
        in_specs=[per_b(qd), per_b(kc), per_b(vc), full(ov), full(kps), per_b(new_pad), per_b(win), per_b(neww),
                  per_b(gts), full(ts), full(tso), full(tw), full(two)] + [page_spec(u) for u in range(pps)],
        out_specs=pl.BlockSpec((1, rows, LANE), lambda bb, j, p: (bb, 0, 0)),
        scratch_shapes=[pltpu.VMEM((rows, 2 * LANE), BF16),
                        pltpu.VMEM((rows, 1), F32), pltpu.VMEM((rows, 1), F32), pltpu.VMEM((rows, LANE), F32),
                        pltpu.VMEM((rows, LANE), F32), pltpu.VMEM((rows, LANE), F32)])
    return pl.pallas_call(
        functools.partial(_nsa_dec_kernel, pps=pps, past=n_pg * page),
        grid_spec=grid_spec,
        out_shape=jax.ShapeDtypeStruct((b, rows, LANE), F32),
        compiler_params=_cparams(("arbitrary", "arbitrary")),
        name="nsa_decode",
    )(pt, qd, kc, vc, ov, kps, new_pad, win, neww, gts, ts, tso, tw, two, *([cache_n] * pps))


def _out_kernel(x_ref, om_ref, zm_ref, on_ref, zn_ref, gate_ref, w_ref, fg_ref, y_ref):
    zm = zm_ref[...].astype(F32)
    zn = zn_ref[...].astype(F32)
    mm = (om_ref[...].astype(F32) * (zm * _sigmoid(zm))).astype(BF16)
    mn = (on_ref[...].astype(F32) * (zn * _sigmoid(zn))).astype(BF16)
    mixed = _dot(mm, w_ref[:W_MOBA, :]) + _dot(mn, w_ref[W_MOBA:, :])
    xn = x_ref[...] + gate_ref[0] * mixed
    inv = lax.rsqrt(jnp.mean(xn * xn, axis=-1, keepdims=True) + RMS_EPS)
    y_ref[...] = (xn * inv) * fg_ref[...]


def _out_proj(x2d, om, zm, on, zn, gate, w_out, fgain, tm, tiles_per_mod):
    r, d = x2d.shape
    mrows = gate.shape[1]
    row = lambda width: pl.BlockSpec((tm, width), lambda i: (i, 0))
    return pl.pallas_call(
        _out_kernel,
        grid=(r // tm,),
        in_specs=[row(d), row(W_MOBA), row(W_MOBA), row(W_NSA), row(W_NSA),
                  pl.BlockSpec((1, mrows, d), lambda i: (i // tiles_per_mod, 0, 0)),
                  pl.BlockSpec((d, d), lambda i: (0, 0)),
                  pl.BlockSpec((1, d), lambda i: (0, 0))],
        out_specs=row(d),
        out_shape=jax.ShapeDtypeStruct((r, d), F32),
        compiler_params=_cparams(("arbitrary",)),
        name="out_proj",
    )(x2d, om, zm, on, zn, gate, w_out, fgain)


def _t5_bucket(rel):
    n = jnp.maximum(rel, 0)
    exact = N_BUCKETS // 2
    nf = jnp.maximum(n, 1).astype(F32)
    large = exact + (jnp.log(nf / exact) / math.log(MAX_DISTANCE / exact) * (N_BUCKETS - exact)).astype(jnp.int32)
    return jnp.where(n < exact, n, jnp.minimum(large, N_BUCKETS - 1))


def _bias_of_rel(rel_bias, rel, heads, shift_far):
    tab = rel_bias[:, heads]
    val = jnp.moveaxis(tab[_t5_bucket(rel)], -1, 0)
    if shift_far:
        val = val - tab[N_BUCKETS - 1].reshape((-1,) + (1,) * rel.ndim)
    return jnp.where(rel >= 0, val, NEG)


def _layout_w_in(w_in):
    d = w_in.shape[0]
    sc = HEAD_DIM ** -0.5
    o = 4 * W_MOBA
    q_m, rest_m = w_in[:, :W_MOBA] * sc, w_in[:, W_MOBA:o]
    q_n = (w_in[:, o:o + W_NSA] * sc).reshape(d, H_NSA, HEAD_DIM)
    o += W_NSA
    kv_n = w_in[:, o:o + 6 * W_NSA_KV]
    o += 6 * W_NSA_KV
    g_n = w_in[:, o:o + 3 * H_NSA]
    z_n = w_in[:, o + 3 * H_NSA:]
    zq = jnp.zeros((d, HEAD_DIM), w_in.dtype)
    qn_exp = jnp.concatenate(
        [jnp.concatenate([q_n[:, h], zq] if h // NSA_GROUP == 0 else [zq, q_n[:, h]], axis=1) for h in range(H_NSA)],
        axis=1)
    gcols = []
    for k in range(H_NSA_KV):
        cols = [g_n[:, (NSA_GROUP * k + g) * 3 + c] for c in range(3) for g in range(NSA_GROUP)]
        gcols.append(jnp.pad(jnp.stack(cols, axis=1), ((0, 0), (0, LANE - len(cols)))))
    w = jnp.concatenate([q_m, rest_m[:, :2 * W_MOBA], rest_m[:, 2 * W_MOBA:], qn_exp, kv_n, z_n] + gcols, axis=1)
    assert w.shape[1] == C_END
    return w.astype(BF16)


def _layout_cmp(pe, w1, w2):
    w = w1.reshape(2, CMP_STRIDE, HEAD_DIM, CMP_HIDDEN)
    z = jnp.zeros_like(w)
    full = jnp.stack([jnp.concatenate([w, z], axis=-1), jnp.concatenate([z, w], axis=-1)], axis=2)
    full = full.reshape(2, CMP_STRIDE * H_NSA_KV * HEAD_DIM, H_NSA_KV * CMP_HIDDEN).astype(BF16)
    zz = jnp.zeros_like(w2)
    w2bd = jnp.concatenate([jnp.concatenate([w2, zz], axis=1), jnp.concatenate([zz, w2], axis=1)], axis=0).astype(BF16)
    pe2 = jnp.broadcast_to(pe.reshape(2, CMP_STRIDE, 1, HEAD_DIM), (2, CMP_STRIDE, H_NSA_KV, HEAD_DIM))
    return pe2.reshape(2, 1, CMP_STRIDE * LANE), full[0], full[1], w2bd


def _block_onehot(t, block):
    return (jnp.arange(t)[:, None] // block == jnp.arange(LANE)[None, :]).astype(BF16)


def _overlap(nc, n_cmp, lanes):
    i = jnp.arange(nc)[:, None]
    s = jnp.arange(lanes)[None, :]
    start = i * CMP_STRIDE
    hit = (start < s * SLC_BLOCK + SLC_BLOCK) & (start + CMP_LEN - 1 >= s * SLC_BLOCK) & (i < n_cmp)
    return hit.astype(BF16)


def kernel(x_prompt, x_sample, c_prompt, c_sample, cache_moba_kv, cache_nsa_kv, state_nsa_win, page_table, w_ada, b_ada, norm_gain, w_in, cmp_pe, cmp_k_w1, cmp_k_w2, cmp_v_w1, cmp_v_w2, w_out, rel_bias, final_gain):
    depth = w_in.shape[0]
    assert depth == 1
    B, T, D = x_prompt.shape
    BS, S, _ = x_sample.shape
    n_pg = page_table.shape[1]
    page = cache_moba_kv.shape[2]
    P = n_pg * page
    WB = state_nsa_win.shape[2]
    assert D == (H_MOBA + H_NSA) * HEAD_DIM and T % KEY_TILE == 0 and P % MOBA_BLOCK == 0 and S <= NEW_PAD
    assert T // SLC_BLOCK <= LANE and P // SLC_BLOCK <= LANE and T >= WINDOW and WB == WINDOW
    assert (BS * S) % SUBLANE == 0 and S & (S - 1) == 0

    w = _layout_w_in(w_in[0])
    w_out_b = w_out[0].astype(BF16)
    gain = norm_gain[0].reshape(1, D)
    fgain = final_gain.reshape(1, D)
    kcmp_w = _layout_cmp(cmp_pe[0, 0], cmp_k_w1[0], cmp_k_w2[0])
    vcmp_w = _layout_cmp(cmp_pe[0, 1], cmp_v_w1[0], cmp_v_w2[0])
    pe2 = jnp.stack([kcmp_w[0], vcmp_w[0]], axis=0)
    cmp_consts = (pe2,) + kcmp_w[1:] + vcmp_w[1:]
    heads_m = jnp.arange(H_MOBA)
    heads_n = H_MOBA + jnp.arange(H_NSA)

    m_all = B + BS
    m_pad = -(-m_all // SUBLANE) * SUBLANE
    c_all = jnp.pad(jnp.concatenate([c_prompt, c_sample], axis=0), ((0, m_pad - m_all), (0, 0)))
    mod = _ada(c_all, w_ada[0], b_ada[0])
    shift, scale, gate = mod[:, :D], mod[:, D:2 * D], mod[:, 2 * D:]

    a = jnp.arange(KEY_TILE)[:, None]
    jk = jnp.arange(2 * KEY_TILE)[None, :]
    rel_s = a + KEY_TILE - jk
    tab_m = _bias_of_rel(rel_bias, rel_s, heads_m, True).reshape(H_MOBA // 2, 2 * KEY_TILE, 2 * KEY_TILE)
    tab_s = _bias_of_rel(rel_bias, rel_s, heads_n, True).reshape(H_NSA_KV, NSA_GROUP * KEY_TILE, 2 * KEY_TILE)
    jw = jnp.arange(WINDOW + KEY_TILE)[None, :]
    rel_w = a + WINDOW - jw
    tab_w = _bias_of_rel(rel_bias, jnp.where(rel_w < WINDOW, rel_w, -1), heads_n, False)
    tab_w = tab_w.reshape(H_NSA_KV, NSA_GROUP * KEY_TILE, WINDOW + KEY_TILE)

    tpm = T // PROJ_TM
    (qm, mkv32, mkv16, zm, qn, nkv32, nkv16, wkv32, wkv16, zn, gts, kmean) = _proj(
        x_prompt.reshape(B * T, D), scale[:B].reshape(B, 1, D), shift[:B].reshape(B, 1, D), gain, w,
        PROJ_TM, tpm, True)
    nb = T // MOBA_BLOCK
    kmean_pad = jnp.pad(kmean.reshape(B, nb, W_MOBA), ((0, 0), (0, LANE - nb), (0, 0)))
    o_m = _moba_prompt(qm.reshape(B, T, W_MOBA), mkv16.reshape(B, T, 2 * W_MOBA), kmean_pad,
                       _block_onehot(T, MOBA_BLOCK), tab_m)
    pt_prompt = jnp.arange(B * (T // page), dtype=jnp.int32).reshape(B, T // page)
    kc_p, vc_p = _compress(nkv32.reshape(B * T // page, page, 4 * W_NSA_KV), pt_prompt, *cmp_consts)
    nc_p = T // CMP_STRIDE
    o_n = _nsa_prompt(qn.reshape(B, T, H_NSA * LANE), kc_p, vc_p, _overlap(nc_p, nc_p - 1, LANE),
                      nkv16.reshape(B, T, 2 * LANE), wkv16.reshape(B, T, 2 * LANE), _block_onehot(T, SLC_BLOCK),
                      tab_s, tab_w, gts.reshape(B, T, 2 * LANE))
    y_prompt = _out_proj(x_prompt.reshape(B * T, D), o_m.reshape(B * T, W_MOBA), zm, o_n.reshape(B * T, W_NSA), zn,
                         gate[:B].reshape(B, 1, D), w_out_b, fgain, PROJ_TM, tpm).reshape(B, T, D)
    moba_kv_prompt = mkv32.reshape(1, B, T, 2, H_MOBA, HEAD_DIM)
    nsa_kv_prompt = nkv32.reshape(1, B, T, 4, H_NSA_KV, HEAD_DIM)
    win_prompt = wkv32.reshape(B, T, 2 * LANE)[:, T - WINDOW:].reshape(1, B, WINDOW, 2, H_NSA_KV, HEAD_DIM)

    RS = BS * S
    rep = lambda m: jnp.repeat(m[B:B + BS], S, axis=0).reshape(1, RS, D)
    (qm_s, mkv32_s, _, zm_s, qn_s, nkv32_s, _, wkv32_s, _, zn_s, gts_s) = _proj(
        x_sample.reshape(RS, D), rep(scale), rep(shift), gain, w, RS, 1, False)
    pad_new = lambda m: jnp.pad(m.reshape(BS, S, -1), ((0, 0), (0, NEW_PAD - S), (0, 0)))
    s_idx = jnp.arange(S)

    q_rep = jnp.repeat(qm_s.reshape(BS, S, 1, H_MOBA, HEAD_DIM), H_MOBA, axis=2)
    eye = (jnp.arange(H_MOBA)[:, None] == jnp.arange(H_MOBA)[None, :])[None, None, :, :, None]
    qbd = jnp.where(eye, q_rep, 0).reshape(BS, S * H_MOBA, W_MOBA)
    rel_d = jnp.concatenate([MOBA_BLOCK + s_idx[:, None] - jnp.arange(MOBA_BLOCK)[None, :],
                             s_idx[:, None] - jnp.arange(NEW_PAD)[None, :]], axis=1)
    td = jnp.moveaxis(_bias_of_rel(rel_bias, rel_d, heads_m, True), 0, 1).reshape(S * H_MOBA, -1)
    td = jnp.pad(td, ((0, 0), (0, 3 * LANE - td.shape[1])))
    o_m_s = _moba_decode(cache_moba_kv[0].reshape(-1, page, 2 * W_MOBA), page_table, qbd, pad_new(mkv32_s), td)

    cache_n = cache_nsa_kv[0].reshape(-1, page, 4 * W_NSA_KV)
    kc_s, vc_s = _compress(cache_n, page_table, *cmp_consts)
    nc_s = P // CMP_STRIDE
    n_cmp_s = (P + S - CMP_LEN) // CMP_STRIDE + 1
    order = lambda m: jnp.transpose(m, (0, 3, 2, 1) + tuple(range(4, m.ndim)))
    rows_n = NSA_GROUP * H_NSA_KV * S
    qd = order(qn_s.reshape(BS, S, H_NSA_KV, NSA_GROUP, LANE)).reshape(BS, rows_n, LANE)
    g3 = gts_s.reshape(BS, S, H_NSA_KV, LANE)[..., :3 * NSA_GROUP].reshape(BS, S, H_NSA_KV, 3, NSA_GROUP)
    gts_d = order(jnp.swapaxes(g3, 3, 4)).reshape(BS, rows_n, 3)
    heads_d = jnp.transpose(heads_n.reshape(H_NSA_KV, NSA_GROUP), (1, 0)).reshape(-1)

    def dec_table(rel, shift_far):
        t = _bias_of_rel(rel_bias, rel, heads_d, shift_far)
        return t.reshape(rows_n, rel.shape[1])

    ts = dec_table(page + s_idx[:, None] - jnp.arange(page)[None, :], True)
    rel_own = s_idx[:, None] - jnp.arange(NEW_PAD)[None, :]
    tso = dec_table(rel_own, True)
    rel_win = WB + s_idx[:, None] - jnp.arange(WB)[None, :]
    tw = dec_table(jnp.where(rel_win < WINDOW, rel_win, -1), False)
    two = dec_table(rel_own, False)
    win_state = state_nsa_win[0].reshape(BS, WB, 2 * LANE)
    o_n_raw = _nsa_decode(cache_n, page_table, qd, kc_s, vc_s, _overlap(nc_s, n_cmp_s, 2 * LANE),
                          _block_onehot(P, SLC_BLOCK), pad_new(nkv32_s), win_state, pad_new(wkv32_s), gts_d,
                          ts, tso, tw, two)
    o5 = o_n_raw.reshape(BS, NSA_GROUP, H_NSA_KV, S, H_NSA_KV, HEAD_DIM)
    o_n_s = jnp.stack([o5[:, :, k, :, k] for k in range(H_NSA_KV)], axis=1)
    o_n_s = jnp.transpose(o_n_s, (0, 3, 1, 2, 4)).reshape(RS, W_NSA)
    y_sample = _out_proj(x_sample.reshape(RS, D), o_m_s.reshape(RS, W_MOBA).astype(BF16), zm_s,
                         o_n_s.astype(BF16), zn_s, rep(gate), w_out_b, fgain, RS, 1).reshape(BS, S, D)
    moba_kv_sample = mkv32_s.reshape(1, BS, S, 2, H_MOBA, HEAD_DIM)
    nsa_kv_sample = nkv32_s.reshape(1, BS, S, 4, H_NSA_KV, HEAD_DIM)
    win_sample = jnp.concatenate([win_state[:, S:], wkv32_s.reshape(BS, S, 2 * LANE)], axis=1)
    win_sample = win_sample.reshape(1, BS, WB, 2, H_NSA_KV, HEAD_DIM)
    return (y_prompt, y_sample, moba_kv_prompt, moba_kv_sample, nsa_kv_prompt, nsa_kv_sample, win_prompt, win_sample)
```

```python
import functools
import math

import jax
import jax.numpy as jnp
from jax import lax
from jax.experimental import pallas as pl
from jax.experimental.pallas import tpu as pltpu

F32 = jnp.float32
BF16 = jnp.bfloat16

HEAD_DIM = 64
H_MOBA = 8
H_NSA = 8
H_NSA_KV = 2
NSA_GROUP = 4
W_MOBA = H_MOBA * HEAD_DIM
W_NSA = H_NSA * HEAD_DIM
W_NSA_KV = H_NSA_KV * HEAD_DIM
MOBA_BLOCK = 256
MOBA_TOPK = 3
CMP_LEN = 32
CMP_STRIDE = 16
CMP_HIDDEN = 2 * HEAD_DIM
SLC_BLOCK = 64
SLC_TOPN = 16
WINDOW = 512
N_BUCKETS = 32
MAX_DISTANCE = 128
RMS_EPS = 1e-6
NEG = -1e30
FORCE = 1e9
TINY = 1e-30

LANE = 128
SUBLANE = 8
KEY_TILE = 256
NSA_TQ = 128
PROJ_TM = 256
PAGES_PER_STEP = 8
NEW_PAD = 8
VMEM_LIMIT = 56 * 1024 * 1024

C_QM, C_MKV, C_ZM, C_QN, C_NKV, C_WKV, C_ZN, C_G, C_END = 0, 512, 1536, 2048, 3072, 3584, 3840, 4352, 4608

_NT = (((1,), (1,)), ((), ()))


def _dot(a, b):
    return jnp.dot(a, b, preferred_element_type=F32)


def _dot_nt(a, b):
    return lax.dot_general(a, b, _NT, preferred_element_type=F32)


def _split_bf16(a):
    hi = a.astype(BF16)
    lo = (a - hi.astype(F32)).astype(BF16)
    return hi, lo


def _sigmoid(x):
    return 1.0 / (1.0 + jnp.exp(-x))


def _cparams(sem):
    return pltpu.CompilerParams(dimension_semantics=sem, vmem_limit_bytes=VMEM_LIMIT)


def _ada_kernel(c_ref, w_ref, b_ref, o_ref):
    ch, cl = _split_bf16(c_ref[...])
    wh, wl = _split_bf16(w_ref[...])
    o_ref[...] = _dot(ch, wh) + _dot(ch, wl) + _dot(cl, wh) + b_ref[...]


def _ada(c_all, w_ada, b_ada):
    m, d = c_all.shape
    n = w_ada.shape[1]
    tn = 512
    return pl.pallas_call(
        _ada_kernel,
        grid=(n // tn,),
        in_specs=[pl.BlockSpec((m, d), lambda j: (0, 0)),
                  pl.BlockSpec((d, tn), lambda j: (0, j)),
                  pl.BlockSpec((1, tn), lambda j: (0, j))],
        out_specs=pl.BlockSpec((m, tn), lambda j: (0, j)),
        out_shape=jax.ShapeDtypeStruct((m, n), F32),
        compiler_params=_cparams(("arbitrary",)),
        name="ada",
    )(c_all, w_ada, b_ada.reshape(1, n))


def _proj_kernel(x_ref, sc_ref, sh_ref, gain_ref, w_ref,
                 qm_ref, mkv32_ref, mkv16_ref, zm_ref, qn_ref, nkv32_ref, nkv16_ref, wkv32_ref, wkv16_ref,
                 zn_ref, g_ref, *km_refs):
    x = x_ref[...]
    inv = lax.rsqrt(jnp.mean(x * x, axis=-1, keepdims=True) + RMS_EPS)
    h = (x * inv) * gain_ref[...] * (1.0 + sc_ref[0]) + sh_ref[0]
    hb = h.astype(BF16)
    col = lambda a, b: _dot(hb, w_ref[:, a:b])
    qm_ref[...] = col(C_QM, C_MKV).astype(BF16)
    mkv = col(C_MKV, C_ZM)
    mkv32_ref[...] = mkv
    mkv16_ref[...] = mkv.astype(BF16)
    zm_ref[...] = col(C_ZM, C_QN).astype(BF16)
    qn_ref[...] = col(C_QN, C_NKV).astype(BF16)
    nkv = col(C_NKV, C_WKV)
    nkv32_ref[...] = nkv
    nkv16_ref[...] = nkv[:, 2 * W_NSA_KV:].astype(BF16)
    wkv = col(C_WKV, C_ZN)
    wkv32_ref[...] = wkv
    wkv16_ref[...] = wkv.astype(BF16)
    zn_ref[...] = col(C_ZN, C_G).astype(BF16)
    g_ref[...] = _sigmoid(col(C_G, C_END))
    if km_refs:
        km_refs[0][0] = jnp.mean(mkv[:, :W_MOBA], axis=0, keepdims=True)


def _proj(x2d, sc, sh, gain, w, tm, tiles_per_mod, with_kmean):
    r, d = x2d.shape
    nt = r // tm
    mrows = sc.shape[1]
    row = lambda width: pl.BlockSpec((tm, width), lambda i: (i, 0))
    widths = [(512, BF16), (1024, F32), (1024, BF16), (512, BF16), (1024, BF16), (512, F32), (256, BF16),
              (256, F32), (256, BF16), (512, BF16), (256, F32)]
    out_specs = [row(wd) for wd, _ in widths]
    out_shape = [jax.ShapeDtypeStruct((r, wd), dt) for wd, dt in widths]
    if with_kmean:
        out_specs.append(pl.BlockSpec((1, 1, W_MOBA), lambda i: (i, 0, 0)))
        out_shape.append(jax.ShapeDtypeStruct((nt, 1, W_MOBA), F32))
    return pl.pallas_call(
        _proj_kernel,
        grid=(nt,),
        in_specs=[row(d),
                  pl.BlockSpec((1, mrows, d), lambda i: (i // tiles_per_mod, 0, 0)),
                  pl.BlockSpec((1, mrows, d), lambda i: (i // tiles_per_mod, 0, 0)),
                  pl.BlockSpec((1, d), lambda i: (0, 0)),
                  pl.BlockSpec((d, C_END), lambda i: (0, 0))],
        out_specs=out_specs,
        out_shape=out_shape,
        compiler_params=_cparams(("arbitrary",)),
        name="proj",
    )(x2d, sc, sh, gain, w)


def _flash_step(s, v, m_ref, l_ref, acc_ref):
    m_old = m_ref[...]
    m_new = jnp.maximum(m_old, jnp.max(s, axis=1, keepdims=True))
    alpha = jnp.exp(m_old - m_new)
    p = jnp.exp(s - m_new)
    l_ref[...] = alpha * l_ref[...] + jnp.sum(p, axis=1, keepdims=True)
    acc_ref[...] = alpha * acc_ref[...] + _dot(p.astype(BF16), v)
    m_ref[...] = m_new


def _flash_init(m_ref, l_ref, acc_ref):
    m_ref[...] = jnp.full(m_ref.shape, NEG, F32)
    l_ref[...] = jnp.zeros(l_ref.shape, F32)
    acc_ref[...] = jnp.zeros(acc_ref.shape, F32)


def _flash_step_t(s, vt, m_ref, l_ref, acc_ref):
    m_old = m_ref[...]
    m_new = jnp.maximum(m_old, jnp.max(s, axis=0, keepdims=True))
    alpha = jnp.exp(m_old - m_new)
    p = jnp.exp(s - m_new)
    l_ref[...] = alpha * l_ref[...] + jnp.sum(p, axis=0, keepdims=True)
    acc_ref[...] = alpha * acc_ref[...] + _dot(vt, p.astype(BF16))
    m_ref[...] = m_new


def _topk_mask(vals, k, axis=1):
    idxf = lax.broadcasted_iota(jnp.int32, vals.shape, axis).astype(F32)

    def body(_, taken):
        cur = jnp.where(taken > 0.0, -jnp.inf, vals)
        mx = jnp.max(cur, axis=axis, keepdims=True)
        first = jnp.min(jnp.where(cur == mx, idxf, 1e9), axis=axis, keepdims=True)
        return jnp.where(idxf == first, 1.0, taken)

    return lax.fori_loop(0, k, body, jnp.zeros(vals.shape, F32))


def _masked_softmax(logits, valid, axis=1):
    lm = jnp.where(valid, logits, NEG)
    p = jnp.exp(lm - jnp.max(lm, axis=axis, keepdims=True)) * valid.astype(F32)
    return p / jnp.maximum(jnp.sum(p, axis=axis, keepdims=True), TINY)


def _moba_kernel(qt_ref, k_ref, vt_ref, km_ref, kp_ref, tab_ref, o_ref, qaug_ref, m_ref, l_ref, acc_ref):
    i = pl.program_id(2)
    tq = qt_ref.shape[2]
    qt = qt_ref[0]
    row = lax.broadcasted_iota(jnp.int32, qt.shape, 0)
    zero = jnp.zeros_like(qt)
    q2 = jnp.concatenate([jnp.where(row < HEAD_DIM, qt, zero), jnp.where(row >= HEAD_DIM, qt, zero)], axis=1)
    sc = _dot(km_ref[0].astype(BF16), q2)
    blk = lax.broadcasted_iota(jnp.int32, sc.shape, 0)
    cand = blk < i
    taken = _topk_mask(jnp.where(cand, sc, NEG), MOBA_TOPK, axis=0)
    sel = ((taken > 0.0) & cand) | (blk == i)
    qaug_ref[...] = jnp.concatenate([q2, jnp.where(sel, 0.0, NEG).astype(BF16)], axis=0)
    _flash_init(m_ref, l_ref, acc_ref)

    def tile_logits(n):
        off = pl.multiple_of(n * KEY_TILE, KEY_TILE)
        kaug = jnp.concatenate([k_ref[0, pl.ds(off, KEY_TILE), :], kp_ref[pl.ds(off, KEY_TILE), :]], axis=1)
        return _dot(kaug, qaug_ref[...]), vt_ref[0, :, pl.ds(off, KEY_TILE)]

    def far_body(n, carry):
        s, vt = tile_logits(n)
        _flash_step_t(s, vt, m_ref, l_ref, acc_ref)
        return carry

    lax.fori_loop(0, jnp.maximum(i - 1, 0), far_body, 0)

    @pl.when(i >= 1)
    def _():
        s, vt = tile_logits(i - 1)
        _flash_step_t(s + tab_ref[0, 0:KEY_TILE, :], vt, m_ref, l_ref, acc_ref)

    s, vt = tile_logits(i)
    _flash_step_t(s + tab_ref[0, KEY_TILE:2 * KEY_TILE, :], vt, m_ref, l_ref, acc_ref)
    o = acc_ref[...] / jnp.maximum(l_ref[...], TINY)
    o_ref[0] = jnp.where(row < HEAD_DIM, o[:, :tq], o[:, tq:]).astype(BF16)


def _moba_prompt(qt, mkv16, vt, kmean_pad, kp, tab):
    b, _, t = qt.shape
    tq = MOBA_BLOCK
    nq = t // tq
    hp = H_MOBA // 2
    cols = 2 * tq
    return pl.pallas_call(
        _moba_kernel,
        grid=(b, hp, nq),
        in_specs=[pl.BlockSpec((1, LANE, tq), lambda bb, h, i: (bb, h, i)),
                  pl.BlockSpec((1, t, LANE), lambda bb, h, i: (bb, 0, h)),
                  pl.BlockSpec((1, LANE, t), lambda bb, h, i: (bb, h, 0)),
                  pl.BlockSpec((1, LANE, LANE), lambda bb, h, i: (bb, 0, h)),
                  pl.BlockSpec((t, LANE), lambda bb, h, i: (0, 0)),
                  pl.BlockSpec((1, 2 * KEY_TILE, cols), lambda bb, h, i: (h, 0, 0))],
        out_specs=pl.BlockSpec((1, LANE, tq), lambda bb, h, i: (bb, h, i)),
        out_shape=jax.ShapeDtypeStruct((b, W_MOBA, t), BF16),
        scratch_shapes=[pltpu.VMEM((2 * LANE, cols), BF16),
                        pltpu.VMEM((1, cols), F32), pltpu.VMEM((1, cols), F32), pltpu.VMEM((LANE, cols), F32)],
        compiler_params=_cparams(("arbitrary", "arbitrary", "arbitrary")),
        name="moba_prompt",
    )(qt, mkv16, vt, kmean_pad, kp, tab)


def _cmp_kernel(pt_ref, *refs, pps):
    del pt_ref
    pages_k, pages_v = refs[:pps], refs[pps:2 * pps]
    pe_ref, kwa_ref, kwb_ref, kw2_ref, vwa_ref, vwb_ref, vw2_ref, kc_ref, vc_ref, uk_ref, uv_ref = refs[2 * pps:]
    j = pl.program_id(1)
    groups = pages_k[0].shape[1] // CMP_STRIDE
    for u in range(pps):
        row0 = pl.multiple_of((j * pps + u) * groups, groups)
        for l in range(CMP_STRIDE):
            rows_l = pl.ds(l, groups, stride=CMP_STRIDE)
            uk_ref[pl.ds(row0, groups), l * LANE:(l + 1) * LANE] = pages_k[u][0, rows_l, :]
            uv_ref[pl.ds(row0, groups), l * LANE:(l + 1) * LANE] = pages_v[u][0, rows_l, :]

    @pl.when(j == pl.num_programs(1) - 1)
    def _():
        nc = uk_ref.shape[0]
        last = lax.broadcasted_iota(jnp.int32, (nc, LANE), 0) == nc - 1

        def compress(u_ref, pe, wa_ref, wb_ref, w2_ref, out_ref):
            u = u_ref[...]
            pa = _dot((u + pe[0]).astype(BF16), wa_ref[...])
            pb = _dot((u + pe[1]).astype(BF16), wb_ref[...])
            pre = pa + pltpu.roll(pb, nc - 1, 0)
            hid = pre * _sigmoid(pre)
            out = _dot(hid.astype(BF16), w2_ref[...])
            out_ref[0] = jnp.where(last, 0.0, out).astype(BF16)

        compress(uk_ref, pe_ref[0], kwa_ref, kwb_ref, kw2_ref, kc_ref)
        compress(uv_ref, pe_ref[1], vwa_ref, vwb_ref, vw2_ref, vc_ref)


def _compress(pages_arr, pt, pe, kwa, kwb, kw2, vwa, vwb, vw2):
    b, n_pg = pt.shape
    page = pages_arr.shape[1]
    pps = min(PAGES_PER_STEP, n_pg)
    nc = n_pg * page // CMP_STRIDE
    page_spec = lambda u, col: pl.BlockSpec((1, page, LANE), lambda bb, j, p: (p[bb, j * pps + u], 0, col))
    full = lambda a: pl.BlockSpec(a.shape, lambda bb, j, p: (0,) * a.ndim)
    consts = (pe, kwa, kwb, kw2, vwa, vwb, vw2)
    grid_spec = pltpu.PrefetchScalarGridSpec(
        num_scalar_prefetch=1,
        grid=(b, n_pg // pps),
        in_specs=[page_spec(u, col) for col in range(2) for u in range(pps)] + [full(a) for a in consts],
        out_specs=[pl.BlockSpec((1, nc, LANE), lambda bb, j, p: (bb, 0, 0))] * 2,
        scratch_shapes=[pltpu.VMEM((nc, CMP_STRIDE * LANE), F32)] * 2)
    return pl.pallas_call(
        functools.partial(_cmp_kernel, pps=pps),
        grid_spec=grid_spec,
        out_shape=[jax.ShapeDtypeStruct((b, nc, LANE), BF16)] * 2,
        compiler_params=_cparams(("arbitrary", "arbitrary")),
        name="nsa_compress",
    )(pt, *([pages_arr] * (2 * pps)), *consts)


def _nsa_kernel(qt_ref, kc_ref, vct_ref, ovt_ref, ks_ref, vst_ref, kw_ref, vwt_ref, kp_ref, ts_ref, tw_ref, gt_ref,
                o_ref, qaug_ref, m_ref, l_ref, acc_ref, ocmp_ref, oslc_ref):
    k = pl.program_id(1)
    i = pl.program_id(2)
    tq = qt_ref.shape[2]
    q4 = jnp.concatenate([qt_ref[0, g * LANE:(g + 1) * LANE, :] for g in range(NSA_GROUP)], axis=1)
    qpos1 = i * tq + lax.broadcasted_iota(jnp.int32, (1, tq), 1)
    qpos = jnp.concatenate([qpos1] * NSA_GROUP, axis=1)

    lc = _dot(kc_ref[0], q4)
    tok = lax.broadcasted_iota(jnp.int32, lc.shape, 0)
    pc = _masked_softmax(lc, tok * CMP_STRIDE + (CMP_LEN - 1) <= qpos, axis=0)
    ocmp_ref[...] = _dot(vct_ref[0], pc.astype(BF16))

    pcs = pc[:, 0:tq] + pc[:, tq:2 * tq] + pc[:, 2 * tq:3 * tq] + pc[:, 3 * tq:4 * tq]
    ph, plo = _split_bf16(pcs)
    imp = _dot(ovt_ref[...], ph) + _dot(ovt_ref[...], plo)
    jb = lax.broadcasted_iota(jnp.int32, imp.shape, 0)
    cur = qpos1 >> int(math.log2(SLC_BLOCK))
    avail = jb <= cur
    forced = (jb == 0) | (jb == cur) | (jb == cur - 1)
    imp = jnp.where(avail, jnp.where(forced, FORCE, imp), NEG)
    sel = (_topk_mask(imp, SLC_TOPN, axis=0) > 0.0) & avail
    selb = jnp.where(sel, 0.0, NEG).astype(BF16)
    qaug_ref[...] = jnp.concatenate([q4, jnp.concatenate([selb] * NSA_GROUP, axis=1)], axis=0)

    cd = (i * tq) // KEY_TILE
    a0 = pl.multiple_of((i * tq) % KEY_TILE, tq)

    def table(t_ref, r0):
        return jnp.concatenate([t_ref[0, r0:r0 + KEY_TILE, pl.ds(g * KEY_TILE + a0, tq)] for g in range(NSA_GROUP)],
                               axis=1)

    _flash_init(m_ref, l_ref, acc_ref)

    def slc_logits(n):
        off = pl.multiple_of(n * KEY_TILE, KEY_TILE)
        kaug = jnp.concatenate([ks_ref[0, pl.ds(off, KEY_TILE), :], kp_ref[pl.ds(off, KEY_TILE), :]], axis=1)
        return _dot(kaug, qaug_ref[...]), vst_ref[0, :, pl.ds(off, KEY_TILE)]

    def far_body(n, carry):
        s, vt = slc_logits(n)
        _flash_step_t(s, vt, m_ref, l_ref, acc_ref)
        return carry

    lax.fori_loop(0, jnp.maximum(cd - 1, 0), far_body, 0)

    @pl.when(cd >= 1)
    def _():
        s, vt = slc_logits(cd - 1)
        _flash_step_t(s + table(ts_ref, 0), vt, m_ref, l_ref, acc_ref)

    s, vt = slc_logits(cd)
    _flash_step_t(s + table(ts_ref, KEY_TILE), vt, m_ref, l_ref, acc_ref)
    oslc_ref[...] = acc_ref[...] / jnp.maximum(l_ref[...], TINY)

    _flash_init(m_ref, l_ref, acc_ref)
    n_win = WINDOW // KEY_TILE
    for w in range(n_win + 1):
        n = cd - n_win + w

        @pl.when(n >= 0)
        def _():
            off = pl.multiple_of(n * KEY_TILE, KEY_TILE)
            s = _dot(kw_ref[0, pl.ds(off, KEY_TILE), :], q4) + table(tw_ref, w * KEY_TILE)
            _flash_step_t(s, vwt_ref[0, :, pl.ds(off, KEY_TILE)], m_ref, l_ref, acc_ref)

    owin = acc_ref[...] / jnp.maximum(l_ref[...], TINY)

    ocmp = ocmp_ref[...]
    oslc = oslc_ref[...]
    gts = gt_ref[0]
    for g in range(NSA_GROUP):
        c = slice(g * tq, (g + 1) * tq)
        og = (gts[g:g + 1, :] * ocmp[:, c] + gts[NSA_GROUP + g:NSA_GROUP + g + 1, :] * oslc[:, c]
              + gts[2 * NSA_GROUP + g:2 * NSA_GROUP + g + 1, :] * owin[:, c])
        o_ref[0, g * HEAD_DIM:(g + 1) * HEAD_DIM, :] = jnp.where(k == 0, og[:HEAD_DIM], og[HEAD_DIM:]).astype(BF16)


def _nsa_prompt(qt, kc, vct, ovt, nkv16, vst, wkv16, vwt, kps, ts, tw, gt):
    b, _, t = qt.shape
    tq = min(NSA_TQ, t)
    nq = t // tq
    cols = NSA_GROUP * tq
    nc = kc.shape[1]
    keys = pl.BlockSpec((1, t, LANE), lambda bb, k, i: (bb, 0, 0))
    vals = pl.BlockSpec((1, LANE, t), lambda bb, k, i: (bb, 0, 0))
    return pl.pallas_call(
        _nsa_kernel,
        grid=(b, H_NSA_KV, nq),
        in_specs=[pl.BlockSpec((1, NSA_GROUP * LANE, tq), lambda bb, k, i: (bb, k, i)),
                  pl.BlockSpec((1, nc, LANE), lambda bb, k, i: (bb, 0, 0)),
                  pl.BlockSpec((1, LANE, nc), lambda bb, k, i: (bb, 0, 0)),
                  pl.BlockSpec(ovt.shape, lambda bb, k, i: (0, 0)),
                  keys, vals, keys, vals,
                  pl.BlockSpec((t, LANE), lambda bb, k, i: (0, 0)),
                  pl.BlockSpec((1,) + ts.shape[1:], lambda bb, k, i: (k, 0, 0)),
                  pl.BlockSpec((1,) + tw.shape[1:], lambda bb, k, i: (k, 0, 0)),
                  pl.BlockSpec((1, LANE, tq), lambda bb, k, i: (bb, k, i))],
        out_specs=pl.BlockSpec((1, NSA_GROUP * HEAD_DIM, tq), lambda bb, k, i: (bb, k, i)),
        out_shape=jax.ShapeDtypeStruct((b, W_NSA, t), BF16),
        scratch_shapes=[pltpu.VMEM((2 * LANE, cols), BF16),
                        pltpu.VMEM((1, cols), F32), pltpu.VMEM((1, cols), F32), pltpu.VMEM((LANE, cols), F32),
                        pltpu.VMEM((LANE, cols), F32), pltpu.VMEM((LANE, cols), F32)],
        compiler_params=_cparams(("arbitrary", "arbitrary", "arbitrary")),
        name="nsa_prompt",
    )(qt, kc, vct, ovt, nkv16, vst, wkv16, vwt, kps, ts, tw, gt)


def _moba_dec_kernel(pt_ref, qbd_ref, new_ref, td_ref, *refs, pps, nblk):
    del pt_ref
    pages = refs[:pps]
    o_ref, km_ref, mall_ref, lall_ref, oall_ref = refs[pps:]
    j = pl.program_id(1)
    qbd = qbd_ref[0]
    rows = qbd.shape[0]
    lane = lax.broadcasted_iota(jnp.int32, (rows, LANE), 1)
    ppb = MOBA_BLOCK // pages[0].shape[1]

    @pl.when(j == 0)
    def _():
        km_ref[...] = jnp.zeros(km_ref.shape, F32)
        mall_ref[...] = jnp.full(mall_ref.shape, NEG, F32)
        lall_ref[...] = jnp.zeros(lall_ref.shape, F32)

    for u in range(pps // ppb):
        blk = j * (pps // ppb) + u
        kf = jnp.concatenate([pages[u * ppb + w][0, :, :W_MOBA] for w in range(ppb)], axis=0)
        vf = jnp.concatenate([pages[u * ppb + w][0, :, W_MOBA:] for w in range(ppb)], axis=0)
        km_ref[pl.ds(blk, 1), :] = jnp.mean(kf, axis=0, keepdims=True)
        near = (blk == nblk - 1).astype(F32)
        s = _dot_nt(qbd, kf.astype(BF16)) + near * td_ref[:, 0:MOBA_BLOCK]
        mb = jnp.max(s, axis=1, keepdims=True)
        p = jnp.exp(s - mb)
        mall_ref[...] = jnp.where(lane == blk, mb, mall_ref[...])
        lall_ref[...] = jnp.where(lane == blk, jnp.sum(p, axis=1, keepdims=True), lall_ref[...])
        oall_ref[pl.ds(blk, 1)] = _dot(p.astype(BF16), vf.astype(BF16))[None]

    @pl.when(j == pl.num_programs(1) - 1)
    def _():
        kn = new_ref[0, :, :W_MOBA].astype(BF16)
        vn = new_ref[0, :, W_MOBA:].astype(BF16)
        s_own = _dot_nt(qbd, kn) + td_ref[:, MOBA_BLOCK:MOBA_BLOCK + NEW_PAD]
        m_own = jnp.max(s_own, axis=1, keepdims=True)
        p_own = jnp.exp(s_own - m_own)
        l_own = jnp.sum(p_own, axis=1, keepdims=True)
        o_own = _dot(p_own.astype(BF16), vn)
        sc = _dot_nt(qbd, km_ref[...].astype(BF16))
        cand = lane < nblk
        sel = (_topk_mask(jnp.where(cand, sc, NEG), MOBA_TOPK) > 0.0) & cand
        mall = mall_ref[...]
        m_fin = jnp.maximum(jnp.max(jnp.where(sel, mall, NEG), axis=1, keepdims=True), m_own)
        w = jnp.where(sel, jnp.exp(mall - m_fin), 0.0)
        w_own = jnp.exp(m_own - m_fin)
        l = jnp.sum(w * lall_ref[...], axis=1, keepdims=True) + w_own * l_own
        acc = w_own * o_own
        for n in range(nblk):
            acc = acc + w[:, n:n + 1] * oall_ref[n]
        o = acc / jnp.maximum(l, TINY)
        rowh = lax.broadcasted_iota(jnp.int32, o.shape, 0) & (H_MOBA - 1)
        laneh = lax.broadcasted_iota(jnp.int32, o.shape, 1) >> int(math.log2(HEAD_DIM))
        o = jnp.where(rowh == laneh, o, 0.0)
        o_ref[0] = jnp.sum(o.reshape(rows // H_MOBA, H_MOBA, W_MOBA), axis=1)


def _moba_decode(cache_m, pt, qbd, new_pad, td):
    b, n_pg = pt.shape
    page = cache_m.shape[1]
    pps = min(PAGES_PER_STEP, n_pg)
    nblk = n_pg * page // MOBA_BLOCK
    rows = qbd.shape[1]
    s_new = rows // H_MOBA
    page_spec = lambda u: pl.BlockSpec((1, page, 2 * W_MOBA), lambda bb, j, p: (p[bb, j * pps + u], 0, 0))
    grid_spec = pltpu.PrefetchScalarGridSpec(
        num_scalar_prefetch=1,
        grid=(b, n_pg // pps),
        in_specs=[pl.BlockSpec((1, rows, W_MOBA), lambda bb, j, p: (bb, 0, 0)),
                  pl.BlockSpec((1, NEW_PAD, 2 * W_MOBA), lambda bb, j, p: (bb, 0, 0)),
                  pl.BlockSpec(td.shape, lambda bb, j, p: (0, 0))] + [page_spec(u) for u in range(pps)],
        out_specs=pl.BlockSpec((1, s_new, W_MOBA), lambda bb, j, p: (bb, 0, 0)),
        scratch_shapes=[pltpu.VMEM((LANE, W_MOBA), F32), pltpu.VMEM((rows, LANE), F32), pltpu.VMEM((rows, LANE), F32),
                        pltpu.VMEM((nblk, rows, W_MOBA), F32)])
    return pl.pallas_call(
        functools.partial(_moba_dec_kernel, pps=pps, nblk=nblk),
        grid_spec=grid_spec,
        out_shape=jax.ShapeDtypeStruct((b, s_new, W_MOBA), F32),
        compiler_params=_cparams(("arbitrary", "arbitrary")),
        name="moba_decode",
    )(pt, qbd, new_pad, td, *([cache_m] * pps))


def _nsa_dec_kernel(pt_ref, qd_ref, kc_ref, vc_ref, ov_ref, kp_ref, new_ref, win_ref, neww_ref, g_ref,
                    ts_ref, tso_ref, tw_ref, two_ref, *refs, pps, past):
    del pt_ref
    pages = refs[:pps]
    o_ref, qaug_ref, m_ref, l_ref, acc_ref, ocmp_ref, selown_ref = refs[pps:]
    j = pl.program_id(1)
    nj = pl.num_programs(1)
    qd = qd_ref[0]
    rows = qd.shape[0]
    page = pages[0].shape[1]
    grp = rows // NSA_GROUP
    s_new = grp // H_NSA_KV

    @pl.when(j == 0)
    def _():
        lc = _dot_nt(qd, kc_ref[0])
        tok = lax.broadcasted_iota(jnp.int32, lc.shape, 1)
        qpos = past + (lax.broadcasted_iota(jnp.int32, (rows, 1), 0) & (s_new - 1))
        pc = _masked_softmax(lc, tok * CMP_STRIDE + (CMP_LEN - 1) <= qpos)
        ocmp_ref[...] = _dot(pc.astype(BF16), vc_ref[0])
        pcs = pc[0:grp] + pc[grp:2 * grp] + pc[2 * grp:3 * grp] + pc[3 * grp:4 * grp]
        ph, plo = _split_bf16(pcs)
        imp = _dot(ph, ov_ref[...]) + _dot(plo, ov_ref[...])
        jb = lax.broadcasted_iota(jnp.int32, imp.shape, 1)
        cur = qpos[0:grp] >> int(math.log2(SLC_BLOCK))
        avail = jb <= cur
        forced = (jb == 0) | (jb == cur) | (jb == cur - 1)
        imp = jnp.where(avail, jnp.where(forced, FORCE, imp), NEG)
        sel = (_topk_mask(imp, SLC_TOPN) > 0.0) & avail
        selb = jnp.concatenate([jnp.where(sel, 0.0, NEG)] * NSA_GROUP, axis=0)
        qaug_ref[...] = jnp.concatenate([qd, selb[:, :LANE].astype(BF16)], axis=1)
        selown_ref[...] = selb[:, LANE:]
        _flash_init(m_ref, l_ref, acc_ref)

    for u in range(pps):
        pg = j * pps + u
        off = pl.multiple_of(pg * page, page)
        kaug = jnp.concatenate([pages[u][0, :, :LANE].astype(BF16), kp_ref[pl.ds(off, page), :]], axis=1)
        near = (pg == nj * pps - 1).astype(F32)
        s = _dot_nt(qaug_ref[...], kaug) + near * ts_ref[...]
        _flash_step(s, pages[u][0, :, LANE:].astype(BF16), m_ref, l_ref, acc_ref)

    @pl.when(j == nj - 1)
    def _():
        new = new_ref[0]
        s_own = _dot_nt(qd, new[:, 2 * LANE:3 * LANE].astype(BF16)) + tso_ref[...] + selown_ref[:, 0:1]
        _flash_step(s_own, new[:, 3 * LANE:].astype(BF16), m_ref, l_ref, acc_ref)
        oslc = acc_ref[...] / jnp.maximum(l_ref[...], TINY)
        _flash_init(m_ref, l_ref, acc_ref)
        win = win_ref[0]
        _flash_step(_dot_nt(qd, win[:, :LANE].astype(BF16)) + tw_ref[...], win[:, LANE:].astype(BF16),
                    m_ref, l_ref, acc_ref)
        neww = neww_ref[0]
        _flash_step(_dot_nt(qd, neww[:, :LANE].astype(BF16)) + two_ref[...], neww[:, LANE:].astype(BF16),
                    m_ref, l_ref, acc_ref)
        owin = acc_ref[...] / jnp.maximum(l_ref[...], TINY)
        gts = g_ref[0]
        o_ref[0] = gts[:, 0:1] * ocmp_ref[...] + gts[:, 1:2] * oslc + gts[:, 2:3] * owin


def _nsa_decode(cache_n, pt, qd, kc, vc, ov, kps, new_pad, win, neww, gts, ts, tso, tw, two):
    b, n_pg = pt.shape
    page = cache_n.shape[1]
    pps = min(PAGES_PER_STEP, n_pg)
    rows = qd.shape[1]
    nc = kc.shape[1]
    full = lambda a: pl.BlockSpec(a.shape, lambda bb, j, p: (0,) * a.ndim)
    per_b = lambda a: pl.BlockSpec((1,) + a.shape[1:], lambda bb, j, p: (bb,) + (0,) * (a.ndim - 1))
    page_spec = lambda u: pl.BlockSpec((1, page, 2 * LANE), lambda bb, j, p: (p[bb, j * pps + u], 0, 1))
    grid_spec = pltpu.PrefetchScalarGridSpec(
        num_scalar_prefetch=1,
        grid=(b, n_pg // pps),
        in_specs=[per_b(qd), per_b(kc), per_b(vc), full(ov), full(kps), per_b(new_pad), per_b(win), per_b(neww),
                  per_b(gts), full(ts), full(tso), full(tw), full(two)] + [page_spec(u) for u in range(pps)],
        out_specs=pl.BlockSpec((1, rows, LANE), lambda bb, j, p: (bb, 0, 0)),
        scratch_shapes=[pltpu.VMEM((rows, 2 * LANE), BF16),
                        pltpu.VMEM((rows, 1), F32), pltpu.VMEM((rows, 1), F32), pltpu.VMEM((rows, LANE), F32),
                        pltpu.VMEM((rows, LANE), F32), pltpu.VMEM((rows, LANE), F32)])
    return pl.pallas_call(
        functools.partial(_nsa_dec_kernel, pps=pps, past=n_pg * page),
        grid_spec=grid_spec,
        out_shape=jax.ShapeDtypeStruct((b, rows, LANE), F32),
        compiler_params=_cparams(("arbitrary", "arbitrary")),
        name="nsa_decode",
    )(pt, qd, kc, vc, ov, kps, new_pad, win, neww, gts, ts, tso, tw, two, *([cache_n] * pps))


def _out_kernel(x_ref, om_ref, zm_ref, on_ref, zn_ref, gate_ref, w_ref, fg_ref, y_ref):
    zm = zm_ref[...].astype(F32)
    zn = zn_ref[...].astype(F32)
    mm = (om_ref[...].astype(F32) * (zm * _sigmoid(zm))).astype(BF16)
    mn = (on_ref[...].astype(F32) * (zn * _sigmoid(zn))).astype(BF16)
    mixed = _dot(mm, w_ref[:W_MOBA, :]) + _dot(mn, w_ref[W_MOBA:, :])
    xn = x_ref[...] + gate_ref[0] * mixed
    inv = lax.rsqrt(jnp.mean(xn * xn, axis=-1, keepdims=True) + RMS_EPS)
    y_ref[...] = (xn * inv) * fg_ref[...]


def _out_proj(x2d, om, zm, on, zn, gate, w_out, fgain, tm, tiles_per_mod):
    r, d = x2d.shape
    mrows = gate.shape[1]
    row = lambda width: pl.BlockSpec((tm, width), lambda i: (i, 0))
    return pl.pallas_call(
        _out_kernel,
        grid=(r // tm,),
        in_specs=[row(d), row(W_MOBA), row(W_MOBA), row(W_NSA), row(W_NSA),
                  pl.BlockSpec((1, mrows, d), lambda i: (i // tiles_per_mod, 0, 0)),
                  pl.BlockSpec((d, d), lambda i: (0, 0)),
                  pl.BlockSpec((1, d), lambda i: (0, 0))],
        out_specs=row(d),
        out_shape=jax.ShapeDtypeStruct((r, d), F32),
        compiler_params=_cparams(("arbitrary",)),
        name="out_proj",
    )(x2d, om, zm, on, zn, gate, w_out, fgain)


def _t5_bucket(rel):
    n = jnp.maximum(rel, 0)
    exact = N_BUCKETS // 2
    nf = jnp.maximum(n, 1).astype(F32)
    large = exact + (jnp.log(nf / exact) / math.log(MAX_DISTANCE / exact) * (N_BUCKETS - exact)).astype(jnp.int32)
    return jnp.where(n < exact, n, jnp.minimum(large, N_BUCKETS - 1))


def _bias_of_rel(tab, rel, shift_far):
    bucket = _t5_bucket(rel)[None]
    col = lambda kk: tab[kk].reshape((-1,) + (1,) * rel.ndim)
    val = jnp.broadcast_to(col(N_BUCKETS - 1), (tab.shape[1],) + rel.shape)
    for kk in range(N_BUCKETS - 1):
        val = jnp.where(bucket == kk, col(kk), val)
    if shift_far:
        val = val - col(N_BUCKETS - 1)
    return jnp.where(rel[None] >= 0, val, NEG)


def _layout_w_in(w_in):
    d = w_in.shape[0]
    sc = HEAD_DIM ** -0.5
    o = 4 * W_MOBA
    q_m, rest_m = w_in[:, :W_MOBA] * sc, w_in[:, W_MOBA:o]
    q_n = (w_in[:, o:o + W_NSA] * sc).reshape(d, H_NSA, HEAD_DIM)
    o += W_NSA
    kv_n = w_in[:, o:o + 6 * W_NSA_KV]
    o += 6 * W_NSA_KV
    g_n = w_in[:, o:o + 3 * H_NSA]
    z_n = w_in[:, o + 3 * H_NSA:]
    zq = jnp.zeros((d, HEAD_DIM), w_in.dtype)
    qn_exp = jnp.concatenate(
        [jnp.concatenate([q_n[:, h], zq] if h // NSA_GROUP == 0 else [zq, q_n[:, h]], axis=1) for h in range(H_NSA)],
        axis=1)
    gcols = []
    for k in range(H_NSA_KV):
        cols = [g_n[:, (NSA_GROUP * k + g) * 3 + c] for c in range(3) for g in range(NSA_GROUP)]
        gcols.append(jnp.pad(jnp.stack(cols, axis=1), ((0, 0), (0, LANE - len(cols)))))
    w = jnp.concatenate([q_m, rest_m[:, :2 * W_MOBA], rest_m[:, 2 * W_MOBA:], qn_exp, kv_n, z_n] + gcols, axis=1)
    assert w.shape[1] == C_END
    return w.astype(BF16)


def _layout_cmp(pe, w1, w2):
    w = w1.reshape(2, CMP_STRIDE, HEAD_DIM, CMP_HIDDEN)
    z = jnp.zeros_like(w)
    full = jnp.stack([jnp.concatenate([w, z], axis=-1), jnp.concatenate([z, w], axis=-1)], axis=2)
    full = full.reshape(2, CMP_STRIDE * H_NSA_KV * HEAD_DIM, H_NSA_KV * CMP_HIDDEN).astype(BF16)
    zz = jnp.zeros_like(w2)
    w2bd = jnp.concatenate([jnp.concatenate([w2, zz], axis=1), jnp.concatenate([zz, w2], axis=1)], axis=0).astype(BF16)
    pe2 = jnp.broadcast_to(pe.reshape(2, CMP_STRIDE, 1, HEAD_DIM), (2, CMP_STRIDE, H_NSA_KV, HEAD_DIM))
    return pe2.reshape(2, 1, CMP_STRIDE * LANE), full[0], full[1], w2bd


def _block_onehot(t, block):
    return (jnp.arange(t)[:, None] // block == jnp.arange(LANE)[None, :]).astype(BF16)


def _overlap(nc, n_cmp, lanes):
    i = jnp.arange(nc)[:, None]
    s = jnp.arange(lanes)[None, :]
    start = i * CMP_STRIDE
    hit = (start < s * SLC_BLOCK + SLC_BLOCK) & (start + CMP_LEN - 1 >= s * SLC_BLOCK) & (i < n_cmp)
    return hit.astype(BF16)


def kernel(x_prompt, x_sample, c_prompt, c_sample, cache_moba_kv, cache_nsa_kv, state_nsa_win, page_table, w_ada, b_ada, norm_gain, w_in, cmp_pe, cmp_k_w1, cmp_k_w2, cmp_v_w1, cmp_v_w2, w_out, rel_bias, final_gain):
    depth = w_in.shape[0]
    assert depth == 1
    B, T, D = x_prompt.shape
    BS, S, _ = x_sample.shape
    n_pg = page_table.shape[1]
    page = cache_moba_kv.shape[2]
    P = n_pg * page
    WB = state_nsa_win.shape[2]
    assert D == (H_MOBA + H_NSA) * HEAD_DIM and T % KEY_TILE == 0 and P % MOBA_BLOCK == 0 and S <= NEW_PAD
    assert T // SLC_BLOCK <= LANE and P // SLC_BLOCK <= LANE and T >= WINDOW and WB == WINDOW
    assert (BS * S) % SUBLANE == 0 and S & (S - 1) == 0

    w = _layout_w_in(w_in[0])
    w_out_b = w_out[0].astype(BF16)
    gain = norm_gain[0].reshape(1, D)
    fgain = final_gain.reshape(1, D)
    kcmp_w = _layout_cmp(cmp_pe[0, 0], cmp_k_w1[0], cmp_k_w2[0])
    vcmp_w = _layout_cmp(cmp_pe[0, 1], cmp_v_w1[0], cmp_v_w2[0])
    pe2 = jnp.stack([kcmp_w[0], vcmp_w[0]], axis=0)
    cmp_consts = (pe2,) + kcmp_w[1:] + vcmp_w[1:]
    bias_m = rel_bias[:, :H_MOBA]
    bias_n = rel_bias[:, H_MOBA:]

    m_all = B + BS
    m_pad = -(-m_all // SUBLANE) * SUBLANE
    c_all = jnp.pad(jnp.concatenate([c_prompt, c_sample], axis=0), ((0, m_pad - m_all), (0, 0)))
    mod = _ada(c_all, w_ada[0], b_ada[0])
    shift, scale, gate = mod[:, :D], mod[:, D:2 * D], mod[:, 2 * D:]

    a = jnp.arange(KEY_TILE)[None, :]
    jk = jnp.arange(2 * KEY_TILE)[:, None]
    rel_s = a + KEY_TILE - jk
    heads_major = lambda tb, n_grp: jnp.transpose(
        tb.reshape(n_grp, -1, tb.shape[1], KEY_TILE), (0, 2, 1, 3)).reshape(n_grp, tb.shape[1], -1)
    tab_m = heads_major(_bias_of_rel(bias_m, rel_s, True), H_MOBA // 2)
    tab_s = heads_major(_bias_of_rel(bias_n, rel_s, True), H_NSA_KV)
    jw = jnp.arange(WINDOW + KEY_TILE)[:, None]
    rel_w = a + WINDOW - jw
    tab_w = heads_major(_bias_of_rel(bias_n, jnp.where(rel_w < WINDOW, rel_w, -1), False), H_NSA_KV)

    tpm = T // PROJ_TM
    (qm, mkv32, mkv16, zm, qn, nkv32, nkv16, wkv32, wkv16, zn, gts, kmean) = _proj(
        x_prompt.reshape(B * T, D), scale[:B].reshape(B, 1, D), shift[:B].reshape(B, 1, D), gain, w,
        PROJ_TM, tpm, True)
    nb = T // MOBA_BLOCK
    kmean_pad = jnp.pad(kmean.reshape(B, nb, W_MOBA), ((0, 0), (0, LANE - nb), (0, 0)))
    feat_major = lambda m: jnp.swapaxes(m.reshape(B, T, -1), 1, 2)
    mkv16 = mkv16.reshape(B, T, 2 * W_MOBA)
    o_m = _moba_prompt(feat_major(qm), mkv16, feat_major(mkv16[..., W_MOBA:]), kmean_pad,
                       _block_onehot(T, MOBA_BLOCK), tab_m)
    pt_prompt = jnp.arange(B * (T // page), dtype=jnp.int32).reshape(B, T // page)
    kc_p, vc_p = _compress(nkv32.reshape(B * T // page, page, 4 * W_NSA_KV), pt_prompt, *cmp_consts)
    nc_p = T // CMP_STRIDE
    nkv16 = nkv16.reshape(B, T, 2 * LANE)
    wkv16 = wkv16.reshape(B, T, 2 * LANE)
    o_n = _nsa_prompt(feat_major(qn), kc_p, jnp.swapaxes(vc_p, 1, 2), _overlap(nc_p, nc_p - 1, LANE).T,
                      nkv16, feat_major(nkv16[..., LANE:]), wkv16, feat_major(wkv16[..., LANE:]),
                      _block_onehot(T, SLC_BLOCK), tab_s, tab_w, feat_major(gts))
    token_major = lambda m: jnp.swapaxes(m, 1, 2).reshape(B * T, -1)
    y_prompt = _out_proj(x_prompt.reshape(B * T, D), token_major(o_m), zm, token_major(o_n), zn,
                         gate[:B].reshape(B, 1, D), w_out_b, fgain, PROJ_TM, tpm).reshape(B, T, D)
    moba_kv_prompt = mkv32.reshape(1, B, T, 2, H_MOBA, HEAD_DIM)
    nsa_kv_prompt = nkv32.reshape(1, B, T, 4, H_NSA_KV, HEAD_DIM)
    win_prompt = wkv32.reshape(B, T, 2 * LANE)[:, T - WINDOW:].reshape(1, B, WINDOW, 2, H_NSA_KV, HEAD_DIM)

    RS = BS * S
    rep = lambda m: jnp.repeat(m[B:B + BS], S, axis=0).reshape(1, RS, D)
    (qm_s, mkv32_s, _, zm_s, qn_s, nkv32_s, _, wkv32_s, _, zn_s, gts_s) = _proj(
        x_sample.reshape(RS, D), rep(scale), rep(shift), gain, w, RS, 1, False)
    pad_new = lambda m: jnp.pad(m.reshape(BS, S, -1), ((0, 0), (0, NEW_PAD - S), (0, 0)))
    s_idx = jnp.arange(S)

    q_rep = jnp.repeat(qm_s.reshape(BS, S, 1, H_MOBA, HEAD_DIM), H_MOBA, axis=2)
    eye = (jnp.arange(H_MOBA)[:, None] == jnp.arange(H_MOBA)[None, :])[None, None, :, :, None]
    qbd = jnp.where(eye, q_rep, 0).reshape(BS, S * H_MOBA, W_MOBA)
    rel_d = jnp.concatenate([MOBA_BLOCK + s_idx[:, None] - jnp.arange(MOBA_BLOCK)[None, :],
                             s_idx[:, None] - jnp.arange(NEW_PAD)[None, :]], axis=1)
    td = jnp.moveaxis(_bias_of_rel(bias_m, rel_d, True), 0, 1).reshape(S * H_MOBA, -1)
    td = jnp.pad(td, ((0, 0), (0, 3 * LANE - td.shape[1])))
    o_m_s = _moba_decode(cache_moba_kv[0].reshape(-1, page, 2 * W_MOBA), page_table, qbd, pad_new(mkv32_s), td)

    cache_n = cache_nsa_kv[0].reshape(-1, page, 4 * W_NSA_KV)
    kc_s, vc_s = _compress(cache_n, page_table, *cmp_consts)
    nc_s = P // CMP_STRIDE
    n_cmp_s = (P + S - CMP_LEN) // CMP_STRIDE + 1
    order = lambda m: jnp.transpose(m, (0, 3, 2, 1) + tuple(range(4, m.ndim)))
    rows_n = NSA_GROUP * H_NSA_KV * S
    qd = order(qn_s.reshape(BS, S, H_NSA_KV, NSA_GROUP, LANE)).reshape(BS, rows_n, LANE)
    g3 = gts_s.reshape(BS, S, H_NSA_KV, LANE)[..., :3 * NSA_GROUP].reshape(BS, S, H_NSA_KV, 3, NSA_GROUP)
    gts_d = order(jnp.swapaxes(g3, 3, 4)).reshape(BS, rows_n, 3)
    bias_d = jnp.swapaxes(bias_n.reshape(N_BUCKETS, H_NSA_KV, NSA_GROUP), 1, 2).reshape(N_BUCKETS, H_NSA)

    def dec_table(rel, shift_far):
        return _bias_of_rel(bias_d, rel, shift_far).reshape(rows_n, rel.shape[1])

    ts = dec_table(page + s_idx[:, None] - jnp.arange(page)[None, :], True)
    rel_own = s_idx[:, None] - jnp.arange(NEW_PAD)[None, :]
    tso = dec_table(rel_own, True)
    rel_win = WB + s_idx[:, None] - jnp.arange(WB)[None, :]
    tw = dec_table(jnp.where(rel_win < WINDOW, rel_win, -1), False)
    two = dec_table(rel_own, False)
    win_state = state_nsa_win[0].reshape(BS, WB, 2 * LANE)
    o_n_raw = _nsa_decode(cache_n, page_table, qd, kc_s, vc_s, _overlap(nc_s, n_cmp_s, 2 * LANE),
                          _block_onehot(P, SLC_BLOCK), pad_new(nkv32_s), win_state, pad_new(wkv32_s), gts_d,
                          ts, tso, tw, two)
    o5 = o_n_raw.reshape(BS, NSA_GROUP, H_NSA_KV, S, H_NSA_KV, HEAD_DIM)
    o_n_s = jnp.stack([o5[:, :, k, :, k] for k in range(H_NSA_KV)], axis=1)
    o_n_s = jnp.transpose(o_n_s, (0, 3, 1, 2, 4)).reshape(RS, W_NSA)
    y_sample = _out_proj(x_sample.reshape(RS, D), o_m_s.reshape(RS, W_MOBA).astype(BF16), zm_s,
                         o_n_s.astype(BF16), zn_s, rep(gate), w_out_b, fgain, RS, 1).reshape(BS, S, D)
    moba_kv_sample = mkv32_s.reshape(1, BS, S, 2, H_MOBA, HEAD_DIM)
    nsa_kv_sample = nkv32_s.reshape(1, BS, S, 4, H_NSA_KV, HEAD_DIM)
    win_sample = jnp.concatenate([win_state[:, S:], wkv32_s.reshape(BS, S, 2 * LANE)], axis=1)
    win_sample = win_sample.reshape(1, BS, WB, 2, H_NSA_KV, HEAD_DIM)
    return (y_prompt, y_sample, moba_kv_prompt, moba_kv_sample, nsa_kv_prompt, nsa_kv_sample, win_prompt, win_sample)
```

```python
import functools
import math

import jax
import jax.numpy as jnp
from jax import lax
from jax.experimental import pallas as pl
from jax.experimental.pallas import tpu as pltpu

F32 = jnp.float32
BF16 = jnp.bfloat16

HEAD_DIM = 64
H_MOBA = 8
H_NSA = 8
H_NSA_KV = 2
NSA_GROUP = 4
W_MOBA = H_MOBA * HEAD_DIM
W_NSA = H_NSA * HEAD_DIM
W_NSA_KV = H_NSA_KV * HEAD_DIM
MOBA_BLOCK = 256
MOBA_TOPK = 3
CMP_LEN = 32
CMP_STRIDE = 16
CMP_HIDDEN = 2 * HEAD_DIM
SLC_BLOCK = 64
SLC_TOPN = 16
WINDOW = 512
N_BUCKETS = 32
MAX_DISTANCE = 128
RMS_EPS = 1e-6
NEG = -1e30
FORCE = 1e9
TINY = 1e-30

LANE = 128
SUBLANE = 8
KEY_TILE = 256
NSA_TQ = 128
PROJ_TM = 256
PAGES_PER_STEP = 8
NEW_PAD = 8
ONES_ROWS = 16
LOG2E = math.log2(math.e)
VMEM_LIMIT = 56 * 1024 * 1024

C_QM, C_MKV, C_ZM, C_QN, C_NKV, C_WKV, C_ZN, C_G, C_END = 0, 512, 1536, 2048, 3072, 3584, 3840, 4352, 4608

_NT = (((1,), (1,)), ((), ()))


def _dot(a, b):
    return jnp.dot(a, b, preferred_element_type=F32)


def _dot_nt(a, b):
    return lax.dot_general(a, b, _NT, preferred_element_type=F32)


def _split_bf16(a):
    hi = a.astype(BF16)
    lo = (a - hi.astype(F32)).astype(BF16)
    return hi, lo


def _sigmoid(x):
    return 1.0 / (1.0 + jnp.exp(-x))


def _cparams(sem):
    return pltpu.CompilerParams(dimension_semantics=sem, vmem_limit_bytes=VMEM_LIMIT)


def _ada_kernel(c_ref, w_ref, b_ref, o_ref):
    ch, cl = _split_bf16(c_ref[...])
    wh, wl = _split_bf16(w_ref[...])
    o_ref[...] = _dot(ch, wh) + _dot(ch, wl) + _dot(cl, wh) + b_ref[...]


def _ada(c_all, w_ada, b_ada):
    m, d = c_all.shape
    n = w_ada.shape[1]
    tn = 512
    return pl.pallas_call(
        _ada_kernel,
        grid=(n // tn,),
        in_specs=[pl.BlockSpec((m, d), lambda j: (0, 0)),
                  pl.BlockSpec((d, tn), lambda j: (0, j)),
                  pl.BlockSpec((1, tn), lambda j: (0, j))],
        out_specs=pl.BlockSpec((m, tn), lambda j: (0, j)),
        out_shape=jax.ShapeDtypeStruct((m, n), F32),
        compiler_params=_cparams(("arbitrary",)),
        name="ada",
    )(c_all, w_ada, b_ada.reshape(1, n))


def _proj_kernel(x_ref, sc_ref, sh_ref, gain_ref, w_ref,
                 qm_ref, mkv32_ref, mkv16_ref, zm_ref, qn_ref, nkv32_ref, nkv16_ref, wkv32_ref, wkv16_ref,
                 zn_ref, g_ref, *km_refs):
    x = x_ref[...]
    inv = lax.rsqrt(jnp.mean(x * x, axis=-1, keepdims=True) + RMS_EPS)
    h = (x * inv) * gain_ref[...] * (1.0 + sc_ref[0]) + sh_ref[0]
    hb = h.astype(BF16)
    col = lambda a, b: _dot(hb, w_ref[:, a:b])
    qm_ref[...] = col(C_QM, C_MKV).astype(BF16)
    mkv = col(C_MKV, C_ZM)
    mkv32_ref[...] = mkv
    mkv16_ref[...] = mkv.astype(BF16)
    zm_ref[...] = col(C_ZM, C_QN).astype(BF16)
    qn_ref[...] = col(C_QN, C_NKV).astype(BF16)
    nkv = col(C_NKV, C_WKV)
    nkv32_ref[...] = nkv
    nkv16_ref[...] = nkv[:, 2 * W_NSA_KV:].astype(BF16)
    wkv = col(C_WKV, C_ZN)
    wkv32_ref[...] = wkv
    wkv16_ref[...] = wkv.astype(BF16)
    zn_ref[...] = col(C_ZN, C_G).astype(BF16)
    g_ref[...] = _sigmoid(col(C_G, C_END))
    if km_refs:
        km_refs[0][0] = jnp.mean(mkv[:, :W_MOBA], axis=0, keepdims=True)


def _proj(x2d, sc, sh, gain, w, tm, tiles_per_mod, with_kmean):
    r, d = x2d.shape
    nt = r // tm
    mrows = sc.shape[1]
    row = lambda width: pl.BlockSpec((tm, width), lambda i: (i, 0))
    widths = [(512, BF16), (1024, F32), (1024, BF16), (512, BF16), (1024, BF16), (512, F32), (256, BF16),
              (256, F32), (256, BF16), (512, BF16), (256, F32)]
    out_specs = [row(wd) for wd, _ in widths]
    out_shape = [jax.ShapeDtypeStruct((r, wd), dt) for wd, dt in widths]
    if with_kmean:
        out_specs.append(pl.BlockSpec((1, 1, W_MOBA), lambda i: (i, 0, 0)))
        out_shape.append(jax.ShapeDtypeStruct((nt, 1, W_MOBA), F32))
    return pl.pallas_call(
        _proj_kernel,
        grid=(nt,),
        in_specs=[row(d),
                  pl.BlockSpec((1, mrows, d), lambda i: (i // tiles_per_mod, 0, 0)),
                  pl.BlockSpec((1, mrows, d), lambda i: (i // tiles_per_mod, 0, 0)),
                  pl.BlockSpec((1, d), lambda i: (0, 0)),
                  pl.BlockSpec((d, C_END), lambda i: (0, 0))],
        out_specs=out_specs,
        out_shape=out_shape,
        compiler_params=_cparams(("arbitrary",)),
        name="proj",
    )(x2d, sc, sh, gain, w)


def _flash_step(s, v, m_ref, l_ref, acc_ref):
    m_old = m_ref[...]
    m_new = jnp.maximum(m_old, jnp.max(s, axis=1, keepdims=True))
    alpha = jnp.exp2(m_old - m_new)
    p = jnp.exp2(s - m_new)
    l_ref[...] = alpha * l_ref[...] + jnp.sum(p, axis=1, keepdims=True)
    acc_ref[...] = alpha * acc_ref[...] + _dot(p.astype(BF16), v)
    m_ref[...] = m_new


def _flash_init(m_ref, l_ref, acc_ref):
    m_ref[...] = jnp.full(m_ref.shape, NEG, F32)
    l_ref[...] = jnp.zeros(l_ref.shape, F32)
    acc_ref[...] = jnp.zeros(acc_ref.shape, F32)


def _flash_step_t(s, vt, m_ref, acc_ref):
    m_old = m_ref[...]
    m_new = jnp.maximum(m_old, jnp.max(s, axis=0, keepdims=True))
    alpha = jnp.exp2(m_old - m_new)
    p = jnp.exp2(s - m_new).astype(BF16)
    vt_aug = jnp.concatenate([vt, jnp.ones((ONES_ROWS, vt.shape[1]), BF16)], axis=0)
    acc_ref[...] = alpha * acc_ref[...] + _dot(vt_aug, p)
    m_ref[...] = m_new


def _flash_init_t(m_ref, acc_ref):
    m_ref[...] = jnp.full(m_ref.shape, NEG, F32)
    acc_ref[...] = jnp.zeros(acc_ref.shape, F32)


def _flash_result_t(acc_ref):
    width = acc_ref.shape[0] - ONES_ROWS
    return acc_ref[0:width, :] / jnp.maximum(acc_ref[width:width + 1, :], TINY)


def _sweep_tiles(logits_at, values_at, table_at, i_diag, m_ref, acc_ref):
    big = 2 * KEY_TILE
    n_big = jnp.maximum(i_diag - 1, 0) // 2

    def far_body(n, s):
        nxt = logits_at(pl.multiple_of((n + 1) * big, big), big)
        _flash_step_t(s, values_at(pl.multiple_of(n * big, big), big), m_ref, acc_ref)
        return nxt

    s = lax.fori_loop(0, n_big, far_body, logits_at(0, big))
    odd = (i_diag & 1) == 1
    toff = jnp.where(odd, KEY_TILE, jnp.where(i_diag == 0, 2 * KEY_TILE, 0))
    _flash_step_t(s + table_at(pl.multiple_of(toff, KEY_TILE), big),
                  values_at(pl.multiple_of(n_big * big, big), big), m_ref, acc_ref)

    @pl.when((i_diag >= 2) & jnp.logical_not(odd))
    def _():
        off = pl.multiple_of(i_diag * KEY_TILE, KEY_TILE)
        _flash_step_t(logits_at(off, KEY_TILE) + table_at(2 * KEY_TILE, KEY_TILE), values_at(off, KEY_TILE),
                      m_ref, acc_ref)


def _topk_mask(vals, k, axis=1):
    idxf = lax.broadcasted_iota(jnp.int32, vals.shape, axis).astype(F32)

    def body(_, taken):
        cur = jnp.where(taken > 0.0, -jnp.inf, vals)
        mx = jnp.max(cur, axis=axis, keepdims=True)
        first = jnp.min(jnp.where(cur == mx, idxf, 1e9), axis=axis, keepdims=True)
        return jnp.where(idxf == first, 1.0, taken)

    return lax.fori_loop(0, k, body, jnp.zeros(vals.shape, F32))


def _masked_softmax(logits, valid, axis=1):
    lm = jnp.where(valid, logits, NEG)
    p = jnp.exp2(lm - jnp.max(lm, axis=axis, keepdims=True)) * valid.astype(F32)
    return p / jnp.maximum(jnp.sum(p, axis=axis, keepdims=True), TINY)


def _moba_kernel(qt_ref, k_ref, vt_ref, km_ref, kp_ref, tab_ref, o_ref, qaug_ref, m_ref, acc_ref):
    i = pl.program_id(2)
    tq = qt_ref.shape[2]
    qt = qt_ref[0]
    row = lax.broadcasted_iota(jnp.int32, qt.shape, 0)
    zero = jnp.zeros_like(qt)
    q2 = jnp.concatenate([jnp.where(row < HEAD_DIM, qt, zero), jnp.where(row >= HEAD_DIM, qt, zero)], axis=1)
    sc = _dot(km_ref[0].astype(BF16), q2)
    blk = lax.broadcasted_iota(jnp.int32, sc.shape, 0)
    cand = blk < i
    taken = _topk_mask(jnp.where(cand, sc, NEG), MOBA_TOPK, axis=0)
    sel = ((taken > 0.0) & cand) | (blk == i)
    qaug_ref[...] = jnp.concatenate([q2, jnp.where(sel, 0.0, NEG).astype(BF16)], axis=0)
    _flash_init_t(m_ref, acc_ref)

    def logits_at(off, n):
        kaug = jnp.concatenate([k_ref[0, pl.ds(off, n), :], kp_ref[pl.ds(off, n), :]], axis=1)
        return _dot(kaug, qaug_ref[...])

    _sweep_tiles(logits_at, lambda off, n: vt_ref[0, :, pl.ds(off, n)], lambda off, n: tab_ref[0, pl.ds(off, n), :],
                 i, m_ref, acc_ref)
    o = _flash_result_t(acc_ref)
    o_ref[0] = jnp.where(row < HEAD_DIM, o[:, :tq], o[:, tq:]).astype(BF16)


def _moba_prompt(qt, mkv16, vt, kmean_pad, kp, tab):
    b, _, t = qt.shape
    tq = MOBA_BLOCK
    nq = t // tq
    hp = H_MOBA // 2
    cols = 2 * tq
    return pl.pallas_call(
        _moba_kernel,
        grid=(b, hp, nq),
        in_specs=[pl.BlockSpec((1, LANE, tq), lambda bb, h, i: (bb, h, i)),
                  pl.BlockSpec((1, t, LANE), lambda bb, h, i: (bb, 0, h)),
                  pl.BlockSpec((1, LANE, t), lambda bb, h, i: (bb, h, 0)),
                  pl.BlockSpec((1, LANE, LANE), lambda bb, h, i: (bb, 0, h)),
                  pl.BlockSpec((t, LANE), lambda bb, h, i: (0, 0), pipeline_mode=pl.Buffered(1)),
                  pl.BlockSpec((1, 4 * KEY_TILE, cols), lambda bb, h, i: (h, 0, 0), pipeline_mode=pl.Buffered(1))],
        out_specs=pl.BlockSpec((1, LANE, tq), lambda bb, h, i: (bb, h, i)),
        out_shape=jax.ShapeDtypeStruct((b, W_MOBA, t), BF16),
        scratch_shapes=[pltpu.VMEM((2 * LANE, cols), BF16),
                        pltpu.VMEM((1, cols), F32), pltpu.VMEM((LANE + ONES_ROWS, cols), F32)],
        compiler_params=_cparams(("arbitrary", "arbitrary", "arbitrary")),
        name="moba_prompt",
    )(qt, mkv16, vt, kmean_pad, kp, tab)


def _cmp_kernel(pt_ref, *refs, pps):
    del pt_ref
    pages_k, pages_v = refs[:pps], refs[pps:2 * pps]
    pe_ref, kwa_ref, kwb_ref, kw2_ref, vwa_ref, vwb_ref, vw2_ref, kc_ref, vc_ref, uk_ref, uv_ref = refs[2 * pps:]
    j = pl.program_id(1)
    groups = pages_k[0].shape[1] // CMP_STRIDE
    for u in range(pps):
        row0 = pl.multiple_of((j * pps + u) * groups, groups)
        for l in range(CMP_STRIDE):
            rows_l = pl.ds(l, groups, stride=CMP_STRIDE)
            uk_ref[pl.ds(row0, groups), l * LANE:(l + 1) * LANE] = pages_k[u][0, rows_l, :]
            uv_ref[pl.ds(row0, groups), l * LANE:(l + 1) * LANE] = pages_v[u][0, rows_l, :]

    @pl.when(j == pl.num_programs(1) - 1)
    def _():
        nc = uk_ref.shape[0]
        last = lax.broadcasted_iota(jnp.int32, (nc, LANE), 0) == nc - 1

        def compress(u_ref, pe, wa_ref, wb_ref, w2_ref, out_ref):
            u = u_ref[...]
            pa = _dot((u + pe[0]).astype(BF16), wa_ref[...])
            pb = _dot((u + pe[1]).astype(BF16), wb_ref[...])
            pre = pa + pltpu.roll(pb, nc - 1, 0)
            hid = pre * _sigmoid(pre)
            out = _dot(hid.astype(BF16), w2_ref[...])
            out_ref[0] = jnp.where(last, 0.0, out).astype(BF16)

        compress(uk_ref, pe_ref[0], kwa_ref, kwb_ref, kw2_ref, kc_ref)
        compress(uv_ref, pe_ref[1], vwa_ref, vwb_ref, vw2_ref, vc_ref)


def _compress(pages_arr, pt, pe, kwa, kwb, kw2, vwa, vwb, vw2):
    b, n_pg = pt.shape
    page = pages_arr.shape[1]
    pps = min(PAGES_PER_STEP, n_pg)
    nc = n_pg * page // CMP_STRIDE
    page_spec = lambda u, col: pl.BlockSpec((1, page, LANE), lambda bb, j, p: (p[bb, j * pps + u], 0, col))
    full = lambda a: pl.BlockSpec(a.shape, lambda bb, j, p: (0,) * a.ndim)
    consts = (pe, kwa, kwb, kw2, vwa, vwb, vw2)
    grid_spec = pltpu.PrefetchScalarGridSpec(
        num_scalar_prefetch=1,
        grid=(b, n_pg // pps),
        in_specs=[page_spec(u, col) for col in range(2) for u in range(pps)] + [full(a) for a in consts],
        out_specs=[pl.BlockSpec((1, nc, LANE), lambda bb, j, p: (bb, 0, 0))] * 2,
        scratch_shapes=[pltpu.VMEM((nc, CMP_STRIDE * LANE), F32)] * 2)
    return pl.pallas_call(
        functools.partial(_cmp_kernel, pps=pps),
        grid_spec=grid_spec,
        out_shape=[jax.ShapeDtypeStruct((b, nc, LANE), BF16)] * 2,
        compiler_params=_cparams(("arbitrary", "arbitrary")),
        name="nsa_compress",
    )(pt, *([pages_arr] * (2 * pps)), *consts)


def _nsa_kernel(qt_ref, kc_ref, vct_ref, ovt_ref, ks_ref, vst_ref, kw_ref, vwt_ref, kp_ref, ts_ref, tw_ref, gt_ref,
                o_ref, qaug_ref, m_ref, acc_ref, ocmp_ref, oslc_ref):
    k = pl.program_id(1)
    i = pl.program_id(2)
    tq = qt_ref.shape[2]
    q4 = jnp.concatenate([qt_ref[0, g * LANE:(g + 1) * LANE, :] for g in range(NSA_GROUP)], axis=1)
    qpos1 = i * tq + lax.broadcasted_iota(jnp.int32, (1, tq), 1)
    qpos = jnp.concatenate([qpos1] * NSA_GROUP, axis=1)

    lc = _dot(kc_ref[0], q4)
    tok = lax.broadcasted_iota(jnp.int32, lc.shape, 0)
    pc = _masked_softmax(lc, tok * CMP_STRIDE + (CMP_LEN - 1) <= qpos, axis=0)
    ocmp_ref[...] = _dot(vct_ref[0], pc.astype(BF16))

    pcs = pc[:, 0:tq] + pc[:, tq:2 * tq] + pc[:, 2 * tq:3 * tq] + pc[:, 3 * tq:4 * tq]
    ph, plo = _split_bf16(pcs)
    imp = _dot(ovt_ref[...], ph) + _dot(ovt_ref[...], plo)
    jb = lax.broadcasted_iota(jnp.int32, imp.shape, 0)
    cur = qpos1 >> int(math.log2(SLC_BLOCK))
    avail = jb <= cur
    forced = (jb == 0) | (jb == cur) | (jb == cur - 1)
    imp = jnp.where(avail, jnp.where(forced, FORCE, imp), NEG)
    sel = (_topk_mask(imp, SLC_TOPN, axis=0) > 0.0) & avail
    selb = jnp.where(sel, 0.0, NEG).astype(BF16)
    qaug_ref[...] = jnp.concatenate([q4, jnp.concatenate([selb] * NSA_GROUP, axis=1)], axis=0)

    cd = (i * tq) // KEY_TILE
    a0 = pl.multiple_of((i * tq) % KEY_TILE, tq)

    def table(t_ref, r0, n):
        return jnp.concatenate([t_ref[0, pl.ds(r0, n), pl.ds(g * KEY_TILE + a0, tq)] for g in range(NSA_GROUP)],
                               axis=1)

    _flash_init_t(m_ref, acc_ref)

    def slc_logits_at(off, n):
        kaug = jnp.concatenate([ks_ref[0, pl.ds(off, n), :], kp_ref[pl.ds(off, n), :]], axis=1)
        return _dot(kaug, qaug_ref[...])

    _sweep_tiles(slc_logits_at, lambda off, n: vst_ref[0, :, pl.ds(off, n)], functools.partial(table, ts_ref),
                 cd, m_ref, acc_ref)
    oslc_ref[...] = _flash_result_t(acc_ref)

    _flash_init_t(m_ref, acc_ref)
    n_win = WINDOW // KEY_TILE
    span = WINDOW + KEY_TILE
    first = jnp.maximum(cd - n_win, 0)
    off = pl.multiple_of(first * KEY_TILE, KEY_TILE)
    toff = pl.multiple_of((first - (cd - n_win)) * KEY_TILE, KEY_TILE)
    s = _dot(kw_ref[0, pl.ds(off, span), :], q4) + table(tw_ref, toff, span)
    _flash_step_t(s, vwt_ref[0, :, pl.ds(off, span)], m_ref, acc_ref)
    owin = _flash_result_t(acc_ref)

    ocmp = ocmp_ref[...]
    oslc = oslc_ref[...]
    gts = gt_ref[0]
    for g in range(NSA_GROUP):
        c = slice(g * tq, (g + 1) * tq)
        og = (gts[g:g + 1, :] * ocmp[:, c] + gts[NSA_GROUP + g:NSA_GROUP + g + 1, :] * oslc[:, c]
              + gts[2 * NSA_GROUP + g:2 * NSA_GROUP + g + 1, :] * owin[:, c])
        o_ref[0, g * HEAD_DIM:(g + 1) * HEAD_DIM, :] = jnp.where(k == 0, og[:HEAD_DIM], og[HEAD_DIM:]).astype(BF16)


def _nsa_prompt(qt, kc, vct, ovt, nkv16, vst, wkv16, vwt, kps, ts, tw, gt):
    b, _, t = qt.shape
    tq = min(NSA_TQ, t)
    nq = t // tq
    cols = NSA_GROUP * tq
    nc = kc.shape[1]
    keys = pl.BlockSpec((1, t, LANE), lambda bb, k, i: (bb, 0, 0))
    vals = pl.BlockSpec((1, LANE, t), lambda bb, k, i: (bb, 0, 0))
    return pl.pallas_call(
        _nsa_kernel,
        grid=(b, H_NSA_KV, nq),
        in_specs=[pl.BlockSpec((1, NSA_GROUP * LANE, tq), lambda bb, k, i: (bb, k, i)),
                  pl.BlockSpec((1, nc, LANE), lambda bb, k, i: (bb, 0, 0)),
                  pl.BlockSpec((1, LANE, nc), lambda bb, k, i: (bb, 0, 0)),
                  pl.BlockSpec(ovt.shape, lambda bb, k, i: (0, 0)),
                  keys, vals, keys, vals,
                  pl.BlockSpec((t, LANE), lambda bb, k, i: (0, 0), pipeline_mode=pl.Buffered(1)),
                  pl.BlockSpec((1,) + ts.shape[1:], lambda bb, k, i: (k, 0, 0), pipeline_mode=pl.Buffered(1)),
                  pl.BlockSpec((1,) + tw.shape[1:], lambda bb, k, i: (k, 0, 0), pipeline_mode=pl.Buffered(1)),
                  pl.BlockSpec((1, LANE, tq), lambda bb, k, i: (bb, k, i))],
        out_specs=pl.BlockSpec((1, NSA_GROUP * HEAD_DIM, tq), lambda bb, k, i: (bb, k, i)),
        out_shape=jax.ShapeDtypeStruct((b, W_NSA, t), BF16),
        scratch_shapes=[pltpu.VMEM((2 * LANE, cols), BF16),
                        pltpu.VMEM((1, cols), F32), pltpu.VMEM((LANE + ONES_ROWS, cols), F32),
                        pltpu.VMEM((LANE, cols), F32), pltpu.VMEM((LANE, cols), F32)],
        compiler_params=_cparams(("arbitrary", "arbitrary", "arbitrary")),
        name="nsa_prompt",
    )(qt, kc, vct, ovt, nkv16, vst, wkv16, vwt, kps, ts, tw, gt)


def _moba_dec_kernel(pt_ref, qbd_ref, new_ref, td_ref, *refs, pps, nblk):
    del pt_ref
    pages = refs[:pps]
    o_ref, km_ref, mall_ref, lall_ref, oall_ref = refs[pps:]
    j = pl.program_id(1)
    qbd = qbd_ref[0]
    rows = qbd.shape[0]
    lane = lax.broadcasted_iota(jnp.int32, (rows, LANE), 1)
    ppb = MOBA_BLOCK // pages[0].shape[1]

    @pl.when(j == 0)
    def _():
        km_ref[...] = jnp.zeros(km_ref.shape, F32)
        mall_ref[...] = jnp.full(mall_ref.shape, NEG, F32)
        lall_ref[...] = jnp.zeros(lall_ref.shape, F32)

    for u in range(pps // ppb):
        blk = j * (pps // ppb) + u
        kf = jnp.concatenate([pages[u * ppb + w][0, :, :W_MOBA] for w in range(ppb)], axis=0)
        vf = jnp.concatenate([pages[u * ppb + w][0, :, W_MOBA:] for w in range(ppb)], axis=0)
        km_ref[pl.ds(blk, 1), :] = jnp.mean(kf, axis=0, keepdims=True)
        near = (blk == nblk - 1).astype(F32)
        s = _dot_nt(qbd, kf.astype(BF16)) + near * td_ref[:, 0:MOBA_BLOCK]
        mb = jnp.max(s, axis=1, keepdims=True)
        p = jnp.exp2(s - mb)
        mall_ref[...] = jnp.where(lane == blk, mb, mall_ref[...])
        lall_ref[...] = jnp.where(lane == blk, jnp.sum(p, axis=1, keepdims=True), lall_ref[...])
        oall_ref[pl.ds(blk, 1)] = _dot(p.astype(BF16), vf.astype(BF16))[None]

    @pl.when(j == pl.num_programs(1) - 1)
    def _():
        kn = new_ref[0, :, :W_MOBA].astype(BF16)
        vn = new_ref[0, :, W_MOBA:].astype(BF16)
        s_own = _dot_nt(qbd, kn) + td_ref[:, MOBA_BLOCK:MOBA_BLOCK + NEW_PAD]
        m_own = jnp.max(s_own, axis=1, keepdims=True)
        p_own = jnp.exp2(s_own - m_own)
        l_own = jnp.sum(p_own, axis=1, keepdims=True)
        o_own = _dot(p_own.astype(BF16), vn)
        sc = _dot_nt(qbd, km_ref[...].astype(BF16))
        cand = lane < nblk
        sel = (_topk_mask(jnp.where(cand, sc, NEG), MOBA_TOPK) > 0.0) & cand
        mall = mall_ref[...]
        m_fin = jnp.maximum(jnp.max(jnp.where(sel, mall, NEG), axis=1, keepdims=True), m_own)
        w = jnp.where(sel, jnp.exp2(mall - m_fin), 0.0)
        w_own = jnp.exp2(m_own - m_fin)
        l = jnp.sum(w * lall_ref[...], axis=1, keepdims=True) + w_own * l_own
        acc = w_own * o_own
        for n in range(nblk):
            acc = acc + w[:, n:n + 1] * oall_ref[n]
        o = acc / jnp.maximum(l, TINY)
        rowh = lax.broadcasted_iota(jnp.int32, o.shape, 0) & (H_MOBA - 1)
        laneh = lax.broadcasted_iota(jnp.int32, o.shape, 1) >> int(math.log2(HEAD_DIM))
        o = jnp.where(rowh == laneh, o, 0.0)
        o_ref[0] = jnp.sum(o.reshape(rows // H_MOBA, H_MOBA, W_MOBA), axis=1)


def _moba_decode(cache_m, pt, qbd, new_pad, td):
    b, n_pg = pt.shape
    page = cache_m.shape[1]
    pps = min(PAGES_PER_STEP, n_pg)
    nblk = n_pg * page // MOBA_BLOCK
    rows = qbd.shape[1]
    s_new = rows // H_MOBA
    page_spec = lambda u: pl.BlockSpec((1, page, 2 * W_MOBA), lambda bb, j, p: (p[bb, j * pps + u], 0, 0))
    grid_spec = pltpu.PrefetchScalarGridSpec(
        num_scalar_prefetch=1,
        grid=(b, n_pg // pps),
        in_specs=[pl.BlockSpec((1, rows, W_MOBA), lambda bb, j, p: (bb, 0, 0)),
                  pl.BlockSpec((1, NEW_PAD, 2 * W_MOBA), lambda bb, j, p: (bb, 0, 0)),
                  pl.BlockSpec(td.shape, lambda bb, j, p: (0, 0))] + [page_spec(u) for u in range(pps)],
        out_specs=pl.BlockSpec((1, s_new, W_MOBA), lambda bb, j, p: (bb, 0, 0)),
        scratch_shapes=[pltpu.VMEM((LANE, W_MOBA), F32), pltpu.VMEM((rows, LANE), F32), pltpu.VMEM((rows, LANE), F32),
                        pltpu.VMEM((nblk, rows, W_MOBA), F32)])
    return pl.pallas_call(
        functools.partial(_moba_dec_kernel, pps=pps, nblk=nblk),
        grid_spec=grid_spec,
        out_shape=jax.ShapeDtypeStruct((b, s_new, W_MOBA), F32),
        compiler_params=_cparams(("arbitrary", "arbitrary")),
        name="moba_decode",
    )(pt, qbd, new_pad, td, *([cache_m] * pps))


def _nsa_dec_kernel(pt_ref, qd_ref, kc_ref, vc_ref, ov_ref, kp_ref, new_ref, win_ref, neww_ref, g_ref,
                    ts_ref, tso_ref, tw_ref, two_ref, *refs, pps, past):
    del pt_ref
    pages = refs[:pps]
    o_ref, qaug_ref, m_ref, l_ref, acc_ref, ocmp_ref, selown_ref = refs[pps:]
    j = pl.program_id(1)
    nj = pl.num_programs(1)
    qd = qd_ref[0]
    rows = qd.shape[0]
    page = pages[0].shape[1]
    grp = rows // NSA_GROUP
    s_new = grp // H_NSA_KV

    @pl.when(j == 0)
    def _():
        lc = _dot_nt(qd, kc_ref[0])
        tok = lax.broadcasted_iota(jnp.int32, lc.shape, 1)
        qpos = past + (lax.broadcasted_iota(jnp.int32, (rows, 1), 0) & (s_new - 1))
        pc = _masked_softmax(lc, tok * CMP_STRIDE + (CMP_LEN - 1) <= qpos)
        ocmp_ref[...] = _dot(pc.astype(BF16), vc_ref[0])
        pcs = pc[0:grp] + pc[grp:2 * grp] + pc[2 * grp:3 * grp] + pc[3 * grp:4 * grp]
        ph, plo = _split_bf16(pcs)
        imp = _dot(ph, ov_ref[...]) + _dot(plo, ov_ref[...])
        jb = lax.broadcasted_iota(jnp.int32, imp.shape, 1)
        cur = qpos[0:grp] >> int(math.log2(SLC_BLOCK))
        avail = jb <= cur
        forced = (jb == 0) | (jb == cur) | (jb == cur - 1)
        imp = jnp.where(avail, jnp.where(forced, FORCE, imp), NEG)
        sel = (_topk_mask(imp, SLC_TOPN) > 0.0) & avail
        selb = jnp.concatenate([jnp.where(sel, 0.0, NEG)] * NSA_GROUP, axis=0)
        qaug_ref[...] = jnp.concatenate([qd, selb[:, :LANE].astype(BF16)], axis=1)
        selown_ref[...] = selb[:, LANE:]
        _flash_init(m_ref, l_ref, acc_ref)

    for u in range(pps):
        pg = j * pps + u
        off = pl.multiple_of(pg * page, page)
        kaug = jnp.concatenate([pages[u][0, :, :LANE].astype(BF16), kp_ref[pl.ds(off, page), :]], axis=1)
        near = (pg == nj * pps - 1).astype(F32)
        s = _dot_nt(qaug_ref[...], kaug) + near * ts_ref[...]
        _flash_step(s, pages[u][0, :, LANE:].astype(BF16), m_ref, l_ref, acc_ref)

    @pl.when(j == nj - 1)
    def _():
        new = new_ref[0]
        s_own = _dot_nt(qd, new[:, 2 * LANE:3 * LANE].astype(BF16)) + tso_ref[...] + selown_ref[:, 0:1]
        _flash_step(s_own, new[:, 3 * LANE:].astype(BF16), m_ref, l_ref, acc_ref)
        oslc = acc_ref[...] / jnp.maximum(l_ref[...], TINY)
        _flash_init(m_ref, l_ref, acc_ref)
        win = win_ref[0]
        _flash_step(_dot_nt(qd, win[:, :LANE].astype(BF16)) + tw_ref[...], win[:, LANE:].astype(BF16),
                    m_ref, l_ref, acc_ref)
        neww = neww_ref[0]
        _flash_step(_dot_nt(qd, neww[:, :LANE].astype(BF16)) + two_ref[...], neww[:, LANE:].astype(BF16),
                    m_ref, l_ref, acc_ref)
        owin = acc_ref[...] / jnp.maximum(l_ref[...], TINY)
        gts = g_ref[0]
        o_ref[0] = gts[:, 0:1] * ocmp_ref[...] + gts[:, 1:2] * oslc + gts[:, 2:3] * owin


def _nsa_decode(cache_n, pt, qd, kc, vc, ov, kps, new_pad, win, neww, gts, ts, tso, tw, two):
    b, n_pg = pt.shape
    page = cache_n.shape[1]
    pps = min(PAGES_PER_STEP, n_pg)
    rows = qd.shape[1]
    nc = kc.shape[1]
    full = lambda a: pl.BlockSpec(a.shape, lambda bb, j, p: (0,) * a.ndim)
    per_b = lambda a: pl.BlockSpec((1,) + a.shape[1:], lambda bb, j, p: (bb,) + (0,) * (a.ndim - 1))
    page_spec = lambda u: pl.BlockSpec((1, page, 2 * LANE), lambda bb, j, p: (p[bb, j * pps + u], 0, 1))
    grid_spec = pltpu.PrefetchScalarGridSpec(
        num_scalar_prefetch=1,
        grid=(b, n_pg // pps),
        in_specs=[per_b(qd), per_b(kc), per_b(vc), full(ov), full(kps), per_b(new_pad), per_b(win), per_b(neww),
                  per_b(gts), full(ts), full(tso), full(tw), full(two)] + [page_spec(u) for u in range(pps)],
        out_specs=pl.BlockSpec((1, rows, LANE), lambda bb, j, p: (bb, 0, 0)),
        scratch_shapes=[pltpu.VMEM((rows, 2 * LANE), BF16),
                        pltpu.VMEM((rows, 1), F32), pltpu.VMEM((rows, 1), F32), pltpu.VMEM((rows, LANE), F32),
                        pltpu.VMEM((rows, LANE), F32), pltpu.VMEM((rows, LANE), F32)])
    return pl.pallas_call(
        functools.partial(_nsa_dec_kernel, pps=pps, past=n_pg * page),
        grid_spec=grid_spec,
        out_shape=jax.ShapeDtypeStruct((b, rows, LANE), F32),
        compiler_params=_cparams(("arbitrary", "arbitrary")),
        name="nsa_decode",
    )(pt, qd, kc, vc, ov, kps, new_pad, win, neww, gts, ts, tso, tw, two, *([cache_n] * pps))


def _out_kernel(x_ref, om_ref, zm_ref, on_ref, zn_ref, gate_ref, w_ref, fg_ref, y_ref):
    zm = zm_ref[...].astype(F32)
    zn = zn_ref[...].astype(F32)
    mm = (om_ref[...].astype(F32) * (zm * _sigmoid(zm))).astype(BF16)
    mn = (on_ref[...].astype(F32) * (zn * _sigmoid(zn))).astype(BF16)
    mixed = _dot(mm, w_ref[:W_MOBA, :]) + _dot(mn, w_ref[W_MOBA:, :])
    xn = x_ref[...] + gate_ref[0] * mixed
    inv = lax.rsqrt(jnp.mean(xn * xn, axis=-1, keepdims=True) + RMS_EPS)
    y_ref[...] = (xn * inv) * fg_ref[...]


def _out_proj(x2d, om, zm, on, zn, gate, w_out, fgain, tm, tiles_per_mod):
    r, d = x2d.shape
    mrows = gate.shape[1]
    row = lambda width: pl.BlockSpec((tm, width), lambda i: (i, 0))
    return pl.pallas_call(
        _out_kernel,
        grid=(r // tm,),
        in_specs=[row(d), row(W_MOBA), row(W_MOBA), row(W_NSA), row(W_NSA),
                  pl.BlockSpec((1, mrows, d), lambda i: (i // tiles_per_mod, 0, 0)),
                  pl.BlockSpec((d, d), lambda i: (0, 0)),
                  pl.BlockSpec((1, d), lambda i: (0, 0))],
        out_specs=row(d),
        out_shape=jax.ShapeDtypeStruct((r, d), F32),
        compiler_params=_cparams(("arbitrary",)),
        name="out_proj",
    )(x2d, om, zm, on, zn, gate, w_out, fgain)


def _t5_bucket(rel):
    n = jnp.maximum(rel, 0)
    exact = N_BUCKETS // 2
    nf = jnp.maximum(n, 1).astype(F32)
    large = exact + (jnp.log(nf / exact) / math.log(MAX_DISTANCE / exact) * (N_BUCKETS - exact)).astype(jnp.int32)
    return jnp.where(n < exact, n, jnp.minimum(large, N_BUCKETS - 1))


def _bias_of_rel(tab, rel, shift_far):
    bucket = _t5_bucket(rel)[None]
    col = lambda kk: tab[kk].reshape((-1,) + (1,) * rel.ndim)
    val = jnp.broadcast_to(col(N_BUCKETS - 1), (tab.shape[1],) + rel.shape)
    for kk in range(N_BUCKETS - 1):
        val = jnp.where(bucket == kk, col(kk), val)
    if shift_far:
        val = val - col(N_BUCKETS - 1)
    return jnp.where(rel[None] >= 0, val * LOG2E, NEG)


def _layout_w_in(w_in):
    d = w_in.shape[0]
    sc = HEAD_DIM ** -0.5 * LOG2E
    o = 4 * W_MOBA
    q_m, rest_m = w_in[:, :W_MOBA] * sc, w_in[:, W_MOBA:o]
    q_n = (w_in[:, o:o + W_NSA] * sc).reshape(d, H_NSA, HEAD_DIM)
    o += W_NSA
    kv_n = w_in[:, o:o + 6 * W_NSA_KV]
    o += 6 * W_NSA_KV
    g_n = w_in[:, o:o + 3 * H_NSA]
    z_n = w_in[:, o + 3 * H_NSA:]
    zq = jnp.zeros((d, HEAD_DIM), w_in.dtype)
    qn_exp = jnp.concatenate(
        [jnp.concatenate([q_n[:, h], zq] if h // NSA_GROUP == 0 else [zq, q_n[:, h]], axis=1) for h in range(H_NSA)],
        axis=1)
    gcols = []
    for k in range(H_NSA_KV):
        cols = [g_n[:, (NSA_GROUP * k + g) * 3 + c] for c in range(3) for g in range(NSA_GROUP)]
        gcols.append(jnp.pad(jnp.stack(cols, axis=1), ((0, 0), (0, LANE - len(cols)))))
    w = jnp.concatenate([q_m, rest_m[:, :2 * W_MOBA], rest_m[:, 2 * W_MOBA:], qn_exp, kv_n, z_n] + gcols, axis=1)
    assert w.shape[1] == C_END
    return w.astype(BF16)


def _layout_cmp(pe, w1, w2):
    w = w1.reshape(2, CMP_STRIDE, HEAD_DIM, CMP_HIDDEN)
    z = jnp.zeros_like(w)
    full = jnp.stack([jnp.concatenate([w, z], axis=-1), jnp.concatenate([z, w], axis=-1)], axis=2)
    full = full.reshape(2, CMP_STRIDE * H_NSA_KV * HEAD_DIM, H_NSA_KV * CMP_HIDDEN).astype(BF16)
    zz = jnp.zeros_like(w2)
    w2bd = jnp.concatenate([jnp.concatenate([w2, zz], axis=1), jnp.concatenate([zz, w2], axis=1)], axis=0).astype(BF16)
    pe2 = jnp.broadcast_to(pe.reshape(2, CMP_STRIDE, 1, HEAD_DIM), (2, CMP_STRIDE, H_NSA_KV, HEAD_DIM))
    return pe2.reshape(2, 1, CMP_STRIDE * LANE), full[0], full[1], w2bd


def _block_onehot(t, block):
    return (jnp.arange(t)[:, None] // block == jnp.arange(LANE)[None, :]).astype(BF16)


def _overlap(nc, n_cmp, lanes):
    i = jnp.arange(nc)[:, None]
    s = jnp.arange(lanes)[None, :]
    start = i * CMP_STRIDE
    hit = (start < s * SLC_BLOCK + SLC_BLOCK) & (start + CMP_LEN - 1 >= s * SLC_BLOCK) & (i < n_cmp)
    return hit.astype(BF16)


def kernel(x_prompt, x_sample, c_prompt, c_sample, cache_moba_kv, cache_nsa_kv, state_nsa_win, page_table, w_ada, b_ada, norm_gain, w_in, cmp_pe, cmp_k_w1, cmp_k_w2, cmp_v_w1, cmp_v_w2, w_out, rel_bias, final_gain):
    depth = w_in.shape[0]
    assert depth == 1
    B, T, D = x_prompt.shape
    BS, S, _ = x_sample.shape
    n_pg = page_table.shape[1]
    page = cache_moba_kv.shape[2]
    P = n_pg * page
    WB = state_nsa_win.shape[2]
    assert D == (H_MOBA + H_NSA) * HEAD_DIM and T % KEY_TILE == 0 and P % MOBA_BLOCK == 0 and S <= NEW_PAD
    assert T // SLC_BLOCK <= LANE and P // SLC_BLOCK <= LANE and T >= WINDOW and WB == WINDOW
    assert (BS * S) % SUBLANE == 0 and S & (S - 1) == 0

    w = _layout_w_in(w_in[0])
    w_out_b = w_out[0].astype(BF16)
    gain = norm_gain[0].reshape(1, D)
    fgain = final_gain.reshape(1, D)
    kcmp_w = _layout_cmp(cmp_pe[0, 0], cmp_k_w1[0], cmp_k_w2[0])
    vcmp_w = _layout_cmp(cmp_pe[0, 1], cmp_v_w1[0], cmp_v_w2[0])
    pe2 = jnp.stack([kcmp_w[0], vcmp_w[0]], axis=0)
    cmp_consts = (pe2,) + kcmp_w[1:] + vcmp_w[1:]
    bias_m = rel_bias[:, :H_MOBA]
    bias_n = rel_bias[:, H_MOBA:]

    m_all = B + BS
    m_pad = -(-m_all // SUBLANE) * SUBLANE
    c_all = jnp.pad(jnp.concatenate([c_prompt, c_sample], axis=0), ((0, m_pad - m_all), (0, 0)))
    mod = _ada(c_all, w_ada[0], b_ada[0])
    shift, scale, gate = mod[:, :D], mod[:, D:2 * D], mod[:, 2 * D:]

    a = jnp.arange(KEY_TILE)[None, :]
    jk = jnp.arange(2 * KEY_TILE)[:, None]
    rel_s = a + KEY_TILE - jk
    heads_major = lambda tb, n_grp: jnp.transpose(
        tb.reshape(n_grp, -1, tb.shape[1], KEY_TILE), (0, 2, 1, 3)).reshape(n_grp, tb.shape[1], -1)
    rows_of = lambda tb, n, val: jnp.full((tb.shape[0], n, tb.shape[2]), val, F32)
    sweep_rows = lambda tb: jnp.concatenate([rows_of(tb, KEY_TILE, 0.0), tb, rows_of(tb, KEY_TILE, NEG)], axis=1)
    tab_m = sweep_rows(heads_major(_bias_of_rel(bias_m, rel_s, True), H_MOBA // 2))
    tab_s = sweep_rows(heads_major(_bias_of_rel(bias_n, rel_s, True), H_NSA_KV))
    jw = jnp.arange(WINDOW + KEY_TILE)[:, None]
    rel_w = a + WINDOW - jw
    tab_w = heads_major(_bias_of_rel(bias_n, jnp.where(rel_w < WINDOW, rel_w, -1), False), H_NSA_KV)
    tab_w = jnp.concatenate([tab_w, rows_of(tab_w, WINDOW, NEG)], axis=1)

    tpm = T // PROJ_TM
    (qm, mkv32, mkv16, zm, qn, nkv32, nkv16, wkv32, wkv16, zn, gts, kmean) = _proj(
        x_prompt.reshape(B * T, D), scale[:B].reshape(B, 1, D), shift[:B].reshape(B, 1, D), gain, w,
        PROJ_TM, tpm, True)
    nb = T // MOBA_BLOCK
    kmean_pad = jnp.pad(kmean.reshape(B, nb, W_MOBA), ((0, 0), (0, LANE - nb), (0, 0)))
    feat_major = lambda m: jnp.swapaxes(m.reshape(B, T, -1), 1, 2)
    mkv16 = mkv16.reshape(B, T, 2 * W_MOBA)
    o_m = _moba_prompt(feat_major(qm), mkv16, feat_major(mkv16[..., W_MOBA:]), kmean_pad,
                       _block_onehot(T, MOBA_BLOCK), tab_m)
    pt_prompt = jnp.arange(B * (T // page), dtype=jnp.int32).reshape(B, T // page)
    kc_p, vc_p = _compress(nkv32.reshape(B * T // page, page, 4 * W_NSA_KV), pt_prompt, *cmp_consts)
    nc_p = T // CMP_STRIDE
    nkv16 = nkv16.reshape(B, T, 2 * LANE)
    wkv16 = wkv16.reshape(B, T, 2 * LANE)
    o_n = _nsa_prompt(feat_major(qn), kc_p, jnp.swapaxes(vc_p, 1, 2), _overlap(nc_p, nc_p - 1, LANE).T,
                      nkv16, feat_major(nkv16[..., LANE:]), wkv16, feat_major(wkv16[..., LANE:]),
                      _block_onehot(T, SLC_BLOCK), tab_s, tab_w, feat_major(gts))
    token_major = lambda m: jnp.swapaxes(m, 1, 2).reshape(B * T, -1)
    y_prompt = _out_proj(x_prompt.reshape(B * T, D), token_major(o_m), zm, token_major(o_n), zn,
                         gate[:B].reshape(B, 1, D), w_out_b, fgain, PROJ_TM, tpm).reshape(B, T, D)
    moba_kv_prompt = mkv32.reshape(1, B, T, 2, H_MOBA, HEAD_DIM)
    nsa_kv_prompt = nkv32.reshape(1, B, T, 4, H_NSA_KV, HEAD_DIM)
    win_prompt = wkv32.reshape(B, T, 2 * LANE)[:, T - WINDOW:].reshape(1, B, WINDOW, 2, H_NSA_KV, HEAD_DIM)

    RS = BS * S
    rep = lambda m: jnp.repeat(m[B:B + BS], S, axis=0).reshape(1, RS, D)
    (qm_s, mkv32_s, _, zm_s, qn_s, nkv32_s, _, wkv32_s, _, zn_s, gts_s) = _proj(
        x_sample.reshape(RS, D), rep(scale), rep(shift), gain, w, RS, 1, False)
    pad_new = lambda m: jnp.pad(m.reshape(BS, S, -1), ((0, 0), (0, NEW_PAD - S), (0, 0)))
    s_idx = jnp.arange(S)

    q_rep = jnp.repeat(qm_s.reshape(BS, S, 1, H_MOBA, HEAD_DIM), H_MOBA, axis=2)
    eye = (jnp.arange(H_MOBA)[:, None] == jnp.arange(H_MOBA)[None, :])[None, None, :, :, None]
    qbd = jnp.where(eye, q_rep, 0).reshape(BS, S * H_MOBA, W_MOBA)
    rel_d = jnp.concatenate([MOBA_BLOCK + s_idx[:, None] - jnp.arange(MOBA_BLOCK)[None, :],
                             s_idx[:, None] - jnp.arange(NEW_PAD)[None, :]], axis=1)
    td = jnp.moveaxis(_bias_of_rel(bias_m, rel_d, True), 0, 1).reshape(S * H_MOBA, -1)
    td = jnp.pad(td, ((0, 0), (0, 3 * LANE - td.shape[1])))
    o_m_s = _moba_decode(cache_moba_kv[0].reshape(-1, page, 2 * W_MOBA), page_table, qbd, pad_new(mkv32_s), td)

    cache_n = cache_nsa_kv[0].reshape(-1, page, 4 * W_NSA_KV)
    kc_s, vc_s = _compress(cache_n, page_table, *cmp_consts)
    nc_s = P // CMP_STRIDE
    n_cmp_s = (P + S - CMP_LEN) // CMP_STRIDE + 1
    order = lambda m: jnp.transpose(m, (0, 3, 2, 1) + tuple(range(4, m.ndim)))
    rows_n = NSA_GROUP * H_NSA_KV * S
    qd = order(qn_s.reshape(BS, S, H_NSA_KV, NSA_GROUP, LANE)).reshape(BS, rows_n, LANE)
    g3 = gts_s.reshape(BS, S, H_NSA_KV, LANE)[..., :3 * NSA_GROUP].reshape(BS, S, H_NSA_KV, 3, NSA_GROUP)
    gts_d = order(jnp.swapaxes(g3, 3, 4)).reshape(BS, rows_n, 3)
    bias_d = jnp.swapaxes(bias_n.reshape(N_BUCKETS, H_NSA_KV, NSA_GROUP), 1, 2).reshape(N_BUCKETS, H_NSA)

    def dec_table(rel, shift_far):
        return _bias_of_rel(bias_d, rel, shift_far).reshape(rows_n, rel.shape[1])

    ts = dec_table(page + s_idx[:, None] - jnp.arange(page)[None, :], True)
    rel_own = s_idx[:, None] - jnp.arange(NEW_PAD)[None, :]
    tso = dec_table(rel_own, True)
    rel_win = WB + s_idx[:, None] - jnp.arange(WB)[None, :]
    tw = dec_table(jnp.where(rel_win < WINDOW, rel_win, -1), False)
    two = dec_table(rel_own, False)
    win_state = state_nsa_win[0].reshape(BS, WB, 2 * LANE)
    o_n_raw = _nsa_decode(cache_n, page_table, qd, kc_s, vc_s, _overlap(nc_s, n_cmp_s, 2 * LANE),
                          _block_onehot(P, SLC_BLOCK), pad_new(nkv32_s), win_state, pad_new(wkv32_s), gts_d,
                          ts, tso, tw, two)
    o5 = o_n_raw.reshape(BS, NSA_GROUP, H_NSA_KV, S, H_NSA_KV, HEAD_DIM)
    o_n_s = jnp.stack([o5[:, :, k, :, k] for k in range(H_NSA_KV)], axis=1)
    o_n_s = jnp.transpose(o_n_s, (0, 3, 1, 2, 4)).reshape(RS, W_NSA)
    y_sample = _out_proj(x_sample.reshape(RS, D), o_m_s.reshape(RS, W_MOBA).astype(BF16), zm_s,
                         o_n_s.astype(BF16), zn_s, rep(gate), w_out_b, fgain, RS, 1).reshape(BS, S, D)
    moba_kv_sample = mkv32_s.reshape(1, BS, S, 2, H_MOBA, HEAD_DIM)
    nsa_kv_sample = nkv32_s.reshape(1, BS, S, 4, H_NSA_KV, HEAD_DIM)
    win_sample = jnp.concatenate([win_state[:, S:], wkv32_s.reshape(BS, S, 2 * LANE)], axis=1)
    win_sample = win_sample.reshape(1, BS, WB, 2, H_NSA_KV, HEAD_DIM)
    return (y_prompt, y_sample, moba_kv_prompt, moba_kv_sample, nsa_kv_prompt, nsa_kv_sample, win_prompt, win_sample)
```

```python
import functools
import math

import jax
import jax.numpy as jnp
from jax import lax
from jax.experimental import pallas as pl
from jax.experimental.pallas import tpu as pltpu

F32 = jnp.float32
BF16 = jnp.bfloat16

HEAD_DIM = 64
H_MOBA = 8
H_NSA = 8
H_NSA_KV = 2
NSA_GROUP = 4
W_MOBA = H_MOBA * HEAD_DIM
W_NSA = H_NSA * HEAD_DIM
W_NSA_KV = H_NSA_KV * HEAD_DIM
MOBA_BLOCK = 256
MOBA_TOPK = 3
CMP_LEN = 32
CMP_STRIDE = 16
CMP_HIDDEN = 2 * HEAD_DIM
SLC_BLOCK = 64
SLC_TOPN = 16
WINDOW = 512
N_BUCKETS = 32
MAX_DISTANCE = 128
RMS_EPS = 1e-6
NEG = -1e30
FORCE = 1e9
TINY = 1e-30

LANE = 128
SUBLANE = 8
KEY_TILE = 256
NSA_TQ = 128
PROJ_TM = 256
PAGES_PER_STEP = 8
NEW_PAD = 8
ONES_ROWS = 16
LOG2E = math.log2(math.e)
VMEM_LIMIT = 56 * 1024 * 1024

C_QM, C_MKV, C_ZM, C_QN, C_NKV, C_WKV, C_ZN, C_G, C_END = 0, 512, 1536, 2048, 3072, 3584, 3840, 4352, 4608

_NT = (((1,), (1,)), ((), ()))


def _dot(a, b):
    return jnp.dot(a, b, preferred_element_type=F32)


def _dot_nt(a, b):
    return lax.dot_general(a, b, _NT, preferred_element_type=F32)


def _split_bf16(a):
    hi = a.astype(BF16)
    lo = (a - hi.astype(F32)).astype(BF16)
    return hi, lo


def _sigmoid(x):
    return 1.0 / (1.0 + jnp.exp(-x))


def _cparams(sem):
    return pltpu.CompilerParams(dimension_semantics=sem, vmem_limit_bytes=VMEM_LIMIT)


def _ada_kernel(c_ref, w_ref, b_ref, o_ref):
    ch, cl = _split_bf16(c_ref[...])
    wh, wl = _split_bf16(w_ref[...])
    o_ref[...] = _dot(ch, wh) + _dot(ch, wl) + _dot(cl, wh) + b_ref[...]


def _ada(c_all, w_ada, b_ada):
    m, d = c_all.shape
    n = w_ada.shape[1]
    tn = 512
    return pl.pallas_call(
        _ada_kernel,
        grid=(n // tn,),
        in_specs=[pl.BlockSpec((m, d), lambda j: (0, 0)),
                  pl.BlockSpec((d, tn), lambda j: (0, j)),
                  pl.BlockSpec((1, tn), lambda j: (0, j))],
        out_specs=pl.BlockSpec((m, tn), lambda j: (0, j)),
        out_shape=jax.ShapeDtypeStruct((m, n), F32),
        compiler_params=_cparams(("arbitrary",)),
        name="ada",
    )(c_all, w_ada, b_ada.reshape(1, n))


def _proj_kernel(x_ref, sc_ref, sh_ref, gain_ref, w_ref,
                 qm_ref, mkv32_ref, mkv16_ref, zm_ref, qn_ref, nkv32_ref, nkv16_ref, wkv32_ref, wkv16_ref,
                 zn_ref, g_ref, *km_refs):
    x = x_ref[...]
    inv = lax.rsqrt(jnp.mean(x * x, axis=-1, keepdims=True) + RMS_EPS)
    h = (x * inv) * gain_ref[...] * (1.0 + sc_ref[0]) + sh_ref[0]
    hb = h.astype(BF16)
    col = lambda a, b: _dot(hb, w_ref[:, a:b])
    qm_ref[...] = col(C_QM, C_MKV).astype(BF16)
    mkv = col(C_MKV, C_ZM)
    mkv32_ref[...] = mkv
    mkv16_ref[...] = mkv.astype(BF16)
    zm_ref[...] = col(C_ZM, C_QN).astype(BF16)
    qn_ref[...] = col(C_QN, C_NKV).astype(BF16)
    nkv = col(C_NKV, C_WKV)
    nkv32_ref[...] = nkv
    nkv16_ref[...] = nkv[:, 2 * W_NSA_KV:].astype(BF16)
    wkv = col(C_WKV, C_ZN)
    wkv32_ref[...] = wkv
    wkv16_ref[...] = wkv.astype(BF16)
    zn_ref[...] = col(C_ZN, C_G).astype(BF16)
    g_ref[...] = _sigmoid(col(C_G, C_END))
    if km_refs:
        km_refs[0][0] = jnp.mean(mkv[:, :W_MOBA], axis=0, keepdims=True)


def _proj(x2d, sc, sh, gain, w, tm, tiles_per_mod, with_kmean):
    r, d = x2d.shape
    nt = r // tm
    mrows = sc.shape[1]
    row = lambda width: pl.BlockSpec((tm, width), lambda i: (i, 0))
    widths = [(512, BF16), (1024, F32), (1024, BF16), (512, BF16), (1024, BF16), (512, F32), (256, BF16),
              (256, F32), (256, BF16), (512, BF16), (256, F32)]
    out_specs = [row(wd) for wd, _ in widths]
    out_shape = [jax.ShapeDtypeStruct((r, wd), dt) for wd, dt in widths]
    if with_kmean:
        out_specs.append(pl.BlockSpec((1, 1, W_MOBA), lambda i: (i, 0, 0)))
        out_shape.append(jax.ShapeDtypeStruct((nt, 1, W_MOBA), F32))
    return pl.pallas_call(
        _proj_kernel,
        grid=(nt,),
        in_specs=[row(d),
                  pl.BlockSpec((1, mrows, d), lambda i: (i // tiles_per_mod, 0, 0)),
                  pl.BlockSpec((1, mrows, d), lambda i: (i // tiles_per_mod, 0, 0)),
                  pl.BlockSpec((1, d), lambda i: (0, 0)),
                  pl.BlockSpec((d, C_END), lambda i: (0, 0))],
        out_specs=out_specs,
        out_shape=out_shape,
        compiler_params=_cparams(("arbitrary",)),
        name="proj",
    )(x2d, sc, sh, gain, w)


def _flash_step(s, v, m_ref, l_ref, acc_ref, feature_major=False):
    m_old = m_ref[...]
    m_new = jnp.maximum(m_old, jnp.max(s, axis=1, keepdims=True))
    alpha = jnp.exp2(m_old - m_new)
    p = jnp.exp2(s - m_new)
    l_ref[...] = alpha * l_ref[...] + jnp.sum(p, axis=1, keepdims=True)
    pv = _dot_nt(p.astype(BF16), v) if feature_major else _dot(p.astype(BF16), v)
    acc_ref[...] = alpha * acc_ref[...] + pv
    m_ref[...] = m_new


def _flash_init(m_ref, l_ref, acc_ref):
    m_ref[...] = jnp.full(m_ref.shape, NEG, F32)
    l_ref[...] = jnp.zeros(l_ref.shape, F32)
    acc_ref[...] = jnp.zeros(acc_ref.shape, F32)


def _flash_step_t(s, vt, m_ref, acc_ref):
    m_old = m_ref[...]
    m_new = jnp.maximum(m_old, jnp.max(s, axis=0, keepdims=True))
    alpha = jnp.exp2(m_old - m_new)
    p = jnp.exp2(s - m_new).astype(BF16)
    vt_aug = jnp.concatenate([vt, jnp.ones((ONES_ROWS, vt.shape[1]), BF16)], axis=0)
    acc_ref[...] = alpha * acc_ref[...] + _dot(vt_aug, p)
    m_ref[...] = m_new


def _flash_init_t(m_ref, acc_ref):
    m_ref[...] = jnp.full(m_ref.shape, NEG, F32)
    acc_ref[...] = jnp.zeros(acc_ref.shape, F32)


def _flash_result_t(acc_ref):
    width = acc_ref.shape[0] - ONES_ROWS
    return acc_ref[0:width, :] / jnp.maximum(acc_ref[width:width + 1, :], TINY)


def _sweep_tiles(logits_at, values_at, table_at, i_diag, m_ref, acc_ref):
    big = 2 * KEY_TILE
    n_big = jnp.maximum(i_diag - 1, 0) // 2

    def far_body(n, s):
        nxt = logits_at(pl.multiple_of((n + 1) * big, big), big)
        _flash_step_t(s, values_at(pl.multiple_of(n * big, big), big), m_ref, acc_ref)
        return nxt

    s = lax.fori_loop(0, n_big, far_body, logits_at(0, big))
    odd = (i_diag & 1) == 1
    toff = jnp.where(odd, KEY_TILE, jnp.where(i_diag == 0, 2 * KEY_TILE, 0))
    _flash_step_t(s + table_at(pl.multiple_of(toff, KEY_TILE), big),
                  values_at(pl.multiple_of(n_big * big, big), big), m_ref, acc_ref)

    @pl.when((i_diag >= 2) & jnp.logical_not(odd))
    def _():
        off = pl.multiple_of(i_diag * KEY_TILE, KEY_TILE)
        _flash_step_t(logits_at(off, KEY_TILE) + table_at(2 * KEY_TILE, KEY_TILE), values_at(off, KEY_TILE),
                      m_ref, acc_ref)


def _topk_mask(vals, k, axis=1):
    idxf = lax.broadcasted_iota(jnp.int32, vals.shape, axis).astype(F32)

    def body(_, taken):
        cur = jnp.where(taken > 0.0, -jnp.inf, vals)
        mx = jnp.max(cur, axis=axis, keepdims=True)
        first = jnp.min(jnp.where(cur == mx, idxf, 1e9), axis=axis, keepdims=True)
        return jnp.where(idxf == first, 1.0, taken)

    return lax.fori_loop(0, k, body, jnp.zeros(vals.shape, F32))


def _masked_softmax(logits, valid, axis=1):
    lm = jnp.where(valid, logits, NEG)
    p = jnp.exp2(lm - jnp.max(lm, axis=axis, keepdims=True)) * valid.astype(F32)
    return p / jnp.maximum(jnp.sum(p, axis=axis, keepdims=True), TINY)


def _moba_kernel(qt_ref, k_ref, vt_ref, km_ref, kp_ref, tab_ref, o_ref, qaug_ref, m_ref, acc_ref):
    i = pl.program_id(2)
    tq = qt_ref.shape[2]
    qt = qt_ref[0]
    row = lax.broadcasted_iota(jnp.int32, qt.shape, 0)
    zero = jnp.zeros_like(qt)
    q2 = jnp.concatenate([jnp.where(row < HEAD_DIM, qt, zero), jnp.where(row >= HEAD_DIM, qt, zero)], axis=1)
    sc = _dot(km_ref[0].astype(BF16), q2)
    blk = lax.broadcasted_iota(jnp.int32, sc.shape, 0)
    cand = blk < i
    taken = _topk_mask(jnp.where(cand, sc, NEG), MOBA_TOPK, axis=0)
    sel = ((taken > 0.0) & cand) | (blk == i)
    qaug_ref[...] = jnp.concatenate([q2, jnp.where(sel, 0.0, NEG).astype(BF16)], axis=0)
    _flash_init_t(m_ref, acc_ref)

    def logits_at(off, n):
        kaug = jnp.concatenate([k_ref[0, pl.ds(off, n), :], kp_ref[pl.ds(off, n), :]], axis=1)
        return _dot(kaug, qaug_ref[...])

    _sweep_tiles(logits_at, lambda off, n: vt_ref[0, :, pl.ds(off, n)], lambda off, n: tab_ref[0, pl.ds(off, n), :],
                 i, m_ref, acc_ref)
    o = _flash_result_t(acc_ref)
    o_ref[0] = jnp.where(row < HEAD_DIM, o[:, :tq], o[:, tq:]).astype(BF16)


def _moba_prompt(qt, mkv16, vt, kmean_pad, kp, tab):
    b, _, t = qt.shape
    tq = MOBA_BLOCK
    nq = t // tq
    hp = H_MOBA // 2
    cols = 2 * tq
    return pl.pallas_call(
        _moba_kernel,
        grid=(b, hp, nq),
        in_specs=[pl.BlockSpec((1, LANE, tq), lambda bb, h, i: (bb, h, i)),
                  pl.BlockSpec((1, t, LANE), lambda bb, h, i: (bb, 0, h)),
                  pl.BlockSpec((1, LANE, t), lambda bb, h, i: (bb, h, 0)),
                  pl.BlockSpec((1, LANE, LANE), lambda bb, h, i: (bb, 0, h)),
                  pl.BlockSpec((t, LANE), lambda bb, h, i: (0, 0), pipeline_mode=pl.Buffered(1)),
                  pl.BlockSpec((1, 4 * KEY_TILE, cols), lambda bb, h, i: (h, 0, 0), pipeline_mode=pl.Buffered(1))],
        out_specs=pl.BlockSpec((1, LANE, tq), lambda bb, h, i: (bb, h, i)),
        out_shape=jax.ShapeDtypeStruct((b, W_MOBA, t), BF16),
        scratch_shapes=[pltpu.VMEM((2 * LANE, cols), BF16),
                        pltpu.VMEM((1, cols), F32), pltpu.VMEM((LANE + ONES_ROWS, cols), F32)],
        compiler_params=_cparams(("arbitrary", "arbitrary", "arbitrary")),
        name="moba_prompt",
    )(qt, mkv16, vt, kmean_pad, kp, tab)


def _cmp_kernel(pt_ref, *refs, pps):
    del pt_ref
    pages = refs[:pps]
    (pe_ref, kwa_ref, kwb_ref, kw2_ref, vwa_ref, vwb_ref, vw2_ref, kc_ref, vc_ref,
     uk_ref, uv_ref, sk_ref, sv_ref) = refs[pps:]
    j = pl.program_id(1)
    page = pages[0].shape[2]
    groups = pps * page // CMP_STRIDE
    for u in range(pps):
        xt = pages[u][0]
        sk_ref[u * page:(u + 1) * page, :] = xt[:LANE, :].T
        sv_ref[u * page:(u + 1) * page, :] = xt[LANE:, :].T
    row0 = pl.multiple_of(j * groups, groups)
    for l in range(CMP_STRIDE):
        rows_l = pl.ds(l, groups, stride=CMP_STRIDE)
        uk_ref[pl.ds(row0, groups), l * LANE:(l + 1) * LANE] = sk_ref[rows_l, :]
        uv_ref[pl.ds(row0, groups), l * LANE:(l + 1) * LANE] = sv_ref[rows_l, :]

    @pl.when(j == pl.num_programs(1) - 1)
    def _():
        nc = uk_ref.shape[0]
        last = lax.broadcasted_iota(jnp.int32, (nc, LANE), 0) == nc - 1

        def compress(u_ref, pe, wa_ref, wb_ref, w2_ref, out_ref):
            u = u_ref[...]
            pa = _dot((u + pe[0]).astype(BF16), wa_ref[...])
            pb = _dot((u + pe[1]).astype(BF16), wb_ref[...])
            pre = pa + pltpu.roll(pb, nc - 1, 0)
            hid = pre * _sigmoid(pre)
            out = _dot(hid.astype(BF16), w2_ref[...])
            out_ref[0] = jnp.where(last, 0.0, out).astype(BF16)

        compress(uk_ref, pe_ref[0], kwa_ref, kwb_ref, kw2_ref, kc_ref)
        compress(uv_ref, pe_ref[1], vwa_ref, vwb_ref, vw2_ref, vc_ref)


def _compress(pages_arr, pt, pe, kwa, kwb, kw2, vwa, vwb, vw2):
    b, n_pg = pt.shape
    page = pages_arr.shape[2]
    pps = min(PAGES_PER_STEP, n_pg)
    nc = n_pg * page // CMP_STRIDE
    page_spec = lambda u: pl.BlockSpec((1, 2 * LANE, page), lambda bb, j, p: (p[bb, j * pps + u], 0, 0))
    full = lambda a: pl.BlockSpec(a.shape, lambda bb, j, p: (0,) * a.ndim)
    consts = (pe, kwa, kwb, kw2, vwa, vwb, vw2)
    grid_spec = pltpu.PrefetchScalarGridSpec(
        num_scalar_prefetch=1,
        grid=(b, n_pg // pps),
        in_specs=[page_spec(u) for u in range(pps)] + [full(a) for a in consts],
        out_specs=[pl.BlockSpec((1, nc, LANE), lambda bb, j, p: (bb, 0, 0))] * 2,
        scratch_shapes=[pltpu.VMEM((nc, CMP_STRIDE * LANE), F32)] * 2 + [pltpu.VMEM((pps * page, LANE), F32)] * 2)
    return pl.pallas_call(
        functools.partial(_cmp_kernel, pps=pps),
        grid_spec=grid_spec,
        out_shape=[jax.ShapeDtypeStruct((b, nc, LANE), BF16)] * 2,
        compiler_params=_cparams(("arbitrary", "arbitrary")),
        name="nsa_compress",
    )(pt, *([pages_arr] * pps), *consts)


def _nsa_kernel(qt_ref, kc_ref, vct_ref, ovt_ref, ks_ref, vst_ref, kw_ref, vwt_ref, kp_ref, ts_ref, tw_ref, gt_ref,
                o_ref, qaug_ref, m_ref, acc_ref, ocmp_ref, oslc_ref):
    k = pl.program_id(1)
    i = pl.program_id(2)
    tq = qt_ref.shape[2]
    q4 = jnp.concatenate([qt_ref[0, g * LANE:(g + 1) * LANE, :] for g in range(NSA_GROUP)], axis=1)
    qpos1 = i * tq + lax.broadcasted_iota(jnp.int32, (1, tq), 1)
    qpos = jnp.concatenate([qpos1] * NSA_GROUP, axis=1)

    lc = _dot(kc_ref[0], q4)
    tok = lax.broadcasted_iota(jnp.int32, lc.shape, 0)
    pc = _masked_softmax(lc, tok * CMP_STRIDE + (CMP_LEN - 1) <= qpos, axis=0)
    ocmp_ref[...] = _dot(vct_ref[0], pc.astype(BF16))

    pcs = pc[:, 0:tq] + pc[:, tq:2 * tq] + pc[:, 2 * tq:3 * tq] + pc[:, 3 * tq:4 * tq]
    ph, plo = _split_bf16(pcs)
    imp = _dot(ovt_ref[...], ph) + _dot(ovt_ref[...], plo)
    jb = lax.broadcasted_iota(jnp.int32, imp.shape, 0)
    cur = qpos1 >> int(math.log2(SLC_BLOCK))
    avail = jb <= cur
    forced = (jb == 0) | (jb == cur) | (jb == cur - 1)
    imp = jnp.where(avail, jnp.where(forced, FORCE, imp), NEG)
    sel = (_topk_mask(imp, SLC_TOPN, axis=0) > 0.0) & avail
    selb = jnp.where(sel, 0.0, NEG).astype(BF16)
    qaug_ref[...] = jnp.concatenate([q4, jnp.concatenate([selb] * NSA_GROUP, axis=1)], axis=0)

    cd = (i * tq) // KEY_TILE
    a0 = pl.multiple_of((i * tq) % KEY_TILE, tq)

    def table(t_ref, r0, n):
        return jnp.concatenate([t_ref[0, pl.ds(r0, n), pl.ds(g * KEY_TILE + a0, tq)] for g in range(NSA_GROUP)],
                               axis=1)

    _flash_init_t(m_ref, acc_ref)

    def slc_logits_at(off, n):
        kaug = jnp.concatenate([ks_ref[0, pl.ds(off, n), :], kp_ref[pl.ds(off, n), :]], axis=1)
        return _dot(kaug, qaug_ref[...])

    _sweep_tiles(slc_logits_at, lambda off, n: vst_ref[0, :, pl.ds(off, n)], functools.partial(table, ts_ref),
                 cd, m_ref, acc_ref)
    oslc_ref[...] = _flash_result_t(acc_ref)

    _flash_init_t(m_ref, acc_ref)
    n_win = WINDOW // KEY_TILE
    span = WINDOW + KEY_TILE
    first = jnp.maximum(cd - n_win, 0)
    off = pl.multiple_of(first * KEY_TILE, KEY_TILE)
    toff = pl.multiple_of((first - (cd - n_win)) * KEY_TILE, KEY_TILE)
    s = _dot(kw_ref[0, pl.ds(off, span), :], q4) + table(tw_ref, toff, span)
    _flash_step_t(s, vwt_ref[0, :, pl.ds(off, span)], m_ref, acc_ref)
    owin = _flash_result_t(acc_ref)

    ocmp = ocmp_ref[...]
    oslc = oslc_ref[...]
    gts = gt_ref[0]
    for g in range(NSA_GROUP):
        c = slice(g * tq, (g + 1) * tq)
        og = (gts[g:g + 1, :] * ocmp[:, c] + gts[NSA_GROUP + g:NSA_GROUP + g + 1, :] * oslc[:, c]
              + gts[2 * NSA_GROUP + g:2 * NSA_GROUP + g + 1, :] * owin[:, c])
        o_ref[0, g * HEAD_DIM:(g + 1) * HEAD_DIM, :] = jnp.where(k == 0, og[:HEAD_DIM], og[HEAD_DIM:]).astype(BF16)


def _nsa_prompt(qt, kc, vct, ovt, nkv16, vst, wkv16, vwt, kps, ts, tw, gt):
    b, _, t = qt.shape
    tq = min(NSA_TQ, t)
    nq = t // tq
    cols = NSA_GROUP * tq
    nc = kc.shape[1]
    keys = pl.BlockSpec((1, t, LANE), lambda bb, k, i: (bb, 0, 0))
    vals = pl.BlockSpec((1, LANE, t), lambda bb, k, i: (bb, 0, 0))
    return pl.pallas_call(
        _nsa_kernel,
        grid=(b, H_NSA_KV, nq),
        in_specs=[pl.BlockSpec((1, NSA_GROUP * LANE, tq), lambda bb, k, i: (bb, k, i)),
                  pl.BlockSpec((1, nc, LANE), lambda bb, k, i: (bb, 0, 0)),
                  pl.BlockSpec((1, LANE, nc), lambda bb, k, i: (bb, 0, 0)),
                  pl.BlockSpec(ovt.shape, lambda bb, k, i: (0, 0)),
                  keys, vals, keys, vals,
                  pl.BlockSpec((t, LANE), lambda bb, k, i: (0, 0), pipeline_mode=pl.Buffered(1)),
                  pl.BlockSpec((1,) + ts.shape[1:], lambda bb, k, i: (k, 0, 0), pipeline_mode=pl.Buffered(1)),
                  pl.BlockSpec((1,) + tw.shape[1:], lambda bb, k, i: (k, 0, 0), pipeline_mode=pl.Buffered(1)),
                  pl.BlockSpec((1, LANE, tq), lambda bb, k, i: (bb, k, i))],
        out_specs=pl.BlockSpec((1, NSA_GROUP * HEAD_DIM, tq), lambda bb, k, i: (bb, k, i)),
        out_shape=jax.ShapeDtypeStruct((b, W_NSA, t), BF16),
        scratch_shapes=[pltpu.VMEM((2 * LANE, cols), BF16),
                        pltpu.VMEM((1, cols), F32), pltpu.VMEM((LANE + ONES_ROWS, cols), F32),
                        pltpu.VMEM((LANE, cols), F32), pltpu.VMEM((LANE, cols), F32)],
        compiler_params=_cparams(("arbitrary", "arbitrary", "arbitrary")),
        name="nsa_prompt",
    )(qt, kc, vct, ovt, nkv16, vst, wkv16, vwt, kps, ts, tw, gt)


def _moba_dec_kernel(pt_ref, qbd_ref, new_ref, td_ref, *refs, pps, nblk):
    del pt_ref
    pages = refs[:pps]
    o_ref, sc_ref, mall_ref, lall_ref, oall_ref = refs[pps:]
    j = pl.program_id(1)
    qbd = qbd_ref[0]
    rows = qbd.shape[0]
    lane = lax.broadcasted_iota(jnp.int32, (rows, LANE), 1)
    ppb = MOBA_BLOCK // pages[0].shape[2]

    @pl.when(j == 0)
    def _():
        sc_ref[...] = jnp.zeros(sc_ref.shape, F32)
        mall_ref[...] = jnp.full(mall_ref.shape, NEG, F32)
        lall_ref[...] = jnp.zeros(lall_ref.shape, F32)

    for u in range(pps // ppb):
        blk = j * (pps // ppb) + u
        kt = jnp.concatenate([pages[u * ppb + w][0, :W_MOBA, :] for w in range(ppb)], axis=1)
        vt = jnp.concatenate([pages[u * ppb + w][0, W_MOBA:, :] for w in range(ppb)], axis=1)
        s = _dot(qbd, kt.astype(BF16))
        sc_ref[...] = jnp.where(lane == blk, jnp.sum(s, axis=1, keepdims=True), sc_ref[...])
        near = (blk == nblk - 1).astype(F32)
        s = s + near * td_ref[:, 0:MOBA_BLOCK]
        mb = jnp.max(s, axis=1, keepdims=True)
        p = jnp.exp2(s - mb)
        mall_ref[...] = jnp.where(lane == blk, mb, mall_ref[...])
        lall_ref[...] = jnp.where(lane == blk, jnp.sum(p, axis=1, keepdims=True), lall_ref[...])
        oall_ref[pl.ds(blk, 1)] = _dot_nt(p.astype(BF16), vt.astype(BF16))[None]

    @pl.when(j == pl.num_programs(1) - 1)
    def _():
        kn = new_ref[0, :, :W_MOBA].astype(BF16)
        vn = new_ref[0, :, W_MOBA:].astype(BF16)
        s_own = _dot_nt(qbd, kn) + td_ref[:, MOBA_BLOCK:MOBA_BLOCK + NEW_PAD]
        m_own = jnp.max(s_own, axis=1, keepdims=True)
        p_own = jnp.exp2(s_own - m_own)
        l_own = jnp.sum(p_own, axis=1, keepdims=True)
        o_own = _dot(p_own.astype(BF16), vn)
        sc = sc_ref[...]
        cand = lane < nblk
        sel = (_topk_mask(jnp.where(cand, sc, NEG), MOBA_TOPK) > 0.0) & cand
        mall = mall_ref[...]
        m_fin = jnp.maximum(jnp.max(jnp.where(sel, mall, NEG), axis=1, keepdims=True), m_own)
        w = jnp.where(sel, jnp.exp2(mall - m_fin), 0.0)
        w_own = jnp.exp2(m_own - m_fin)
        l = jnp.sum(w * lall_ref[...], axis=1, keepdims=True) + w_own * l_own
        acc = w_own * o_own
        for n in range(nblk):
            acc = acc + w[:, n:n + 1] * oall_ref[n]
        o = acc / jnp.maximum(l, TINY)
        rowh = lax.broadcasted_iota(jnp.int32, o.shape, 0) & (H_MOBA - 1)
        laneh = lax.broadcasted_iota(jnp.int32, o.shape, 1) >> int(math.log2(HEAD_DIM))
        o = jnp.where(rowh == laneh, o, 0.0)
        o_ref[0] = jnp.sum(o.reshape(rows // H_MOBA, H_MOBA, W_MOBA), axis=1)


def _moba_decode(cache_m, pt, qbd, new_pad, td):
    b, n_pg = pt.shape
    page = cache_m.shape[2]
    pps = min(PAGES_PER_STEP, n_pg)
    nblk = n_pg * page // MOBA_BLOCK
    rows = qbd.shape[1]
    s_new = rows // H_MOBA
    page_spec = lambda u: pl.BlockSpec((1, 2 * W_MOBA, page), lambda bb, j, p: (p[bb, j * pps + u], 0, 0))
    grid_spec = pltpu.PrefetchScalarGridSpec(
        num_scalar_prefetch=1,
        grid=(b, n_pg // pps),
        in_specs=[pl.BlockSpec((1, rows, W_MOBA), lambda bb, j, p: (bb, 0, 0)),
                  pl.BlockSpec((1, NEW_PAD, 2 * W_MOBA), lambda bb, j, p: (bb, 0, 0)),
                  pl.BlockSpec(td.shape, lambda bb, j, p: (0, 0))] + [page_spec(u) for u in range(pps)],
        out_specs=pl.BlockSpec((1, s_new, W_MOBA), lambda bb, j, p: (bb, 0, 0)),
        scratch_shapes=[pltpu.VMEM((rows, LANE), F32), pltpu.VMEM((rows, LANE), F32), pltpu.VMEM((rows, LANE), F32),
                        pltpu.VMEM((nblk, rows, W_MOBA), F32)])
    return pl.pallas_call(
        functools.partial(_moba_dec_kernel, pps=pps, nblk=nblk),
        grid_spec=grid_spec,
        out_shape=jax.ShapeDtypeStruct((b, s_new, W_MOBA), F32),
        compiler_params=_cparams(("arbitrary", "arbitrary")),
        name="moba_decode",
    )(pt, qbd, new_pad, td, *([cache_m] * pps))


def _nsa_dec_kernel(pt_ref, qd_ref, kc_ref, vc_ref, ov_ref, kp_ref, new_ref, win_ref, neww_ref, g_ref,
                    ts_ref, tso_ref, tw_ref, two_ref, *refs, pps, past):
    del pt_ref
    pages = refs[:pps]
    o_ref, qaug_ref, m_ref, l_ref, acc_ref, ocmp_ref, selown_ref = refs[pps:]
    j = pl.program_id(1)
    nj = pl.num_programs(1)
    qd = qd_ref[0]
    rows = qd.shape[0]
    page = pages[0].shape[2]
    grp = rows // NSA_GROUP
    s_new = grp // H_NSA_KV

    @pl.when(j == 0)
    def _():
        lc = _dot_nt(qd, kc_ref[0])
        tok = lax.broadcasted_iota(jnp.int32, lc.shape, 1)
        qpos = past + (lax.broadcasted_iota(jnp.int32, (rows, 1), 0) & (s_new - 1))
        pc = _masked_softmax(lc, tok * CMP_STRIDE + (CMP_LEN - 1) <= qpos)
        ocmp_ref[...] = _dot(pc.astype(BF16), vc_ref[0])
        pcs = pc[0:grp] + pc[grp:2 * grp] + pc[2 * grp:3 * grp] + pc[3 * grp:4 * grp]
        ph, plo = _split_bf16(pcs)
        imp = _dot(ph, ov_ref[...]) + _dot(plo, ov_ref[...])
        jb = lax.broadcasted_iota(jnp.int32, imp.shape, 1)
        cur = qpos[0:grp] >> int(math.log2(SLC_BLOCK))
        avail = jb <= cur
        forced = (jb == 0) | (jb == cur) | (jb == cur - 1)
        imp = jnp.where(avail, jnp.where(forced, FORCE, imp), NEG)
        sel = (_topk_mask(imp, SLC_TOPN) > 0.0) & avail
        selb = jnp.concatenate([jnp.where(sel, 0.0, NEG)] * NSA_GROUP, axis=0)
        qaug_ref[...] = jnp.concatenate([qd, selb[:, :LANE].astype(BF16)], axis=1)
        selown_ref[...] = selb[:, LANE:]
        _flash_init(m_ref, l_ref, acc_ref)

    span = pps * page
    off = pl.multiple_of(j * span, span)
    kst = jnp.concatenate([pages[u][0, :LANE, :] for u in range(pps)], axis=1).astype(BF16)
    vst = jnp.concatenate([pages[u][0, LANE:, :] for u in range(pps)], axis=1).astype(BF16)
    kaug = jnp.concatenate([kst, kp_ref[:, pl.ds(off, span)]], axis=0)
    near = (j == nj - 1).astype(F32)
    s = _dot(qaug_ref[...], kaug) + near * ts_ref[...]
    _flash_step(s, vst, m_ref, l_ref, acc_ref, feature_major=True)

    @pl.when(j == nj - 1)
    def _():
        new = new_ref[0]
        s_own = _dot_nt(qd, new[:, 2 * LANE:3 * LANE].astype(BF16)) + tso_ref[...] + selown_ref[:, 0:1]
        _flash_step(s_own, new[:, 3 * LANE:].astype(BF16), m_ref, l_ref, acc_ref)
        oslc = acc_ref[...] / jnp.maximum(l_ref[...], TINY)
        _flash_init(m_ref, l_ref, acc_ref)
        win = win_ref[0]
        _flash_step(_dot(qd, win[:LANE, :].astype(BF16)) + tw_ref[...], win[LANE:, :].astype(BF16),
                    m_ref, l_ref, acc_ref, feature_major=True)
        neww = neww_ref[0]
        _flash_step(_dot_nt(qd, neww[:, :LANE].astype(BF16)) + two_ref[...], neww[:, LANE:].astype(BF16),
                    m_ref, l_ref, acc_ref)
        owin = acc_ref[...] / jnp.maximum(l_ref[...], TINY)
        gts = g_ref[0]
        o_ref[0] = gts[:, 0:1] * ocmp_ref[...] + gts[:, 1:2] * oslc + gts[:, 2:3] * owin


def _nsa_decode(cache_n, pt, qd, kc, vc, ov, kps, new_pad, win, neww, gts, ts, tso, tw, two):
    b, n_pg = pt.shape
    page = cache_n.shape[2]
    pps = min(PAGES_PER_STEP, n_pg)
    rows = qd.shape[1]
    nc = kc.shape[1]
    full = lambda a: pl.BlockSpec(a.shape, lambda bb, j, p: (0,) * a.ndim)
    per_b = lambda a: pl.BlockSpec((1,) + a.shape[1:], lambda bb, j, p: (bb,) + (0,) * (a.ndim - 1))
    page_spec = lambda u: pl.BlockSpec((1, 2 * LANE, page), lambda bb, j, p: (p[bb, j * pps + u], 1, 0))
    grid_spec = pltpu.PrefetchScalarGridSpec(
        num_scalar_prefetch=1,
        grid=(b, n_pg // pps),
        in_specs=[per_b(qd), per_b(kc), per_b(vc), full(ov), full(kps), per_b(new_pad), per_b(win), per_b(neww),
                  per_b(gts), full(ts), full(tso), full(tw), full(two)] + [page_spec(u) for u in range(pps)],
        out_specs=pl.BlockSpec((1, rows, LANE), lambda bb, j, p: (bb, 0, 0)),
        scratch_shapes=[pltpu.VMEM((rows, 2 * LANE), BF16),
                        pltpu.VMEM((rows, 1), F32), pltpu.VMEM((rows, 1), F32), pltpu.VMEM((rows, LANE), F32),
                        pltpu.VMEM((rows, LANE), F32), pltpu.VMEM((rows, LANE), F32)])
    return pl.pallas_call(
        functools.partial(_nsa_dec_kernel, pps=pps, past=n_pg * page),
        grid_spec=grid_spec,
        out_shape=jax.ShapeDtypeStruct((b, rows, LANE), F32),
        compiler_params=_cparams(("arbitrary", "arbitrary")),
        name="nsa_decode",
    )(pt, qd, kc, vc, ov, kps, new_pad, win, neww, gts, ts, tso, tw, two, *([cache_n] * pps))


def _out_kernel(x_ref, om_ref, zm_ref, on_ref, zn_ref, gate_ref, w_ref, fg_ref, y_ref):
    zm = zm_ref[...].astype(F32)
    zn = zn_ref[...].astype(F32)
    mm = (om_ref[...].astype(F32) * (zm * _sigmoid(zm))).astype(BF16)
    mn = (on_ref[...].astype(F32) * (zn * _sigmoid(zn))).astype(BF16)
    mixed = _dot(mm, w_ref[:W_MOBA, :]) + _dot(mn, w_ref[W_MOBA:, :])
    xn = x_ref[...] + gate_ref[0] * mixed
    inv = lax.rsqrt(jnp.mean(xn * xn, axis=-1, keepdims=True) + RMS_EPS)
    y_ref[...] = (xn * inv) * fg_ref[...]


def _out_proj(x2d, om, zm, on, zn, gate, w_out, fgain, tm, tiles_per_mod):
    r, d = x2d.shape
    mrows = gate.shape[1]
    row = lambda width: pl.BlockSpec((tm, width), lambda i: (i, 0))
    return pl.pallas_call(
        _out_kernel,
        grid=(r // tm,),
        in_specs=[row(d), row(W_MOBA), row(W_MOBA), row(W_NSA), row(W_NSA),
                  pl.BlockSpec((1, mrows, d), lambda i: (i // tiles_per_mod, 0, 0)),
                  pl.BlockSpec((d, d), lambda i: (0, 0)),
                  pl.BlockSpec((1, d), lambda i: (0, 0))],
        out_specs=row(d),
        out_shape=jax.ShapeDtypeStruct((r, d), F32),
        compiler_params=_cparams(("arbitrary",)),
        name="out_proj",
    )(x2d, om, zm, on, zn, gate, w_out, fgain)


def _t5_bucket(rel):
    n = jnp.maximum(rel, 0)
    exact = N_BUCKETS // 2
    nf = jnp.maximum(n, 1).astype(F32)
    large = exact + (jnp.log(nf / exact) / math.log(MAX_DISTANCE / exact) * (N_BUCKETS - exact)).astype(jnp.int32)
    return jnp.where(n < exact, n, jnp.minimum(large, N_BUCKETS - 1))


def _bias_of_rel(tab, rel, shift_far):
    bucket = _t5_bucket(rel)[None]
    col = lambda kk: tab[kk].reshape((-1,) + (1,) * rel.ndim)
    val = jnp.broadcast_to(col(N_BUCKETS - 1), (tab.shape[1],) + rel.shape)
    for kk in range(N_BUCKETS - 1):
        val = jnp.where(bucket == kk, col(kk), val)
    if shift_far:
        val = val - col(N_BUCKETS - 1)
    return jnp.where(rel[None] >= 0, val * LOG2E, NEG)


def _layout_w_in(w_in):
    d = w_in.shape[0]
    sc = HEAD_DIM ** -0.5 * LOG2E
    o = 4 * W_MOBA
    q_m, rest_m = w_in[:, :W_MOBA] * sc, w_in[:, W_MOBA:o]
    q_n = (w_in[:, o:o + W_NSA] * sc).reshape(d, H_NSA, HEAD_DIM)
    o += W_NSA
    kv_n = w_in[:, o:o + 6 * W_NSA_KV]
    o += 6 * W_NSA_KV
    g_n = w_in[:, o:o + 3 * H_NSA]
    z_n = w_in[:, o + 3 * H_NSA:]
    zq = jnp.zeros((d, HEAD_DIM), w_in.dtype)
    qn_exp = jnp.concatenate(
        [jnp.concatenate([q_n[:, h], zq] if h // NSA_GROUP == 0 else [zq, q_n[:, h]], axis=1) for h in range(H_NSA)],
        axis=1)
    gcols = []
    for k in range(H_NSA_KV):
        cols = [g_n[:, (NSA_GROUP * k + g) * 3 + c] for c in range(3) for g in range(NSA_GROUP)]
        gcols.append(jnp.pad(jnp.stack(cols, axis=1), ((0, 0), (0, LANE - len(cols)))))
    w = jnp.concatenate([q_m, rest_m[:, :2 * W_MOBA], rest_m[:, 2 * W_MOBA:], qn_exp, kv_n, z_n] + gcols, axis=1)
    assert w.shape[1] == C_END
    return w.astype(BF16)


def _layout_cmp(pe, w1, w2):
    w = w1.reshape(2, CMP_STRIDE, HEAD_DIM, CMP_HIDDEN)
    z = jnp.zeros_like(w)
    full = jnp.stack([jnp.concatenate([w, z], axis=-1), jnp.concatenate([z, w], axis=-1)], axis=2)
    full = full.reshape(2, CMP_STRIDE * H_NSA_KV * HEAD_DIM, H_NSA_KV * CMP_HIDDEN).astype(BF16)
    zz = jnp.zeros_like(w2)
    w2bd = jnp.concatenate([jnp.concatenate([w2, zz], axis=1), jnp.concatenate([zz, w2], axis=1)], axis=0).astype(BF16)
    pe2 = jnp.broadcast_to(pe.reshape(2, CMP_STRIDE, 1, HEAD_DIM), (2, CMP_STRIDE, H_NSA_KV, HEAD_DIM))
    return pe2.reshape(2, 1, CMP_STRIDE * LANE), full[0], full[1], w2bd


def _block_onehot(t, block):
    return (jnp.arange(t)[:, None] // block == jnp.arange(LANE)[None, :]).astype(BF16)


def _overlap(nc, n_cmp, lanes):
    i = jnp.arange(nc)[:, None]
    s = jnp.arange(lanes)[None, :]
    start = i * CMP_STRIDE
    hit = (start < s * SLC_BLOCK + SLC_BLOCK) & (start + CMP_LEN - 1 >= s * SLC_BLOCK) & (i < n_cmp)
    return hit.astype(BF16)


def kernel(x_prompt, x_sample, c_prompt, c_sample, cache_moba_kv, cache_nsa_kv, state_nsa_win, page_table, w_ada, b_ada, norm_gain, w_in, cmp_pe, cmp_k_w1, cmp_k_w2, cmp_v_w1, cmp_v_w2, w_out, rel_bias, final_gain):
    depth = w_in.shape[0]
    assert depth == 1
    B, T, D = x_prompt.shape
    BS, S, _ = x_sample.shape
    n_pg = page_table.shape[1]
    page = cache_moba_kv.shape[2]
    P = n_pg * page
    WB = state_nsa_win.shape[2]
    assert D == (H_MOBA + H_NSA) * HEAD_DIM and T % KEY_TILE == 0 and P % MOBA_BLOCK == 0 and S <= NEW_PAD
    assert T // SLC_BLOCK <= LANE and P // SLC_BLOCK <= LANE and T >= WINDOW and WB == WINDOW
    assert (BS * S) % SUBLANE == 0 and S & (S - 1) == 0

    w = _layout_w_in(w_in[0])
    w_out_b = w_out[0].astype(BF16)
    gain = norm_gain[0].reshape(1, D)
    fgain = final_gain.reshape(1, D)
    kcmp_w = _layout_cmp(cmp_pe[0, 0], cmp_k_w1[0], cmp_k_w2[0])
    vcmp_w = _layout_cmp(cmp_pe[0, 1], cmp_v_w1[0], cmp_v_w2[0])
    pe2 = jnp.stack([kcmp_w[0], vcmp_w[0]], axis=0)
    cmp_consts = (pe2,) + kcmp_w[1:] + vcmp_w[1:]
    bias_m = rel_bias[:, :H_MOBA]
    bias_n = rel_bias[:, H_MOBA:]

    m_all = B + BS
    m_pad = -(-m_all // SUBLANE) * SUBLANE
    c_all = jnp.pad(jnp.concatenate([c_prompt, c_sample], axis=0), ((0, m_pad - m_all), (0, 0)))
    mod = _ada(c_all, w_ada[0], b_ada[0])
    shift, scale, gate = mod[:, :D], mod[:, D:2 * D], mod[:, 2 * D:]

    a = jnp.arange(KEY_TILE)[None, :]
    jk = jnp.arange(2 * KEY_TILE)[:, None]
    rel_s = a + KEY_TILE - jk
    heads_major = lambda tb, n_grp: jnp.transpose(
        tb.reshape(n_grp, -1, tb.shape[1], KEY_TILE), (0, 2, 1, 3)).reshape(n_grp, tb.shape[1], -1)
    rows_of = lambda tb, n, val: jnp.full((tb.shape[0], n, tb.shape[2]), val, F32)
    sweep_rows = lambda tb: jnp.concatenate([rows_of(tb, KEY_TILE, 0.0), tb, rows_of(tb, KEY_TILE, NEG)], axis=1)
    tab_m = sweep_rows(heads_major(_bias_of_rel(bias_m, rel_s, True), H_MOBA // 2))
    tab_s = sweep_rows(heads_major(_bias_of_rel(bias_n, rel_s, True), H_NSA_KV))
    jw = jnp.arange(WINDOW + KEY_TILE)[:, None]
    rel_w = a + WINDOW - jw
    tab_w = heads_major(_bias_of_rel(bias_n, jnp.where(rel_w < WINDOW, rel_w, -1), False), H_NSA_KV)
    tab_w = jnp.concatenate([tab_w, rows_of(tab_w, WINDOW, NEG)], axis=1)

    tpm = T // PROJ_TM
    (qm, mkv32, mkv16, zm, qn, nkv32, nkv16, wkv32, wkv16, zn, gts, kmean) = _proj(
        x_prompt.reshape(B * T, D), scale[:B].reshape(B, 1, D), shift[:B].reshape(B, 1, D), gain, w,
        PROJ_TM, tpm, True)
    nb = T // MOBA_BLOCK
    kmean_pad = jnp.pad(kmean.reshape(B, nb, W_MOBA), ((0, 0), (0, LANE - nb), (0, 0)))
    feat_major = lambda m: jnp.swapaxes(m.reshape(B, T, -1), 1, 2)
    mkv16 = mkv16.reshape(B, T, 2 * W_MOBA)
    o_m = _moba_prompt(feat_major(qm), mkv16, feat_major(mkv16[..., W_MOBA:]), kmean_pad,
                       _block_onehot(T, MOBA_BLOCK), tab_m)
    pt_prompt = jnp.arange(B * (T // page), dtype=jnp.int32).reshape(B, T // page)
    kvc_pages = jnp.swapaxes(nkv32.reshape(B * T // page, page, 4 * W_NSA_KV)[..., :2 * LANE], 1, 2)
    kc_p, vc_p = _compress(kvc_pages, pt_prompt, *cmp_consts)
    nc_p = T // CMP_STRIDE
    nkv16 = nkv16.reshape(B, T, 2 * LANE)
    wkv16 = wkv16.reshape(B, T, 2 * LANE)
    o_n = _nsa_prompt(feat_major(qn), kc_p, jnp.swapaxes(vc_p, 1, 2), _overlap(nc_p, nc_p - 1, LANE).T,
                      nkv16, feat_major(nkv16[..., LANE:]), wkv16, feat_major(wkv16[..., LANE:]),
                      _block_onehot(T, SLC_BLOCK), tab_s, tab_w, feat_major(gts))
    token_major = lambda m: jnp.swapaxes(m, 1, 2).reshape(B * T, -1)
    y_prompt = _out_proj(x_prompt.reshape(B * T, D), token_major(o_m), zm, token_major(o_n), zn,
                         gate[:B].reshape(B, 1, D), w_out_b, fgain, PROJ_TM, tpm).reshape(B, T, D)
    moba_kv_prompt = mkv32.reshape(1, B, T, 2, H_MOBA, HEAD_DIM)
    nsa_kv_prompt = nkv32.reshape(1, B, T, 4, H_NSA_KV, HEAD_DIM)
    win_prompt = wkv32.reshape(B, T, 2 * LANE)[:, T - WINDOW:].reshape(1, B, WINDOW, 2, H_NSA_KV, HEAD_DIM)

    RS = BS * S
    rep = lambda m: jnp.repeat(m[B:B + BS], S, axis=0).reshape(1, RS, D)
    (qm_s, mkv32_s, _, zm_s, qn_s, nkv32_s, _, wkv32_s, _, zn_s, gts_s) = _proj(
        x_sample.reshape(RS, D), rep(scale), rep(shift), gain, w, RS, 1, False)
    pad_new = lambda m: jnp.pad(m.reshape(BS, S, -1), ((0, 0), (0, NEW_PAD - S), (0, 0)))
    s_idx = jnp.arange(S)

    q_rep = jnp.repeat(qm_s.reshape(BS, S, 1, H_MOBA, HEAD_DIM), H_MOBA, axis=2)
    eye = (jnp.arange(H_MOBA)[:, None] == jnp.arange(H_MOBA)[None, :])[None, None, :, :, None]
    qbd = jnp.where(eye, q_rep, 0).reshape(BS, S * H_MOBA, W_MOBA)
    rel_d = jnp.concatenate([MOBA_BLOCK + s_idx[:, None] - jnp.arange(MOBA_BLOCK)[None, :],
                             s_idx[:, None] - jnp.arange(NEW_PAD)[None, :]], axis=1)
    td = jnp.moveaxis(_bias_of_rel(bias_m, rel_d, True), 0, 1).reshape(S * H_MOBA, -1)
    td = jnp.pad(td, ((0, 0), (0, 3 * LANE - td.shape[1])))
    feat_pages = lambda c: jnp.transpose(c, (0, 2, 3, 4, 1)).reshape(c.shape[0], -1, c.shape[1])
    o_m_s = _moba_decode(feat_pages(cache_moba_kv[0]), page_table, qbd, pad_new(mkv32_s), td)

    cache_n = feat_pages(cache_nsa_kv[0])
    kc_s, vc_s = _compress(cache_n, page_table, *cmp_consts)
    nc_s = P // CMP_STRIDE
    n_cmp_s = (P + S - CMP_LEN) // CMP_STRIDE + 1
    order = lambda m: jnp.transpose(m, (0, 3, 2, 1) + tuple(range(4, m.ndim)))
    rows_n = NSA_GROUP * H_NSA_KV * S
    qd = order(qn_s.reshape(BS, S, H_NSA_KV, NSA_GROUP, LANE)).reshape(BS, rows_n, LANE)
    g3 = gts_s.reshape(BS, S, H_NSA_KV, LANE)[..., :3 * NSA_GROUP].reshape(BS, S, H_NSA_KV, 3, NSA_GROUP)
    gts_d = order(jnp.swapaxes(g3, 3, 4)).reshape(BS, rows_n, 3)
    bias_d = jnp.swapaxes(bias_n.reshape(N_BUCKETS, H_NSA_KV, NSA_GROUP), 1, 2).reshape(N_BUCKETS, H_NSA)

    def dec_table(rel, shift_far):
        return _bias_of_rel(bias_d, rel, shift_far).reshape(rows_n, rel.shape[1])

    ts = dec_table(page + s_idx[:, None] - jnp.arange(page)[None, :], True)
    ts = jnp.pad(ts, ((0, 0), (min(PAGES_PER_STEP, n_pg) * page - page, 0)))
    rel_own = s_idx[:, None] - jnp.arange(NEW_PAD)[None, :]
    tso = dec_table(rel_own, True)
    rel_win = WB + s_idx[:, None] - jnp.arange(WB)[None, :]
    tw = dec_table(jnp.where(rel_win < WINDOW, rel_win, -1), False)
    two = dec_table(rel_own, False)
    win_state = feat_pages(state_nsa_win[0])
    o_n_raw = _nsa_decode(cache_n, page_table, qd, kc_s, vc_s, _overlap(nc_s, n_cmp_s, 2 * LANE),
                          _block_onehot(P, SLC_BLOCK).T, pad_new(nkv32_s), win_state, pad_new(wkv32_s), gts_d,
                          ts, tso, tw, two)
    o5 = o_n_raw.reshape(BS, NSA_GROUP, H_NSA_KV, S, H_NSA_KV, HEAD_DIM)
    o_n_s = jnp.stack([o5[:, :, k, :, k] for k in range(H_NSA_KV)], axis=1)
    o_n_s = jnp.transpose(o_n_s, (0, 3, 1, 2, 4)).reshape(RS, W_NSA)
    y_sample = _out_proj(x_sample.reshape(RS, D), o_m_s.reshape(RS, W_MOBA).astype(BF16), zm_s,
                         o_n_s.astype(BF16), zn_s, rep(gate), w_out_b, fgain, RS, 1).reshape(BS, S, D)
    moba_kv_sample = mkv32_s.reshape(1, BS, S, 2, H_MOBA, HEAD_DIM)
    nsa_kv_sample = nkv32_s.reshape(1, BS, S, 4, H_NSA_KV, HEAD_DIM)
    win_sample = jnp.concatenate([win_state[:, :, S:], jnp.swapaxes(wkv32_s.reshape(BS, S, 2 * LANE), 1, 2)], axis=2)
    win_sample = jnp.transpose(win_sample.reshape(1, BS, 2, H_NSA_KV, HEAD_DIM, WB), (0, 1, 5, 2, 3, 4))
    return (y_prompt, y_sample, moba_kv_prompt, moba_kv_sample, nsa_kv_prompt, nsa_kv_sample, win_prompt, win_sample)
```

```python
import functools
import math

import jax
import jax.numpy as jnp
import numpy as np
from jax import lax
from jax.experimental import pallas as pl
from jax.experimental.pallas import tpu as pltpu

F32 = jnp.float32
BF16 = jnp.bfloat16

HEAD_DIM = 64
H_MOBA = 8
H_NSA = 8
H_NSA_KV = 2
NSA_GROUP = 4
W_MOBA = H_MOBA * HEAD_DIM
W_NSA = H_NSA * HEAD_DIM
W_NSA_KV = H_NSA_KV * HEAD_DIM
MOBA_BLOCK = 256
MOBA_TOPK = 3
CMP_LEN = 32
CMP_STRIDE = 16
CMP_HIDDEN = 2 * HEAD_DIM
SLC_BLOCK = 64
SLC_TOPN = 16
WINDOW = 512
N_BUCKETS = 32
MAX_DISTANCE = 128
RMS_EPS = 1e-6
NEG = -1e30
FORCE = 1e9
TINY = 1e-30

LANE = 128
SUBLANE = 8
KEY_TILE = 256
NSA_TQ = 256
PROJ_TM = 256
PAGES_PER_STEP = 8
NEW_PAD = 8
ONES_ROWS = 16
LOG2E = math.log2(math.e)
VMEM_LIMIT = 56 * 1024 * 1024

C_QM, C_MKV, C_ZM, C_QN, C_NKV, C_WKV, C_ZN, C_G, C_END = 0, 512, 1536, 2048, 3072, 3584, 3840, 4352, 4608

_NT = (((1,), (1,)), ((), ()))


def _dot(a, b):
    return jnp.dot(a, b, preferred_element_type=F32)


def _dot_nt(a, b):
    return lax.dot_general(a, b, _NT, preferred_element_type=F32)


def _split_bf16(a):
    hi = a.astype(BF16)
    lo = (a - hi.astype(F32)).astype(BF16)
    return hi, lo


def _sigmoid(x):
    return 1.0 / (1.0 + jnp.exp(-x))


def _cparams(sem):
    return pltpu.CompilerParams(dimension_semantics=sem, vmem_limit_bytes=VMEM_LIMIT)


def _ada_kernel(c_ref, w_ref, b_ref, o_ref):
    ch, cl = _split_bf16(c_ref[...])
    wh, wl = _split_bf16(w_ref[...])
    o_ref[...] = _dot(ch, wh) + _dot(ch, wl) + _dot(cl, wh) + b_ref[...]


def _ada(c_all, w_ada, b_ada):
    m, d = c_all.shape
    n = w_ada.shape[1]
    tn = 512
    return pl.pallas_call(
        _ada_kernel,
        grid=(n // tn,),
        in_specs=[pl.BlockSpec((m, d), lambda j: (0, 0)),
                  pl.BlockSpec((d, tn), lambda j: (0, j)),
                  pl.BlockSpec((1, tn), lambda j: (0, j))],
        out_specs=pl.BlockSpec((m, tn), lambda j: (0, j)),
        out_shape=jax.ShapeDtypeStruct((m, n), F32),
        compiler_params=_cparams(("arbitrary",)),
        name="ada",
    )(c_all, w_ada, b_ada.reshape(1, n))


def _modulated_norm(x, gain, scale, shift):
    inv = lax.rsqrt(jnp.mean(x * x, axis=-1, keepdims=True) + RMS_EPS)
    return (x * inv) * gain * (1.0 + scale) + shift


def _proj_kernel(x_ref, sc_ref, sh_ref, gain_ref, w_ref, qm_ref, mkv_ref, zm_ref, qn_ref, nkv_ref, wkv_ref, zn_ref, g_ref):
    hb = _modulated_norm(x_ref[...], gain_ref[...], sc_ref[...], sh_ref[...]).astype(BF16)
    col = lambda a, b: _dot(hb, w_ref[:, a:b])
    qm_ref[...] = col(C_QM, C_MKV).astype(BF16)
    mkv_ref[...] = col(C_MKV, C_ZM)
    zm_ref[...] = col(C_ZM, C_QN).astype(BF16)
    qn_ref[...] = col(C_QN, C_NKV).astype(BF16)
    nkv_ref[...] = col(C_NKV, C_WKV)
    wkv_ref[...] = col(C_WKV, C_ZN)
    zn_ref[...] = col(C_ZN, C_G).astype(BF16)
    g_ref[...] = _sigmoid(col(C_G, C_END))


def _proj(x2d, sc, sh, gain, w):
    r, d = x2d.shape
    full = lambda width: pl.BlockSpec((r, width), lambda i: (0, 0))
    widths = [(512, BF16), (1024, F32), (512, BF16), (1024, BF16), (512, F32), (256, F32), (512, BF16), (256, F32)]
    return pl.pallas_call(
        _proj_kernel,
        grid=(1,),
        in_specs=[full(d), full(d), full(d), pl.BlockSpec((1, d), lambda i: (0, 0)),
                  pl.BlockSpec((d, C_END), lambda i: (0, 0))],
        out_specs=[full(wd) for wd, _ in widths],
        out_shape=[jax.ShapeDtypeStruct((r, wd), dt) for wd, dt in widths],
        compiler_params=_cparams(("arbitrary",)),
        name="proj_decode",
    )(x2d, sc, sh, gain, w)


R_QM, R_MKV, R_QN, R_NKV, R_WKV, R_G, R_END = 0, 512, 1536, 2560, 3072, 3328, 3584


def _proj_fm_kernel(x_ref, sc_ref, sh_ref, gain_ref, w_ref, wt_ref,
                    qmt_ref, mkvt_ref, vmt_ref, qnt_ref, nkvt_ref, vst_ref, wkvt_ref, vwt_ref, gt_ref,
                    km_ref, ksw_ref, zm_ref, zn_ref, kmean_ref):
    h = _modulated_norm(x_ref[...], gain_ref[...], sc_ref[0], sh_ref[0])
    hb = h.astype(BF16)
    ht = h.T.astype(BF16)
    frow = lambda a, b: _dot(wt_ref[a:b, :], ht)
    col = lambda a, b: _dot(hb, w_ref[:, a:b])
    qmt_ref[0] = frow(R_QM, R_MKV).astype(BF16)
    mkvt = frow(R_MKV, R_QN)
    mkvt_ref[0] = mkvt
    vmt_ref[0] = mkvt[W_MOBA:].astype(BF16)
    qnt_ref[0] = frow(R_QN, R_NKV).astype(BF16)
    nkvt = frow(R_NKV, R_WKV)
    nkvt_ref[0] = nkvt
    vst_ref[0] = nkvt[3 * W_NSA_KV:].astype(BF16)
    wkvt = frow(R_WKV, R_G)
    wkvt_ref[0] = wkvt
    vwt_ref[0] = wkvt[W_NSA_KV:].astype(BF16)
    gt_ref[0] = _sigmoid(frow(R_G, R_END))
    km = col(C_MKV, C_MKV + W_MOBA)
    km_ref[...] = km.astype(BF16)
    kmean_ref[0] = jnp.mean(km, axis=0, keepdims=True)
    ksw_ref[...] = jnp.concatenate([col(C_NKV + 2 * W_NSA_KV, C_NKV + 3 * W_NSA_KV), col(C_WKV, C_WKV + W_NSA_KV)],
                                   axis=1).astype(BF16)
    zm_ref[...] = col(C_ZM, C_QN).astype(BF16)
    zn_ref[...] = col(C_ZN, C_G).astype(BF16)


def _proj_fm(x2d, sc, sh, gain, w, wt, b, tm):
    r, d = x2d.shape
    t = r // b
    tpb = t // tm
    nt = r // tm
    row = lambda width: pl.BlockSpec((tm, width), lambda i: (i, 0))
    fm = lambda rows: pl.BlockSpec((1, rows, tm), lambda i: (i // tpb, 0, i % tpb))
    mod = pl.BlockSpec((1, 1, d), lambda i: (i // tpb, 0, 0))
    const = lambda a: pl.BlockSpec(a.shape, lambda i: (0, 0), pipeline_mode=pl.Buffered(1))
    fm_outs = [(512, BF16), (1024, F32), (512, BF16), (1024, BF16), (512, F32), (128, BF16), (256, F32), (128, BF16),
               (256, F32)]
    tm_outs = [(512, BF16), (256, BF16), (512, BF16), (512, BF16)]
    return pl.pallas_call(
        _proj_fm_kernel,
        grid=(nt,),
        in_specs=[row(d), mod, mod, pl.BlockSpec((1, d), lambda i: (0, 0)), const(w), const(wt)],
        out_specs=[fm(rows) for rows, _ in fm_outs] + [row(wd) for wd, _ in tm_outs]
        + [pl.BlockSpec((1, 1, W_MOBA), lambda i: (i, 0, 0))],
        out_shape=[jax.ShapeDtypeStruct((b, rows, t), dt) for rows, dt in fm_outs]
        + [jax.ShapeDtypeStruct((r, wd), dt) for wd, dt in tm_outs] + [jax.ShapeDtypeStruct((nt, 1, W_MOBA), F32)],
        compiler_params=_cparams(("arbitrary",)),
        name="proj_prompt",
    )(x2d, sc, sh, gain, w, wt)


def _flash_step(s, v, m_ref, l_ref, acc_ref, feature_major=False):
    m_old = m_ref[...]
    m_new = jnp.maximum(m_old, jnp.max(s, axis=1, keepdims=True))
    alpha = jnp.exp2(m_old - m_new)
    p = jnp.exp2(s - m_new)
    l_ref[...] = alpha * l_ref[...] + jnp.sum(p, axis=1, keepdims=True)
    pv = _dot_nt(p.astype(BF16), v) if feature_major else _dot(p.astype(BF16), v)
    acc_ref[...] = alpha * acc_ref[...] + pv
    m_ref[...] = m_new


def _flash_init(m_ref, l_ref, acc_ref):
    m_ref[...] = jnp.full(m_ref.shape, NEG, F32)
    l_ref[...] = jnp.zeros(l_ref.shape, F32)
    acc_ref[...] = jnp.zeros(acc_ref.shape, F32)


def _flash_step_t(s, vt, m_ref, acc_ref):
    alpha, p = _softmax_weights_t(s, m_ref)
    acc_ref[...] = alpha * acc_ref[...] + _weighted_values_t(vt, p)


def _softmax_weights_t(s, m_ref):
    m_old = m_ref[...]
    m_new = jnp.maximum(m_old, jnp.max(s, axis=0, keepdims=True))
    m_ref[...] = m_new
    return jnp.exp2(m_old - m_new), jnp.exp2(s - m_new).astype(BF16)


def _weighted_values_t(vt, p):
    vt_aug = jnp.concatenate([vt, jnp.ones((ONES_ROWS, vt.shape[1]), BF16)], axis=0)
    return _dot(vt_aug, p)


def _flash_init_t(m_ref, acc_ref):
    m_ref[...] = jnp.full(m_ref.shape, NEG, F32)
    acc_ref[...] = jnp.zeros(acc_ref.shape, F32)


def _flash_result_t(acc_ref):
    width = acc_ref.shape[0] - ONES_ROWS
    return acc_ref[0:width, :] / jnp.maximum(acc_ref[width:width + 1, :], TINY)


def _sweep_tiles(logits_at, values_at, table_at, i_diag, m_ref, acc_ref):
    big = 2 * KEY_TILE
    n_big = jnp.maximum(i_diag - 1, 0) // 2

    def far_body(n, s):
        nxt = logits_at(pl.multiple_of((n + 1) * big, big), big)
        _flash_step_t(s, values_at(pl.multiple_of(n * big, big), big), m_ref, acc_ref)
        return nxt

    s = lax.fori_loop(0, n_big, far_body, logits_at(0, big))
    odd = (i_diag & 1) == 1
    toff = jnp.where(odd, KEY_TILE, jnp.where(i_diag == 0, 2 * KEY_TILE, 0))
    _flash_step_t(s + table_at(pl.multiple_of(toff, KEY_TILE), big),
                  values_at(pl.multiple_of(n_big * big, big), big), m_ref, acc_ref)

    @pl.when((i_diag >= 2) & jnp.logical_not(odd))
    def _():
        off = pl.multiple_of(i_diag * KEY_TILE, KEY_TILE)
        _flash_step_t(logits_at(off, KEY_TILE) + table_at(2 * KEY_TILE, KEY_TILE), values_at(off, KEY_TILE),
                      m_ref, acc_ref)


def _topk_mask(vals, k, axis=1):
    idxf = lax.broadcasted_iota(jnp.int32, vals.shape, axis).astype(F32)

    def body(_, taken):
        cur = jnp.where(taken > 0.0, -jnp.inf, vals)
        mx = jnp.max(cur, axis=axis, keepdims=True)
        first = jnp.min(jnp.where(cur == mx, idxf, 1e9), axis=axis, keepdims=True)
        return jnp.where(idxf == first, 1.0, taken)

    return lax.fori_loop(0, k, body, jnp.zeros(vals.shape, F32))


def _masked_softmax(logits, valid, axis=1):
    lm = jnp.where(valid, logits, NEG)
    p = jnp.exp2(lm - jnp.max(lm, axis=axis, keepdims=True)) * valid.astype(F32)
    return p / jnp.maximum(jnp.sum(p, axis=axis, keepdims=True), TINY)


def _moba_kernel(qt_ref, k_ref, vt_ref, km_ref, kp_ref, tab_ref, o_ref, qaug_ref, m_ref, acc_ref):
    i = pl.program_id(2)
    tq = qt_ref.shape[2]
    qt = qt_ref[0]
    row = lax.broadcasted_iota(jnp.int32, qt.shape, 0)
    zero = jnp.zeros_like(qt)
    q2 = jnp.concatenate([jnp.where(row < HEAD_DIM, qt, zero), jnp.where(row >= HEAD_DIM, qt, zero)], axis=1)
    sc = _dot(km_ref[0].astype(BF16), q2)
    blk = lax.broadcasted_iota(jnp.int32, sc.shape, 0)
    cand = blk < i
    taken = _topk_mask(jnp.where(cand, sc, NEG), MOBA_TOPK, axis=0)
    sel = ((taken > 0.0) & cand) | (blk == i)
    qaug_ref[...] = jnp.concatenate([q2, jnp.where(sel, 0.0, NEG).astype(BF16)], axis=0)
    _flash_init_t(m_ref, acc_ref)

    def logits_at(off, n):
        kaug = jnp.concatenate([k_ref[0, pl.ds(off, n), :], kp_ref[pl.ds(off, n), :]], axis=1)
        return _dot(kaug, qaug_ref[...])

    _sweep_tiles(logits_at, lambda off, n: vt_ref[0, :, pl.ds(off, n)], lambda off, n: tab_ref[0, pl.ds(off, n), :],
                 i, m_ref, acc_ref)
    o = _flash_result_t(acc_ref)
    o_ref[0] = jnp.where(row < HEAD_DIM, o[:, :tq], o[:, tq:]).astype(BF16)


def _moba_prompt(qt, mkv16, vt, kmean_pad, kp, tab):
    b, _, t = qt.shape
    tq = MOBA_BLOCK
    nq = t // tq
    hp = H_MOBA // 2
    cols = 2 * tq
    return pl.pallas_call(
        _moba_kernel,
        grid=(b, hp, nq),
        in_specs=[pl.BlockSpec((1, LANE, tq), lambda bb, h, i: (bb, h, i)),
                  pl.BlockSpec((1, t, LANE), lambda bb, h, i: (bb, 0, h)),
                  pl.BlockSpec((1, LANE, t), lambda bb, h, i: (bb, h, 0)),
                  pl.BlockSpec((1, LANE, LANE), lambda bb, h, i: (bb, 0, h)),
                  pl.BlockSpec((t, LANE), lambda bb, h, i: (0, 0), pipeline_mode=pl.Buffered(1)),
                  pl.BlockSpec((1, 4 * KEY_TILE, cols), lambda bb, h, i: (h, 0, 0), pipeline_mode=pl.Buffered(1))],
        out_specs=pl.BlockSpec((1, LANE, tq), lambda bb, h, i: (bb, h, i)),
        out_shape=jax.ShapeDtypeStruct((b, W_MOBA, t), BF16),
        scratch_shapes=[pltpu.VMEM((2 * LANE, cols), BF16),
                        pltpu.VMEM((1, cols), F32), pltpu.VMEM((LANE + ONES_ROWS, cols), F32)],
        compiler_params=_cparams(("arbitrary", "arbitrary", "arbitrary")),
        name="moba_prompt",
    )(qt, mkv16, vt, kmean_pad, kp, tab)


def _cmp_kernel(pt_ref, *refs, pps):
    del pt_ref
    pages = refs[:pps]
    (pe_ref, kwa_ref, kwb_ref, kw2_ref, vwa_ref, vwb_ref, vw2_ref, kc_ref, vc_ref,
     uk_ref, uv_ref, sk_ref, sv_ref) = refs[pps:]
    j = pl.program_id(1)
    page = pages[0].shape[2]
    groups = pps * page // CMP_STRIDE
    for u in range(pps):
        xt = pages[u][0]
        sk_ref[u * page:(u + 1) * page, :] = xt[:LANE, :].T
        sv_ref[u * page:(u + 1) * page, :] = xt[LANE:, :].T
    row0 = pl.multiple_of(j * groups, groups)
    for l in range(CMP_STRIDE):
        rows_l = pl.ds(l, groups, stride=CMP_STRIDE)
        uk_ref[pl.ds(row0, groups), l * LANE:(l + 1) * LANE] = sk_ref[rows_l, :]
        uv_ref[pl.ds(row0, groups), l * LANE:(l + 1) * LANE] = sv_ref[rows_l, :]

    @pl.when(j == pl.num_programs(1) - 1)
    def _():
        nc = uk_ref.shape[0]
        last = lax.broadcasted_iota(jnp.int32, (nc, LANE), 0) == nc - 1

        def compress(u_ref, pe, wa_ref, wb_ref, w2_ref, out_ref):
            u = u_ref[...]
            pa = _dot((u + pe[0]).astype(BF16), wa_ref[...])
            pb = _dot((u + pe[1]).astype(BF16), wb_ref[...])
            pre = pa + pltpu.roll(pb, nc - 1, 0)
            hid = pre * _sigmoid(pre)
            out = _dot(hid.astype(BF16), w2_ref[...])
            out_ref[0] = jnp.where(last, 0.0, out).astype(BF16)

        compress(uk_ref, pe_ref[0], kwa_ref, kwb_ref, kw2_ref, kc_ref)
        compress(uv_ref, pe_ref[1], vwa_ref, vwb_ref, vw2_ref, vc_ref)


def _compress(pages_arr, pt, page, pe, kwa, kwb, kw2, vwa, vwb, vw2, paged):
    b, n_pg = pt.shape
    pps = min(PAGES_PER_STEP, n_pg)
    nc = n_pg * page // CMP_STRIDE
    if paged:
        page_spec = lambda u: pl.BlockSpec((1, 2 * LANE, page), lambda bb, j, p: (p[bb, j * pps + u], 0, 0))
    else:
        page_spec = lambda u: pl.BlockSpec((1, 2 * LANE, page), lambda bb, j, p: (bb, 0, j * pps + u))
    full = lambda a: pl.BlockSpec(a.shape, lambda bb, j, p: (0,) * a.ndim)
    consts = (pe, kwa, kwb, kw2, vwa, vwb, vw2)
    grid_spec = pltpu.PrefetchScalarGridSpec(
        num_scalar_prefetch=1,
        grid=(b, n_pg // pps),
        in_specs=[page_spec(u) for u in range(pps)] + [full(a) for a in consts],
        out_specs=[pl.BlockSpec((1, nc, LANE), lambda bb, j, p: (bb, 0, 0))] * 2,
        scratch_shapes=[pltpu.VMEM((nc, CMP_STRIDE * LANE), F32)] * 2 + [pltpu.VMEM((pps * page, LANE), F32)] * 2)
    return pl.pallas_call(
        functools.partial(_cmp_kernel, pps=pps),
        grid_spec=grid_spec,
        out_shape=[jax.ShapeDtypeStruct((b, nc, LANE), BF16)] * 2,
        compiler_params=_cparams(("arbitrary", "arbitrary")),
        name="nsa_compress",
    )(pt, *([pages_arr] * pps), *consts)


def _nsa_kernel(qt_ref, kc_ref, vct_ref, ovt_ref, ks_ref, vst_ref, kw_ref, vwt_ref, kp_ref, ts_ref, tw_ref, gt_ref,
                o_ref, qaug_ref, m_ref, acc_ref, ocmp_ref, oslc_ref):
    k = pl.program_id(1)
    i = pl.program_id(2)
    tq = qt_ref.shape[2]
    q4 = jnp.concatenate([qt_ref[0, g * LANE:(g + 1) * LANE, :] for g in range(NSA_GROUP)], axis=1)
    qpos1 = i * tq + lax.broadcasted_iota(jnp.int32, (1, tq), 1)
    qpos = jnp.concatenate([qpos1] * NSA_GROUP, axis=1)

    lc = _dot(kc_ref[0], q4)
    tok = lax.broadcasted_iota(jnp.int32, lc.shape, 0)
    pc = _masked_softmax(lc, tok * CMP_STRIDE + (CMP_LEN - 1) <= qpos, axis=0)
    ocmp_ref[...] = _dot(vct_ref[0], pc.astype(BF16))

    pcs = pc[:, 0:tq] + pc[:, tq:2 * tq] + pc[:, 2 * tq:3 * tq] + pc[:, 3 * tq:4 * tq]
    ph, plo = _split_bf16(pcs)
    imp = _dot(ovt_ref[...], ph) + _dot(ovt_ref[...], plo)
    jb = lax.broadcasted_iota(jnp.int32, imp.shape, 0)
    cur = qpos1 >> int(math.log2(SLC_BLOCK))
    avail = jb <= cur
    forced = (jb == 0) | (jb == cur) | (jb == cur - 1)
    imp = jnp.where(avail, jnp.where(forced, FORCE, imp), NEG)
    sel = (_topk_mask(imp, SLC_TOPN, axis=0) > 0.0) & avail
    selb = jnp.where(sel, 0.0, NEG).astype(BF16)
    qaug_ref[...] = jnp.concatenate([q4, jnp.concatenate([selb] * NSA_GROUP, axis=1)], axis=0)

    cd = (i * tq) // KEY_TILE
    a0 = pl.multiple_of((i * tq) % KEY_TILE, tq)

    def table(t_ref, r0, n):
        return jnp.concatenate([t_ref[0, pl.ds(r0, n), pl.ds(g * KEY_TILE + a0, tq)] for g in range(NSA_GROUP)],
                               axis=1)

    _flash_init_t(m_ref, acc_ref)

    def slc_logits_at(off, n):
        kaug = jnp.concatenate([ks_ref[0, pl.ds(off, n), :], kp_ref[pl.ds(off, n), :]], axis=1)
        return _dot(kaug, qaug_ref[...])

    _sweep_tiles(slc_logits_at, lambda off, n: vst_ref[0, :, pl.ds(off, n)], functools.partial(table, ts_ref),
                 cd, m_ref, acc_ref)
    oslc_ref[...] = _flash_result_t(acc_ref)

    _flash_init_t(m_ref, acc_ref)
    n_win = WINDOW // KEY_TILE
    span = WINDOW + KEY_TILE
    first = jnp.maximum(cd - n_win, 0)
    off = pl.multiple_of(first * KEY_TILE, KEY_TILE)
    toff = pl.multiple_of((first - (cd - n_win)) * KEY_TILE, KEY_TILE)
    s = _dot(kw_ref[0, pl.ds(off, span), :], q4) + table(tw_ref, toff, span)
    _flash_step_t(s, vwt_ref[0, :, pl.ds(off, span)], m_ref, acc_ref)
    owin = _flash_result_t(acc_ref)

    ocmp = ocmp_ref[...]
    oslc = oslc_ref[...]
    gts = gt_ref[0]
    for g in range(NSA_GROUP):
        c = slice(g * tq, (g + 1) * tq)
        og = (gts[g:g + 1, :] * ocmp[:, c] + gts[NSA_GROUP + g:NSA_GROUP + g + 1, :] * oslc[:, c]
              + gts[2 * NSA_GROUP + g:2 * NSA_GROUP + g + 1, :] * owin[:, c])
        o_ref[0, g * HEAD_DIM:(g + 1) * HEAD_DIM, :] = jnp.where(k == 0, og[:HEAD_DIM], og[HEAD_DIM:]).astype(BF16)


def _nsa_prompt(qt, kc, vct, ovt, ksw, vst, vwt, kps, ts, tw, gt):
    b, _, t = qt.shape
    tq = min(NSA_TQ, t)
    nq = t // tq
    cols = NSA_GROUP * tq
    nc = kc.shape[1]
    keys = lambda col: pl.BlockSpec((1, t, LANE), lambda bb, k, i: (bb, 0, col))
    vals = pl.BlockSpec((1, LANE, t), lambda bb, k, i: (bb, 0, 0))
    return pl.pallas_call(
        _nsa_kernel,
        grid=(b, H_NSA_KV, nq),
        in_specs=[pl.BlockSpec((1, NSA_GROUP * LANE, tq), lambda bb, k, i: (bb, k, i)),
                  pl.BlockSpec((1, nc, LANE), lambda bb, k, i: (bb, 0, 0)),
                  pl.BlockSpec((1, LANE, nc), lambda bb, k, i: (bb, 0, 0)),
                  pl.BlockSpec(ovt.shape, lambda bb, k, i: (0, 0)),
                  keys(0), vals, keys(1), vals,
                  pl.BlockSpec((t, LANE), lambda bb, k, i: (0, 0), pipeline_mode=pl.Buffered(1)),
                  pl.BlockSpec((1,) + ts.shape[1:], lambda bb, k, i: (k, 0, 0), pipeline_mode=pl.Buffered(1)),
                  pl.BlockSpec((1,) + tw.shape[1:], lambda bb, k, i: (k, 0, 0), pipeline_mode=pl.Buffered(1)),
                  pl.BlockSpec((1, LANE, tq), lambda bb, k, i: (bb, k, i))],
        out_specs=pl.BlockSpec((1, NSA_GROUP * HEAD_DIM, tq), lambda bb, k, i: (bb, k, i)),
        out_shape=jax.ShapeDtypeStruct((b, W_NSA, t), BF16),
        scratch_shapes=[pltpu.VMEM((2 * LANE, cols), BF16),
                        pltpu.VMEM((1, cols), F32), pltpu.VMEM((LANE + ONES_ROWS, cols), F32),
                        pltpu.VMEM((LANE, cols), F32), pltpu.VMEM((LANE, cols), F32)],
        compiler_params=_cparams(("arbitrary", "arbitrary", "arbitrary")),
        name="nsa_prompt",
    )(qt, kc, vct, ovt, ksw, vst, ksw, vwt, kps, ts, tw, gt)


def _moba_dec_kernel(pt_ref, qbd_ref, new_ref, td_ref, *refs, pps, nblk):
    del pt_ref
    pages = refs[:pps]
    o_ref, sc_ref, mall_ref, lall_ref, oall_ref = refs[pps:]
    j = pl.program_id(1)
    qbd = qbd_ref[0]
    rows = qbd.shape[0]
    lane = lax.broadcasted_iota(jnp.int32, (rows, LANE), 1)
    ppb = MOBA_BLOCK // pages[0].shape[2]

    @pl.when(j == 0)
    def _():
        sc_ref[...] = jnp.zeros(sc_ref.shape, F32)
        mall_ref[...] = jnp.full(mall_ref.shape, NEG, F32)
        lall_ref[...] = jnp.zeros(lall_ref.shape, F32)

    for u in range(pps // ppb):
        blk = j * (pps // ppb) + u
        kt = jnp.concatenate([pages[u * ppb + w][0, :W_MOBA, :] for w in range(ppb)], axis=1)
        vt = jnp.concatenate([pages[u * ppb + w][0, W_MOBA:, :] for w in range(ppb)], axis=1)
        s = _dot(qbd, kt.astype(BF16))
        sc_ref[...] = jnp.where(lane == blk, jnp.sum(s, axis=1, keepdims=True), sc_ref[...])
        near = (blk == nblk - 1).astype(F32)
        s = s + near * td_ref[:, 0:MOBA_BLOCK]
        mb = jnp.max(s, axis=1, keepdims=True)
        p = jnp.exp2(s - mb)
        mall_ref[...] = jnp.where(lane == blk, mb, mall_ref[...])
        lall_ref[...] = jnp.where(lane == blk, jnp.sum(p, axis=1, keepdims=True), lall_ref[...])
        oall_ref[pl.ds(blk, 1)] = _dot_nt(p.astype(BF16), vt.astype(BF16))[None]

    @pl.when(j == pl.num_programs(1) - 1)
    def _():
        kn = new_ref[0, :, :W_MOBA].astype(BF16)
        vn = new_ref[0, :, W_MOBA:].astype(BF16)
        s_own = _dot_nt(qbd, kn) + td_ref[:, MOBA_BLOCK:MOBA_BLOCK + NEW_PAD]
        m_own = jnp.max(s_own, axis=1, keepdims=True)
        p_own = jnp.exp2(s_own - m_own)
        l_own = jnp.sum(p_own, axis=1, keepdims=True)
        o_own = _dot(p_own.astype(BF16), vn)
        sc = sc_ref[...]
        cand = lane < nblk
        sel = (_topk_mask(jnp.where(cand, sc, NEG), MOBA_TOPK) > 0.0) & cand
        mall = mall_ref[...]
        m_fin = jnp.maximum(jnp.max(jnp.where(sel, mall, NEG), axis=1, keepdims=True), m_own)
        w = jnp.where(sel, jnp.exp2(mall - m_fin), 0.0)
        w_own = jnp.exp2(m_own - m_fin)
        l = jnp.sum(w * lall_ref[...], axis=1, keepdims=True) + w_own * l_own
        acc = w_own * o_own
        for n in range(nblk):
            acc = acc + w[:, n:n + 1] * oall_ref[n]
        o = acc / jnp.maximum(l, TINY)
        rowh = lax.broadcasted_iota(jnp.int32, o.shape, 0) & (H_MOBA - 1)
        laneh = lax.broadcasted_iota(jnp.int32, o.shape, 1) >> int(math.log2(HEAD_DIM))
        o = jnp.where(rowh == laneh, o, 0.0)
        o_ref[0] = jnp.sum(o.reshape(rows // H_MOBA, H_MOBA, W_MOBA), axis=1)


def _moba_decode(cache_m, pt, qbd, new_pad, td):
    b, n_pg = pt.shape
    page = cache_m.shape[2]
    pps = min(PAGES_PER_STEP, n_pg)
    nblk = n_pg * page // MOBA_BLOCK
    rows = qbd.shape[1]
    s_new = rows // H_MOBA
    page_spec = lambda u: pl.BlockSpec((1, 2 * W_MOBA, page), lambda bb, j, p: (p[bb, j * pps + u], 0, 0))
    grid_spec = pltpu.PrefetchScalarGridSpec(
        num_scalar_prefetch=1,
        grid=(b, n_pg // pps),
        in_specs=[pl.BlockSpec((1, rows, W_MOBA), lambda bb, j, p: (bb, 0, 0)),
                  pl.BlockSpec((1, NEW_PAD, 2 * W_MOBA), lambda bb, j, p: (bb, 0, 0)),
                  pl.BlockSpec(td.shape, lambda bb, j, p: (0, 0))] + [page_spec(u) for u in range(pps)],
        out_specs=pl.BlockSpec((1, s_new, W_MOBA), lambda bb, j, p: (bb, 0, 0)),
        scratch_shapes=[pltpu.VMEM((rows, LANE), F32), pltpu.VMEM((rows, LANE), F32), pltpu.VMEM((rows, LANE), F32),
                        pltpu.VMEM((nblk, rows, W_MOBA), F32)])
    return pl.pallas_call(
        functools.partial(_moba_dec_kernel, pps=pps, nblk=nblk),
        grid_spec=grid_spec,
        out_shape=jax.ShapeDtypeStruct((b, s_new, W_MOBA), F32),
        compiler_params=_cparams(("arbitrary", "arbitrary")),
        name="moba_decode",
    )(pt, qbd, new_pad, td, *([cache_m] * pps))


def _nsa_dec_kernel(pt_ref, qd_ref, kc_ref, vc_ref, ov_ref, kp_ref, new_ref, win_ref, neww_ref, g_ref,
                    ts_ref, tso_ref, tw_ref, two_ref, *refs, pps, past):
    del pt_ref
    pages = refs[:pps]
    o_ref, qaug_ref, m_ref, l_ref, acc_ref, ocmp_ref, selown_ref = refs[pps:]
    j = pl.program_id(1)
    nj = pl.num_programs(1)
    qd = qd_ref[0]
    rows = qd.shape[0]
    page = pages[0].shape[2]
    grp = rows // NSA_GROUP
    s_new = grp // H_NSA_KV

    @pl.when(j == 0)
    def _():
        lc = _dot_nt(qd, kc_ref[0])
        tok = lax.broadcasted_iota(jnp.int32, lc.shape, 1)
        qpos = past + (lax.broadcasted_iota(jnp.int32, (rows, 1), 0) & (s_new - 1))
        pc = _masked_softmax(lc, tok * CMP_STRIDE + (CMP_LEN - 1) <= qpos)
        ocmp_ref[...] = _dot(pc.astype(BF16), vc_ref[0])
        pcs = pc[0:grp] + pc[grp:2 * grp] + pc[2 * grp:3 * grp] + pc[3 * grp:4 * grp]
        ph, plo = _split_bf16(pcs)
        imp = _dot(ph, ov_ref[...]) + _dot(plo, ov_ref[...])
        jb = lax.broadcasted_iota(jnp.int32, imp.shape, 1)
        cur = qpos[0:grp] >> int(math.log2(SLC_BLOCK))
        avail = jb <= cur
        forced = (jb == 0) | (jb == cur) | (jb == cur - 1)
        imp = jnp.where(avail, jnp.where(forced, FORCE, imp), NEG)
        sel = (_topk_mask(imp, SLC_TOPN) > 0.0) & avail
        selb = jnp.concatenate([jnp.where(sel, 0.0, NEG)] * NSA_GROUP, axis=0)
        qaug_ref[...] = jnp.concatenate([qd, selb[:, :LANE].astype(BF16)], axis=1)
        n_own = past // SLC_BLOCK
        selown_ref[...] = jnp.broadcast_to(selb[:, n_own:n_own + 1], selown_ref.shape)
        _flash_init(m_ref, l_ref, acc_ref)

    span = pps * page
    off = pl.multiple_of(j * span, span)
    kst = jnp.concatenate([pages[u][0, :LANE, :] for u in range(pps)], axis=1).astype(BF16)
    vst = jnp.concatenate([pages[u][0, LANE:, :] for u in range(pps)], axis=1).astype(BF16)
    kaug = jnp.concatenate([kst, kp_ref[:, pl.ds(off, span)]], axis=0)
    near = (j == nj - 1).astype(F32)
    s = _dot(qaug_ref[...], kaug) + near * ts_ref[...]
    _flash_step(s, vst, m_ref, l_ref, acc_ref, feature_major=True)

    @pl.when(j == nj - 1)
    def _():
        new = new_ref[0]
        s_own = _dot_nt(qd, new[:, 2 * LANE:3 * LANE].astype(BF16)) + tso_ref[...] + selown_ref[:, 0:1]
        _flash_step(s_own, new[:, 3 * LANE:].astype(BF16), m_ref, l_ref, acc_ref)
        oslc = acc_ref[...] / jnp.maximum(l_ref[...], TINY)
        _flash_init(m_ref, l_ref, acc_ref)
        win = win_ref[0]
        _flash_step(_dot(qd, win[:LANE, :].astype(BF16)) + tw_ref[...], win[LANE:, :].astype(BF16),
                    m_ref, l_ref, acc_ref, feature_major=True)
        neww = neww_ref[0]
        _flash_step(_dot_nt(qd, neww[:, :LANE].astype(BF16)) + two_ref[...], neww[:, LANE:].astype(BF16),
                    m_ref, l_ref, acc_ref)
        owin = acc_ref[...] / jnp.maximum(l_ref[...], TINY)
        gts = g_ref[0]
        o_ref[0] = gts[:, 0:1] * ocmp_ref[...] + gts[:, 1:2] * oslc + gts[:, 2:3] * owin


def _nsa_decode(cache_n, pt, qd, kc, vc, ov, kps, new_pad, win, neww, gts, ts, tso, tw, two):
    b, n_pg = pt.shape
    page = cache_n.shape[2]
    pps = min(PAGES_PER_STEP, n_pg)
    rows = qd.shape[1]
    nc = kc.shape[1]
    full = lambda a: pl.BlockSpec(a.shape, lambda bb, j, p: (0,) * a.ndim)
    per_b = lambda a: pl.BlockSpec((1,) + a.shape[1:], lambda bb, j, p: (bb,) + (0,) * (a.ndim - 1))
    page_spec = lambda u: pl.BlockSpec((1, 2 * LANE, page), lambda bb, j, p: (p[bb, j * pps + u], 1, 0))
    grid_spec = pltpu.PrefetchScalarGridSpec(
        num_scalar_prefetch=1,
        grid=(b, n_pg // pps),
        in_specs=[per_b(qd), per_b(kc), per_b(vc), full(ov), full(kps), per_b(new_pad), per_b(win), per_b(neww),
                  per_b(gts), full(ts), full(tso), full(tw), full(two)] + [page_spec(u) for u in range(pps)],
        out_specs=pl.BlockSpec((1, rows, LANE), lambda bb, j, p: (bb, 0, 0)),
        scratch_shapes=[pltpu.VMEM((rows, 2 * LANE), BF16),
                        pltpu.VMEM((rows, 1), F32), pltpu.VMEM((rows, 1), F32), pltpu.VMEM((rows, LANE), F32),
                        pltpu.VMEM((rows, LANE), F32), pltpu.VMEM((rows, LANE), F32)])
    return pl.pallas_call(
        functools.partial(_nsa_dec_kernel, pps=pps, past=n_pg * page),
        grid_spec=grid_spec,
        out_shape=jax.ShapeDtypeStruct((b, rows, LANE), F32),
        compiler_params=_cparams(("arbitrary", "arbitrary")),
        name="nsa_decode",
    )(pt, qd, kc, vc, ov, kps, new_pad, win, neww, gts, ts, tso, tw, two, *([cache_n] * pps))


def _out_kernel(x_ref, om_ref, zm_ref, on_ref, zn_ref, gate_ref, w_ref, fg_ref, y_ref, *, feature_major):
    zm = zm_ref[...].astype(F32)
    zn = zn_ref[...].astype(F32)
    if feature_major:
        om = om_ref[0].astype(F32).T
        on = on_ref[0].astype(F32).T
    else:
        om = om_ref[...].astype(F32)
        on = on_ref[...].astype(F32)
    mm = (om * (zm * _sigmoid(zm))).astype(BF16)
    mn = (on * (zn * _sigmoid(zn))).astype(BF16)
    mixed = _dot(mm, w_ref[:W_MOBA, :]) + _dot(mn, w_ref[W_MOBA:, :])
    xn = x_ref[...] + gate_ref[0] * mixed
    inv = lax.rsqrt(jnp.mean(xn * xn, axis=-1, keepdims=True) + RMS_EPS)
    y_ref[...] = (xn * inv) * fg_ref[...]


def _out_proj(x2d, om, zm, on, zn, gate, w_out, fgain, tm, tiles_per_mod):
    r, d = x2d.shape
    mrows = gate.shape[1]
    feature_major = om.ndim == 3
    row = lambda width: pl.BlockSpec((tm, width), lambda i: (i, 0))
    if feature_major:
        mixer = pl.BlockSpec((1, om.shape[1], tm), lambda i: (i // tiles_per_mod, 0, i % tiles_per_mod))
    else:
        mixer = row(W_MOBA)
    return pl.pallas_call(
        functools.partial(_out_kernel, feature_major=feature_major),
        grid=(r // tm,),
        in_specs=[row(d), mixer, row(W_MOBA), mixer, row(W_NSA),
                  pl.BlockSpec((1, mrows, d), lambda i: (i // tiles_per_mod, 0, 0)),
                  pl.BlockSpec((d, d), lambda i: (0, 0)),
                  pl.BlockSpec((1, d), lambda i: (0, 0))],
        out_specs=row(d),
        out_shape=jax.ShapeDtypeStruct((r, d), F32),
        compiler_params=_cparams(("arbitrary",)),
        name="out_proj",
    )(x2d, om, zm, on, zn, gate, w_out, fgain)


def _t5_bucket(rel):
    n = np.maximum(rel, 0)
    exact = N_BUCKETS // 2
    x = np.log(np.maximum(n, 1) / exact) / math.log(MAX_DISTANCE / exact) * (N_BUCKETS - exact)
    near_boundary = (np.abs(x - np.round(x)) < 1e-3) & (n > exact) & (n < MAX_DISTANCE)
    assert not near_boundary.any()
    large = exact + np.floor(x + 1e-6).astype(np.int64)
    return np.where(n < exact, n, np.minimum(large, N_BUCKETS - 1)).astype(np.int32)


def _bias_of_rel(tab, rel, shift_far):
    onehot = (jnp.asarray(_t5_bucket(rel))[..., None] == jnp.arange(N_BUCKETS, dtype=jnp.int32)).astype(F32)
    val = jnp.moveaxis(jnp.dot(onehot, tab, precision=lax.Precision.HIGHEST), -1, 0)
    if shift_far:
        val = val - tab[N_BUCKETS - 1].reshape((-1,) + (1,) * rel.ndim)
    return jnp.where(jnp.asarray(rel >= 0)[None], val * LOG2E, NEG)


def _layout_w_in(w_in):
    d = w_in.shape[0]
    sc = HEAD_DIM ** -0.5 * LOG2E
    o = 4 * W_MOBA
    q_m, rest_m = w_in[:, :W_MOBA] * sc, w_in[:, W_MOBA:o]
    q_n = (w_in[:, o:o + W_NSA] * sc).reshape(d, H_NSA_KV, NSA_GROUP, HEAD_DIM)
    qn_exp = jnp.concatenate([jnp.pad(q_n[:, 0], ((0, 0), (0, 0), (0, HEAD_DIM))),
                              jnp.pad(q_n[:, 1], ((0, 0), (0, 0), (HEAD_DIM, 0)))], axis=1).reshape(d, H_NSA * LANE)
    o += W_NSA
    kv_n = w_in[:, o:o + 6 * W_NSA_KV]
    o += 6 * W_NSA_KV
    g_n = jnp.swapaxes(w_in[:, o:o + 3 * H_NSA].reshape(d, H_NSA_KV, NSA_GROUP, 3), 2, 3)
    g_n = jnp.pad(g_n.reshape(d, H_NSA_KV, 3 * NSA_GROUP), ((0, 0), (0, 0), (0, LANE - 3 * NSA_GROUP)))
    z_n = w_in[:, o + 3 * H_NSA:]
    w = jnp.concatenate([q_m, rest_m, qn_exp, kv_n, z_n, g_n.reshape(d, H_NSA_KV * LANE)], axis=1).astype(BF16)
    assert w.shape[1] == C_END
    wt = jnp.concatenate([w[:, C_QM:C_ZM], w[:, C_QN:C_ZN], w[:, C_G:C_END]], axis=1).T
    assert wt.shape[0] == R_END
    return w, wt


def _layout_cmp(pe, w1, w2):
    w = w1.reshape(2, CMP_STRIDE, HEAD_DIM, CMP_HIDDEN)
    z = jnp.zeros_like(w)
    full = jnp.stack([jnp.concatenate([w, z], axis=-1), jnp.concatenate([z, w], axis=-1)], axis=2)
    full = full.reshape(2, CMP_STRIDE * H_NSA_KV * HEAD_DIM, H_NSA_KV * CMP_HIDDEN).astype(BF16)
    zz = jnp.zeros_like(w2)
    w2bd = jnp.concatenate([jnp.concatenate([w2, zz], axis=1), jnp.concatenate([zz, w2], axis=1)], axis=0).astype(BF16)
    pe2 = jnp.broadcast_to(pe.reshape(2, CMP_STRIDE, 1, HEAD_DIM), (2, CMP_STRIDE, H_NSA_KV, HEAD_DIM))
    return pe2.reshape(2, 1, CMP_STRIDE * LANE), full[0], full[1], w2bd


def _block_onehot(t, block):
    return (jnp.arange(t)[:, None] // block == jnp.arange(LANE)[None, :]).astype(BF16)


def _overlap(nc, n_cmp, lanes):
    i = jnp.arange(nc)[:, None]
    s = jnp.arange(lanes)[None, :]
    start = i * CMP_STRIDE
    hit = (start < s * SLC_BLOCK + SLC_BLOCK) & (start + CMP_LEN - 1 >= s * SLC_BLOCK) & (i < n_cmp)
    return hit.astype(BF16)


def kernel(x_prompt, x_sample, c_prompt, c_sample, cache_moba_kv, cache_nsa_kv, state_nsa_win, page_table, w_ada, b_ada, norm_gain, w_in, cmp_pe, cmp_k_w1, cmp_k_w2, cmp_v_w1, cmp_v_w2, w_out, rel_bias, final_gain):
    depth = w_in.shape[0]
    assert depth == 1
    B, T, D = x_prompt.shape
    BS, S, _ = x_sample.shape
    n_pg = page_table.shape[1]
    page = cache_moba_kv.shape[2]
    P = n_pg * page
    WB = state_nsa_win.shape[2]
    assert D == (H_MOBA + H_NSA) * HEAD_DIM and T % KEY_TILE == 0 and P % MOBA_BLOCK == 0 and S <= NEW_PAD
    assert T // SLC_BLOCK <= LANE and P // SLC_BLOCK <= LANE and T >= WINDOW and WB == WINDOW
    assert (BS * S) % SUBLANE == 0 and S & (S - 1) == 0

    w, wt = _layout_w_in(w_in[0])
    w_out_b = w_out[0].astype(BF16)
    gain = norm_gain[0].reshape(1, D)
    fgain = final_gain.reshape(1, D)
    kcmp_w = _layout_cmp(cmp_pe[0, 0], cmp_k_w1[0], cmp_k_w2[0])
    vcmp_w = _layout_cmp(cmp_pe[0, 1], cmp_v_w1[0], cmp_v_w2[0])
    pe2 = jnp.stack([kcmp_w[0], vcmp_w[0]], axis=0)
    cmp_consts = (pe2,) + kcmp_w[1:] + vcmp_w[1:]
    bias_m = rel_bias[:, :H_MOBA]
    bias_n = rel_bias[:, H_MOBA:]

    m_all = B + BS
    m_pad = -(-m_all // SUBLANE) * SUBLANE
    c_all = jnp.pad(jnp.concatenate([c_prompt, c_sample], axis=0), ((0, m_pad - m_all), (0, 0)))
    mod = _ada(c_all, w_ada[0], b_ada[0])
    shift, scale, gate = mod[:, :D], mod[:, D:2 * D], mod[:, 2 * D:]

    a = np.arange(KEY_TILE)[None, :]
    jk = np.arange(2 * KEY_TILE)[:, None]
    rel_s = a + KEY_TILE - jk
    heads_major = lambda tb, n_grp: jnp.transpose(
        tb.reshape(n_grp, -1, tb.shape[1], KEY_TILE), (0, 2, 1, 3)).reshape(n_grp, tb.shape[1], -1)
    rows_of = lambda tb, n, val: jnp.full((tb.shape[0], n, tb.shape[2]), val, F32)
    sweep_rows = lambda tb: jnp.concatenate([rows_of(tb, KEY_TILE, 0.0), tb, rows_of(tb, KEY_TILE, NEG)], axis=1)
    tab_m = sweep_rows(heads_major(_bias_of_rel(bias_m, rel_s, True), H_MOBA // 2))
    tab_s = sweep_rows(heads_major(_bias_of_rel(bias_n, rel_s, True), H_NSA_KV))
    jw = np.arange(WINDOW + KEY_TILE)[:, None]
    rel_w = a + WINDOW - jw
    tab_w = heads_major(_bias_of_rel(bias_n, np.where(rel_w < WINDOW, rel_w, -1), False), H_NSA_KV)
    tab_w = jnp.concatenate([tab_w, rows_of(tab_w, WINDOW, NEG)], axis=1)

    tpm = T // PROJ_TM
    (qmt, mkvt, vmt, qnt, nkvt, vst, wkvt, vwt, gt, km, ksw, zm, zn, kmean) = _proj_fm(
        x_prompt.reshape(B * T, D), scale[:B].reshape(B, 1, D), shift[:B].reshape(B, 1, D), gain, w, wt, B, PROJ_TM)
    nb = T // MOBA_BLOCK
    kmean_pad = jnp.pad(kmean.reshape(B, nb, W_MOBA), ((0, 0), (0, LANE - nb), (0, 0)))
    o_m = _moba_prompt(qmt, km.reshape(B, T, W_MOBA), vmt, kmean_pad, _block_onehot(T, MOBA_BLOCK), tab_m)
    pt_prompt = jnp.zeros((B, T // page), jnp.int32)
    kc_p, vc_p = _compress(nkvt, pt_prompt, page, *cmp_consts, paged=False)
    nc_p = T // CMP_STRIDE
    o_n = _nsa_prompt(qnt, kc_p, jnp.swapaxes(vc_p, 1, 2), _overlap(nc_p, nc_p - 1, LANE).T,
                      ksw.reshape(B, T, 2 * LANE), vst, vwt, _block_onehot(T, SLC_BLOCK), tab_s, tab_w, gt)
    y_prompt = _out_proj(x_prompt.reshape(B * T, D), o_m, zm, o_n, zn,
                         gate[:B].reshape(B, 1, D), w_out_b, fgain, PROJ_TM, tpm).reshape(B, T, D)
    tokens_first = lambda m, c: jnp.transpose(m.reshape(1, B, c, -1, HEAD_DIM, m.shape[-1]), (0, 1, 5, 2, 3, 4))
    moba_kv_prompt = tokens_first(mkvt, 2)
    nsa_kv_prompt = tokens_first(nkvt, 4)
    win_prompt = tokens_first(wkvt[:, :, T - WINDOW:], 2)

    RS = BS * S
    rep = lambda m: jnp.repeat(m[B:B + BS], S, axis=0)
    (qm_s, mkv32_s, zm_s, qn_s, nkv32_s, wkv32_s, zn_s, gts_s) = _proj(
        x_sample.reshape(RS, D), rep(scale), rep(shift), gain, w)
    pad_new = lambda m: jnp.pad(m.reshape(BS, S, -1), ((0, 0), (0, NEW_PAD - S), (0, 0)))
    s_idx = np.arange(S)

    q_rep = jnp.repeat(qm_s.reshape(BS, S, 1, H_MOBA, HEAD_DIM), H_MOBA, axis=2)
    eye = (jnp.arange(H_MOBA)[:, None] == jnp.arange(H_MOBA)[None, :])[None, None, :, :, None]
    qbd = jnp.where(eye, q_rep, 0).reshape(BS, S * H_MOBA, W_MOBA)
    rel_d = np.concatenate([MOBA_BLOCK + s_idx[:, None] - np.arange(MOBA_BLOCK)[None, :],
                            s_idx[:, None] - np.arange(NEW_PAD)[None, :]], axis=1)
    td = jnp.moveaxis(_bias_of_rel(bias_m, rel_d, True), 0, 1).reshape(S * H_MOBA, -1)
    td = jnp.pad(td, ((0, 0), (0, 3 * LANE - td.shape[1])))
    feat_pages = lambda c: jnp.transpose(c, (0, 2, 3, 4, 1)).reshape(c.shape[0], -1, c.shape[1])
    o_m_s = _moba_decode(feat_pages(cache_moba_kv[0]), page_table, qbd, pad_new(mkv32_s), td)

    cache_n = feat_pages(cache_nsa_kv[0])
    kc_s, vc_s = _compress(cache_n, page_table, page, *cmp_consts, paged=True)
    nc_s = P // CMP_STRIDE
    n_cmp_s = (P + S - CMP_LEN) // CMP_STRIDE + 1
    order = lambda m: jnp.transpose(m, (0, 3, 2, 1) + tuple(range(4, m.ndim)))
    rows_n = NSA_GROUP * H_NSA_KV * S
    qd = order(qn_s.reshape(BS, S, H_NSA_KV, NSA_GROUP, LANE)).reshape(BS, rows_n, LANE)
    g3 = gts_s.reshape(BS, S, H_NSA_KV, LANE)[..., :3 * NSA_GROUP].reshape(BS, S, H_NSA_KV, 3, NSA_GROUP)
    gts_d = order(jnp.swapaxes(g3, 3, 4)).reshape(BS, rows_n, 3)
    bias_d = jnp.swapaxes(bias_n.reshape(N_BUCKETS, H_NSA_KV, NSA_GROUP), 1, 2).reshape(N_BUCKETS, H_NSA)

    def dec_table(rel, shift_far):
        return _bias_of_rel(bias_d, rel, shift_far).reshape(rows_n, rel.shape[1])

    ts = dec_table(page + s_idx[:, None] - np.arange(page)[None, :], True)
    ts = jnp.pad(ts, ((0, 0), (min(PAGES_PER_STEP, n_pg) * page - page, 0)))
    rel_own = s_idx[:, None] - np.arange(NEW_PAD)[None, :]
    tso = dec_table(rel_own, True)
    rel_win = WB + s_idx[:, None] - np.arange(WB)[None, :]
    tw = dec_table(np.where(rel_win < WINDOW, rel_win, -1), False)
    two = dec_table(rel_own, False)
    win_state = feat_pages(state_nsa_win[0])
    o_n_raw = _nsa_decode(cache_n, page_table, qd, kc_s, vc_s, _overlap(nc_s, n_cmp_s, 2 * LANE),
                          _block_onehot(P, SLC_BLOCK).T, pad_new(nkv32_s), win_state, pad_new(wkv32_s), gts_d,
                          ts, tso, tw, two)
    o5 = o_n_raw.reshape(BS, NSA_GROUP, H_NSA_KV, S, H_NSA_KV, HEAD_DIM)
    o_n_s = jnp.stack([o5[:, :, k, :, k] for k in range(H_NSA_KV)], axis=1)
    o_n_s = jnp.transpose(o_n_s, (0, 3, 1, 2, 4)).reshape(RS, W_NSA)
    y_sample = _out_proj(x_sample.reshape(RS, D), o_m_s.reshape(RS, W_MOBA).astype(BF16), zm_s,
                         o_n_s.astype(BF16), zn_s, rep(gate).reshape(1, RS, D), w_out_b, fgain, RS, 1).reshape(BS, S, D)
    moba_kv_sample = mkv32_s.reshape(1, BS, S, 2, H_MOBA, HEAD_DIM)
    nsa_kv_sample = nkv32_s.reshape(1, BS, S, 4, H_NSA_KV, HEAD_DIM)
    win_sample = jnp.concatenate([win_state[:, :, S:], jnp.swapaxes(wkv32_s.reshape(BS, S, 2 * LANE), 1, 2)], axis=2)
    win_sample = jnp.transpose(win_sample.reshape(1, BS, 2, H_NSA_KV, HEAD_DIM, WB), (0, 1, 5, 2, 3, 4))
    return (y_prompt, y_sample, moba_kv_prompt, moba_kv_sample, nsa_kv_prompt, nsa_kv_sample, win_prompt, win_sample)
```

```python
import functools
import math

import jax
import jax.numpy as jnp
import numpy as np
from jax import lax
from jax.experimental import pallas as pl
from jax.experimental.pallas import tpu as pltpu

F32 = jnp.float32
BF16 = jnp.bfloat16

HEAD_DIM = 64
H_MOBA = 8
H_NSA = 8
H_NSA_KV = 2
NSA_GROUP = 4
W_MOBA = H_MOBA * HEAD_DIM
W_NSA = H_NSA * HEAD_DIM
W_NSA_KV = H_NSA_KV * HEAD_DIM
MOBA_BLOCK = 256
MOBA_TOPK = 3
CMP_LEN = 32
CMP_STRIDE = 16
CMP_HIDDEN = 2 * HEAD_DIM
SLC_BLOCK = 64
SLC_TOPN = 16
WINDOW = 512
N_BUCKETS = 32
MAX_DISTANCE = 128
RMS_EPS = 1e-6
NEG = -1e30
FORCE = 1e9
TINY = 1e-30

LANE = 128
SUBLANE = 8
KEY_TILE = 256
SWEEP_TILES = 2
NSA_TQ = 256
PROJ_TM = 256
PAGES_PER_STEP = 16
NEW_PAD = 8
ONES_ROWS = 16
LOG2E = math.log2(math.e)
VMEM_LIMIT = 56 * 1024 * 1024

C_QM, C_MKV, C_ZM, C_QN, C_NKV, C_WKV, C_ZN, C_G, C_END = 0, 512, 1536, 2048, 3072, 3584, 3840, 4352, 4608

_NT = (((1,), (1,)), ((), ()))


def _dot(a, b):
    return jnp.dot(a, b, preferred_element_type=F32)


def _dot_nt(a, b):
    return lax.dot_general(a, b, _NT, preferred_element_type=F32)


def _split_bf16(a):
    hi = a.astype(BF16)
    lo = (a - hi.astype(F32)).astype(BF16)
    return hi, lo


def _sigmoid(x):
    return 1.0 / (1.0 + jnp.exp(-x))


def _cparams(sem):
    return pltpu.CompilerParams(dimension_semantics=sem, vmem_limit_bytes=VMEM_LIMIT)


def _ada_kernel(c_ref, w_ref, b_ref, o_ref):
    ch, cl = _split_bf16(c_ref[...])
    wh, wl = _split_bf16(w_ref[...])
    o_ref[...] = _dot(ch, wh) + _dot(ch, wl) + _dot(cl, wh) + b_ref[...]


def _ada(c_all, w_ada, b_ada):
    m, d = c_all.shape
    n = w_ada.shape[1]
    tn = 512
    return pl.pallas_call(
        _ada_kernel,
        grid=(n // tn,),
        in_specs=[pl.BlockSpec((m, d), lambda j: (0, 0)),
                  pl.BlockSpec((d, tn), lambda j: (0, j)),
                  pl.BlockSpec((1, tn), lambda j: (0, j))],
        out_specs=pl.BlockSpec((m, tn), lambda j: (0, j)),
        out_shape=jax.ShapeDtypeStruct((m, n), F32),
        compiler_params=_cparams(("arbitrary",)),
        name="ada",
    )(c_all, w_ada, b_ada.reshape(1, n))


def _modulated_norm(x, gain, scale, shift):
    inv = lax.rsqrt(jnp.mean(x * x, axis=-1, keepdims=True) + RMS_EPS)
    return (x * inv) * gain * (1.0 + scale) + shift


def _proj_kernel(x_ref, sc_ref, sh_ref, gain_ref, w_ref, qm_ref, mkv_ref, zm_ref, qn_ref, nkv_ref, wkv_ref, zn_ref, g_ref):
    hb = _modulated_norm(x_ref[...], gain_ref[...], sc_ref[...], sh_ref[...]).astype(BF16)
    col = lambda a, b: _dot(hb, w_ref[:, a:b])
    qm_ref[...] = col(C_QM, C_MKV).astype(BF16)
    mkv_ref[...] = col(C_MKV, C_ZM)
    zm_ref[...] = col(C_ZM, C_QN).astype(BF16)
    qn_ref[...] = col(C_QN, C_NKV).astype(BF16)
    nkv_ref[...] = col(C_NKV, C_WKV)
    wkv_ref[...] = col(C_WKV, C_ZN)
    zn_ref[...] = col(C_ZN, C_G).astype(BF16)
    g_ref[...] = _sigmoid(col(C_G, C_END))


def _proj(x2d, sc, sh, gain, w):
    r, d = x2d.shape
    full = lambda width: pl.BlockSpec((r, width), lambda i: (0, 0))
    widths = [(512, BF16), (1024, F32), (512, BF16), (1024, BF16), (512, F32), (256, F32), (512, BF16), (256, F32)]
    return pl.pallas_call(
        _proj_kernel,
        grid=(1,),
        in_specs=[full(d), full(d), full(d), pl.BlockSpec((1, d), lambda i: (0, 0)),
                  pl.BlockSpec((d, C_END), lambda i: (0, 0))],
        out_specs=[full(wd) for wd, _ in widths],
        out_shape=[jax.ShapeDtypeStruct((r, wd), dt) for wd, dt in widths],
        compiler_params=_cparams(("arbitrary",)),
        name="proj_decode",
    )(x2d, sc, sh, gain, w)


R_QM, R_MKV, R_QN, R_NKV, R_WKV, R_G, R_END = 0, 512, 1536, 2560, 3072, 3328, 3584


def _proj_fm_kernel(x_ref, sc_ref, sh_ref, gain_ref, w_ref, wt_ref,
                    qmt_ref, mkvt_ref, vmt_ref, qnt_ref, nkvt_ref, vst_ref, wkvt_ref, vwt_ref, gt_ref,
                    km_ref, ksw_ref, zm_ref, zn_ref, kmean_ref):
    h = _modulated_norm(x_ref[...], gain_ref[...], sc_ref[0], sh_ref[0])
    hb = h.astype(BF16)
    ht = h.T.astype(BF16)
    frow = lambda a, b: _dot(wt_ref[a:b, :], ht)
    col = lambda a, b: _dot(hb, w_ref[:, a:b])
    qmt_ref[0] = frow(R_QM, R_MKV).astype(BF16)
    mkvt = frow(R_MKV, R_QN)
    mkvt_ref[0] = mkvt
    vmt_ref[0] = mkvt[W_MOBA:].astype(BF16)
    qnt_ref[0] = frow(R_QN, R_NKV).astype(BF16)
    nkvt = frow(R_NKV, R_WKV)
    nkvt_ref[0] = nkvt
    vst_ref[0] = nkvt[3 * W_NSA_KV:].astype(BF16)
    wkvt = frow(R_WKV, R_G)
    wkvt_ref[0] = wkvt
    vwt_ref[0] = wkvt[W_NSA_KV:].astype(BF16)
    gt_ref[0] = _sigmoid(frow(R_G, R_END))
    km = col(C_MKV, C_MKV + W_MOBA)
    km_ref[...] = km.astype(BF16)
    kmean_ref[0] = jnp.mean(km, axis=0, keepdims=True)
    ksw_ref[...] = jnp.concatenate([col(C_NKV + 2 * W_NSA_KV, C_NKV + 3 * W_NSA_KV), col(C_WKV, C_WKV + W_NSA_KV)],
                                   axis=1).astype(BF16)
    zm_ref[...] = col(C_ZM, C_QN).astype(BF16)
    zn_ref[...] = col(C_ZN, C_G).astype(BF16)


def _proj_fm(x2d, sc, sh, gain, w, wt, b, tm):
    r, d = x2d.shape
    t = r // b
    tpb = t // tm
    nt = r // tm
    row = lambda width: pl.BlockSpec((tm, width), lambda i: (i, 0))
    fm = lambda rows: pl.BlockSpec((1, rows, tm), lambda i: (i // tpb, 0, i % tpb))
    mod = pl.BlockSpec((1, 1, d), lambda i: (i // tpb, 0, 0))
    const = lambda a: pl.BlockSpec(a.shape, lambda i: (0, 0), pipeline_mode=pl.Buffered(1))
    fm_outs = [(512, BF16), (1024, F32), (512, BF16), (1024, BF16), (512, F32), (128, BF16), (256, F32), (128, BF16),
               (256, F32)]
    tm_outs = [(512, BF16), (256, BF16), (512, BF16), (512, BF16)]
    return pl.pallas_call(
        _proj_fm_kernel,
        grid=(nt,),
        in_specs=[row(d), mod, mod, pl.BlockSpec((1, d), lambda i: (0, 0)), const(w), const(wt)],
        out_specs=[fm(rows) for rows, _ in fm_outs] + [row(wd) for wd, _ in tm_outs]
        + [pl.BlockSpec((1, 1, W_MOBA), lambda i: (i, 0, 0))],
        out_shape=[jax.ShapeDtypeStruct((b, rows, t), dt) for rows, dt in fm_outs]
        + [jax.ShapeDtypeStruct((r, wd), dt) for wd, dt in tm_outs] + [jax.ShapeDtypeStruct((nt, 1, W_MOBA), F32)],
        compiler_params=_cparams(("arbitrary",)),
        name="proj_prompt",
    )(x2d, sc, sh, gain, w, wt)


def _flash_step(s, v, m_ref, l_ref, acc_ref, feature_major=False):
    m_old = m_ref[...]
    m_new = jnp.maximum(m_old, jnp.max(s, axis=1, keepdims=True))
    alpha = jnp.exp2(m_old - m_new)
    p = jnp.exp2(s - m_new)
    l_ref[...] = alpha * l_ref[...] + jnp.sum(p, axis=1, keepdims=True)
    pv = _dot_nt(p.astype(BF16), v) if feature_major else _dot(p.astype(BF16), v)
    acc_ref[...] = alpha * acc_ref[...] + pv
    m_ref[...] = m_new


def _flash_init(m_ref, l_ref, acc_ref):
    m_ref[...] = jnp.full(m_ref.shape, NEG, F32)
    l_ref[...] = jnp.zeros(l_ref.shape, F32)
    acc_ref[...] = jnp.zeros(acc_ref.shape, F32)


def _flash_step_t(s, vt, m_ref, acc_ref):
    alpha, p = _softmax_weights_t(s, m_ref)
    acc_ref[...] = alpha * acc_ref[...] + _weighted_values_t(vt, p)


def _softmax_weights_t(s, m_ref):
    m_old = m_ref[...]
    m_new = jnp.maximum(m_old, jnp.max(s, axis=0, keepdims=True))
    m_ref[...] = m_new
    return jnp.exp2(m_old - m_new), jnp.exp2(s - m_new).astype(BF16)


def _weighted_values_t(vt, p):
    vt_aug = jnp.concatenate([vt, jnp.ones((ONES_ROWS, vt.shape[1]), BF16)], axis=0)
    return _dot(vt_aug, p)


def _flash_init_t(m_ref, acc_ref):
    m_ref[...] = jnp.full(m_ref.shape, NEG, F32)
    acc_ref[...] = jnp.zeros(acc_ref.shape, F32)


def _flash_result_t(acc_ref):
    width = acc_ref.shape[0] - ONES_ROWS
    return acc_ref[0:width, :] / jnp.maximum(acc_ref[width:width + 1, :], TINY)


def _sweep_tiles(logits_at, values_at, table_at, i_diag, m_ref, acc_ref):
    g = SWEEP_TILES
    big = g * KEY_TILE
    n_far = jnp.maximum(i_diag - 1, 0)
    n_big = n_far // g
    rem = n_far - n_big * g

    def far_body(n, s):
        nxt = logits_at(pl.multiple_of((n + 1) * big, big), big)
        _flash_step_t(s, values_at(pl.multiple_of(n * big, big), big), m_ref, acc_ref)
        return nxt

    s = lax.fori_loop(0, n_big, far_body, logits_at(0, big))
    first_block = jnp.where(i_diag == 0, g, g - 1 - rem)
    _flash_step_t(s + table_at(pl.multiple_of(first_block * KEY_TILE, KEY_TILE), big),
                  values_at(pl.multiple_of(n_big * big, big), big), m_ref, acc_ref)

    @pl.when((i_diag >= 1) & (rem == g - 1))
    def _():
        off = pl.multiple_of(i_diag * KEY_TILE, KEY_TILE)
        _flash_step_t(logits_at(off, KEY_TILE) + table_at(g * KEY_TILE, KEY_TILE), values_at(off, KEY_TILE),
                      m_ref, acc_ref)


def _topk_mask(vals, k, axis=1):
    idxf = lax.broadcasted_iota(jnp.int32, vals.shape, axis).astype(F32)

    def body(_, taken):
        cur = jnp.where(taken > 0.0, -jnp.inf, vals)
        mx = jnp.max(cur, axis=axis, keepdims=True)
        first = jnp.min(jnp.where(cur == mx, idxf, 1e9), axis=axis, keepdims=True)
        return jnp.where(idxf == first, 1.0, taken)

    return lax.fori_loop(0, k, body, jnp.zeros(vals.shape, F32))


def _masked_softmax(logits, valid, axis=1):
    lm = jnp.where(valid, logits, NEG)
    p = jnp.exp2(lm - jnp.max(lm, axis=axis, keepdims=True)) * valid.astype(F32)
    return p / jnp.maximum(jnp.sum(p, axis=axis, keepdims=True), TINY)


def _moba_kernel(qt_ref, k_ref, vt_ref, km_ref, kp_ref, tab_ref, o_ref, qaug_ref, m_ref, acc_ref):
    i = pl.program_id(2)
    tq = qt_ref.shape[2]
    qt = qt_ref[0]
    row = lax.broadcasted_iota(jnp.int32, qt.shape, 0)
    zero = jnp.zeros_like(qt)
    q2 = jnp.concatenate([jnp.where(row < HEAD_DIM, qt, zero), jnp.where(row >= HEAD_DIM, qt, zero)], axis=1)
    sc = _dot(km_ref[0].astype(BF16), q2)
    blk = lax.broadcasted_iota(jnp.int32, sc.shape, 0)
    cand = blk < i
    taken = _topk_mask(jnp.where(cand, sc, NEG), MOBA_TOPK, axis=0)
    sel = ((taken > 0.0) & cand) | (blk == i)
    qaug_ref[...] = jnp.concatenate([q2, jnp.where(sel, 0.0, NEG).astype(BF16)], axis=0)
    _flash_init_t(m_ref, acc_ref)

    def logits_at(off, n):
        kaug = jnp.concatenate([k_ref[0, pl.ds(off, n), :], kp_ref[pl.ds(off, n), :]], axis=1)
        return _dot(kaug, qaug_ref[...])

    _sweep_tiles(logits_at, lambda off, n: vt_ref[0, :, pl.ds(off, n)], lambda off, n: tab_ref[0, pl.ds(off, n), :],
                 i, m_ref, acc_ref)
    o = _flash_result_t(acc_ref)
    o_ref[0] = jnp.where(row < HEAD_DIM, o[:, :tq], o[:, tq:]).astype(BF16)


def _moba_prompt(qt, mkv16, vt, kmean_pad, kp, tab):
    b, _, t = qt.shape
    tq = MOBA_BLOCK
    nq = t // tq
    hp = H_MOBA // 2
    cols = 2 * tq
    return pl.pallas_call(
        _moba_kernel,
        grid=(b, hp, nq),
        in_specs=[pl.BlockSpec((1, LANE, tq), lambda bb, h, i: (bb, h, i)),
                  pl.BlockSpec((1, t, LANE), lambda bb, h, i: (bb, 0, h)),
                  pl.BlockSpec((1, LANE, t), lambda bb, h, i: (bb, h, 0)),
                  pl.BlockSpec((1, LANE, LANE), lambda bb, h, i: (bb, 0, h)),
                  pl.BlockSpec((t, LANE), lambda bb, h, i: (0, 0), pipeline_mode=pl.Buffered(1)),
                  pl.BlockSpec((1,) + tab.shape[1:], lambda bb, h, i: (h, 0, 0), pipeline_mode=pl.Buffered(1))],
        out_specs=pl.BlockSpec((1, LANE, tq), lambda bb, h, i: (bb, h, i)),
        out_shape=jax.ShapeDtypeStruct((b, W_MOBA, t), BF16),
        scratch_shapes=[pltpu.VMEM((2 * LANE, cols), BF16),
                        pltpu.VMEM((1, cols), F32), pltpu.VMEM((LANE + ONES_ROWS, cols), F32)],
        compiler_params=_cparams(("arbitrary", "arbitrary", "arbitrary")),
        name="moba_prompt",
    )(qt, mkv16, vt, kmean_pad, kp, tab)


def _cmp_kernel(pt_ref, *refs, pps):
    del pt_ref
    pages = refs[:pps]
    (pe_ref, kwa_ref, kwb_ref, kw2_ref, vwa_ref, vwb_ref, vw2_ref, kc_ref, vc_ref,
     uk_ref, uv_ref, sk_ref, sv_ref) = refs[pps:]
    j = pl.program_id(1)
    page = pages[0].shape[2]
    groups = pps * page // CMP_STRIDE
    for u in range(pps):
        xt = pages[u][0]
        sk_ref[u * page:(u + 1) * page, :] = xt[:LANE, :].T
        sv_ref[u * page:(u + 1) * page, :] = xt[LANE:, :].T
    row0 = pl.multiple_of(j * groups, groups)
    for l in range(CMP_STRIDE):
        rows_l = pl.ds(l, groups, stride=CMP_STRIDE)
        uk_ref[pl.ds(row0, groups), l * LANE:(l + 1) * LANE] = sk_ref[rows_l, :]
        uv_ref[pl.ds(row0, groups), l * LANE:(l + 1) * LANE] = sv_ref[rows_l, :]

    @pl.when(j == pl.num_programs(1) - 1)
    def _():
        nc = uk_ref.shape[0]
        last = lax.broadcasted_iota(jnp.int32, (nc, LANE), 0) == nc - 1

        def compress(u_ref, pe, wa_ref, wb_ref, w2_ref, out_ref):
            u = u_ref[...]
            pa = _dot((u + pe[0]).astype(BF16), wa_ref[...])
            pb = _dot((u + pe[1]).astype(BF16), wb_ref[...])
            pre = pa + pltpu.roll(pb, nc - 1, 0)
            hid = pre * _sigmoid(pre)
            out = _dot(hid.astype(BF16), w2_ref[...])
            out_ref[0] = jnp.where(last, 0.0, out).astype(BF16)

        compress(uk_ref, pe_ref[0], kwa_ref, kwb_ref, kw2_ref, kc_ref)
        compress(uv_ref, pe_ref[1], vwa_ref, vwb_ref, vw2_ref, vc_ref)


def _compress(pages_arr, pt, page, pe, kwa, kwb, kw2, vwa, vwb, vw2, paged):
    b, n_pg = pt.shape
    pps = min(PAGES_PER_STEP, n_pg)
    nc = n_pg * page // CMP_STRIDE
    if paged:
        page_spec = lambda u: pl.BlockSpec((1, 2 * LANE, page), lambda bb, j, p: (p[bb, j * pps + u], 0, 0))
    else:
        page_spec = lambda u: pl.BlockSpec((1, 2 * LANE, page), lambda bb, j, p: (bb, 0, j * pps + u))
    full = lambda a: pl.BlockSpec(a.shape, lambda bb, j, p: (0,) * a.ndim)
    consts = (pe, kwa, kwb, kw2, vwa, vwb, vw2)
    grid_spec = pltpu.PrefetchScalarGridSpec(
        num_scalar_prefetch=1,
        grid=(b, n_pg // pps),
        in_specs=[page_spec(u) for u in range(pps)] + [full(a) for a in consts],
        out_specs=[pl.BlockSpec((1, nc, LANE), lambda bb, j, p: (bb, 0, 0))] * 2,
        scratch_shapes=[pltpu.VMEM((nc, CMP_STRIDE * LANE), F32)] * 2 + [pltpu.VMEM((pps * page, LANE), F32)] * 2)
    return pl.pallas_call(
        functools.partial(_cmp_kernel, pps=pps),
        grid_spec=grid_spec,
        out_shape=[jax.ShapeDtypeStruct((b, nc, LANE), BF16)] * 2,
        compiler_params=_cparams(("arbitrary", "arbitrary")),
        name="nsa_compress",
    )(pt, *([pages_arr] * pps), *consts)


def _nsa_kernel(qt_ref, kc_ref, vct_ref, ovt_ref, ks_ref, vst_ref, kw_ref, vwt_ref, kp_ref, ts_ref, tw_ref, gt_ref,
                o_ref, qaug_ref, m_ref, acc_ref, ocmp_ref, oslc_ref):
    k = pl.program_id(1)
    i = pl.program_id(2)
    tq = qt_ref.shape[2]
    q4 = jnp.concatenate([qt_ref[0, g * LANE:(g + 1) * LANE, :] for g in range(NSA_GROUP)], axis=1)
    qpos1 = i * tq + lax.broadcasted_iota(jnp.int32, (1, tq), 1)
    qpos = jnp.concatenate([qpos1] * NSA_GROUP, axis=1)

    lc = _dot(kc_ref[0], q4)
    tok = lax.broadcasted_iota(jnp.int32, lc.shape, 0)
    lm = jnp.where(tok * CMP_STRIDE + (CMP_LEN - 1) <= qpos, lc, NEG)
    p = jnp.exp2(lm - jnp.max(lm, axis=0, keepdims=True))
    norm = jnp.where(qpos >= CMP_LEN - 1, 1.0 / jnp.maximum(jnp.sum(p, axis=0, keepdims=True), TINY), 0.0)
    ocmp_ref[...] = _dot(vct_ref[0], p.astype(BF16)) * norm
    pc = p * norm

    pcs = pc[:, 0:tq] + pc[:, tq:2 * tq] + pc[:, 2 * tq:3 * tq] + pc[:, 3 * tq:4 * tq]
    ph, plo = _split_bf16(pcs)
    imp = _dot(ovt_ref[...], ph) + _dot(ovt_ref[...], plo)
    jb = lax.broadcasted_iota(jnp.int32, imp.shape, 0)
    cur = qpos1 >> int(math.log2(SLC_BLOCK))
    avail = jb <= cur
    forced = (jb == 0) | (jb == cur) | (jb == cur - 1)
    imp = jnp.where(avail, jnp.where(forced, FORCE, imp), NEG)
    sel = (_topk_mask(imp, SLC_TOPN, axis=0) > 0.0) & avail
    selb = jnp.where(sel, 0.0, NEG).astype(BF16)
    qaug_ref[...] = jnp.concatenate([q4, jnp.concatenate([selb] * NSA_GROUP, axis=1)], axis=0)

    cd = (i * tq) // KEY_TILE
    a0 = pl.multiple_of((i * tq) % KEY_TILE, tq)

    def table(t_ref, r0, n):
        return jnp.concatenate([t_ref[0, pl.ds(r0, n), pl.ds(g * KEY_TILE + a0, tq)] for g in range(NSA_GROUP)],
                               axis=1)

    _flash_init_t(m_ref, acc_ref)

    def slc_logits_at(off, n):
        kaug = jnp.concatenate([ks_ref[0, pl.ds(off, n), :], kp_ref[pl.ds(off, n), :]], axis=1)
        return _dot(kaug, qaug_ref[...])

    _sweep_tiles(slc_logits_at, lambda off, n: vst_ref[0, :, pl.ds(off, n)], functools.partial(table, ts_ref),
                 cd, m_ref, acc_ref)
    oslc_ref[...] = _flash_result_t(acc_ref)

    _flash_init_t(m_ref, acc_ref)
    n_win = WINDOW // KEY_TILE
    span = WINDOW + KEY_TILE
    first = jnp.maximum(cd - n_win, 0)
    off = pl.multiple_of(first * KEY_TILE, KEY_TILE)
    toff = pl.multiple_of((first - (cd - n_win)) * KEY_TILE, KEY_TILE)
    s = _dot(kw_ref[0, pl.ds(off, span), :], q4) + table(tw_ref, toff, span)
    _flash_step_t(s, vwt_ref[0, :, pl.ds(off, span)], m_ref, acc_ref)
    owin = _flash_result_t(acc_ref)

    ocmp = ocmp_ref[...]
    oslc = oslc_ref[...]
    gts = gt_ref[0]
    for g in range(NSA_GROUP):
        c = slice(g * tq, (g + 1) * tq)
        og = (gts[g:g + 1, :] * ocmp[:, c] + gts[NSA_GROUP + g:NSA_GROUP + g + 1, :] * oslc[:, c]
              + gts[2 * NSA_GROUP + g:2 * NSA_GROUP + g + 1, :] * owin[:, c])
        o_ref[0, g * HEAD_DIM:(g + 1) * HEAD_DIM, :] = jnp.where(k == 0, og[:HEAD_DIM], og[HEAD_DIM:]).astype(BF16)


def _nsa_prompt(qt, kc, vct, ovt, ksw, vst, vwt, kps, ts, tw, gt):
    b, _, t = qt.shape
    tq = min(NSA_TQ, t)
    nq = t // tq
    cols = NSA_GROUP * tq
    nc = kc.shape[1]
    keys = lambda col: pl.BlockSpec((1, t, LANE), lambda bb, k, i: (bb, 0, col))
    vals = pl.BlockSpec((1, LANE, t), lambda bb, k, i: (bb, 0, 0))
    return pl.pallas_call(
        _nsa_kernel,
        grid=(b, H_NSA_KV, nq),
        in_specs=[pl.BlockSpec((1, NSA_GROUP * LANE, tq), lambda bb, k, i: (bb, k, i)),
                  pl.BlockSpec((1, nc, LANE), lambda bb, k, i: (bb, 0, 0)),
                  pl.BlockSpec((1, LANE, nc), lambda bb, k, i: (bb, 0, 0)),
                  pl.BlockSpec(ovt.shape, lambda bb, k, i: (0, 0)),
                  keys(0), vals, keys(1), vals,
                  pl.BlockSpec((t, LANE), lambda bb, k, i: (0, 0), pipeline_mode=pl.Buffered(1)),
                  pl.BlockSpec((1,) + ts.shape[1:], lambda bb, k, i: (k, 0, 0), pipeline_mode=pl.Buffered(1)),
                  pl.BlockSpec((1,) + tw.shape[1:], lambda bb, k, i: (k, 0, 0), pipeline_mode=pl.Buffered(1)),
                  pl.BlockSpec((1, LANE, tq), lambda bb, k, i: (bb, k, i))],
        out_specs=pl.BlockSpec((1, NSA_GROUP * HEAD_DIM, tq), lambda bb, k, i: (bb, k, i)),
        out_shape=jax.ShapeDtypeStruct((b, W_NSA, t), BF16),
        scratch_shapes=[pltpu.VMEM((2 * LANE, cols), BF16),
                        pltpu.VMEM((1, cols), F32), pltpu.VMEM((LANE + ONES_ROWS, cols), F32),
                        pltpu.VMEM((LANE, cols), F32), pltpu.VMEM((LANE, cols), F32)],
        compiler_params=_cparams(("arbitrary", "arbitrary", "arbitrary")),
        name="nsa_prompt",
    )(qt, kc, vct, ovt, ksw, vst, ksw, vwt, kps, ts, tw, gt)


def _moba_dec_kernel(pt_ref, qbd_ref, new_ref, td_ref, *refs, pps, nblk):
    del pt_ref
    pages = refs[:pps]
    o_ref, sc_ref, mall_ref, lall_ref, oall_ref = refs[pps:]
    j = pl.program_id(1)
    qbd = qbd_ref[0]
    rows = qbd.shape[0]
    lane = lax.broadcasted_iota(jnp.int32, (rows, LANE), 1)
    ppb = MOBA_BLOCK // pages[0].shape[2]

    @pl.when(j == 0)
    def _():
        sc_ref[...] = jnp.zeros(sc_ref.shape, F32)
        mall_ref[...] = jnp.full(mall_ref.shape, NEG, F32)
        lall_ref[...] = jnp.zeros(lall_ref.shape, F32)

    for u in range(pps // ppb):
        blk = j * (pps // ppb) + u
        kt = jnp.concatenate([pages[u * ppb + w][0, :W_MOBA, :] for w in range(ppb)], axis=1)
        vt = jnp.concatenate([pages[u * ppb + w][0, W_MOBA:, :] for w in range(ppb)], axis=1)
        s = _dot(qbd, kt.astype(BF16))
        sc_ref[...] = jnp.where(lane == blk, jnp.sum(s, axis=1, keepdims=True), sc_ref[...])
        near = (blk == nblk - 1).astype(F32)
        s = s + near * td_ref[:, 0:MOBA_BLOCK]
        mb = jnp.max(s, axis=1, keepdims=True)
        p = jnp.exp2(s - mb)
        mall_ref[...] = jnp.where(lane == blk, mb, mall_ref[...])
        lall_ref[...] = jnp.where(lane == blk, jnp.sum(p, axis=1, keepdims=True), lall_ref[...])
        oall_ref[pl.ds(blk, 1)] = _dot_nt(p.astype(BF16), vt.astype(BF16))[None]

    @pl.when(j == pl.num_programs(1) - 1)
    def _():
        kn = new_ref[0, :, :W_MOBA].astype(BF16)
        vn = new_ref[0, :, W_MOBA:].astype(BF16)
        s_own = _dot_nt(qbd, kn) + td_ref[:, MOBA_BLOCK:MOBA_BLOCK + NEW_PAD]
        m_own = jnp.max(s_own, axis=1, keepdims=True)
        p_own = jnp.exp2(s_own - m_own)
        l_own = jnp.sum(p_own, axis=1, keepdims=True)
        o_own = _dot(p_own.astype(BF16), vn)
        sc = sc_ref[...]
        cand = lane < nblk
        sel = (_topk_mask(jnp.where(cand, sc, NEG), MOBA_TOPK) > 0.0) & cand
        mall = mall_ref[...]
        m_fin = jnp.maximum(jnp.max(jnp.where(sel, mall, NEG), axis=1, keepdims=True), m_own)
        w = jnp.where(sel, jnp.exp2(mall - m_fin), 0.0)
        w_own = jnp.exp2(m_own - m_fin)
        l = jnp.sum(w * lall_ref[...], axis=1, keepdims=True) + w_own * l_own
        acc = w_own * o_own
        for n in range(nblk):
            acc = acc + w[:, n:n + 1] * oall_ref[n]
        o = acc / jnp.maximum(l, TINY)
        rowh = lax.broadcasted_iota(jnp.int32, o.shape, 0) & (H_MOBA - 1)
        laneh = lax.broadcasted_iota(jnp.int32, o.shape, 1) >> int(math.log2(HEAD_DIM))
        o = jnp.where(rowh == laneh, o, 0.0)
        o_ref[0] = jnp.sum(o.reshape(rows // H_MOBA, H_MOBA, W_MOBA), axis=1)


def _moba_decode(cache_m, pt, qbd, new_pad, td):
    b, n_pg = pt.shape
    page = cache_m.shape[2]
    pps = min(PAGES_PER_STEP, n_pg)
    nblk = n_pg * page // MOBA_BLOCK
    rows = qbd.shape[1]
    s_new = rows // H_MOBA
    page_spec = lambda u: pl.BlockSpec((1, 2 * W_MOBA, page), lambda bb, j, p: (p[bb, j * pps + u], 0, 0))
    grid_spec = pltpu.PrefetchScalarGridSpec(
        num_scalar_prefetch=1,
        grid=(b, n_pg // pps),
        in_specs=[pl.BlockSpec((1, rows, W_MOBA), lambda bb, j, p: (bb, 0, 0)),
                  pl.BlockSpec((1, NEW_PAD, 2 * W_MOBA), lambda bb, j, p: (bb, 0, 0)),
                  pl.BlockSpec(td.shape, lambda bb, j, p: (0, 0))] + [page_spec(u) for u in range(pps)],
        out_specs=pl.BlockSpec((1, s_new, W_MOBA), lambda bb, j, p: (bb, 0, 0)),
        scratch_shapes=[pltpu.VMEM((rows, LANE), F32), pltpu.VMEM((rows, LANE), F32), pltpu.VMEM((rows, LANE), F32),
                        pltpu.VMEM((nblk, rows, W_MOBA), F32)])
    return pl.pallas_call(
        functools.partial(_moba_dec_kernel, pps=pps, nblk=nblk),
        grid_spec=grid_spec,
        out_shape=jax.ShapeDtypeStruct((b, s_new, W_MOBA), F32),
        compiler_params=_cparams(("arbitrary", "arbitrary")),
        name="moba_decode",
    )(pt, qbd, new_pad, td, *([cache_m] * pps))


def _nsa_dec_kernel(pt_ref, qd_ref, kc_ref, vc_ref, ov_ref, kp_ref, new_ref, win_ref, neww_ref, g_ref,
                    ts_ref, tso_ref, tw_ref, two_ref, *refs, pps, past):
    del pt_ref
    pages = refs[:pps]
    o_ref, qaug_ref, m_ref, l_ref, acc_ref, ocmp_ref, selown_ref = refs[pps:]
    j = pl.program_id(1)
    nj = pl.num_programs(1)
    qd = qd_ref[0]
    rows = qd.shape[0]
    page = pages[0].shape[2]
    grp = rows // NSA_GROUP
    s_new = grp // H_NSA_KV

    @pl.when(j == 0)
    def _():
        lc = _dot_nt(qd, kc_ref[0])
        tok = lax.broadcasted_iota(jnp.int32, lc.shape, 1)
        qpos = past + (lax.broadcasted_iota(jnp.int32, (rows, 1), 0) & (s_new - 1))
        pc = _masked_softmax(lc, tok * CMP_STRIDE + (CMP_LEN - 1) <= qpos)
        ocmp_ref[...] = _dot(pc.astype(BF16), vc_ref[0])
        pcs = pc[0:grp] + pc[grp:2 * grp] + pc[2 * grp:3 * grp] + pc[3 * grp:4 * grp]
        ph, plo = _split_bf16(pcs)
        imp = _dot(ph, ov_ref[...]) + _dot(plo, ov_ref[...])
        jb = lax.broadcasted_iota(jnp.int32, imp.shape, 1)
        cur = qpos[0:grp] >> int(math.log2(SLC_BLOCK))
        avail = jb <= cur
        forced = (jb == 0) | (jb == cur) | (jb == cur - 1)
        imp = jnp.where(avail, jnp.where(forced, FORCE, imp), NEG)
        sel = (_topk_mask(imp, SLC_TOPN) > 0.0) & avail
        selb = jnp.concatenate([jnp.where(sel, 0.0, NEG)] * NSA_GROUP, axis=0)
        qaug_ref[...] = jnp.concatenate([qd, selb[:, :LANE].astype(BF16)], axis=1)
        n_own = past // SLC_BLOCK
        selown_ref[...] = jnp.broadcast_to(selb[:, n_own:n_own + 1], selown_ref.shape)
        _flash_init(m_ref, l_ref, acc_ref)

    span = pps * page
    off = pl.multiple_of(j * span, span)
    kst = jnp.concatenate([pages[u][0, :LANE, :] for u in range(pps)], axis=1).astype(BF16)
    vst = jnp.concatenate([pages[u][0, LANE:, :] for u in range(pps)], axis=1).astype(BF16)
    kaug = jnp.concatenate([kst, kp_ref[:, pl.ds(off, span)]], axis=0)
    near = (j == nj - 1).astype(F32)
    s = _dot(qaug_ref[...], kaug) + near * ts_ref[...]
    _flash_step(s, vst, m_ref, l_ref, acc_ref, feature_major=True)

    @pl.when(j == nj - 1)
    def _():
        new = new_ref[0]
        s_own = _dot_nt(qd, new[:, 2 * LANE:3 * LANE].astype(BF16)) + tso_ref[...] + selown_ref[:, 0:1]
        _flash_step(s_own, new[:, 3 * LANE:].astype(BF16), m_ref, l_ref, acc_ref)
        oslc = acc_ref[...] / jnp.maximum(l_ref[...], TINY)
        _flash_init(m_ref, l_ref, acc_ref)
        win = win_ref[0]
        _flash_step(_dot(qd, win[:LANE, :].astype(BF16)) + tw_ref[...], win[LANE:, :].astype(BF16),
                    m_ref, l_ref, acc_ref, feature_major=True)
        neww = neww_ref[0]
        _flash_step(_dot_nt(qd, neww[:, :LANE].astype(BF16)) + two_ref[...], neww[:, LANE:].astype(BF16),
                    m_ref, l_ref, acc_ref)
        owin = acc_ref[...] / jnp.maximum(l_ref[...], TINY)
        gts = g_ref[0]
        o_ref[0] = gts[:, 0:1] * ocmp_ref[...] + gts[:, 1:2] * oslc + gts[:, 2:3] * owin


def _nsa_decode(cache_n, pt, qd, kc, vc, ov, kps, new_pad, win, neww, gts, ts, tso, tw, two):
    b, n_pg = pt.shape
    page = cache_n.shape[2]
    pps = min(PAGES_PER_STEP, n_pg)
    rows = qd.shape[1]
    nc = kc.shape[1]
    full = lambda a: pl.BlockSpec(a.shape, lambda bb, j, p: (0,) * a.ndim)
    per_b = lambda a: pl.BlockSpec((1,) + a.shape[1:], lambda bb, j, p: (bb,) + (0,) * (a.ndim - 1))
    page_spec = lambda u: pl.BlockSpec((1, 2 * LANE, page), lambda bb, j, p: (p[bb, j * pps + u], 1, 0))
    grid_spec = pltpu.PrefetchScalarGridSpec(
        num_scalar_prefetch=1,
        grid=(b, n_pg // pps),
        in_specs=[per_b(qd), per_b(kc), per_b(vc), full(ov), full(kps), per_b(new_pad), per_b(win), per_b(neww),
                  per_b(gts), full(ts), full(tso), full(tw), full(two)] + [page_spec(u) for u in range(pps)],
        out_specs=pl.BlockSpec((1, rows, LANE), lambda bb, j, p: (bb, 0, 0)),
        scratch_shapes=[pltpu.VMEM((rows, 2 * LANE), BF16),
                        pltpu.VMEM((rows, 1), F32), pltpu.VMEM((rows, 1), F32), pltpu.VMEM((rows, LANE), F32),
                        pltpu.VMEM((rows, LANE), F32), pltpu.VMEM((rows, LANE), F32)])
    return pl.pallas_call(
        functools.partial(_nsa_dec_kernel, pps=pps, past=n_pg * page),
        grid_spec=grid_spec,
        out_shape=jax.ShapeDtypeStruct((b, rows, LANE), F32),
        compiler_params=_cparams(("arbitrary", "arbitrary")),
        name="nsa_decode",
    )(pt, qd, kc, vc, ov, kps, new_pad, win, neww, gts, ts, tso, tw, two, *([cache_n] * pps))


def _out_kernel(x_ref, om_ref, zm_ref, on_ref, zn_ref, gate_ref, w_ref, fg_ref, y_ref, *, feature_major):
    zm = zm_ref[...].astype(F32)
    zn = zn_ref[...].astype(F32)
    if feature_major:
        om = om_ref[0].astype(F32).T
        on = on_ref[0].astype(F32).T
    else:
        om = om_ref[...].astype(F32)
        on = on_ref[...].astype(F32)
    mm = (om * (zm * _sigmoid(zm))).astype(BF16)
    mn = (on * (zn * _sigmoid(zn))).astype(BF16)
    mixed = _dot(mm, w_ref[:W_MOBA, :]) + _dot(mn, w_ref[W_MOBA:, :])
    xn = x_ref[...] + gate_ref[0] * mixed
    inv = lax.rsqrt(jnp.mean(xn * xn, axis=-1, keepdims=True) + RMS_EPS)
    y_ref[...] = (xn * inv) * fg_ref[...]


def _out_proj(x2d, om, zm, on, zn, gate, w_out, fgain, tm, tiles_per_mod):
    r, d = x2d.shape
    mrows = gate.shape[1]
    feature_major = om.ndim == 3
    row = lambda width: pl.BlockSpec((tm, width), lambda i: (i, 0))
    if feature_major:
        mixer = pl.BlockSpec((1, om.shape[1], tm), lambda i: (i // tiles_per_mod, 0, i % tiles_per_mod))
    else:
        mixer = row(W_MOBA)
    return pl.pallas_call(
        functools.partial(_out_kernel, feature_major=feature_major),
        grid=(r // tm,),
        in_specs=[row(d), mixer, row(W_MOBA), mixer, row(W_NSA),
                  pl.BlockSpec((1, mrows, d), lambda i: (i // tiles_per_mod, 0, 0)),
                  pl.BlockSpec((d, d), lambda i: (0, 0)),
                  pl.BlockSpec((1, d), lambda i: (0, 0))],
        out_specs=row(d),
        out_shape=jax.ShapeDtypeStruct((r, d), F32),
        compiler_params=_cparams(("arbitrary",)),
        name="out_proj",
    )(x2d, om, zm, on, zn, gate, w_out, fgain)


def _t5_bucket(rel):
    n = np.maximum(rel, 0)
    exact = N_BUCKETS // 2
    x = np.log(np.maximum(n, 1) / exact) / math.log(MAX_DISTANCE / exact) * (N_BUCKETS - exact)
    near_boundary = (np.abs(x - np.round(x)) < 1e-3) & (n > exact) & (n < MAX_DISTANCE)
    assert not near_boundary.any()
    large = exact + np.floor(x + 1e-6).astype(np.int64)
    return np.where(n < exact, n, np.minimum(large, N_BUCKETS - 1)).astype(np.int32)


def _bias_of_rel(tab, rel, shift_far):
    onehot = (jnp.asarray(_t5_bucket(rel))[..., None] == jnp.arange(N_BUCKETS, dtype=jnp.int32)).astype(F32)
    val = jnp.moveaxis(jnp.dot(onehot, tab, precision=lax.Precision.HIGHEST), -1, 0)
    if shift_far:
        val = val - tab[N_BUCKETS - 1].reshape((-1,) + (1,) * rel.ndim)
    return jnp.where(jnp.asarray(rel >= 0)[None], val * LOG2E, NEG)


def _layout_w_in(w_in):
    d = w_in.shape[0]
    sc = HEAD_DIM ** -0.5 * LOG2E
    o = 4 * W_MOBA
    q_m, rest_m = w_in[:, :W_MOBA] * sc, w_in[:, W_MOBA:o]
    q_n = (w_in[:, o:o + W_NSA] * sc).reshape(d, H_NSA_KV, NSA_GROUP, HEAD_DIM)
    qn_exp = jnp.concatenate([jnp.pad(q_n[:, 0], ((0, 0), (0, 0), (0, HEAD_DIM))),
                              jnp.pad(q_n[:, 1], ((0, 0), (0, 0), (HEAD_DIM, 0)))], axis=1).reshape(d, H_NSA * LANE)
    o += W_NSA
    kv_n = w_in[:, o:o + 6 * W_NSA_KV]
    o += 6 * W_NSA_KV
    g_n = jnp.swapaxes(w_in[:, o:o + 3 * H_NSA].reshape(d, H_NSA_KV, NSA_GROUP, 3), 2, 3)
    g_n = jnp.pad(g_n.reshape(d, H_NSA_KV, 3 * NSA_GROUP), ((0, 0), (0, 0), (0, LANE - 3 * NSA_GROUP)))
    z_n = w_in[:, o + 3 * H_NSA:]
    w = jnp.concatenate([q_m, rest_m, qn_exp, kv_n, z_n, g_n.reshape(d, H_NSA_KV * LANE)], axis=1).astype(BF16)
    assert w.shape[1] == C_END
    wt = jnp.concatenate([w[:, C_QM:C_ZM], w[:, C_QN:C_ZN], w[:, C_G:C_END]], axis=1).T
    assert wt.shape[0] == R_END
    return w, wt


def _layout_cmp(pe, w1, w2):
    w = w1.reshape(2, CMP_STRIDE, HEAD_DIM, CMP_HIDDEN)
    z = jnp.zeros_like(w)
    full = jnp.stack([jnp.concatenate([w, z], axis=-1), jnp.concatenate([z, w], axis=-1)], axis=2)
    full = full.reshape(2, CMP_STRIDE * H_NSA_KV * HEAD_DIM, H_NSA_KV * CMP_HIDDEN).astype(BF16)
    zz = jnp.zeros_like(w2)
    w2bd = jnp.concatenate([jnp.concatenate([w2, zz], axis=1), jnp.concatenate([zz, w2], axis=1)], axis=0).astype(BF16)
    pe2 = jnp.broadcast_to(pe.reshape(2, CMP_STRIDE, 1, HEAD_DIM), (2, CMP_STRIDE, H_NSA_KV, HEAD_DIM))
    return pe2.reshape(2, 1, CMP_STRIDE * LANE), full[0], full[1], w2bd


def _block_onehot(t, block):
    return (jnp.arange(t)[:, None] // block == jnp.arange(LANE)[None, :]).astype(BF16)


def _overlap(nc, n_cmp, lanes):
    i = jnp.arange(nc)[:, None]
    s = jnp.arange(lanes)[None, :]
    start = i * CMP_STRIDE
    hit = (start < s * SLC_BLOCK + SLC_BLOCK) & (start + CMP_LEN - 1 >= s * SLC_BLOCK) & (i < n_cmp)
    return hit.astype(BF16)


def kernel(x_prompt, x_sample, c_prompt, c_sample, cache_moba_kv, cache_nsa_kv, state_nsa_win, page_table, w_ada, b_ada, norm_gain, w_in, cmp_pe, cmp_k_w1, cmp_k_w2, cmp_v_w1, cmp_v_w2, w_out, rel_bias, final_gain):
    depth = w_in.shape[0]
    assert depth == 1
    B, T, D = x_prompt.shape
    BS, S, _ = x_sample.shape
    n_pg = page_table.shape[1]
    page = cache_moba_kv.shape[2]
    P = n_pg * page
    WB = state_nsa_win.shape[2]
    assert D == (H_MOBA + H_NSA) * HEAD_DIM and T % (SWEEP_TILES * KEY_TILE) == 0 and P % MOBA_BLOCK == 0 and S <= NEW_PAD
    assert T // SLC_BLOCK <= LANE and P // SLC_BLOCK <= LANE and T >= WINDOW and WB == WINDOW
    assert (BS * S) % SUBLANE == 0 and S & (S - 1) == 0

    w, wt = _layout_w_in(w_in[0])
    w_out_b = w_out[0].astype(BF16)
    gain = norm_gain[0].reshape(1, D)
    fgain = final_gain.reshape(1, D)
    kcmp_w = _layout_cmp(cmp_pe[0, 0], cmp_k_w1[0], cmp_k_w2[0])
    vcmp_w = _layout_cmp(cmp_pe[0, 1], cmp_v_w1[0], cmp_v_w2[0])
    pe2 = jnp.stack([kcmp_w[0], vcmp_w[0]], axis=0)
    cmp_consts = (pe2,) + kcmp_w[1:] + vcmp_w[1:]
    bias_m = rel_bias[:, :H_MOBA]
    bias_n = rel_bias[:, H_MOBA:]

    m_all = B + BS
    m_pad = -(-m_all // SUBLANE) * SUBLANE
    c_all = jnp.pad(jnp.concatenate([c_prompt, c_sample], axis=0), ((0, m_pad - m_all), (0, 0)))
    mod = _ada(c_all, w_ada[0], b_ada[0])
    shift, scale, gate = mod[:, :D], mod[:, D:2 * D], mod[:, 2 * D:]

    a = np.arange(KEY_TILE)[None, :]
    jk = np.arange(2 * KEY_TILE)[:, None]
    rel_s = a + KEY_TILE - jk
    heads_major = lambda tb, n_grp: jnp.transpose(
        tb.reshape(n_grp, -1, tb.shape[1], KEY_TILE), (0, 2, 1, 3)).reshape(n_grp, tb.shape[1], -1)
    rows_of = lambda tb, n, val: jnp.full((tb.shape[0], n, tb.shape[2]), val, F32)
    pad_rows = (SWEEP_TILES - 1) * KEY_TILE
    sweep_rows = lambda tb: jnp.concatenate([rows_of(tb, pad_rows, 0.0), tb, rows_of(tb, pad_rows, NEG)], axis=1)
    tab_m = sweep_rows(heads_major(_bias_of_rel(bias_m, rel_s, True), H_MOBA // 2))
    tab_s = sweep_rows(heads_major(_bias_of_rel(bias_n, rel_s, True), H_NSA_KV))
    jw = np.arange(WINDOW + KEY_TILE)[:, None]
    rel_w = a + WINDOW - jw
    tab_w = heads_major(_bias_of_rel(bias_n, np.where(rel_w < WINDOW, rel_w, -1), False), H_NSA_KV)
    tab_w = jnp.concatenate([tab_w, rows_of(tab_w, WINDOW, NEG)], axis=1)

    tpm = T // PROJ_TM
    (qmt, mkvt, vmt, qnt, nkvt, vst, wkvt, vwt, gt, km, ksw, zm, zn, kmean) = _proj_fm(
        x_prompt.reshape(B * T, D), scale[:B].reshape(B, 1, D), shift[:B].reshape(B, 1, D), gain, w, wt, B, PROJ_TM)
    nb = T // MOBA_BLOCK
    kmean_pad = jnp.pad(kmean.reshape(B, nb, W_MOBA), ((0, 0), (0, LANE - nb), (0, 0)))
    o_m = _moba_prompt(qmt, km.reshape(B, T, W_MOBA), vmt, kmean_pad, _block_onehot(T, MOBA_BLOCK), tab_m)
    pt_prompt = jnp.zeros((B, T // page), jnp.int32)
    kc_p, vc_p = _compress(nkvt, pt_prompt, page, *cmp_consts, paged=False)
    nc_p = T // CMP_STRIDE
    o_n = _nsa_prompt(qnt, kc_p, jnp.swapaxes(vc_p, 1, 2), _overlap(nc_p, nc_p - 1, LANE).T,
                      ksw.reshape(B, T, 2 * LANE), vst, vwt, _block_onehot(T, SLC_BLOCK), tab_s, tab_w, gt)
    y_prompt = _out_proj(x_prompt.reshape(B * T, D), o_m, zm, o_n, zn,
                         gate[:B].reshape(B, 1, D), w_out_b, fgain, PROJ_TM, tpm).reshape(B, T, D)
    tokens_first = lambda m, c: jnp.transpose(m.reshape(1, B, c, -1, HEAD_DIM, m.shape[-1]), (0, 1, 5, 2, 3, 4))
    moba_kv_prompt = tokens_first(mkvt, 2)
    nsa_kv_prompt = tokens_first(nkvt, 4)
    win_prompt = tokens_first(wkvt[:, :, T - WINDOW:], 2)

    RS = BS * S
    rep = lambda m: jnp.repeat(m[B:B + BS], S, axis=0)
    (qm_s, mkv32_s, zm_s, qn_s, nkv32_s, wkv32_s, zn_s, gts_s) = _proj(
        x_sample.reshape(RS, D), rep(scale), rep(shift), gain, w)
    pad_new = lambda m: jnp.pad(m.reshape(BS, S, -1), ((0, 0), (0, NEW_PAD - S), (0, 0)))
    s_idx = np.arange(S)

    q_rep = jnp.repeat(qm_s.reshape(BS, S, 1, H_MOBA, HEAD_DIM), H_MOBA, axis=2)
    eye = (jnp.arange(H_MOBA)[:, None] == jnp.arange(H_MOBA)[None, :])[None, None, :, :, None]
    qbd = jnp.where(eye, q_rep, 0).reshape(BS, S * H_MOBA, W_MOBA)
    rel_d = np.concatenate([MOBA_BLOCK + s_idx[:, None] - np.arange(MOBA_BLOCK)[None, :],
                            s_idx[:, None] - np.arange(NEW_PAD)[None, :]], axis=1)
    td = jnp.moveaxis(_bias_of_rel(bias_m, rel_d, True), 0, 1).reshape(S * H_MOBA, -1)
    td = jnp.pad(td, ((0, 0), (0, 3 * LANE - td.shape[1])))
    feat_pages = lambda c: jnp.transpose(c, (0, 2, 3, 4, 1)).reshape(c.shape[0], -1, c.shape[1])
    o_m_s = _moba_decode(feat_pages(cache_moba_kv[0]), page_table, qbd, pad_new(mkv32_s), td)

    cache_n = feat_pages(cache_nsa_kv[0])
    kc_s, vc_s = _compress(cache_n, page_table, page, *cmp_consts, paged=True)
    nc_s = P // CMP_STRIDE
    n_cmp_s = (P + S - CMP_LEN) // CMP_STRIDE + 1
    order = lambda m: jnp.transpose(m, (0, 3, 2, 1) + tuple(range(4, m.ndim)))
    rows_n = NSA_GROUP * H_NSA_KV * S
    qd = order(qn_s.reshape(BS, S, H_NSA_KV, NSA_GROUP, LANE)).reshape(BS, rows_n, LANE)
    g3 = gts_s.reshape(BS, S, H_NSA_KV, LANE)[..., :3 * NSA_GROUP].reshape(BS, S, H_NSA_KV, 3, NSA_GROUP)
    gts_d = order(jnp.swapaxes(g3, 3, 4)).reshape(BS, rows_n, 3)
    bias_d = jnp.swapaxes(bias_n.reshape(N_BUCKETS, H_NSA_KV, NSA_GROUP), 1, 2).reshape(N_BUCKETS, H_NSA)

    def dec_table(rel, shift_far):
        return _bias_of_rel(bias_d, rel, shift_far).reshape(rows_n, rel.shape[1])

    ts = dec_table(page + s_idx[:, None] - np.arange(page)[None, :], True)
    ts = jnp.pad(ts, ((0, 0), (min(PAGES_PER_STEP, n_pg) * page - page, 0)))
    rel_own = s_idx[:, None] - np.arange(NEW_PAD)[None, :]
    tso = dec_table(rel_own, True)
    rel_win = WB + s_idx[:, None] - np.arange(WB)[None, :]
    tw = dec_table(np.where(rel_win < WINDOW, rel_win, -1), False)
    two = dec_table(rel_own, False)
    win_state = feat_pages(state_nsa_win[0])
    o_n_raw = _nsa_decode(cache_n, page_table, qd, kc_s, vc_s, _overlap(nc_s, n_cmp_s, 2 * LANE),
                          _block_onehot(P, SLC_BLOCK).T, pad_new(nkv32_s), win_state, pad_new(wkv32_s), gts_d,
                          ts, tso, tw, two)
    o5 = o_n_raw.reshape(BS, NSA_GROUP, H_NSA_KV, S, H_NSA_KV, HEAD_DIM)
    o_n_s = jnp.stack([o5[:, :, k, :, k] for k in range(H_NSA_KV)], axis=1)
    o_n_s = jnp.transpose(o_n_s, (0, 3, 1, 2, 4)).reshape(RS, W_NSA)
    y_sample = _out_proj(x_sample.reshape(RS, D), o_m_s.reshape(RS, W_MOBA).astype(BF16), zm_s,
                         o_n_s.astype(BF16), zn_s, rep(gate).reshape(1, RS, D), w_out_b, fgain, RS, 1).reshape(BS, S, D)
    moba_kv_sample = mkv32_s.reshape(1, BS, S, 2, H_MOBA, HEAD_DIM)
    nsa_kv_sample = nkv32_s.reshape(1, BS, S, 4, H_NSA_KV, HEAD_DIM)
    win_sample = jnp.concatenate([win_state[:, :, S:], jnp.swapaxes(wkv32_s.reshape(BS, S, 2 * LANE), 1, 2)], axis=2)
    win_sample = jnp.transpose(win_sample.reshape(1, BS, 2, H_NSA_KV, HEAD_DIM, WB), (0, 1, 5, 2, 3, 4))
    return (y_prompt, y_sample, moba_kv_prompt, moba_kv_sample, nsa_kv_prompt, nsa_kv_sample, win_prompt, win_sample)
```

```python
import functools
import math

import jax
import jax.numpy as jnp
import numpy as np
from jax import lax
from jax.experimental import pallas as pl
from jax.experimental.pallas import tpu as pltpu

F32 = jnp.float32
BF16 = jnp.bfloat16

HEAD_DIM = 64
H_MOBA = 8
H_NSA = 8
H_NSA_KV = 2
NSA_GROUP = 4
W_MOBA = H_MOBA * HEAD_DIM
W_NSA = H_NSA * HEAD_DIM
W_NSA_KV = H_NSA_KV * HEAD_DIM
MOBA_BLOCK = 256
MOBA_TOPK = 3
CMP_LEN = 32
CMP_STRIDE = 16
CMP_HIDDEN = 2 * HEAD_DIM
SLC_BLOCK = 64
SLC_TOPN = 16
WINDOW = 512
N_BUCKETS = 32
MAX_DISTANCE = 128
RMS_EPS = 1e-6
NEG = -1e30
FORCE = 1e9
TINY = 1e-30

LANE = 128
SUBLANE = 8
KEY_TILE = 256
SWEEP_TILES = 2
NSA_TQ = 256
MOBA_TQ = 2 * MOBA_BLOCK
PROJ_TM = 256
PAGES_PER_STEP = 16
NEW_PAD = 8
ONES_ROWS = 16
LOG2E = math.log2(math.e)
VMEM_LIMIT = 56 * 1024 * 1024

C_QM, C_MKV, C_ZM, C_QN, C_NKV, C_WKV, C_ZN, C_G, C_END = 0, 512, 1536, 2048, 3072, 3584, 3840, 4352, 4608

_NT = (((1,), (1,)), ((), ()))


def _dot(a, b):
    return jnp.dot(a, b, preferred_element_type=F32)


def _dot_nt(a, b):
    return lax.dot_general(a, b, _NT, preferred_element_type=F32)


def _split_bf16(a):
    hi = a.astype(BF16)
    lo = (a - hi.astype(F32)).astype(BF16)
    return hi, lo


def _sigmoid(x):
    return 1.0 / (1.0 + jnp.exp(-x))


def _cparams(sem):
    return pltpu.CompilerParams(dimension_semantics=sem, vmem_limit_bytes=VMEM_LIMIT)


def _ada_kernel(c_ref, w_ref, b_ref, o_ref):
    ch, cl = _split_bf16(c_ref[...])
    wh, wl = _split_bf16(w_ref[...])
    o_ref[...] = _dot(ch, wh) + _dot(ch, wl) + _dot(cl, wh) + b_ref[...]


def _ada(c_all, w_ada, b_ada):
    m, d = c_all.shape
    n = w_ada.shape[1]
    tn = 512
    return pl.pallas_call(
        _ada_kernel,
        grid=(n // tn,),
        in_specs=[pl.BlockSpec((m, d), lambda j: (0, 0)),
                  pl.BlockSpec((d, tn), lambda j: (0, j)),
                  pl.BlockSpec((1, tn), lambda j: (0, j))],
        out_specs=pl.BlockSpec((m, tn), lambda j: (0, j)),
        out_shape=jax.ShapeDtypeStruct((m, n), F32),
        compiler_params=_cparams(("arbitrary",)),
        name="ada",
    )(c_all, w_ada, b_ada.reshape(1, n))


def _modulated_norm(x, gain, scale, shift):
    inv = lax.rsqrt(jnp.mean(x * x, axis=-1, keepdims=True) + RMS_EPS)
    return (x * inv) * gain * (1.0 + scale) + shift


def _proj_kernel(x_ref, sc_ref, sh_ref, gain_ref, w_ref, qm_ref, mkv_ref, zm_ref, qn_ref, nkv_ref, wkv_ref, zn_ref, g_ref):
    hb = _modulated_norm(x_ref[...], gain_ref[...], sc_ref[...], sh_ref[...]).astype(BF16)
    col = lambda a, b: _dot(hb, w_ref[:, a:b])
    qm_ref[...] = col(C_QM, C_MKV).astype(BF16)
    mkv_ref[...] = col(C_MKV, C_ZM)
    zm_ref[...] = col(C_ZM, C_QN).astype(BF16)
    qn_ref[...] = col(C_QN, C_NKV).astype(BF16)
    nkv_ref[...] = col(C_NKV, C_WKV)
    wkv_ref[...] = col(C_WKV, C_ZN)
    zn_ref[...] = col(C_ZN, C_G).astype(BF16)
    g_ref[...] = _sigmoid(col(C_G, C_END))


def _proj(x2d, sc, sh, gain, w):
    r, d = x2d.shape
    full = lambda width: pl.BlockSpec((r, width), lambda i: (0, 0))
    widths = [(512, BF16), (1024, F32), (512, BF16), (1024, BF16), (512, F32), (256, F32), (512, BF16), (256, F32)]
    return pl.pallas_call(
        _proj_kernel,
        grid=(1,),
        in_specs=[full(d), full(d), full(d), pl.BlockSpec((1, d), lambda i: (0, 0)),
                  pl.BlockSpec((d, C_END), lambda i: (0, 0))],
        out_specs=[full(wd) for wd, _ in widths],
        out_shape=[jax.ShapeDtypeStruct((r, wd), dt) for wd, dt in widths],
        compiler_params=_cparams(("arbitrary",)),
        name="proj_decode",
    )(x2d, sc, sh, gain, w)


R_QM, R_MKV, R_QN, R_NKV, R_WKV, R_G, R_END = 0, 512, 1536, 2560, 3072, 3328, 3584


def _proj_fm_kernel(x_ref, sc_ref, sh_ref, gain_ref, w_ref, wt_ref,
                    qmt_ref, mkvt_ref, vmt_ref, qnt_ref, nkvt_ref, vst_ref, wkvt_ref, vwt_ref, gt_ref,
                    km_ref, ksw_ref, zm_ref, zn_ref, kmean_ref):
    h = _modulated_norm(x_ref[...], gain_ref[...], sc_ref[0], sh_ref[0])
    hb = h.astype(BF16)
    ht = h.T.astype(BF16)
    frow = lambda a, b: _dot(wt_ref[a:b, :], ht)
    col = lambda a, b: _dot(hb, w_ref[:, a:b])
    qmt_ref[0] = frow(R_QM, R_MKV).astype(BF16)
    mkvt = frow(R_MKV, R_QN)
    mkvt_ref[0] = mkvt
    vmt_ref[0] = mkvt[W_MOBA:].astype(BF16)
    qnt_ref[0] = frow(R_QN, R_NKV).astype(BF16)
    nkvt = frow(R_NKV, R_WKV)
    nkvt_ref[0] = nkvt
    vst_ref[0] = nkvt[3 * W_NSA_KV:].astype(BF16)
    wkvt = frow(R_WKV, R_G)
    wkvt_ref[0] = wkvt
    vwt_ref[0] = wkvt[W_NSA_KV:].astype(BF16)
    gt_ref[0] = _sigmoid(frow(R_G, R_END))
    km = col(C_MKV, C_MKV + W_MOBA)
    km_ref[...] = km.astype(BF16)
    kmean_ref[0] = jnp.mean(km, axis=0, keepdims=True)
    ksw_ref[...] = jnp.concatenate([col(C_NKV + 2 * W_NSA_KV, C_NKV + 3 * W_NSA_KV), col(C_WKV, C_WKV + W_NSA_KV)],
                                   axis=1).astype(BF16)
    zm_ref[...] = col(C_ZM, C_QN).astype(BF16)
    zn_ref[...] = col(C_ZN, C_G).astype(BF16)


def _proj_fm(x2d, sc, sh, gain, w, wt, b, tm):
    r, d = x2d.shape
    t = r // b
    tpb = t // tm
    nt = r // tm
    row = lambda width: pl.BlockSpec((tm, width), lambda i: (i, 0))
    fm = lambda rows: pl.BlockSpec((1, rows, tm), lambda i: (i // tpb, 0, i % tpb))
    mod = pl.BlockSpec((1, 1, d), lambda i: (i // tpb, 0, 0))
    const = lambda a: pl.BlockSpec(a.shape, lambda i: (0, 0), pipeline_mode=pl.Buffered(1))
    fm_outs = [(512, BF16), (1024, F32), (512, BF16), (1024, BF16), (512, F32), (128, BF16), (256, F32), (128, BF16),
               (256, F32)]
    tm_outs = [(512, BF16), (256, BF16), (512, BF16), (512, BF16)]
    return pl.pallas_call(
        _proj_fm_kernel,
        grid=(nt,),
        in_specs=[row(d), mod, mod, pl.BlockSpec((1, d), lambda i: (0, 0)), const(w), const(wt)],
        out_specs=[fm(rows) for rows, _ in fm_outs] + [row(wd) for wd, _ in tm_outs]
        + [pl.BlockSpec((1, 1, W_MOBA), lambda i: (i, 0, 0))],
        out_shape=[jax.ShapeDtypeStruct((b, rows, t), dt) for rows, dt in fm_outs]
        + [jax.ShapeDtypeStruct((r, wd), dt) for wd, dt in tm_outs] + [jax.ShapeDtypeStruct((nt, 1, W_MOBA), F32)],
        compiler_params=_cparams(("arbitrary",)),
        name="proj_prompt",
    )(x2d, sc, sh, gain, w, wt)


def _flash_step(s, v, m_ref, l_ref, acc_ref, feature_major=False):
    m_old = m_ref[...]
    m_new = jnp.maximum(m_old, jnp.max(s, axis=1, keepdims=True))
    alpha = jnp.exp2(m_old - m_new)
    p = jnp.exp2(s - m_new)
    l_ref[...] = alpha * l_ref[...] + jnp.sum(p, axis=1, keepdims=True)
    pv = _dot_nt(p.astype(BF16), v) if feature_major else _dot(p.astype(BF16), v)
    acc_ref[...] = alpha * acc_ref[...] + pv
    m_ref[...] = m_new


def _flash_init(m_ref, l_ref, acc_ref):
    m_ref[...] = jnp.full(m_ref.shape, NEG, F32)
    l_ref[...] = jnp.zeros(l_ref.shape, F32)
    acc_ref[...] = jnp.zeros(acc_ref.shape, F32)


def _flash_step_t(s, vt, m_ref, acc_ref):
    alpha, p = _softmax_weights_t(s, jnp.max(s, axis=0, keepdims=True), m_ref)
    acc_ref[...] = alpha * acc_ref[...] + _weighted_values_t(vt, p)


def _softmax_weights_t(s, s_max, m_ref):
    m_old = m_ref[...]
    m_new = jnp.maximum(m_old, s_max)
    m_ref[...] = m_new
    return jnp.exp2(m_old - m_new), jnp.exp2(s - m_new).astype(BF16)


def _weighted_values_t(vt, p):
    vt_aug = jnp.concatenate([vt, jnp.ones((ONES_ROWS, vt.shape[1]), BF16)], axis=0)
    return _dot(vt_aug, p)


def _flash_init_t(m_ref, acc_ref):
    m_ref[...] = jnp.full(m_ref.shape, NEG, F32)
    acc_ref[...] = jnp.zeros(acc_ref.shape, F32)


def _flash_result_t(acc_ref):
    width = acc_ref.shape[0] - ONES_ROWS
    return acc_ref[0:width, :] / jnp.maximum(acc_ref[width:width + 1, :], TINY)


def _sweep_tiles(logits_at, values_at, table_at, i_diag, m_ref, acc_ref):
    g = SWEEP_TILES
    big = g * KEY_TILE
    n_far = jnp.maximum(i_diag - 1, 0)
    n_big = n_far // g
    rem = n_far - n_big * g

    def far_body(n, carry):
        s, s_max = carry
        nxt = logits_at(pl.multiple_of((n + 1) * big, big), big)
        alpha, p = _softmax_weights_t(s, s_max, m_ref)
        acc_ref[...] = alpha * acc_ref[...] + _weighted_values_t(values_at(pl.multiple_of(n * big, big), big), p)
        return nxt, jnp.max(nxt, axis=0, keepdims=True)

    s0 = logits_at(0, big)
    s, _ = lax.fori_loop(0, n_big, far_body, (s0, jnp.max(s0, axis=0, keepdims=True)))
    first_block = jnp.where(i_diag == 0, g, g - 1 - rem)
    _flash_step_t(s + table_at(pl.multiple_of(first_block * KEY_TILE, KEY_TILE), big),
                  values_at(pl.multiple_of(n_big * big, big), big), m_ref, acc_ref)

    @pl.when((i_diag >= 1) & (rem == g - 1))
    def _():
        off = pl.multiple_of(i_diag * KEY_TILE, KEY_TILE)
        _flash_step_t(logits_at(off, KEY_TILE) + table_at(g * KEY_TILE, KEY_TILE), values_at(off, KEY_TILE),
                      m_ref, acc_ref)


def _topk_mask(vals, k, axis=1):
    idxf = lax.broadcasted_iota(jnp.int32, vals.shape, axis).astype(F32)

    def body(_, taken):
        cur = jnp.where(taken > 0.0, -jnp.inf, vals)
        mx = jnp.max(cur, axis=axis, keepdims=True)
        first = jnp.min(jnp.where(cur == mx, idxf, 1e9), axis=axis, keepdims=True)
        return jnp.where(idxf == first, 1.0, taken)

    return lax.fori_loop(0, k, body, jnp.zeros(vals.shape, F32))


def _masked_softmax(logits, valid, axis=1):
    lm = jnp.where(valid, logits, NEG)
    p = jnp.exp2(lm - jnp.max(lm, axis=axis, keepdims=True)) * valid.astype(F32)
    return p / jnp.maximum(jnp.sum(p, axis=axis, keepdims=True), TINY)


def _moba_kernel(qt_ref, k_ref, vt_ref, km_ref, kp_ref, tab_ref, o_ref, qaug_ref, m_ref, acc_ref, *, n_blocks):
    i = pl.program_id(2)
    tq = qt_ref.shape[2]
    pair = 2 * MOBA_BLOCK
    qt = qt_ref[0]
    row = lax.broadcasted_iota(jnp.int32, qt.shape, 0)
    zero = jnp.zeros_like(qt)
    q2 = jnp.concatenate([jnp.where(row < HEAD_DIM, qt, zero), jnp.where(row >= HEAD_DIM, qt, zero)], axis=1)
    nbp = -(-n_blocks // ONES_ROWS) * ONES_ROWS
    sc = _dot(km_ref[0, :nbp, :].astype(BF16), q2)
    blk = lax.broadcasted_iota(jnp.int32, sc.shape, 0)
    col = lax.broadcasted_iota(jnp.int32, (1, sc.shape[1]), 1)
    cur = 2 * i + ((col & (tq - 1)) >> int(math.log2(MOBA_BLOCK)))
    cand = blk < cur
    taken = _topk_mask(jnp.where(cand, sc, NEG), MOBA_TOPK, axis=0)
    sel = ((taken > 0.0) & cand) | (blk == cur)
    selb = jnp.concatenate([jnp.where(sel, 0.0, NEG), jnp.full((LANE - nbp, sc.shape[1]), NEG, F32)], axis=0)
    qaug_ref[...] = jnp.concatenate([q2, selb.astype(BF16)], axis=0)
    _flash_init_t(m_ref, acc_ref)

    def logits_of(p):
        off = pl.multiple_of(p * pair, pair)
        kaug = jnp.concatenate([k_ref[0, pl.ds(off, pair), :], kp_ref[pl.ds(off, pair), :]], axis=1)
        return _dot(kaug, qaug_ref[...])

    values_of = lambda p: vt_ref[0, :, pl.ds(pl.multiple_of(p * pair, pair), pair)]

    def far_body(p, carry):
        s, s_max = carry
        nxt = logits_of(p + 1)
        alpha, w = _softmax_weights_t(s, s_max, m_ref)
        acc_ref[...] = alpha * acc_ref[...] + _weighted_values_t(values_of(p), w)
        return nxt, jnp.max(nxt, axis=0, keepdims=True)

    s0 = logits_of(0)
    s, _ = lax.fori_loop(0, jnp.maximum(i - 1, 0), far_body, (s0, jnp.max(s0, axis=0, keepdims=True)))
    toff = pl.multiple_of(jnp.where(i == 0, pair, 0), pair)
    _flash_step_t(s + tab_ref[0, pl.ds(toff, pair), :], values_of(jnp.maximum(i - 1, 0)), m_ref, acc_ref)

    @pl.when(i >= 1)
    def _():
        _flash_step_t(logits_of(i) + tab_ref[0, pair:2 * pair, :], values_of(i), m_ref, acc_ref)

    o = _flash_result_t(acc_ref)
    o_ref[0] = jnp.where(row < HEAD_DIM, o[:, :tq], o[:, tq:]).astype(BF16)


def _moba_prompt(qt, mkv16, vt, kmean_pad, kp, tab):
    b, _, t = qt.shape
    tq = MOBA_TQ
    nq = t // tq
    hp = H_MOBA // 2
    cols = 2 * tq
    return pl.pallas_call(
        functools.partial(_moba_kernel, n_blocks=t // MOBA_BLOCK),
        grid=(b, hp, nq),
        in_specs=[pl.BlockSpec((1, LANE, tq), lambda bb, h, i: (bb, h, i)),
                  pl.BlockSpec((1, t, LANE), lambda bb, h, i: (bb, 0, h)),
                  pl.BlockSpec((1, LANE, t), lambda bb, h, i: (bb, h, 0)),
                  pl.BlockSpec((1, LANE, LANE), lambda bb, h, i: (bb, 0, h)),
                  pl.BlockSpec((t, LANE), lambda bb, h, i: (0, 0), pipeline_mode=pl.Buffered(1)),
                  pl.BlockSpec((1,) + tab.shape[1:], lambda bb, h, i: (h, 0, 0), pipeline_mode=pl.Buffered(1))],
        out_specs=pl.BlockSpec((1, LANE, tq), lambda bb, h, i: (bb, h, i)),
        out_shape=jax.ShapeDtypeStruct((b, W_MOBA, t), BF16),
        scratch_shapes=[pltpu.VMEM((2 * LANE, cols), BF16),
                        pltpu.VMEM((1, cols), F32), pltpu.VMEM((LANE + ONES_ROWS, cols), F32)],
        compiler_params=_cparams(("arbitrary", "arbitrary", "arbitrary")),
        name="moba_prompt",
    )(qt, mkv16, vt, kmean_pad, kp, tab)


def _cmp_kernel(pt_ref, *refs, pps):
    del pt_ref
    pages = refs[:pps]
    (pe_ref, kwa_ref, kwb_ref, kw2_ref, vwa_ref, vwb_ref, vw2_ref, kc_ref, vc_ref,
     uk_ref, uv_ref, sk_ref, sv_ref) = refs[pps:]
    j = pl.program_id(1)
    page = pages[0].shape[2]
    groups = pps * page // CMP_STRIDE
    for u in range(pps):
        xt = pages[u][0]
        sk_ref[u * page:(u + 1) * page, :] = xt[:LANE, :].T
        sv_ref[u * page:(u + 1) * page, :] = xt[LANE:, :].T
    row0 = pl.multiple_of(j * groups, groups)
    for l in range(CMP_STRIDE):
        rows_l = pl.ds(l, groups, stride=CMP_STRIDE)
        uk_ref[pl.ds(row0, groups), l * LANE:(l + 1) * LANE] = sk_ref[rows_l, :]
        uv_ref[pl.ds(row0, groups), l * LANE:(l + 1) * LANE] = sv_ref[rows_l, :]

    @pl.when(j == pl.num_programs(1) - 1)
    def _():
        nc = uk_ref.shape[0]
        last = lax.broadcasted_iota(jnp.int32, (nc, LANE), 0) == nc - 1

        def compress(u_ref, pe, wa_ref, wb_ref, w2_ref, out_ref):
            u = u_ref[...]
            pa = _dot((u + pe[0]).astype(BF16), wa_ref[...])
            pb = _dot((u + pe[1]).astype(BF16), wb_ref[...])
            pre = pa + pltpu.roll(pb, nc - 1, 0)
            hid = pre * _sigmoid(pre)
            out = _dot(hid.astype(BF16), w2_ref[...])
            out_ref[0] = jnp.where(last, 0.0, out).astype(BF16)

        compress(uk_ref, pe_ref[0], kwa_ref, kwb_ref, kw2_ref, kc_ref)
        compress(uv_ref, pe_ref[1], vwa_ref, vwb_ref, vw2_ref, vc_ref)


def _compress(pages_arr, pt, page, pe, kwa, kwb, kw2, vwa, vwb, vw2, paged):
    b, n_pg = pt.shape
    pps = min(PAGES_PER_STEP, n_pg)
    nc = n_pg * page // CMP_STRIDE
    if paged:
        page_spec = lambda u: pl.BlockSpec((1, 2 * LANE, page), lambda bb, j, p: (p[bb, j * pps + u], 0, 0))
    else:
        page_spec = lambda u: pl.BlockSpec((1, 2 * LANE, page), lambda bb, j, p: (bb, 0, j * pps + u))
    full = lambda a: pl.BlockSpec(a.shape, lambda bb, j, p: (0,) * a.ndim)
    consts = (pe, kwa, kwb, kw2, vwa, vwb, vw2)
    grid_spec = pltpu.PrefetchScalarGridSpec(
        num_scalar_prefetch=1,
        grid=(b, n_pg // pps),
        in_specs=[page_spec(u) for u in range(pps)] + [full(a) for a in consts],
        out_specs=[pl.BlockSpec((1, nc, LANE), lambda bb, j, p: (bb, 0, 0))] * 2,
        scratch_shapes=[pltpu.VMEM((nc, CMP_STRIDE * LANE), F32)] * 2 + [pltpu.VMEM((pps * page, LANE), F32)] * 2)
    return pl.pallas_call(
        functools.partial(_cmp_kernel, pps=pps),
        grid_spec=grid_spec,
        out_shape=[jax.ShapeDtypeStruct((b, nc, LANE), BF16)] * 2,
        compiler_params=_cparams(("arbitrary", "arbitrary")),
        name="nsa_compress",
    )(pt, *([pages_arr] * pps), *consts)


def _nsa_kernel(qt_ref, kc_ref, vct_ref, ovt_ref, ks_ref, vst_ref, kw_ref, vwt_ref, kp_ref, ts_ref, tw_ref, gt_ref,
                o_ref, qaug_ref, m_ref, acc_ref, ocmp_ref, oslc_ref):
    k = pl.program_id(1)
    i = pl.program_id(2)
    tq = qt_ref.shape[2]
    q4 = jnp.concatenate([qt_ref[0, g * LANE:(g + 1) * LANE, :] for g in range(NSA_GROUP)], axis=1)
    qpos1 = i * tq + lax.broadcasted_iota(jnp.int32, (1, tq), 1)
    qpos = jnp.concatenate([qpos1] * NSA_GROUP, axis=1)

    lc = _dot(kc_ref[0], q4)
    tok = lax.broadcasted_iota(jnp.int32, lc.shape, 0)
    lm = jnp.where(tok * CMP_STRIDE + (CMP_LEN - 1) <= qpos, lc, NEG)
    p = jnp.exp2(lm - jnp.max(lm, axis=0, keepdims=True))
    norm = jnp.where(qpos >= CMP_LEN - 1, 1.0 / jnp.maximum(jnp.sum(p, axis=0, keepdims=True), TINY), 0.0)
    ocmp_ref[...] = _dot(vct_ref[0], p.astype(BF16)) * norm
    pc = p * norm

    pcs = pc[:, 0:tq] + pc[:, tq:2 * tq] + pc[:, 2 * tq:3 * tq] + pc[:, 3 * tq:4 * tq]
    ph, plo = _split_bf16(pcs)
    imp = _dot(ovt_ref[...], ph) + _dot(ovt_ref[...], plo)
    jb = lax.broadcasted_iota(jnp.int32, imp.shape, 0)
    cur = qpos1 >> int(math.log2(SLC_BLOCK))
    avail = jb <= cur
    forced = (jb == 0) | (jb == cur) | (jb == cur - 1)
    imp = jnp.where(avail, jnp.where(forced, FORCE, imp), NEG)
    sel = (_topk_mask(imp, SLC_TOPN, axis=0) > 0.0) & avail
    selb = jnp.where(sel, 0.0, NEG).astype(BF16)
    qaug_ref[...] = jnp.concatenate([q4, jnp.concatenate([selb] * NSA_GROUP, axis=1)], axis=0)

    cd = (i * tq) // KEY_TILE
    a0 = pl.multiple_of((i * tq) % KEY_TILE, tq)

    def table(t_ref, r0, n):
        return jnp.concatenate([t_ref[0, pl.ds(r0, n), pl.ds(g * KEY_TILE + a0, tq)] for g in range(NSA_GROUP)],
                               axis=1)

    _flash_init_t(m_ref, acc_ref)

    def slc_logits_at(off, n):
        kaug = jnp.concatenate([ks_ref[0, pl.ds(off, n), :], kp_ref[pl.ds(off, n), :]], axis=1)
        return _dot(kaug, qaug_ref[...])

    _sweep_tiles(slc_logits_at, lambda off, n: vst_ref[0, :, pl.ds(off, n)], functools.partial(table, ts_ref),
                 cd, m_ref, acc_ref)
    oslc_ref[...] = _flash_result_t(acc_ref)

    _flash_init_t(m_ref, acc_ref)
    n_win = WINDOW // KEY_TILE
    span = WINDOW + KEY_TILE
    first = jnp.maximum(cd - n_win, 0)
    off = pl.multiple_of(first * KEY_TILE, KEY_TILE)
    toff = pl.multiple_of((first - (cd - n_win)) * KEY_TILE, KEY_TILE)
    s = _dot(kw_ref[0, pl.ds(off, span), :], q4) + table(tw_ref, toff, span)
    _flash_step_t(s, vwt_ref[0, :, pl.ds(off, span)], m_ref, acc_ref)
    owin = _flash_result_t(acc_ref)

    ocmp = ocmp_ref[...]
    oslc = oslc_ref[...]
    gts = gt_ref[0]
    for g in range(NSA_GROUP):
        c = slice(g * tq, (g + 1) * tq)
        og = (gts[g:g + 1, :] * ocmp[:, c] + gts[NSA_GROUP + g:NSA_GROUP + g + 1, :] * oslc[:, c]
              + gts[2 * NSA_GROUP + g:2 * NSA_GROUP + g + 1, :] * owin[:, c])
        o_ref[0, g * HEAD_DIM:(g + 1) * HEAD_DIM, :] = jnp.where(k == 0, og[:HEAD_DIM], og[HEAD_DIM:]).astype(BF16)


def _nsa_prompt(qt, kc, vct, ovt, ksw, vst, vwt, kps, ts, tw, gt):
    b, _, t = qt.shape
    tq = min(NSA_TQ, t)
    nq = t // tq
    cols = NSA_GROUP * tq
    nc = kc.shape[1]
    keys = lambda col: pl.BlockSpec((1, t, LANE), lambda bb, k, i: (bb, 0, col))
    vals = pl.BlockSpec((1, LANE, t), lambda bb, k, i: (bb, 0, 0))
    return pl.pallas_call(
        _nsa_kernel,
        grid=(b, H_NSA_KV, nq),
        in_specs=[pl.BlockSpec((1, NSA_GROUP * LANE, tq), lambda bb, k, i: (bb, k, i)),
                  pl.BlockSpec((1, nc, LANE), lambda bb, k, i: (bb, 0, 0)),
                  pl.BlockSpec((1, LANE, nc), lambda bb, k, i: (bb, 0, 0)),
                  pl.BlockSpec(ovt.shape, lambda bb, k, i: (0, 0)),
                  keys(0), vals, keys(1), vals,
                  pl.BlockSpec((t, LANE), lambda bb, k, i: (0, 0), pipeline_mode=pl.Buffered(1)),
                  pl.BlockSpec((1,) + ts.shape[1:], lambda bb, k, i: (k, 0, 0), pipeline_mode=pl.Buffered(1)),
                  pl.BlockSpec((1,) + tw.shape[1:], lambda bb, k, i: (k, 0, 0), pipeline_mode=pl.Buffered(1)),
                  pl.BlockSpec((1, LANE, tq), lambda bb, k, i: (bb, k, i))],
        out_specs=pl.BlockSpec((1, NSA_GROUP * HEAD_DIM, tq), lambda bb, k, i: (bb, k, i)),
        out_shape=jax.ShapeDtypeStruct((b, W_NSA, t), BF16),
        scratch_shapes=[pltpu.VMEM((2 * LANE, cols), BF16),
                        pltpu.VMEM((1, cols), F32), pltpu.VMEM((LANE + ONES_ROWS, cols), F32),
                        pltpu.VMEM((LANE, cols), F32), pltpu.VMEM((LANE, cols), F32)],
        compiler_params=_cparams(("arbitrary", "arbitrary", "arbitrary")),
        name="nsa_prompt",
    )(qt, kc, vct, ovt, ksw, vst, ksw, vwt, kps, ts, tw, gt)


def _moba_dec_kernel(pt_ref, qbd_ref, new_ref, td_ref, *refs, pps, nblk):
    del pt_ref
    pages = refs[:pps]
    o_ref, sc_ref, mall_ref, lall_ref, oall_ref = refs[pps:]
    j = pl.program_id(1)
    qbd = qbd_ref[0]
    rows = qbd.shape[0]
    lane = lax.broadcasted_iota(jnp.int32, (rows, LANE), 1)
    ppb = MOBA_BLOCK // pages[0].shape[2]

    @pl.when(j == 0)
    def _():
        sc_ref[...] = jnp.zeros(sc_ref.shape, F32)
        mall_ref[...] = jnp.full(mall_ref.shape, NEG, F32)
        lall_ref[...] = jnp.zeros(lall_ref.shape, F32)

    for u in range(pps // ppb):
        blk = j * (pps // ppb) + u
        kt = jnp.concatenate([pages[u * ppb + w][0, :W_MOBA, :] for w in range(ppb)], axis=1)
        vt = jnp.concatenate([pages[u * ppb + w][0, W_MOBA:, :] for w in range(ppb)], axis=1)
        s = _dot(qbd, kt.astype(BF16))
        sc_ref[...] = jnp.where(lane == blk, jnp.sum(s, axis=1, keepdims=True), sc_ref[...])
        near = (blk == nblk - 1).astype(F32)
        s = s + near * td_ref[:, 0:MOBA_BLOCK]
        mb = jnp.max(s, axis=1, keepdims=True)
        p = jnp.exp2(s - mb)
        mall_ref[...] = jnp.where(lane == blk, mb, mall_ref[...])
        lall_ref[...] = jnp.where(lane == blk, jnp.sum(p, axis=1, keepdims=True), lall_ref[...])
        oall_ref[pl.ds(blk, 1)] = _dot_nt(p.astype(BF16), vt.astype(BF16))[None]

    @pl.when(j == pl.num_programs(1) - 1)
    def _():
        kn = new_ref[0, :, :W_MOBA].astype(BF16)
        vn = new_ref[0, :, W_MOBA:].astype(BF16)
        s_own = _dot_nt(qbd, kn) + td_ref[:, MOBA_BLOCK:MOBA_BLOCK + NEW_PAD]
        m_own = jnp.max(s_own, axis=1, keepdims=True)
        p_own = jnp.exp2(s_own - m_own)
        l_own = jnp.sum(p_own, axis=1, keepdims=True)
        o_own = _dot(p_own.astype(BF16), vn)
        sc = sc_ref[...]
        cand = lane < nblk
        sel = (_topk_mask(jnp.where(cand, sc, NEG), MOBA_TOPK) > 0.0) & cand
        mall = mall_ref[...]
        m_fin = jnp.maximum(jnp.max(jnp.where(sel, mall, NEG), axis=1, keepdims=True), m_own)
        w = jnp.where(sel, jnp.exp2(mall - m_fin), 0.0)
        w_own = jnp.exp2(m_own - m_fin)
        l = jnp.sum(w * lall_ref[...], axis=1, keepdims=True) + w_own * l_own
        acc = w_own * o_own
        for n in range(nblk):
            acc = acc + w[:, n:n + 1] * oall_ref[n]
        o = acc / jnp.maximum(l, TINY)
        rowh = lax.broadcasted_iota(jnp.int32, o.shape, 0) & (H_MOBA - 1)
        laneh = lax.broadcasted_iota(jnp.int32, o.shape, 1) >> int(math.log2(HEAD_DIM))
        o = jnp.where(rowh == laneh, o, 0.0)
        o_ref[0] = jnp.sum(o.reshape(rows // H_MOBA, H_MOBA, W_MOBA), axis=1)


def _moba_decode(cache_m, pt, qbd, new_pad, td):
    b, n_pg = pt.shape
    page = cache_m.shape[2]
    pps = min(PAGES_PER_STEP, n_pg)
    nblk = n_pg * page // MOBA_BLOCK
    rows = qbd.shape[1]
    s_new = rows // H_MOBA
    page_spec = lambda u: pl.BlockSpec((1, 2 * W_MOBA, page), lambda bb, j, p: (p[bb, j * pps + u], 0, 0))
    grid_spec = pltpu.PrefetchScalarGridSpec(
        num_scalar_prefetch=1,
        grid=(b, n_pg // pps),
        in_specs=[pl.BlockSpec((1, rows, W_MOBA), lambda bb, j, p: (bb, 0, 0)),
                  pl.BlockSpec((1, NEW_PAD, 2 * W_MOBA), lambda bb, j, p: (bb, 0, 0)),
                  pl.BlockSpec(td.shape, lambda bb, j, p: (0, 0))] + [page_spec(u) for u in range(pps)],
        out_specs=pl.BlockSpec((1, s_new, W_MOBA), lambda bb, j, p: (bb, 0, 0)),
        scratch_shapes=[pltpu.VMEM((rows, LANE), F32), pltpu.VMEM((rows, LANE), F32), pltpu.VMEM((rows, LANE), F32),
                        pltpu.VMEM((nblk, rows, W_MOBA), F32)])
    return pl.pallas_call(
        functools.partial(_moba_dec_kernel, pps=pps, nblk=nblk),
        grid_spec=grid_spec,
        out_shape=jax.ShapeDtypeStruct((b, s_new, W_MOBA), F32),
        compiler_params=_cparams(("arbitrary", "arbitrary")),
        name="moba_decode",
    )(pt, qbd, new_pad, td, *([cache_m] * pps))


def _nsa_dec_kernel(pt_ref, qd_ref, kc_ref, vc_ref, ov_ref, kp_ref, new_ref, win_ref, neww_ref, g_ref,
                    ts_ref, tso_ref, tw_ref, two_ref, *refs, pps, past):
    del pt_ref
    pages = refs[:pps]
    o_ref, qaug_ref, m_ref, l_ref, acc_ref, ocmp_ref, selown_ref = refs[pps:]
    j = pl.program_id(1)
    nj = pl.num_programs(1)
    qd = qd_ref[0]
    rows = qd.shape[0]
    page = pages[0].shape[2]
    grp = rows // NSA_GROUP
    s_new = grp // H_NSA_KV

    @pl.when(j == 0)
    def _():
        lc = _dot_nt(qd, kc_ref[0])
        tok = lax.broadcasted_iota(jnp.int32, lc.shape, 1)
        qpos = past + (lax.broadcasted_iota(jnp.int32, (rows, 1), 0) & (s_new - 1))
        pc = _masked_softmax(lc, tok * CMP_STRIDE + (CMP_LEN - 1) <= qpos)
        ocmp_ref[...] = _dot(pc.astype(BF16), vc_ref[0])
        pcs = pc[0:grp] + pc[grp:2 * grp] + pc[2 * grp:3 * grp] + pc[3 * grp:4 * grp]
        ph, plo = _split_bf16(pcs)
        imp = _dot(ph, ov_ref[...]) + _dot(plo, ov_ref[...])
        jb = lax.broadcasted_iota(jnp.int32, imp.shape, 1)
        cur = qpos[0:grp] >> int(math.log2(SLC_BLOCK))
        avail = jb <= cur
        forced = (jb == 0) | (jb == cur) | (jb == cur - 1)
        imp = jnp.where(avail, jnp.where(forced, FORCE, imp), NEG)
        sel = (_topk_mask(imp, SLC_TOPN) > 0.0) & avail
        selb = jnp.concatenate([jnp.where(sel, 0.0, NEG)] * NSA_GROUP, axis=0)
        qaug_ref[...] = jnp.concatenate([qd, selb[:, :LANE].astype(BF16)], axis=1)
        n_own = past // SLC_BLOCK
        selown_ref[...] = jnp.broadcast_to(selb[:, n_own:n_own + 1], selown_ref.shape)
        _flash_init(m_ref, l_ref, acc_ref)

    span = pps * page
    off = pl.multiple_of(j * span, span)
    kst = jnp.concatenate([pages[u][0, :LANE, :] for u in range(pps)], axis=1).astype(BF16)
    vst = jnp.concatenate([pages[u][0, LANE:, :] for u in range(pps)], axis=1).astype(BF16)
    kaug = jnp.concatenate([kst, kp_ref[:, pl.ds(off, span)]], axis=0)
    near = (j == nj - 1).astype(F32)
    s = _dot(qaug_ref[...], kaug) + near * ts_ref[...]
    _flash_step(s, vst, m_ref, l_ref, acc_ref, feature_major=True)

    @pl.when(j == nj - 1)
    def _():
        new = new_ref[0]
        s_own = _dot_nt(qd, new[:, 2 * LANE:3 * LANE].astype(BF16)) + tso_ref[...] + selown_ref[:, 0:1]
        _flash_step(s_own, new[:, 3 * LANE:].astype(BF16), m_ref, l_ref, acc_ref)
        oslc = acc_ref[...] / jnp.maximum(l_ref[...], TINY)
        _flash_init(m_ref, l_ref, acc_ref)
        win = win_ref[0]
        _flash_step(_dot(qd, win[:LANE, :].astype(BF16)) + tw_ref[...], win[LANE:, :].astype(BF16),
                    m_ref, l_ref, acc_ref, feature_major=True)
        neww = neww_ref[0]
        _flash_step(_dot_nt(qd, neww[:, :LANE].astype(BF16)) + two_ref[...], neww[:, LANE:].astype(BF16),
                    m_ref, l_ref, acc_ref)
        owin = acc_ref[...] / jnp.maximum(l_ref[...], TINY)
        gts = g_ref[0]
        o_ref[0] = gts[:, 0:1] * ocmp_ref[...] + gts[:, 1:2] * oslc + gts[:, 2:3] * owin


def _nsa_decode(cache_n, pt, qd, kc, vc, ov, kps, new_pad, win, neww, gts, ts, tso, tw, two):
    b, n_pg = pt.shape
    page = cache_n.shape[2]
    pps = min(PAGES_PER_STEP, n_pg)
    rows = qd.shape[1]
    nc = kc.shape[1]
    full = lambda a: pl.BlockSpec(a.shape, lambda bb, j, p: (0,) * a.ndim)
    per_b = lambda a: pl.BlockSpec((1,) + a.shape[1:], lambda bb, j, p: (bb,) + (0,) * (a.ndim - 1))
    page_spec = lambda u: pl.BlockSpec((1, 2 * LANE, page), lambda bb, j, p: (p[bb, j * pps + u], 1, 0))
    grid_spec = pltpu.PrefetchScalarGridSpec(
        num_scalar_prefetch=1,
        grid=(b, n_pg // pps),
        in_specs=[per_b(qd), per_b(kc), per_b(vc), full(ov), full(kps), per_b(new_pad), per_b(win), per_b(neww),
                  per_b(gts), full(ts), full(tso), full(tw), full(two)] + [page_spec(u) for u in range(pps)],
        out_specs=pl.BlockSpec((1, rows, LANE), lambda bb, j, p: (bb, 0, 0)),
        scratch_shapes=[pltpu.VMEM((rows, 2 * LANE), BF16),
                        pltpu.VMEM((rows, 1), F32), pltpu.VMEM((rows, 1), F32), pltpu.VMEM((rows, LANE), F32),
                        pltpu.VMEM((rows, LANE), F32), pltpu.VMEM((rows, LANE), F32)])
    return pl.pallas_call(
        functools.partial(_nsa_dec_kernel, pps=pps, past=n_pg * page),
        grid_spec=grid_spec,
        out_shape=jax.ShapeDtypeStruct((b, rows, LANE), F32),
        compiler_params=_cparams(("arbitrary", "arbitrary")),
        name="nsa_decode",
    )(pt, qd, kc, vc, ov, kps, new_pad, win, neww, gts, ts, tso, tw, two, *([cache_n] * pps))


def _out_kernel(x_ref, om_ref, zm_ref, on_ref, zn_ref, gate_ref, w_ref, fg_ref, y_ref, *, feature_major):
    zm = zm_ref[...].astype(F32)
    zn = zn_ref[...].astype(F32)
    if feature_major:
        om = om_ref[0].astype(F32).T
        on = on_ref[0].astype(F32).T
    else:
        om = om_ref[...].astype(F32)
        on = on_ref[...].astype(F32)
    mm = (om * (zm * _sigmoid(zm))).astype(BF16)
    mn = (on * (zn * _sigmoid(zn))).astype(BF16)
    mixed = _dot(mm, w_ref[:W_MOBA, :]) + _dot(mn, w_ref[W_MOBA:, :])
    xn = x_ref[...] + gate_ref[0] * mixed
    inv = lax.rsqrt(jnp.mean(xn * xn, axis=-1, keepdims=True) + RMS_EPS)
    y_ref[...] = (xn * inv) * fg_ref[...]


def _out_proj(x2d, om, zm, on, zn, gate, w_out, fgain, tm, tiles_per_mod):
    r, d = x2d.shape
    mrows = gate.shape[1]
    feature_major = om.ndim == 3
    row = lambda width: pl.BlockSpec((tm, width), lambda i: (i, 0))
    if feature_major:
        mixer = pl.BlockSpec((1, om.shape[1], tm), lambda i: (i // tiles_per_mod, 0, i % tiles_per_mod))
    else:
        mixer = row(W_MOBA)
    return pl.pallas_call(
        functools.partial(_out_kernel, feature_major=feature_major),
        grid=(r // tm,),
        in_specs=[row(d), mixer, row(W_MOBA), mixer, row(W_NSA),
                  pl.BlockSpec((1, mrows, d), lambda i: (i // tiles_per_mod, 0, 0)),
                  pl.BlockSpec((d, d), lambda i: (0, 0)),
                  pl.BlockSpec((1, d), lambda i: (0, 0))],
        out_specs=row(d),
        out_shape=jax.ShapeDtypeStruct((r, d), F32),
        compiler_params=_cparams(("arbitrary",)),
        name="out_proj",
    )(x2d, om, zm, on, zn, gate, w_out, fgain)


def _t5_bucket(rel):
    n = np.maximum(rel, 0)
    exact = N_BUCKETS // 2
    x = np.log(np.maximum(n, 1) / exact) / math.log(MAX_DISTANCE / exact) * (N_BUCKETS - exact)
    near_boundary = (np.abs(x - np.round(x)) < 1e-3) & (n > exact) & (n < MAX_DISTANCE)
    assert not near_boundary.any()
    large = exact + np.floor(x + 1e-6).astype(np.int64)
    return np.where(n < exact, n, np.minimum(large, N_BUCKETS - 1)).astype(np.int32)


def _bias_of_rel(tab, rel, shift_far):
    onehot = (jnp.asarray(_t5_bucket(rel))[..., None] == jnp.arange(N_BUCKETS, dtype=jnp.int32)).astype(F32)
    val = jnp.moveaxis(jnp.dot(onehot, tab, precision=lax.Precision.HIGHEST), -1, 0)
    if shift_far:
        val = val - tab[N_BUCKETS - 1].reshape((-1,) + (1,) * rel.ndim)
    return jnp.where(jnp.asarray(rel >= 0)[None], val * LOG2E, NEG)


def _layout_w_in(w_in):
    d = w_in.shape[0]
    sc = HEAD_DIM ** -0.5 * LOG2E
    o = 4 * W_MOBA
    q_m, rest_m = w_in[:, :W_MOBA] * sc, w_in[:, W_MOBA:o]
    q_n = (w_in[:, o:o + W_NSA] * sc).reshape(d, H_NSA_KV, NSA_GROUP, HEAD_DIM)
    qn_exp = jnp.concatenate([jnp.pad(q_n[:, 0], ((0, 0), (0, 0), (0, HEAD_DIM))),
                              jnp.pad(q_n[:, 1], ((0, 0), (0, 0), (HEAD_DIM, 0)))], axis=1).reshape(d, H_NSA * LANE)
    o += W_NSA
    kv_n = w_in[:, o:o + 6 * W_NSA_KV]
    o += 6 * W_NSA_KV
    g_n = jnp.swapaxes(w_in[:, o:o + 3 * H_NSA].reshape(d, H_NSA_KV, NSA_GROUP, 3), 2, 3)
    g_n = jnp.pad(g_n.reshape(d, H_NSA_KV, 3 * NSA_GROUP), ((0, 0), (0, 0), (0, LANE - 3 * NSA_GROUP)))
    z_n = w_in[:, o + 3 * H_NSA:]
    w = jnp.concatenate([q_m, rest_m, qn_exp, kv_n, z_n, g_n.reshape(d, H_NSA_KV * LANE)], axis=1).astype(BF16)
    assert w.shape[1] == C_END
    wt = jnp.concatenate([w[:, C_QM:C_ZM], w[:, C_QN:C_ZN], w[:, C_G:C_END]], axis=1).T
    assert wt.shape[0] == R_END
    return w, wt


def _layout_cmp(pe, w1, w2):
    w = w1.reshape(2, CMP_STRIDE, HEAD_DIM, CMP_HIDDEN)
    z = jnp.zeros_like(w)
    full = jnp.stack([jnp.concatenate([w, z], axis=-1), jnp.concatenate([z, w], axis=-1)], axis=2)
    full = full.reshape(2, CMP_STRIDE * H_NSA_KV * HEAD_DIM, H_NSA_KV * CMP_HIDDEN).astype(BF16)
    zz = jnp.zeros_like(w2)
    w2bd = jnp.concatenate([jnp.concatenate([w2, zz], axis=1), jnp.concatenate([zz, w2], axis=1)], axis=0).astype(BF16)
    pe2 = jnp.broadcast_to(pe.reshape(2, CMP_STRIDE, 1, HEAD_DIM), (2, CMP_STRIDE, H_NSA_KV, HEAD_DIM))
    return pe2.reshape(2, 1, CMP_STRIDE * LANE), full[0], full[1], w2bd


def _block_onehot(t, block):
    return (jnp.arange(t)[:, None] // block == jnp.arange(LANE)[None, :]).astype(BF16)


def _overlap(nc, n_cmp, lanes):
    i = jnp.arange(nc)[:, None]
    s = jnp.arange(lanes)[None, :]
    start = i * CMP_STRIDE
    hit = (start < s * SLC_BLOCK + SLC_BLOCK) & (start + CMP_LEN - 1 >= s * SLC_BLOCK) & (i < n_cmp)
    return hit.astype(BF16)


def kernel(x_prompt, x_sample, c_prompt, c_sample, cache_moba_kv, cache_nsa_kv, state_nsa_win, page_table, w_ada, b_ada, norm_gain, w_in, cmp_pe, cmp_k_w1, cmp_k_w2, cmp_v_w1, cmp_v_w2, w_out, rel_bias, final_gain):
    depth = w_in.shape[0]
    assert depth == 1
    B, T, D = x_prompt.shape
    BS, S, _ = x_sample.shape
    n_pg = page_table.shape[1]
    page = cache_moba_kv.shape[2]
    P = n_pg * page
    WB = state_nsa_win.shape[2]
    assert D == (H_MOBA + H_NSA) * HEAD_DIM and T % (SWEEP_TILES * KEY_TILE) == 0 and P % MOBA_BLOCK == 0 and S <= NEW_PAD
    assert T // SLC_BLOCK <= LANE and P // SLC_BLOCK <= LANE and T >= WINDOW and WB == WINDOW
    assert (BS * S) % SUBLANE == 0 and S & (S - 1) == 0

    w, wt = _layout_w_in(w_in[0])
    w_out_b = w_out[0].astype(BF16)
    gain = norm_gain[0].reshape(1, D)
    fgain = final_gain.reshape(1, D)
    kcmp_w = _layout_cmp(cmp_pe[0, 0], cmp_k_w1[0], cmp_k_w2[0])
    vcmp_w = _layout_cmp(cmp_pe[0, 1], cmp_v_w1[0], cmp_v_w2[0])
    pe2 = jnp.stack([kcmp_w[0], vcmp_w[0]], axis=0)
    cmp_consts = (pe2,) + kcmp_w[1:] + vcmp_w[1:]
    bias_m = rel_bias[:, :H_MOBA]
    bias_n = rel_bias[:, H_MOBA:]

    m_all = B + BS
    m_pad = -(-m_all // SUBLANE) * SUBLANE
    c_all = jnp.pad(jnp.concatenate([c_prompt, c_sample], axis=0), ((0, m_pad - m_all), (0, 0)))
    mod = _ada(c_all, w_ada[0], b_ada[0])
    shift, scale, gate = mod[:, :D], mod[:, D:2 * D], mod[:, 2 * D:]

    a = np.arange(KEY_TILE)[None, :]
    jk = np.arange(2 * KEY_TILE)[:, None]
    rel_s = a + KEY_TILE - jk
    heads_major = lambda tb, n_grp: jnp.transpose(
        tb.reshape(n_grp, -1, tb.shape[1], tb.shape[2]), (0, 2, 1, 3)).reshape(n_grp, tb.shape[1], -1)
    rows_of = lambda tb, n, val: jnp.full((tb.shape[0], n, tb.shape[2]), val, F32)
    pad_rows = (SWEEP_TILES - 1) * KEY_TILE
    sweep_rows = lambda tb: jnp.concatenate([rows_of(tb, pad_rows, 0.0), tb, rows_of(tb, pad_rows, NEG)], axis=1)
    rel_m = MOBA_BLOCK + np.arange(MOBA_TQ)[None, :] - np.arange(MOBA_BLOCK + MOBA_TQ)[:, None]
    tab_m = heads_major(_bias_of_rel(bias_m, rel_m, True), H_MOBA // 2)
    tab_m = jnp.concatenate([rows_of(tab_m, MOBA_BLOCK, 0.0), tab_m], axis=1)
    tab_s = sweep_rows(heads_major(_bias_of_rel(bias_n, rel_s, True), H_NSA_KV))
    jw = np.arange(WINDOW + KEY_TILE)[:, None]
    rel_w = a + WINDOW - jw
    tab_w = heads_major(_bias_of_rel(bias_n, np.where(rel_w < WINDOW, rel_w, -1), False), H_NSA_KV)
    tab_w = jnp.concatenate([tab_w, rows_of(tab_w, WINDOW, NEG)], axis=1)

    tpm = T // PROJ_TM
    (qmt, mkvt, vmt, qnt, nkvt, vst, wkvt, vwt, gt, km, ksw, zm, zn, kmean) = _proj_fm(
        x_prompt.reshape(B * T, D), scale[:B].reshape(B, 1, D), shift[:B].reshape(B, 1, D), gain, w, wt, B, PROJ_TM)
    nb = T // MOBA_BLOCK
    kmean_pad = jnp.pad(kmean.reshape(B, nb, W_MOBA), ((0, 0), (0, LANE - nb), (0, 0)))
    o_m = _moba_prompt(qmt, km.reshape(B, T, W_MOBA), vmt, kmean_pad, _block_onehot(T, MOBA_BLOCK), tab_m)
    pt_prompt = jnp.zeros((B, T // page), jnp.int32)
    kc_p, vc_p = _compress(nkvt, pt_prompt, page, *cmp_consts, paged=False)
    nc_p = T // CMP_STRIDE
    o_n = _nsa_prompt(qnt, kc_p, jnp.swapaxes(vc_p, 1, 2), _overlap(nc_p, nc_p - 1, LANE).T,
                      ksw.reshape(B, T, 2 * LANE), vst, vwt, _block_onehot(T, SLC_BLOCK), tab_s, tab_w, gt)
    y_prompt = _out_proj(x_prompt.reshape(B * T, D), o_m, zm, o_n, zn,
                         gate[:B].reshape(B, 1, D), w_out_b, fgain, PROJ_TM, tpm).reshape(B, T, D)
    tokens_first = lambda m, c: jnp.transpose(m.reshape(1, B, c, -1, HEAD_DIM, m.shape[-1]), (0, 1, 5, 2, 3, 4))
    moba_kv_prompt = tokens_first(mkvt, 2)
    nsa_kv_prompt = tokens_first(nkvt, 4)
    win_prompt = tokens_first(wkvt[:, :, T - WINDOW:], 2)

    RS = BS * S
    rep = lambda m: jnp.repeat(m[B:B + BS], S, axis=0)
    (qm_s, mkv32_s, zm_s, qn_s, nkv32_s, wkv32_s, zn_s, gts_s) = _proj(
        x_sample.reshape(RS, D), rep(scale), rep(shift), gain, w)
    pad_new = lambda m: jnp.pad(m.reshape(BS, S, -1), ((0, 0), (0, NEW_PAD - S), (0, 0)))
    s_idx = np.arange(S)

    q_rep = jnp.repeat(qm_s.reshape(BS, S, 1, H_MOBA, HEAD_DIM), H_MOBA, axis=2)
    eye = (jnp.arange(H_MOBA)[:, None] == jnp.arange(H_MOBA)[None, :])[None, None, :, :, None]
    qbd = jnp.where(eye, q_rep, 0).reshape(BS, S * H_MOBA, W_MOBA)
    rel_d = np.concatenate([MOBA_BLOCK + s_idx[:, None] - np.arange(MOBA_BLOCK)[None, :],
                            s_idx[:, None] - np.arange(NEW_PAD)[None, :]], axis=1)
    td = jnp.moveaxis(_bias_of_rel(bias_m, rel_d, True), 0, 1).reshape(S * H_MOBA, -1)
    td = jnp.pad(td, ((0, 0), (0, 3 * LANE - td.shape[1])))
    feat_pages = lambda c: jnp.transpose(c, (0, 2, 3, 4, 1)).reshape(c.shape[0], -1, c.shape[1])
    o_m_s = _moba_decode(feat_pages(cache_moba_kv[0]), page_table, qbd, pad_new(mkv32_s), td)

    cache_n = feat_pages(cache_nsa_kv[0])
    kc_s, vc_s = _compress(cache_n, page_table, page, *cmp_consts, paged=True)
    nc_s = P // CMP_STRIDE
    n_cmp_s = (P + S - CMP_LEN) // CMP_STRIDE + 1
    order = lambda m: jnp.transpose(m, (0, 3, 2, 1) + tuple(range(4, m.ndim)))
    rows_n = NSA_GROUP * H_NSA_KV * S
    qd = order(qn_s.reshape(BS, S, H_NSA_KV, NSA_GROUP, LANE)).reshape(BS, rows_n, LANE)
    g3 = gts_s.reshape(BS, S, H_NSA_KV, LANE)[..., :3 * NSA_GROUP].reshape(BS, S, H_NSA_KV, 3, NSA_GROUP)
    gts_d = order(jnp.swapaxes(g3, 3, 4)).reshape(BS, rows_n, 3)
    bias_d = jnp.swapaxes(bias_n.reshape(N_BUCKETS, H_NSA_KV, NSA_GROUP), 1, 2).reshape(N_BUCKETS, H_NSA)

    def dec_table(rel, shift_far):
        return _bias_of_rel(bias_d, rel, shift_far).reshape(rows_n, rel.shape[1])

    ts = dec_table(page + s_idx[:, None] - np.arange(page)[None, :], True)
    ts = jnp.pad(ts, ((0, 0), (min(PAGES_PER_STEP, n_pg) * page - page, 0)))
    rel_own = s_idx[:, None] - np.arange(NEW_PAD)[None, :]
    tso = dec_table(rel_own, True)
    rel_win = WB + s_idx[:, None] - np.arange(WB)[None, :]
    tw = dec_table(np.where(rel_win < WINDOW, rel_win, -1), False)
    two = dec_table(rel_own, False)
    win_state = feat_pages(state_nsa_win[0])
    o_n_raw = _nsa_decode(cache_n, page_table, qd, kc_s, vc_s, _overlap(nc_s, n_cmp_s, 2 * LANE),
                          _block_onehot(P, SLC_BLOCK).T, pad_new(nkv32_s), win_state, pad_new(wkv32_s), gts_d,
                          ts, tso, tw, two)
    o5 = o_n_raw.reshape(BS, NSA_GROUP, H_NSA_KV, S, H_NSA_KV, HEAD_DIM)
    o_n_s = jnp.stack([o5[:, :, k, :, k] for k in range(H_NSA_KV)], axis=1)
    o_n_s = jnp.transpose(o_n_s, (0, 3, 1, 2, 4)).reshape(RS, W_NSA)
    y_sample = _out_proj(x_sample.reshape(RS, D), o_m_s.reshape(RS, W_MOBA).astype(BF16), zm_s,
                         o_n_s.astype(BF16), zn_s, rep(gate).reshape(1, RS, D), w_out_b, fgain, RS, 1).reshape(BS, S, D)
    moba_kv_sample = mkv32_s.reshape(1, BS, S, 2, H_MOBA, HEAD_DIM)
    nsa_kv_sample = nkv32_s.reshape(1, BS, S, 4, H_NSA_KV, HEAD_DIM)
    win_sample = jnp.concatenate([win_state[:, :, S:], jnp.swapaxes(wkv32_s.reshape(BS, S, 2 * LANE), 1, 2)], axis=2)
    win_sample = jnp.transpose(win_sample.reshape(1, BS, 2, H_NSA_KV, HEAD_DIM, WB), (0, 1, 5, 2, 3, 4))
    return (y_prompt, y_sample, moba_kv_prompt, moba_kv_sample, nsa_kv_prompt, nsa_kv_sample, win_prompt, win_sample)
```

```python
import functools
import math

import jax
import jax.numpy as jnp
import numpy as np
from jax import lax
from jax.experimental import pallas as pl
from jax.experimental.pallas import tpu as pltpu

F32 = jnp.float32
BF16 = jnp.bfloat16

HEAD_DIM = 64
H_MOBA = 8
H_NSA = 8
H_NSA_KV = 2
NSA_GROUP = 4
W_MOBA = H_MOBA * HEAD_DIM
W_NSA = H_NSA * HEAD_DIM
W_NSA_KV = H_NSA_KV * HEAD_DIM
MOBA_BLOCK = 256
MOBA_TOPK = 3
CMP_LEN = 32
CMP_STRIDE = 16
CMP_HIDDEN = 2 * HEAD_DIM
SLC_BLOCK = 64
SLC_TOPN = 16
WINDOW = 512
N_BUCKETS = 32
MAX_DISTANCE = 128
RMS_EPS = 1e-6
NEG = -1e30
FORCE = 1e9
TINY = 1e-30

LANE = 128
SUBLANE = 8
KEY_TILE = 256
SWEEP_TILES = 2
NSA_TQ = 256
MOBA_TQ = 2 * MOBA_BLOCK
PROJ_TM = 256
PAGES_PER_STEP = 16
NEW_PAD = 8
ONES_ROWS = 16
LOG2E = math.log2(math.e)
VMEM_LIMIT = 56 * 1024 * 1024

C_QM, C_MKV, C_ZM, C_QN, C_NKV, C_WKV, C_ZN, C_G, C_END = 0, 512, 1536, 2048, 3072, 3584, 3840, 4352, 4608

_NT = (((1,), (1,)), ((), ()))


def _dot(a, b):
    return jnp.dot(a, b, preferred_element_type=F32)


def _dot_nt(a, b):
    return lax.dot_general(a, b, _NT, preferred_element_type=F32)


def _split_bf16(a):
    hi = a.astype(BF16)
    lo = (a - hi.astype(F32)).astype(BF16)
    return hi, lo


def _sigmoid(x):
    return 1.0 / (1.0 + jnp.exp(-x))


def _cparams(sem):
    return pltpu.CompilerParams(dimension_semantics=sem, vmem_limit_bytes=VMEM_LIMIT)


def _ada_kernel(c_ref, w_ref, b_ref, o_ref):
    ch, cl = _split_bf16(c_ref[...])
    wh, wl = _split_bf16(w_ref[...])
    o_ref[...] = _dot(ch, wh) + _dot(ch, wl) + _dot(cl, wh) + b_ref[...]


def _ada(c_all, w_ada, b_ada):
    m, d = c_all.shape
    n = w_ada.shape[1]
    tn = 512
    return pl.pallas_call(
        _ada_kernel,
        grid=(n // tn,),
        in_specs=[pl.BlockSpec((m, d), lambda j: (0, 0)),
                  pl.BlockSpec((d, tn), lambda j: (0, j)),
                  pl.BlockSpec((1, tn), lambda j: (0, j))],
        out_specs=pl.BlockSpec((m, tn), lambda j: (0, j)),
        out_shape=jax.ShapeDtypeStruct((m, n), F32),
        compiler_params=_cparams(("arbitrary",)),
        name="ada",
    )(c_all, w_ada, b_ada.reshape(1, n))


def _modulated_norm(x, gain, scale, shift):
    inv = lax.rsqrt(jnp.mean(x * x, axis=-1, keepdims=True) + RMS_EPS)
    return (x * inv) * gain * (1.0 + scale) + shift


def _proj_kernel(x_ref, sc_ref, sh_ref, gain_ref, w_ref, qm_ref, mkv_ref, zm_ref, qn_ref, nkv_ref, wkv_ref, zn_ref, g_ref):
    hb = _modulated_norm(x_ref[...], gain_ref[...], sc_ref[...], sh_ref[...]).astype(BF16)
    col = lambda a, b: _dot(hb, w_ref[:, a:b])
    qm_ref[...] = col(C_QM, C_MKV).astype(BF16)
    mkv_ref[...] = col(C_MKV, C_ZM)
    zm_ref[...] = col(C_ZM, C_QN).astype(BF16)
    qn_ref[...] = col(C_QN, C_NKV).astype(BF16)
    nkv_ref[...] = col(C_NKV, C_WKV)
    wkv_ref[...] = col(C_WKV, C_ZN)
    zn_ref[...] = col(C_ZN, C_G).astype(BF16)
    g_ref[...] = _sigmoid(col(C_G, C_END))


def _proj(x2d, sc, sh, gain, w):
    r, d = x2d.shape
    full = lambda width: pl.BlockSpec((r, width), lambda i: (0, 0))
    widths = [(512, BF16), (1024, F32), (512, BF16), (1024, BF16), (512, F32), (256, F32), (512, BF16), (256, F32)]
    return pl.pallas_call(
        _proj_kernel,
        grid=(1,),
        in_specs=[full(d), full(d), full(d), pl.BlockSpec((1, d), lambda i: (0, 0)),
                  pl.BlockSpec((d, C_END), lambda i: (0, 0))],
        out_specs=[full(wd) for wd, _ in widths],
        out_shape=[jax.ShapeDtypeStruct((r, wd), dt) for wd, dt in widths],
        compiler_params=_cparams(("arbitrary",)),
        name="proj_decode",
    )(x2d, sc, sh, gain, w)


R_QM, R_MKV, R_QN, R_NKV, R_WKV, R_G, R_END = 0, 512, 1536, 2560, 3072, 3328, 3584


def _proj_fm_kernel(x_ref, sc_ref, sh_ref, gain_ref, w_ref, wt_ref,
                    qmt_ref, mkvt_ref, vmt_ref, qnt_ref, nkvt_ref, vst_ref, wkvt_ref, vwt_ref, gt_ref,
                    km_ref, ksw_ref, zm_ref, zn_ref, kmean_ref):
    h = _modulated_norm(x_ref[...], gain_ref[...], sc_ref[0], sh_ref[0])
    hb = h.astype(BF16)
    ht = h.T.astype(BF16)
    frow = lambda a, b: _dot(wt_ref[a:b, :], ht)
    col = lambda a, b: _dot(hb, w_ref[:, a:b])
    qmt_ref[0] = frow(R_QM, R_MKV).astype(BF16)
    mkvt = frow(R_MKV, R_QN)
    mkvt_ref[0] = mkvt
    vmt_ref[0] = mkvt[W_MOBA:].astype(BF16)
    qnt_ref[0] = frow(R_QN, R_NKV).astype(BF16)
    nkvt = frow(R_NKV, R_WKV)
    nkvt_ref[0] = nkvt
    vst_ref[0] = nkvt[3 * W_NSA_KV:].astype(BF16)
    wkvt = frow(R_WKV, R_G)
    wkvt_ref[0] = wkvt
    vwt_ref[0] = wkvt[W_NSA_KV:].astype(BF16)
    gt_ref[0] = _sigmoid(frow(R_G, R_END))
    km = col(C_MKV, C_MKV + W_MOBA)
    km_ref[...] = km.astype(BF16)
    kmean_ref[0] = jnp.mean(km, axis=0, keepdims=True)
    ksw_ref[...] = jnp.concatenate([col(C_NKV + 2 * W_NSA_KV, C_NKV + 3 * W_NSA_KV), col(C_WKV, C_WKV + W_NSA_KV)],
                                   axis=1).astype(BF16)
    zm_ref[...] = col(C_ZM, C_QN).astype(BF16)
    zn_ref[...] = col(C_ZN, C_G).astype(BF16)


def _proj_fm(x2d, sc, sh, gain, w, wt, b, tm):
    r, d = x2d.shape
    t = r // b
    tpb = t // tm
    nt = r // tm
    row = lambda width: pl.BlockSpec((tm, width), lambda i: (i, 0))
    fm = lambda rows: pl.BlockSpec((1, rows, tm), lambda i: (i // tpb, 0, i % tpb))
    mod = pl.BlockSpec((1, 1, d), lambda i: (i // tpb, 0, 0))
    const = lambda a: pl.BlockSpec(a.shape, lambda i: (0, 0), pipeline_mode=pl.Buffered(1))
    fm_outs = [(512, BF16), (1024, F32), (512, BF16), (1024, BF16), (512, F32), (128, BF16), (256, F32), (128, BF16),
               (256, F32)]
    tm_outs = [(512, BF16), (256, BF16), (512, BF16), (512, BF16)]
    return pl.pallas_call(
        _proj_fm_kernel,
        grid=(nt,),
        in_specs=[row(d), mod, mod, pl.BlockSpec((1, d), lambda i: (0, 0)), const(w), const(wt)],
        out_specs=[fm(rows) for rows, _ in fm_outs] + [row(wd) for wd, _ in tm_outs]
        + [pl.BlockSpec((1, 1, W_MOBA), lambda i: (i, 0, 0))],
        out_shape=[jax.ShapeDtypeStruct((b, rows, t), dt) for rows, dt in fm_outs]
        + [jax.ShapeDtypeStruct((r, wd), dt) for wd, dt in tm_outs] + [jax.ShapeDtypeStruct((nt, 1, W_MOBA), F32)],
        compiler_params=_cparams(("arbitrary",)),
        name="proj_prompt",
    )(x2d, sc, sh, gain, w, wt)


def _flash_step(s, v, m_ref, l_ref, acc_ref, feature_major=False):
    m_old = m_ref[...]
    m_new = jnp.maximum(m_old, jnp.max(s, axis=1, keepdims=True))
    alpha = jnp.exp2(m_old - m_new)
    p = jnp.exp2(s - m_new)
    l_ref[...] = alpha * l_ref[...] + jnp.sum(p, axis=1, keepdims=True)
    pv = _dot_nt(p.astype(BF16), v) if feature_major else _dot(p.astype(BF16), v)
    acc_ref[...] = alpha * acc_ref[...] + pv
    m_ref[...] = m_new


def _flash_init(m_ref, l_ref, acc_ref):
    m_ref[...] = jnp.full(m_ref.shape, NEG, F32)
    l_ref[...] = jnp.zeros(l_ref.shape, F32)
    acc_ref[...] = jnp.zeros(acc_ref.shape, F32)


def _flash_step_t(s_ref, n_keys, vt, m_ref, acc_ref):
    s_max = jnp.max(s_ref[0:n_keys, :], axis=0, keepdims=True)
    alpha, p = _softmax_weights_t(s_ref[0:n_keys, :], s_max, m_ref)
    acc_ref[...] = alpha * acc_ref[...] + _weighted_values_t(vt, p)


def _softmax_weights_t(s, s_max, m_ref):
    m_old = m_ref[...]
    m_new = jnp.maximum(m_old, s_max)
    m_ref[...] = m_new
    return jnp.exp2(m_old - m_new), jnp.exp2(s - m_new).astype(BF16)


def _weighted_values_t(vt, p):
    vt_aug = jnp.concatenate([vt, jnp.ones((ONES_ROWS, vt.shape[1]), BF16)], axis=0)
    return _dot(vt_aug, p)


def _flash_init_t(m_ref, acc_ref):
    m_ref[...] = jnp.full(m_ref.shape, NEG, F32)
    acc_ref[...] = jnp.zeros(acc_ref.shape, F32)


def _flash_result_t(acc_ref):
    width = acc_ref.shape[0] - ONES_ROWS
    return acc_ref[0:width, :] / jnp.maximum(acc_ref[width:width + 1, :], TINY)


def _sweep_tiles(logits_at, values_at, table_at, i_diag, m_ref, acc_ref, s_ref):
    g = SWEEP_TILES
    big = g * KEY_TILE
    n_far = jnp.maximum(i_diag - 1, 0)
    n_big = n_far // g
    rem = n_far - n_big * g

    def far_body(n, s_max):
        alpha, p = _softmax_weights_t(s_ref[0:big, :], s_max, m_ref)
        acc_ref[...] = alpha * acc_ref[...] + _weighted_values_t(values_at(pl.multiple_of(n * big, big), big), p)
        nxt = logits_at(pl.multiple_of((n + 1) * big, big), big)
        s_ref[0:big, :] = nxt
        return jnp.max(nxt, axis=0, keepdims=True)

    s0 = logits_at(0, big)
    s_ref[0:big, :] = s0
    lax.fori_loop(0, n_big, far_body, jnp.max(s0, axis=0, keepdims=True))
    first_block = jnp.where(i_diag == 0, g, g - 1 - rem)
    s_ref[0:big, :] = s_ref[0:big, :] + table_at(pl.multiple_of(first_block * KEY_TILE, KEY_TILE), big)
    _flash_step_t(s_ref, big, values_at(pl.multiple_of(n_big * big, big), big), m_ref, acc_ref)

    @pl.when((i_diag >= 1) & (rem == g - 1))
    def _():
        off = pl.multiple_of(i_diag * KEY_TILE, KEY_TILE)
        s_ref[0:KEY_TILE, :] = logits_at(off, KEY_TILE) + table_at(g * KEY_TILE, KEY_TILE)
        _flash_step_t(s_ref, KEY_TILE, values_at(off, KEY_TILE), m_ref, acc_ref)


def _topk_mask(vals, k, axis=1):
    idxf = lax.broadcasted_iota(jnp.int32, vals.shape, axis).astype(F32)

    def body(_, taken):
        cur = jnp.where(taken > 0.0, -jnp.inf, vals)
        mx = jnp.max(cur, axis=axis, keepdims=True)
        first = jnp.min(jnp.where(cur == mx, idxf, 1e9), axis=axis, keepdims=True)
        return jnp.where(idxf == first, 1.0, taken)

    return lax.fori_loop(0, k, body, jnp.zeros(vals.shape, F32))


def _masked_softmax(logits, valid, axis=1):
    lm = jnp.where(valid, logits, NEG)
    p = jnp.exp2(lm - jnp.max(lm, axis=axis, keepdims=True)) * valid.astype(F32)
    return p / jnp.maximum(jnp.sum(p, axis=axis, keepdims=True), TINY)


def _moba_kernel(qt_ref, k_ref, vt_ref, km_ref, kp_ref, tab_ref, o_ref, qaug_ref, m_ref, acc_ref, s_ref, *, n_blocks):
    i = pl.program_id(2)
    tq = qt_ref.shape[2]
    pair = 2 * MOBA_BLOCK
    qt = qt_ref[0]
    row = lax.broadcasted_iota(jnp.int32, qt.shape, 0)
    zero = jnp.zeros_like(qt)
    q2 = jnp.concatenate([jnp.where(row < HEAD_DIM, qt, zero), jnp.where(row >= HEAD_DIM, qt, zero)], axis=1)
    nbp = -(-n_blocks // ONES_ROWS) * ONES_ROWS
    sc = _dot(km_ref[0, :nbp, :].astype(BF16), q2)
    blk = lax.broadcasted_iota(jnp.int32, sc.shape, 0)
    col = lax.broadcasted_iota(jnp.int32, (1, sc.shape[1]), 1)
    cur = 2 * i + ((col & (tq - 1)) >> int(math.log2(MOBA_BLOCK)))
    cand = blk < cur
    taken = _topk_mask(jnp.where(cand, sc, NEG), MOBA_TOPK, axis=0)
    sel = ((taken > 0.0) & cand) | (blk == cur)
    selb = jnp.concatenate([jnp.where(sel, 0.0, NEG), jnp.full((LANE - nbp, sc.shape[1]), NEG, F32)], axis=0)
    qaug_ref[...] = jnp.concatenate([q2, selb.astype(BF16)], axis=0)
    _flash_init_t(m_ref, acc_ref)

    def logits_of(p):
        off = pl.multiple_of(p * pair, pair)
        kaug = jnp.concatenate([k_ref[0, pl.ds(off, pair), :], kp_ref[pl.ds(off, pair), :]], axis=1)
        return _dot(kaug, qaug_ref[...])

    values_of = lambda p: vt_ref[0, :, pl.ds(pl.multiple_of(p * pair, pair), pair)]

    def far_body(p, s_max):
        alpha, w = _softmax_weights_t(s_ref[...], s_max, m_ref)
        acc_ref[...] = alpha * acc_ref[...] + _weighted_values_t(values_of(p), w)
        nxt = logits_of(p + 1)
        s_ref[...] = nxt
        return jnp.max(nxt, axis=0, keepdims=True)

    s0 = logits_of(0)
    s_ref[...] = s0
    lax.fori_loop(0, jnp.maximum(i - 1, 0), far_body, jnp.max(s0, axis=0, keepdims=True))
    toff = pl.multiple_of(jnp.where(i == 0, pair, 0), pair)
    s_ref[...] = s_ref[...] + tab_ref[0, pl.ds(toff, pair), :]
    _flash_step_t(s_ref, pair, values_of(jnp.maximum(i - 1, 0)), m_ref, acc_ref)

    @pl.when(i >= 1)
    def _():
        s_ref[...] = logits_of(i) + tab_ref[0, pair:2 * pair, :]
        _flash_step_t(s_ref, pair, values_of(i), m_ref, acc_ref)

    o = _flash_result_t(acc_ref)
    o_ref[0] = jnp.where(row < HEAD_DIM, o[:, :tq], o[:, tq:]).astype(BF16)


def _moba_prompt(qt, mkv16, vt, kmean_pad, kp, tab):
    b, _, t = qt.shape
    tq = MOBA_TQ
    nq = t // tq
    hp = H_MOBA // 2
    cols = 2 * tq
    return pl.pallas_call(
        functools.partial(_moba_kernel, n_blocks=t // MOBA_BLOCK),
        grid=(b, hp, nq),
        in_specs=[pl.BlockSpec((1, LANE, tq), lambda bb, h, i: (bb, h, i)),
                  pl.BlockSpec((1, t, LANE), lambda bb, h, i: (bb, 0, h)),
                  pl.BlockSpec((1, LANE, t), lambda bb, h, i: (bb, h, 0)),
                  pl.BlockSpec((1, LANE, LANE), lambda bb, h, i: (bb, 0, h)),
                  pl.BlockSpec((t, LANE), lambda bb, h, i: (0, 0), pipeline_mode=pl.Buffered(1)),
                  pl.BlockSpec((1,) + tab.shape[1:], lambda bb, h, i: (h, 0, 0), pipeline_mode=pl.Buffered(1))],
        out_specs=pl.BlockSpec((1, LANE, tq), lambda bb, h, i: (bb, h, i)),
        out_shape=jax.ShapeDtypeStruct((b, W_MOBA, t), BF16),
        scratch_shapes=[pltpu.VMEM((2 * LANE, cols), BF16),
                        pltpu.VMEM((1, cols), F32), pltpu.VMEM((LANE + ONES_ROWS, cols), F32),
                        pltpu.VMEM((2 * MOBA_BLOCK, cols), F32)],
        compiler_params=_cparams(("arbitrary", "arbitrary", "arbitrary")),
        name="moba_prompt",
    )(qt, mkv16, vt, kmean_pad, kp, tab)


def _cmp_kernel(pt_ref, *refs, pps):
    del pt_ref
    pages = refs[:pps]
    (pe_ref, kwa_ref, kwb_ref, kw2_ref, vwa_ref, vwb_ref, vw2_ref, kc_ref, vc_ref,
     uk_ref, uv_ref, sk_ref, sv_ref) = refs[pps:]
    j = pl.program_id(1)
    page = pages[0].shape[2]
    groups = pps * page // CMP_STRIDE
    for u in range(pps):
        xt = pages[u][0]
        sk_ref[u * page:(u + 1) * page, :] = xt[:LANE, :].T
        sv_ref[u * page:(u + 1) * page, :] = xt[LANE:, :].T
    row0 = pl.multiple_of(j * groups, groups)
    for l in range(CMP_STRIDE):
        rows_l = pl.ds(l, groups, stride=CMP_STRIDE)
        uk_ref[pl.ds(row0, groups), l * LANE:(l + 1) * LANE] = sk_ref[rows_l, :]
        uv_ref[pl.ds(row0, groups), l * LANE:(l + 1) * LANE] = sv_ref[rows_l, :]

    @pl.when(j == pl.num_programs(1) - 1)
    def _():
        nc = uk_ref.shape[0]
        last = lax.broadcasted_iota(jnp.int32, (nc, LANE), 0) == nc - 1

        def compress(u_ref, pe, wa_ref, wb_ref, w2_ref, out_ref):
            u = u_ref[...]
            pa = _dot((u + pe[0]).astype(BF16), wa_ref[...])
            pb = _dot((u + pe[1]).astype(BF16), wb_ref[...])
            pre = pa + pltpu.roll(pb, nc - 1, 0)
            hid = pre * _sigmoid(pre)
            out = _dot(hid.astype(BF16), w2_ref[...])
            out_ref[0] = jnp.where(last, 0.0, out).astype(BF16)

        compress(uk_ref, pe_ref[0], kwa_ref, kwb_ref, kw2_ref, kc_ref)
        compress(uv_ref, pe_ref[1], vwa_ref, vwb_ref, vw2_ref, vc_ref)


def _compress(pages_arr, pt, page, pe, kwa, kwb, kw2, vwa, vwb, vw2, paged):
    b, n_pg = pt.shape
    pps = min(PAGES_PER_STEP, n_pg)
    nc = n_pg * page // CMP_STRIDE
    if paged:
        page_spec = lambda u: pl.BlockSpec((1, 2 * LANE, page), lambda bb, j, p: (p[bb, j * pps + u], 0, 0))
    else:
        page_spec = lambda u: pl.BlockSpec((1, 2 * LANE, page), lambda bb, j, p: (bb, 0, j * pps + u))
    full = lambda a: pl.BlockSpec(a.shape, lambda bb, j, p: (0,) * a.ndim)
    consts = (pe, kwa, kwb, kw2, vwa, vwb, vw2)
    grid_spec = pltpu.PrefetchScalarGridSpec(
        num_scalar_prefetch=1,
        grid=(b, n_pg // pps),
        in_specs=[page_spec(u) for u in range(pps)] + [full(a) for a in consts],
        out_specs=[pl.BlockSpec((1, nc, LANE), lambda bb, j, p: (bb, 0, 0))] * 2,
        scratch_shapes=[pltpu.VMEM((nc, CMP_STRIDE * LANE), F32)] * 2 + [pltpu.VMEM((pps * page, LANE), F32)] * 2)
    return pl.pallas_call(
        functools.partial(_cmp_kernel, pps=pps),
        grid_spec=grid_spec,
        out_shape=[jax.ShapeDtypeStruct((b, nc, LANE), BF16)] * 2,
        compiler_params=_cparams(("arbitrary", "arbitrary")),
        name="nsa_compress",
    )(pt, *([pages_arr] * pps), *consts)


def _nsa_kernel(qt_ref, kc_ref, vct_ref, ovt_ref, ks_ref, vst_ref, kw_ref, vwt_ref, kp_ref, ts_ref, tw_ref, gt_ref,
                o_ref, qaug_ref, m_ref, acc_ref, ocmp_ref, oslc_ref, s_ref):
    k = pl.program_id(1)
    i = pl.program_id(2)
    tq = qt_ref.shape[2]
    q4 = jnp.concatenate([qt_ref[0, g * LANE:(g + 1) * LANE, :] for g in range(NSA_GROUP)], axis=1)
    qpos1 = i * tq + lax.broadcasted_iota(jnp.int32, (1, tq), 1)
    qpos = jnp.concatenate([qpos1] * NSA_GROUP, axis=1)

    nc = kc_ref.shape[1]
    lc = _dot(kc_ref[0], q4)
    tok = lax.broadcasted_iota(jnp.int32, lc.shape, 0)
    lm = jnp.where(tok * CMP_STRIDE + (CMP_LEN - 1) <= qpos, lc, NEG)
    mx = jnp.max(lm, axis=0, keepdims=True)
    s_ref[0:nc, :] = lm
    s_ref[0:nc, :] = jnp.exp2(s_ref[0:nc, :] - mx)
    norm = jnp.where(qpos >= CMP_LEN - 1,
                     1.0 / jnp.maximum(jnp.sum(s_ref[0:nc, :], axis=0, keepdims=True), TINY), 0.0)
    ocmp_ref[...] = _dot(vct_ref[0], s_ref[0:nc, :].astype(BF16)) * norm

    pcs = sum(s_ref[0:nc, g * tq:(g + 1) * tq] * norm[:, g * tq:(g + 1) * tq] for g in range(NSA_GROUP))
    ph, plo = _split_bf16(pcs)
    imp = _dot(ovt_ref[...], ph) + _dot(ovt_ref[...], plo)
    jb = lax.broadcasted_iota(jnp.int32, imp.shape, 0)
    cur = qpos1 >> int(math.log2(SLC_BLOCK))
    avail = jb <= cur
    forced = (jb == 0) | (jb == cur) | (jb == cur - 1)
    imp = jnp.where(avail, jnp.where(forced, FORCE, imp), NEG)
    sel = (_topk_mask(imp, SLC_TOPN, axis=0) > 0.0) & avail
    selb = jnp.where(sel, 0.0, NEG).astype(BF16)
    qaug_ref[...] = jnp.concatenate([q4, jnp.concatenate([selb] * NSA_GROUP, axis=1)], axis=0)

    cd = (i * tq) // KEY_TILE
    a0 = pl.multiple_of((i * tq) % KEY_TILE, tq)

    def table(t_ref, r0, n):
        return jnp.concatenate([t_ref[0, pl.ds(r0, n), pl.ds(g * KEY_TILE + a0, tq)] for g in range(NSA_GROUP)],
                               axis=1)

    _flash_init_t(m_ref, acc_ref)

    def slc_logits_at(off, n):
        kaug = jnp.concatenate([ks_ref[0, pl.ds(off, n), :], kp_ref[pl.ds(off, n), :]], axis=1)
        return _dot(kaug, qaug_ref[...])

    _sweep_tiles(slc_logits_at, lambda off, n: vst_ref[0, :, pl.ds(off, n)], functools.partial(table, ts_ref),
                 cd, m_ref, acc_ref, s_ref)
    oslc_ref[...] = _flash_result_t(acc_ref)

    _flash_init_t(m_ref, acc_ref)
    n_win = WINDOW // KEY_TILE
    span = WINDOW + KEY_TILE
    first = jnp.maximum(cd - n_win, 0)
    off = pl.multiple_of(first * KEY_TILE, KEY_TILE)
    toff = pl.multiple_of((first - (cd - n_win)) * KEY_TILE, KEY_TILE)
    s_ref[0:span, :] = _dot(kw_ref[0, pl.ds(off, span), :], q4) + table(tw_ref, toff, span)
    _flash_step_t(s_ref, span, vwt_ref[0, :, pl.ds(off, span)], m_ref, acc_ref)
    owin = _flash_result_t(acc_ref)

    ocmp = ocmp_ref[...]
    oslc = oslc_ref[...]
    gts = gt_ref[0]
    for g in range(NSA_GROUP):
        c = slice(g * tq, (g + 1) * tq)
        og = (gts[g:g + 1, :] * ocmp[:, c] + gts[NSA_GROUP + g:NSA_GROUP + g + 1, :] * oslc[:, c]
              + gts[2 * NSA_GROUP + g:2 * NSA_GROUP + g + 1, :] * owin[:, c])
        o_ref[0, g * HEAD_DIM:(g + 1) * HEAD_DIM, :] = jnp.where(k == 0, og[:HEAD_DIM], og[HEAD_DIM:]).astype(BF16)


def _nsa_prompt(qt, kc, vct, ovt, ksw, vst, vwt, kps, ts, tw, gt):
    b, _, t = qt.shape
    tq = min(NSA_TQ, t)
    nq = t // tq
    cols = NSA_GROUP * tq
    nc = kc.shape[1]
    keys = lambda col: pl.BlockSpec((1, t, LANE), lambda bb, k, i: (bb, 0, col))
    vals = pl.BlockSpec((1, LANE, t), lambda bb, k, i: (bb, 0, 0))
    return pl.pallas_call(
        _nsa_kernel,
        grid=(b, H_NSA_KV, nq),
        in_specs=[pl.BlockSpec((1, NSA_GROUP * LANE, tq), lambda bb, k, i: (bb, k, i)),
                  pl.BlockSpec((1, nc, LANE), lambda bb, k, i: (bb, 0, 0)),
                  pl.BlockSpec((1, LANE, nc), lambda bb, k, i: (bb, 0, 0)),
                  pl.BlockSpec(ovt.shape, lambda bb, k, i: (0, 0)),
                  keys(0), vals, keys(1), vals,
                  pl.BlockSpec((t, LANE), lambda bb, k, i: (0, 0), pipeline_mode=pl.Buffered(1)),
                  pl.BlockSpec((1,) + ts.shape[1:], lambda bb, k, i: (k, 0, 0), pipeline_mode=pl.Buffered(1)),
                  pl.BlockSpec((1,) + tw.shape[1:], lambda bb, k, i: (k, 0, 0), pipeline_mode=pl.Buffered(1)),
                  pl.BlockSpec((1, LANE, tq), lambda bb, k, i: (bb, k, i))],
        out_specs=pl.BlockSpec((1, NSA_GROUP * HEAD_DIM, tq), lambda bb, k, i: (bb, k, i)),
        out_shape=jax.ShapeDtypeStruct((b, W_NSA, t), BF16),
        scratch_shapes=[pltpu.VMEM((2 * LANE, cols), BF16),
                        pltpu.VMEM((1, cols), F32), pltpu.VMEM((LANE + ONES_ROWS, cols), F32),
                        pltpu.VMEM((LANE, cols), F32), pltpu.VMEM((LANE, cols), F32),
                        pltpu.VMEM((max(SWEEP_TILES * KEY_TILE, WINDOW + KEY_TILE, nc), cols), F32)],
        compiler_params=_cparams(("arbitrary", "arbitrary", "arbitrary")),
        name="nsa_prompt",
    )(qt, kc, vct, ovt, ksw, vst, ksw, vwt, kps, ts, tw, gt)


def _moba_dec_kernel(pt_ref, qbd_ref, new_ref, td_ref, *refs, pps, nblk):
    del pt_ref
    pages = refs[:pps]
    o_ref, sc_ref, mall_ref, lall_ref, oall_ref = refs[pps:]
    j = pl.program_id(1)
    qbd = qbd_ref[0]
    rows = qbd.shape[0]
    lane = lax.broadcasted_iota(jnp.int32, (rows, LANE), 1)
    ppb = MOBA_BLOCK // pages[0].shape[2]

    @pl.when(j == 0)
    def _():
        sc_ref[...] = jnp.zeros(sc_ref.shape, F32)
        mall_ref[...] = jnp.full(mall_ref.shape, NEG, F32)
        lall_ref[...] = jnp.zeros(lall_ref.shape, F32)

    for u in range(pps // ppb):
        blk = j * (pps // ppb) + u
        kt = jnp.concatenate([pages[u * ppb + w][0, :W_MOBA, :] for w in range(ppb)], axis=1)
        vt = jnp.concatenate([pages[u * ppb + w][0, W_MOBA:, :] for w in range(ppb)], axis=1)
        s = _dot(qbd, kt.astype(BF16))
        sc_ref[...] = jnp.where(lane == blk, jnp.sum(s, axis=1, keepdims=True), sc_ref[...])
        near = (blk == nblk - 1).astype(F32)
        s = s + near * td_ref[:, 0:MOBA_BLOCK]
        mb = jnp.max(s, axis=1, keepdims=True)
        p = jnp.exp2(s - mb)
        mall_ref[...] = jnp.where(lane == blk, mb, mall_ref[...])
        lall_ref[...] = jnp.where(lane == blk, jnp.sum(p, axis=1, keepdims=True), lall_ref[...])
        oall_ref[pl.ds(blk, 1)] = _dot_nt(p.astype(BF16), vt.astype(BF16))[None]

    @pl.when(j == pl.num_programs(1) - 1)
    def _():
        kn = new_ref[0, :, :W_MOBA].astype(BF16)
        vn = new_ref[0, :, W_MOBA:].astype(BF16)
        s_own = _dot_nt(qbd, kn) + td_ref[:, MOBA_BLOCK:MOBA_BLOCK + NEW_PAD]
        m_own = jnp.max(s_own, axis=1, keepdims=True)
        p_own = jnp.exp2(s_own - m_own)
        l_own = jnp.sum(p_own, axis=1, keepdims=True)
        o_own = _dot(p_own.astype(BF16), vn)
        sc = sc_ref[...]
        cand = lane < nblk
        sel = (_topk_mask(jnp.where(cand, sc, NEG), MOBA_TOPK) > 0.0) & cand
        mall = mall_ref[...]
        m_fin = jnp.maximum(jnp.max(jnp.where(sel, mall, NEG), axis=1, keepdims=True), m_own)
        w = jnp.where(sel, jnp.exp2(mall - m_fin), 0.0)
        w_own = jnp.exp2(m_own - m_fin)
        l = jnp.sum(w * lall_ref[...], axis=1, keepdims=True) + w_own * l_own
        acc = w_own * o_own
        for n in range(nblk):
            acc = acc + w[:, n:n + 1] * oall_ref[n]
        o = acc / jnp.maximum(l, TINY)
        rowh = lax.broadcasted_iota(jnp.int32, o.shape, 0) & (H_MOBA - 1)
        laneh = lax.broadcasted_iota(jnp.int32, o.shape, 1) >> int(math.log2(HEAD_DIM))
        o = jnp.where(rowh == laneh, o, 0.0)
        o_ref[0] = jnp.sum(o.reshape(rows // H_MOBA, H_MOBA, W_MOBA), axis=1)


def _moba_decode(cache_m, pt, qbd, new_pad, td):
    b, n_pg = pt.shape
    page = cache_m.shape[2]
    pps = min(PAGES_PER_STEP, n_pg)
    nblk = n_pg * page // MOBA_BLOCK
    rows = qbd.shape[1]
    s_new = rows // H_MOBA
    page_spec = lambda u: pl.BlockSpec((1, 2 * W_MOBA, page), lambda bb, j, p: (p[bb, j * pps + u], 0, 0))
    grid_spec = pltpu.PrefetchScalarGridSpec(
        num_scalar_prefetch=1,
        grid=(b, n_pg // pps),
        in_specs=[pl.BlockSpec((1, rows, W_MOBA), lambda bb, j, p: (bb, 0, 0)),
                  pl.BlockSpec((1, NEW_PAD, 2 * W_MOBA), lambda bb, j, p: (bb, 0, 0)),
                  pl.BlockSpec(td.shape, lambda bb, j, p: (0, 0))] + [page_spec(u) for u in range(pps)],
        out_specs=pl.BlockSpec((1, s_new, W_MOBA), lambda bb, j, p: (bb, 0, 0)),
        scratch_shapes=[pltpu.VMEM((rows, LANE), F32), pltpu.VMEM((rows, LANE), F32), pltpu.VMEM((rows, LANE), F32),
                        pltpu.VMEM((nblk, rows, W_MOBA), F32)])
    return pl.pallas_call(
        functools.partial(_moba_dec_kernel, pps=pps, nblk=nblk),
        grid_spec=grid_spec,
        out_shape=jax.ShapeDtypeStruct((b, s_new, W_MOBA), F32),
        compiler_params=_cparams(("arbitrary", "arbitrary")),
        name="moba_decode",
    )(pt, qbd, new_pad, td, *([cache_m] * pps))


def _nsa_dec_kernel(pt_ref, qd_ref, kc_ref, vc_ref, ov_ref, kp_ref, new_ref, win_ref, neww_ref, g_ref,
                    ts_ref, tso_ref, tw_ref, two_ref, *refs, pps, past):
    del pt_ref
    pages = refs[:pps]
    o_ref, qaug_ref, m_ref, l_ref, acc_ref, ocmp_ref, selown_ref = refs[pps:]
    j = pl.program_id(1)
    nj = pl.num_programs(1)
    qd = qd_ref[0]
    rows = qd.shape[0]
    page = pages[0].shape[2]
    grp = rows // NSA_GROUP
    s_new = grp // H_NSA_KV

    @pl.when(j == 0)
    def _():
        lc = _dot_nt(qd, kc_ref[0])
        tok = lax.broadcasted_iota(jnp.int32, lc.shape, 1)
        qpos = past + (lax.broadcasted_iota(jnp.int32, (rows, 1), 0) & (s_new - 1))
        pc = _masked_softmax(lc, tok * CMP_STRIDE + (CMP_LEN - 1) <= qpos)
        ocmp_ref[...] = _dot(pc.astype(BF16), vc_ref[0])
        pcs = pc[0:grp] + pc[grp:2 * grp] + pc[2 * grp:3 * grp] + pc[3 * grp:4 * grp]
        ph, plo = _split_bf16(pcs)
        imp = _dot(ph, ov_ref[...]) + _dot(plo, ov_ref[...])
        jb = lax.broadcasted_iota(jnp.int32, imp.shape, 1)
        cur = qpos[0:grp] >> int(math.log2(SLC_BLOCK))
        avail = jb <= cur
        forced = (jb == 0) | (jb == cur) | (jb == cur - 1)
        imp = jnp.where(avail, jnp.where(forced, FORCE, imp), NEG)
        sel = (_topk_mask(imp, SLC_TOPN) > 0.0) & avail
        selb = jnp.concatenate([jnp.where(sel, 0.0, NEG)] * NSA_GROUP, axis=0)
        qaug_ref[...] = jnp.concatenate([qd, selb[:, :LANE].astype(BF16)], axis=1)
        n_own = past // SLC_BLOCK
        selown_ref[...] = jnp.broadcast_to(selb[:, n_own:n_own + 1], selown_ref.shape)
        _flash_init(m_ref, l_ref, acc_ref)

    span = pps * page
    off = pl.multiple_of(j * span, span)
    kst = jnp.concatenate([pages[u][0, :LANE, :] for u in range(pps)], axis=1).astype(BF16)
    vst = jnp.concatenate([pages[u][0, LANE:, :] for u in range(pps)], axis=1).astype(BF16)
    kaug = jnp.concatenate([kst, kp_ref[:, pl.ds(off, span)]], axis=0)
    near = (j == nj - 1).astype(F32)
    s = _dot(qaug_ref[...], kaug) + near * ts_ref[...]
    _flash_step(s, vst, m_ref, l_ref, acc_ref, feature_major=True)

    @pl.when(j == nj - 1)
    def _():
        new = new_ref[0]
        s_own = _dot_nt(qd, new[:, 2 * LANE:3 * LANE].astype(BF16)) + tso_ref[...] + selown_ref[:, 0:1]
        _flash_step(s_own, new[:, 3 * LANE:].astype(BF16), m_ref, l_ref, acc_ref)
        oslc = acc_ref[...] / jnp.maximum(l_ref[...], TINY)
        _flash_init(m_ref, l_ref, acc_ref)
        win = win_ref[0]
        _flash_step(_dot(qd, win[:LANE, :].astype(BF16)) + tw_ref[...], win[LANE:, :].astype(BF16),
                    m_ref, l_ref, acc_ref, feature_major=True)
        neww = neww_ref[0]
        _flash_step(_dot_nt(qd, neww[:, :LANE].astype(BF16)) + two_ref[...], neww[:, LANE:].astype(BF16),
                    m_ref, l_ref, acc_ref)
        owin = acc_ref[...] / jnp.maximum(l_ref[...], TINY)
        gts = g_ref[0]
        o_ref[0] = gts[:, 0:1] * ocmp_ref[...] + gts[:, 1:2] * oslc + gts[:, 2:3] * owin


def _nsa_decode(cache_n, pt, qd, kc, vc, ov, kps, new_pad, win, neww, gts, ts, tso, tw, two):
    b, n_pg = pt.shape
    page = cache_n.shape[2]
    pps = min(PAGES_PER_STEP, n_pg)
    rows = qd.shape[1]
    nc = kc.shape[1]
    full = lambda a: pl.BlockSpec(a.shape, lambda bb, j, p: (0,) * a.ndim)
    per_b = lambda a: pl.BlockSpec((1,) + a.shape[1:], lambda bb, j, p: (bb,) + (0,) * (a.ndim - 1))
    page_spec = lambda u: pl.BlockSpec((1, 2 * LANE, page), lambda bb, j, p: (p[bb, j * pps + u], 1, 0))
    grid_spec = pltpu.PrefetchScalarGridSpec(
        num_scalar_prefetch=1,
        grid=(b, n_pg // pps),
        in_specs=[per_b(qd), per_b(kc), per_b(vc), full(ov), full(kps), per_b(new_pad), per_b(win), per_b(neww),
                  per_b(gts), full(ts), full(tso), full(tw), full(two)] + [page_spec(u) for u in range(pps)],
        out_specs=pl.BlockSpec((1, rows, LANE), lambda bb, j, p: (bb, 0, 0)),
        scratch_shapes=[pltpu.VMEM((rows, 2 * LANE), BF16),
                        pltpu.VMEM((rows, 1), F32), pltpu.VMEM((rows, 1), F32), pltpu.VMEM((rows, LANE), F32),
                        pltpu.VMEM((rows, LANE), F32), pltpu.VMEM((rows, LANE), F32)])
    return pl.pallas_call(
        functools.partial(_nsa_dec_kernel, pps=pps, past=n_pg * page),
        grid_spec=grid_spec,
        out_shape=jax.ShapeDtypeStruct((b, rows, LANE), F32),
        compiler_params=_cparams(("arbitrary", "arbitrary")),
        name="nsa_decode",
    )(pt, qd, kc, vc, ov, kps, new_pad, win, neww, gts, ts, tso, tw, two, *([cache_n] * pps))


def _out_kernel(x_ref, om_ref, zm_ref, on_ref, zn_ref, gate_ref, w_ref, fg_ref, y_ref, *, feature_major):
    zm = zm_ref[...].astype(F32)
    zn = zn_ref[...].astype(F32)
    if feature_major:
        om = om_ref[0].astype(F32).T
        on = on_ref[0].astype(F32).T
    else:
        om = om_ref[...].astype(F32)
        on = on_ref[...].astype(F32)
    mm = (om * (zm * _sigmoid(zm))).astype(BF16)
    mn = (on * (zn * _sigmoid(zn))).astype(BF16)
    mixed = _dot(mm, w_ref[:W_MOBA, :]) + _dot(mn, w_ref[W_MOBA:, :])
    xn = x_ref[...] + gate_ref[0] * mixed
    inv = lax.rsqrt(jnp.mean(xn * xn, axis=-1, keepdims=True) + RMS_EPS)
    y_ref[...] = (xn * inv) * fg_ref[...]


def _out_proj(x2d, om, zm, on, zn, gate, w_out, fgain, tm, tiles_per_mod):
    r, d = x2d.shape
    mrows = gate.shape[1]
    feature_major = om.ndim == 3
    row = lambda width: pl.BlockSpec((tm, width), lambda i: (i, 0))
    if feature_major:
        mixer = pl.BlockSpec((1, om.shape[1], tm), lambda i: (i // tiles_per_mod, 0, i % tiles_per_mod))
    else:
        mixer = row(W_MOBA)
    return pl.pallas_call(
        functools.partial(_out_kernel, feature_major=feature_major),
        grid=(r // tm,),
        in_specs=[row(d), mixer, row(W_MOBA), mixer, row(W_NSA),
                  pl.BlockSpec((1, mrows, d), lambda i: (i // tiles_per_mod, 0, 0)),
                  pl.BlockSpec((d, d), lambda i: (0, 0)),
                  pl.BlockSpec((1, d), lambda i: (0, 0))],
        out_specs=row(d),
        out_shape=jax.ShapeDtypeStruct((r, d), F32),
        compiler_params=_cparams(("arbitrary",)),
        name="out_proj",
    )(x2d, om, zm, on, zn, gate, w_out, fgain)


def _t5_bucket(rel):
    n = np.maximum(rel, 0)
    exact = N_BUCKETS // 2
    x = np.log(np.maximum(n, 1) / exact) / math.log(MAX_DISTANCE / exact) * (N_BUCKETS - exact)
    near_boundary = (np.abs(x - np.round(x)) < 1e-3) & (n > exact) & (n < MAX_DISTANCE)
    assert not near_boundary.any()
    large = exact + np.floor(x + 1e-6).astype(np.int64)
    return np.where(n < exact, n, np.minimum(large, N_BUCKETS - 1)).astype(np.int32)


def _bias_of_rel(tab, rel, shift_far):
    onehot = (jnp.asarray(_t5_bucket(rel))[..., None] == jnp.arange(N_BUCKETS, dtype=jnp.int32)).astype(F32)
    val = jnp.moveaxis(jnp.dot(onehot, tab, precision=lax.Precision.HIGHEST), -1, 0)
    if shift_far:
        val = val - tab[N_BUCKETS - 1].reshape((-1,) + (1,) * rel.ndim)
    return jnp.where(jnp.asarray(rel >= 0)[None], val * LOG2E, NEG)


def _layout_w_in(w_in):
    d = w_in.shape[0]
    sc = HEAD_DIM ** -0.5 * LOG2E
    o = 4 * W_MOBA
    q_m, rest_m = w_in[:, :W_MOBA] * sc, w_in[:, W_MOBA:o]
    q_n = (w_in[:, o:o + W_NSA] * sc).reshape(d, H_NSA_KV, NSA_GROUP, HEAD_DIM)
    qn_exp = jnp.concatenate([jnp.pad(q_n[:, 0], ((0, 0), (0, 0), (0, HEAD_DIM))),
                              jnp.pad(q_n[:, 1], ((0, 0), (0, 0), (HEAD_DIM, 0)))], axis=1).reshape(d, H_NSA * LANE)
    o += W_NSA
    kv_n = w_in[:, o:o + 6 * W_NSA_KV]
    o += 6 * W_NSA_KV
    g_n = jnp.swapaxes(w_in[:, o:o + 3 * H_NSA].reshape(d, H_NSA_KV, NSA_GROUP, 3), 2, 3)
    g_n = jnp.pad(g_n.reshape(d, H_NSA_KV, 3 * NSA_GROUP), ((0, 0), (0, 0), (0, LANE - 3 * NSA_GROUP)))
    z_n = w_in[:, o + 3 * H_NSA:]
    w = jnp.concatenate([q_m, rest_m, qn_exp, kv_n, z_n, g_n.reshape(d, H_NSA_KV * LANE)], axis=1).astype(BF16)
    assert w.shape[1] == C_END
    wt = jnp.concatenate([w[:, C_QM:C_ZM], w[:, C_QN:C_ZN], w[:, C_G:C_END]], axis=1).T
    assert wt.shape[0] == R_END
    return w, wt


def _layout_cmp(pe, w1, w2):
    w = w1.reshape(2, CMP_STRIDE, HEAD_DIM, CMP_HIDDEN)
    z = jnp.zeros_like(w)
    full = jnp.stack([jnp.concatenate([w, z], axis=-1), jnp.concatenate([z, w], axis=-1)], axis=2)
    full = full.reshape(2, CMP_STRIDE * H_NSA_KV * HEAD_DIM, H_NSA_KV * CMP_HIDDEN).astype(BF16)
    zz = jnp.zeros_like(w2)
    w2bd = jnp.concatenate([jnp.concatenate([w2, zz], axis=1), jnp.concatenate([zz, w2], axis=1)], axis=0).astype(BF16)
    pe2 = jnp.broadcast_to(pe.reshape(2, CMP_STRIDE, 1, HEAD_DIM), (2, CMP_STRIDE, H_NSA_KV, HEAD_DIM))
    return pe2.reshape(2, 1, CMP_STRIDE * LANE), full[0], full[1], w2bd


def _block_onehot(t, block):
    return (jnp.arange(t)[:, None] // block == jnp.arange(LANE)[None, :]).astype(BF16)


def _overlap(nc, n_cmp, lanes):
    i = jnp.arange(nc)[:, None]
    s = jnp.arange(lanes)[None, :]
    start = i * CMP_STRIDE
    hit = (start < s * SLC_BLOCK + SLC_BLOCK) & (start + CMP_LEN - 1 >= s * SLC_BLOCK) & (i < n_cmp)
    return hit.astype(BF16)


def kernel(x_prompt, x_sample, c_prompt, c_sample, cache_moba_kv, cache_nsa_kv, state_nsa_win, page_table, w_ada, b_ada, norm_gain, w_in, cmp_pe, cmp_k_w1, cmp_k_w2, cmp_v_w1, cmp_v_w2, w_out, rel_bias, final_gain):
    depth = w_in.shape[0]
    assert depth == 1
    B, T, D = x_prompt.shape
    BS, S, _ = x_sample.shape
    n_pg = page_table.shape[1]
    page = cache_moba_kv.shape[2]
    P = n_pg * page
    WB = state_nsa_win.shape[2]
    assert D == (H_MOBA + H_NSA) * HEAD_DIM and T % (SWEEP_TILES * KEY_TILE) == 0 and P % MOBA_BLOCK == 0 and S <= NEW_PAD
    assert T // SLC_BLOCK <= LANE and P // SLC_BLOCK <= LANE and T >= WINDOW and WB == WINDOW
    assert (BS * S) % SUBLANE == 0 and S & (S - 1) == 0

    w, wt = _layout_w_in(w_in[0])
    w_out_b = w_out[0].astype(BF16)
    gain = norm_gain[0].reshape(1, D)
    fgain = final_gain.reshape(1, D)
    kcmp_w = _layout_cmp(cmp_pe[0, 0], cmp_k_w1[0], cmp_k_w2[0])
    vcmp_w = _layout_cmp(cmp_pe[0, 1], cmp_v_w1[0], cmp_v_w2[0])
    pe2 = jnp.stack([kcmp_w[0], vcmp_w[0]], axis=0)
    cmp_consts = (pe2,) + kcmp_w[1:] + vcmp_w[1:]
    bias_m = rel_bias[:, :H_MOBA]
    bias_n = rel_bias[:, H_MOBA:]

    m_all = B + BS
    m_pad = -(-m_all // SUBLANE) * SUBLANE
    c_all = jnp.pad(jnp.concatenate([c_prompt, c_sample], axis=0), ((0, m_pad - m_all), (0, 0)))
    mod = _ada(c_all, w_ada[0], b_ada[0])
    shift, scale, gate = mod[:, :D], mod[:, D:2 * D], mod[:, 2 * D:]

    a = np.arange(KEY_TILE)[None, :]
    jk = np.arange(2 * KEY_TILE)[:, None]
    rel_s = a + KEY_TILE - jk
    heads_major = lambda tb, n_grp: jnp.transpose(
        tb.reshape(n_grp, -1, tb.shape[1], tb.shape[2]), (0, 2, 1, 3)).reshape(n_grp, tb.shape[1], -1)
    rows_of = lambda tb, n, val: jnp.full((tb.shape[0], n, tb.shape[2]), val, F32)
    pad_rows = (SWEEP_TILES - 1) * KEY_TILE
    sweep_rows = lambda tb: jnp.concatenate([rows_of(tb, pad_rows, 0.0), tb, rows_of(tb, pad_rows, NEG)], axis=1)
    rel_m = MOBA_BLOCK + np.arange(MOBA_TQ)[None, :] - np.arange(MOBA_BLOCK + MOBA_TQ)[:, None]
    tab_m = heads_major(_bias_of_rel(bias_m, rel_m, True), H_MOBA // 2)
    tab_m = jnp.concatenate([rows_of(tab_m, MOBA_BLOCK, 0.0), tab_m], axis=1)
    tab_s = sweep_rows(heads_major(_bias_of_rel(bias_n, rel_s, True), H_NSA_KV))
    jw = np.arange(WINDOW + KEY_TILE)[:, None]
    rel_w = a + WINDOW - jw
    tab_w = heads_major(_bias_of_rel(bias_n, np.where(rel_w < WINDOW, rel_w, -1), False), H_NSA_KV)
    tab_w = jnp.concatenate([tab_w, rows_of(tab_w, WINDOW, NEG)], axis=1)

    tpm = T // PROJ_TM
    (qmt, mkvt, vmt, qnt, nkvt, vst, wkvt, vwt, gt, km, ksw, zm, zn, kmean) = _proj_fm(
        x_prompt.reshape(B * T, D), scale[:B].reshape(B, 1, D), shift[:B].reshape(B, 1, D), gain, w, wt, B, PROJ_TM)
    nb = T // MOBA_BLOCK
    kmean_pad = jnp.pad(kmean.reshape(B, nb, W_MOBA), ((0, 0), (0, LANE - nb), (0, 0)))
    o_m = _moba_prompt(qmt, km.reshape(B, T, W_MOBA), vmt, kmean_pad, _block_onehot(T, MOBA_BLOCK), tab_m)
    pt_prompt = jnp.zeros((B, T // page), jnp.int32)
    kc_p, vc_p = _compress(nkvt, pt_prompt, page, *cmp_consts, paged=False)
    nc_p = T // CMP_STRIDE
    o_n = _nsa_prompt(qnt, kc_p, jnp.swapaxes(vc_p, 1, 2), _overlap(nc_p, nc_p - 1, LANE).T,
                      ksw.reshape(B, T, 2 * LANE), vst, vwt, _block_onehot(T, SLC_BLOCK), tab_s, tab_w, gt)
    y_prompt = _out_proj(x_prompt.reshape(B * T, D), o_m, zm, o_n, zn,
                         gate[:B].reshape(B, 1, D), w_out_b, fgain, PROJ_TM, tpm).reshape(B, T, D)
    tokens_first = lambda m, c: jnp.transpose(m.reshape(1, B, c, -1, HEAD_DIM, m.shape[-1]), (0, 1, 5, 2, 3, 4))
    moba_kv_prompt = tokens_first(mkvt, 2)
    nsa_kv_prompt = tokens_first(nkvt, 4)
    win_prompt = tokens_first(wkvt[:, :, T - WINDOW:], 2)

    RS = BS * S
    rep = lambda m: jnp.repeat(m[B:B + BS], S, axis=0)
    (qm_s, mkv32_s, zm_s, qn_s, nkv32_s, wkv32_s, zn_s, gts_s) = _proj(
        x_sample.reshape(RS, D), rep(scale), rep(shift), gain, w)
    pad_new = lambda m: jnp.pad(m.reshape(BS, S, -1), ((0, 0), (0, NEW_PAD - S), (0, 0)))
    s_idx = np.arange(S)

    q_rep = jnp.repeat(qm_s.reshape(BS, S, 1, H_MOBA, HEAD_DIM), H_MOBA, axis=2)
    eye = (jnp.arange(H_MOBA)[:, None] == jnp.arange(H_MOBA)[None, :])[None, None, :, :, None]
    qbd = jnp.where(eye, q_rep, 0).reshape(BS, S * H_MOBA, W_MOBA)
    rel_d = np.concatenate([MOBA_BLOCK + s_idx[:, None] - np.arange(MOBA_BLOCK)[None, :],
                            s_idx[:, None] - np.arange(NEW_PAD)[None, :]], axis=1)
    td = jnp.moveaxis(_bias_of_rel(bias_m, rel_d, True), 0, 1).reshape(S * H_MOBA, -1)
    td = jnp.pad(td, ((0, 0), (0, 3 * LANE - td.shape[1])))
    feat_pages = lambda c: jnp.transpose(c, (0, 2, 3, 4, 1)).reshape(c.shape[0], -1, c.shape[1])
    o_m_s = _moba_decode(feat_pages(cache_moba_kv[0]), page_table, qbd, pad_new(mkv32_s), td)

    cache_n = feat_pages(cache_nsa_kv[0])
    kc_s, vc_s = _compress(cache_n, page_table, page, *cmp_consts, paged=True)
    nc_s = P // CMP_STRIDE
    n_cmp_s = (P + S - CMP_LEN) // CMP_STRIDE + 1
    order = lambda m: jnp.transpose(m, (0, 3, 2, 1) + tuple(range(4, m.ndim)))
    rows_n = NSA_GROUP * H_NSA_KV * S
    qd = order(qn_s.reshape(BS, S, H_NSA_KV, NSA_GROUP, LANE)).reshape(BS, rows_n, LANE)
    g3 = gts_s.reshape(BS, S, H_NSA_KV, LANE)[..., :3 * NSA_GROUP].reshape(BS, S, H_NSA_KV, 3, NSA_GROUP)
    gts_d = order(jnp.swapaxes(g3, 3, 4)).reshape(BS, rows_n, 3)
    bias_d = jnp.swapaxes(bias_n.reshape(N_BUCKETS, H_NSA_KV, NSA_GROUP), 1, 2).reshape(N_BUCKETS, H_NSA)

    def dec_table(rel, shift_far):
        return _bias_of_rel(bias_d, rel, shift_far).reshape(rows_n, rel.shape[1])

    ts = dec_table(page + s_idx[:, None] - np.arange(page)[None, :], True)
    ts = jnp.pad(ts, ((0, 0), (min(PAGES_PER_STEP, n_pg) * page - page, 0)))
    rel_own = s_idx[:, None] - np.arange(NEW_PAD)[None, :]
    tso = dec_table(rel_own, True)
    rel_win = WB + s_idx[:, None] - np.arange(WB)[None, :]
    tw = dec_table(np.where(rel_win < WINDOW, rel_win, -1), False)
    two = dec_table(rel_own, False)
    win_state = feat_pages(state_nsa_win[0])
    o_n_raw = _nsa_decode(cache_n, page_table, qd, kc_s, vc_s, _overlap(nc_s, n_cmp_s, 2 * LANE),
                          _block_onehot(P, SLC_BLOCK).T, pad_new(nkv32_s), win_state, pad_new(wkv32_s), gts_d,
                          ts, tso, tw, two)
    o5 = o_n_raw.reshape(BS, NSA_GROUP, H_NSA_KV, S, H_NSA_KV, HEAD_DIM)
    o_n_s = jnp.stack([o5[:, :, k, :, k] for k in range(H_NSA_KV)], axis=1)
    o_n_s = jnp.transpose(o_n_s, (0, 3, 1, 2, 4)).reshape(RS, W_NSA)
    y_sample = _out_proj(x_sample.reshape(RS, D), o_m_s.reshape(RS, W_MOBA).astype(BF16), zm_s,
                         o_n_s.astype(BF16), zn_s, rep(gate).reshape(1, RS, D), w_out_b, fgain, RS, 1).reshape(BS, S, D)
    moba_kv_sample = mkv32_s.reshape(1, BS, S, 2, H_MOBA, HEAD_DIM)
    nsa_kv_sample = nkv32_s.reshape(1, BS, S, 4, H_NSA_KV, HEAD_DIM)
    win_sample = jnp.concatenate([win_state[:, :, S:], jnp.swapaxes(wkv32_s.reshape(BS, S, 2 * LANE), 1, 2)], axis=2)
    win_sample = jnp.transpose(win_sample.reshape(1, BS, 2, H_NSA_KV, HEAD_DIM, WB), (0, 1, 5, 2, 3, 4))
    return (y_prompt, y_sample, moba_kv_prompt, moba_kv_sample, nsa_kv_prompt, nsa_kv_sample, win_prompt, win_sample)
```

```python
import functools
import math

import jax
import jax.numpy as jnp
import numpy as np
from jax import lax
from jax.experimental import pallas as pl
from jax.experimental.pallas import tpu as pltpu

F32 = jnp.float32
BF16 = jnp.bfloat16

HEAD_DIM = 64
H_MOBA = 8
H_NSA = 8
H_NSA_KV = 2
NSA_GROUP = 4
W_MOBA = H_MOBA * HEAD_DIM
W_NSA = H_NSA * HEAD_DIM
W_NSA_KV = H_NSA_KV * HEAD_DIM
MOBA_BLOCK = 256
MOBA_TOPK = 3
CMP_LEN = 32
CMP_STRIDE = 16
CMP_HIDDEN = 2 * HEAD_DIM
SLC_BLOCK = 64
SLC_TOPN = 16
WINDOW = 512
N_BUCKETS = 32
MAX_DISTANCE = 128
RMS_EPS = 1e-6
NEG = -1e30
FORCE = 1e9
TINY = 1e-30

LANE = 128
SUBLANE = 8
KEY_TILE = 256
SWEEP_TILES = 2
NSA_TQ = 256
MOBA_TQ = 2 * MOBA_BLOCK
PROJ_TM = 256
PAGES_PER_STEP = 32
NEW_PAD = 8
ONES_ROWS = 16
LOG2E = math.log2(math.e)
VMEM_LIMIT = 56 * 1024 * 1024

C_QM, C_MKV, C_ZM, C_QN, C_NKV, C_WKV, C_ZN, C_G, C_END = 0, 512, 1536, 2048, 3072, 3584, 3840, 4352, 4608

_NT = (((1,), (1,)), ((), ()))


def _dot(a, b):
    return jnp.dot(a, b, preferred_element_type=F32)


def _dot_nt(a, b):
    return lax.dot_general(a, b, _NT, preferred_element_type=F32)


def _split_bf16(a):
    hi = a.astype(BF16)
    lo = (a - hi.astype(F32)).astype(BF16)
    return hi, lo


def _sigmoid(x):
    return 1.0 / (1.0 + jnp.exp(-x))


def _cparams(sem):
    return pltpu.CompilerParams(dimension_semantics=sem, vmem_limit_bytes=VMEM_LIMIT)


def _ada_kernel(c_ref, w_ref, b_ref, o_ref):
    ch, cl = _split_bf16(c_ref[...])
    wh, wl = _split_bf16(w_ref[...])
    o_ref[...] = _dot(ch, wh) + _dot(ch, wl) + _dot(cl, wh) + b_ref[...]


def _ada(c_all, w_ada, b_ada):
    m, d = c_all.shape
    n = w_ada.shape[1]
    tn = 512
    return pl.pallas_call(
        _ada_kernel,
        grid=(n // tn,),
        in_specs=[pl.BlockSpec((m, d), lambda j: (0, 0)),
                  pl.BlockSpec((d, tn), lambda j: (0, j)),
                  pl.BlockSpec((1, tn), lambda j: (0, j))],
        out_specs=pl.BlockSpec((m, tn), lambda j: (0, j)),
        out_shape=jax.ShapeDtypeStruct((m, n), F32),
        compiler_params=_cparams(("arbitrary",)),
        name="ada",
    )(c_all, w_ada, b_ada.reshape(1, n))


def _modulated_norm(x, gain, scale, shift):
    inv = lax.rsqrt(jnp.mean(x * x, axis=-1, keepdims=True) + RMS_EPS)
    return (x * inv) * gain * (1.0 + scale) + shift


def _proj_kernel(x_ref, sc_ref, sh_ref, gain_ref, w_ref, qm_ref, mkv_ref, zm_ref, qn_ref, nkv_ref, wkv_ref, zn_ref, g_ref):
    hb = _modulated_norm(x_ref[...], gain_ref[...], sc_ref[...], sh_ref[...]).astype(BF16)
    col = lambda a, b: _dot(hb, w_ref[:, a:b])
    qm_ref[...] = col(C_QM, C_MKV).astype(BF16)
    mkv_ref[...] = col(C_MKV, C_ZM)
    zm_ref[...] = col(C_ZM, C_QN).astype(BF16)
    qn_ref[...] = col(C_QN, C_NKV).astype(BF16)
    nkv_ref[...] = col(C_NKV, C_WKV)
    wkv_ref[...] = col(C_WKV, C_ZN)
    zn_ref[...] = col(C_ZN, C_G).astype(BF16)
    g_ref[...] = _sigmoid(col(C_G, C_END))


def _proj(x2d, sc, sh, gain, w):
    r, d = x2d.shape
    full = lambda width: pl.BlockSpec((r, width), lambda i: (0, 0))
    widths = [(512, BF16), (1024, F32), (512, BF16), (1024, BF16), (512, F32), (256, F32), (512, BF16), (256, F32)]
    return pl.pallas_call(
        _proj_kernel,
        grid=(1,),
        in_specs=[full(d), full(d), full(d), pl.BlockSpec((1, d), lambda i: (0, 0)),
                  pl.BlockSpec((d, C_END), lambda i: (0, 0))],
        out_specs=[full(wd) for wd, _ in widths],
        out_shape=[jax.ShapeDtypeStruct((r, wd), dt) for wd, dt in widths],
        compiler_params=_cparams(("arbitrary",)),
        name="proj_decode",
    )(x2d, sc, sh, gain, w)


R_QM, R_MKV, R_QN, R_NKV, R_WKV, R_G, R_END = 0, 512, 1536, 2560, 3072, 3328, 3584


def _proj_fm_kernel(x_ref, sc_ref, sh_ref, gain_ref, w_ref, wt_ref,
                    qmt_ref, mkvt_ref, vmt_ref, qnt_ref, nkvt_ref, vst_ref, wkvt_ref, vwt_ref, gt_ref,
                    km_ref, ksw_ref, zm_ref, zn_ref, kmean_ref):
    h = _modulated_norm(x_ref[...], gain_ref[...], sc_ref[0], sh_ref[0])
    hb = h.astype(BF16)
    ht = h.T.astype(BF16)
    frow = lambda a, b: _dot(wt_ref[a:b, :], ht)
    col = lambda a, b: _dot(hb, w_ref[:, a:b])
    qmt_ref[0] = frow(R_QM, R_MKV).astype(BF16)
    mkvt = frow(R_MKV, R_QN)
    mkvt_ref[0] = mkvt
    vmt_ref[0] = mkvt[W_MOBA:].astype(BF16)
    qnt_ref[0] = frow(R_QN, R_NKV).astype(BF16)
    nkvt = frow(R_NKV, R_WKV)
    nkvt_ref[0] = nkvt
    vst_ref[0] = nkvt[3 * W_NSA_KV:].astype(BF16)
    wkvt = frow(R_WKV, R_G)
    wkvt_ref[0] = wkvt
    vwt_ref[0] = wkvt[W_NSA_KV:].astype(BF16)
    gt_ref[0] = _sigmoid(frow(R_G, R_END))
    km = col(C_MKV, C_MKV + W_MOBA)
    km_ref[...] = km.astype(BF16)
    kmean_ref[0] = jnp.mean(km, axis=0, keepdims=True)
    ksw_ref[...] = jnp.concatenate([col(C_NKV + 2 * W_NSA_KV, C_NKV + 3 * W_NSA_KV), col(C_WKV, C_WKV + W_NSA_KV)],
                                   axis=1).astype(BF16)
    zm_ref[...] = col(C_ZM, C_QN).astype(BF16)
    zn_ref[...] = col(C_ZN, C_G).astype(BF16)


def _proj_fm(x2d, sc, sh, gain, w, wt, b, tm):
    r, d = x2d.shape
    t = r // b
    tpb = t // tm
    nt = r // tm
    row = lambda width: pl.BlockSpec((tm, width), lambda i: (i, 0))
    fm = lambda rows: pl.BlockSpec((1, rows, tm), lambda i: (i // tpb, 0, i % tpb))
    mod = pl.BlockSpec((1, 1, d), lambda i: (i // tpb, 0, 0))
    const = lambda a: pl.BlockSpec(a.shape, lambda i: (0, 0), pipeline_mode=pl.Buffered(1))
    fm_outs = [(512, BF16), (1024, F32), (512, BF16), (1024, BF16), (512, F32), (128, BF16), (256, F32), (128, BF16),
               (256, F32)]
    tm_outs = [(512, BF16), (256, BF16), (512, BF16), (512, BF16)]
    return pl.pallas_call(
        _proj_fm_kernel,
        grid=(nt,),
        in_specs=[row(d), mod, mod, pl.BlockSpec((1, d), lambda i: (0, 0)), const(w), const(wt)],
        out_specs=[fm(rows) for rows, _ in fm_outs] + [row(wd) for wd, _ in tm_outs]
        + [pl.BlockSpec((1, 1, W_MOBA), lambda i: (i, 0, 0))],
        out_shape=[jax.ShapeDtypeStruct((b, rows, t), dt) for rows, dt in fm_outs]
        + [jax.ShapeDtypeStruct((r, wd), dt) for wd, dt in tm_outs] + [jax.ShapeDtypeStruct((nt, 1, W_MOBA), F32)],
        compiler_params=_cparams(("arbitrary",)),
        name="proj_prompt",
    )(x2d, sc, sh, gain, w, wt)


def _flash_step(s, v, m_ref, l_ref, acc_ref, feature_major=False):
    m_old = m_ref[...]
    m_new = jnp.maximum(m_old, jnp.max(s, axis=1, keepdims=True))
    alpha = jnp.exp2(m_old - m_new)
    p = jnp.exp2(s - m_new)
    l_ref[...] = alpha * l_ref[...] + jnp.sum(p, axis=1, keepdims=True)
    pv = _dot_nt(p.astype(BF16), v) if feature_major else _dot(p.astype(BF16), v)
    acc_ref[...] = alpha * acc_ref[...] + pv
    m_ref[...] = m_new


def _flash_init(m_ref, l_ref, acc_ref):
    m_ref[...] = jnp.full(m_ref.shape, NEG, F32)
    l_ref[...] = jnp.zeros(l_ref.shape, F32)
    acc_ref[...] = jnp.zeros(acc_ref.shape, F32)


def _flash_step_t(s_ref, n_keys, vt, m_ref, acc_ref):
    s_max = jnp.max(s_ref[0:n_keys, :], axis=0, keepdims=True)
    alpha, p = _softmax_weights_t(s_ref[0:n_keys, :], s_max, m_ref)
    acc_ref[...] = alpha * acc_ref[...] + _weighted_values_t(vt, p)


def _softmax_weights_t(s, s_max, m_ref):
    m_old = m_ref[...]
    m_new = jnp.maximum(m_old, s_max)
    m_ref[...] = m_new
    return jnp.exp2(m_old - m_new), jnp.exp2(s - m_new).astype(BF16)


def _weighted_values_t(vt, p):
    vt_aug = jnp.concatenate([vt, jnp.ones((ONES_ROWS, vt.shape[1]), BF16)], axis=0)
    return _dot(vt_aug, p)


def _flash_init_t(m_ref, acc_ref):
    m_ref[...] = jnp.full(m_ref.shape, NEG, F32)
    acc_ref[...] = jnp.zeros(acc_ref.shape, F32)


def _flash_result_t(acc_ref):
    width = acc_ref.shape[0] - ONES_ROWS
    return acc_ref[0:width, :] / jnp.maximum(acc_ref[width:width + 1, :], TINY)


def _sweep_tiles(logits_at, values_at, table_at, i_diag, m_ref, acc_ref, s_ref):
    g = SWEEP_TILES
    big = g * KEY_TILE
    n_far = jnp.maximum(i_diag - 1, 0)
    n_big = n_far // g
    rem = n_far - n_big * g

    def far_body(n, s_max):
        alpha, p = _softmax_weights_t(s_ref[0:big, :], s_max, m_ref)
        acc_ref[...] = alpha * acc_ref[...] + _weighted_values_t(values_at(pl.multiple_of(n * big, big), big), p)
        nxt = logits_at(pl.multiple_of((n + 1) * big, big), big)
        s_ref[0:big, :] = nxt
        return jnp.max(nxt, axis=0, keepdims=True)

    s0 = logits_at(0, big)
    s_ref[0:big, :] = s0
    lax.fori_loop(0, n_big, far_body, jnp.max(s0, axis=0, keepdims=True))
    first_block = jnp.where(i_diag == 0, g, g - 1 - rem)
    s_ref[0:big, :] = s_ref[0:big, :] + table_at(pl.multiple_of(first_block * KEY_TILE, KEY_TILE), big)
    _flash_step_t(s_ref, big, values_at(pl.multiple_of(n_big * big, big), big), m_ref, acc_ref)

    @pl.when((i_diag >= 1) & (rem == g - 1))
    def _():
        off = pl.multiple_of(i_diag * KEY_TILE, KEY_TILE)
        s_ref[0:KEY_TILE, :] = logits_at(off, KEY_TILE) + table_at(g * KEY_TILE, KEY_TILE)
        _flash_step_t(s_ref, KEY_TILE, values_at(off, KEY_TILE), m_ref, acc_ref)


def _topk_mask(vals, k, axis=1):
    idxf = lax.broadcasted_iota(jnp.int32, vals.shape, axis).astype(F32)

    def body(_, taken):
        cur = jnp.where(taken > 0.0, -jnp.inf, vals)
        mx = jnp.max(cur, axis=axis, keepdims=True)
        first = jnp.min(jnp.where(cur == mx, idxf, 1e9), axis=axis, keepdims=True)
        return jnp.where(idxf == first, 1.0, taken)

    return lax.fori_loop(0, k, body, jnp.zeros(vals.shape, F32))


def _masked_softmax(logits, valid, axis=1):
    lm = jnp.where(valid, logits, NEG)
    p = jnp.exp2(lm - jnp.max(lm, axis=axis, keepdims=True)) * valid.astype(F32)
    return p / jnp.maximum(jnp.sum(p, axis=axis, keepdims=True), TINY)


def _moba_kernel(qt_ref, k_ref, vt_ref, km_ref, kp_ref, tab_ref, o_ref, qaug_ref, m_ref, acc_ref, s_ref, *, n_blocks):
    i = pl.program_id(2)
    tq = qt_ref.shape[2]
    pair = 2 * MOBA_BLOCK
    qt = qt_ref[0]
    row = lax.broadcasted_iota(jnp.int32, qt.shape, 0)
    zero = jnp.zeros_like(qt)
    q2 = jnp.concatenate([jnp.where(row < HEAD_DIM, qt, zero), jnp.where(row >= HEAD_DIM, qt, zero)], axis=1)
    nbp = -(-n_blocks // ONES_ROWS) * ONES_ROWS
    sc = _dot(km_ref[0, :nbp, :].astype(BF16), q2)
    blk = lax.broadcasted_iota(jnp.int32, sc.shape, 0)
    col = lax.broadcasted_iota(jnp.int32, (1, sc.shape[1]), 1)
    cur = 2 * i + ((col & (tq - 1)) >> int(math.log2(MOBA_BLOCK)))
    cand = blk < cur
    taken = _topk_mask(jnp.where(cand, sc, NEG), MOBA_TOPK, axis=0)
    sel = ((taken > 0.0) & cand) | (blk == cur)
    selb = jnp.concatenate([jnp.where(sel, 0.0, NEG), jnp.full((LANE - nbp, sc.shape[1]), NEG, F32)], axis=0)
    qaug_ref[...] = jnp.concatenate([q2, selb.astype(BF16)], axis=0)
    _flash_init_t(m_ref, acc_ref)

    def logits_of(p):
        off = pl.multiple_of(p * pair, pair)
        kaug = jnp.concatenate([k_ref[0, pl.ds(off, pair), :], kp_ref[pl.ds(off, pair), :]], axis=1)
        return _dot(kaug, qaug_ref[...])

    values_of = lambda p: vt_ref[0, :, pl.ds(pl.multiple_of(p * pair, pair), pair)]

    def far_body(p, s_max):
        alpha, w = _softmax_weights_t(s_ref[...], s_max, m_ref)
        acc_ref[...] = alpha * acc_ref[...] + _weighted_values_t(values_of(p), w)
        nxt = logits_of(p + 1)
        s_ref[...] = nxt
        return jnp.max(nxt, axis=0, keepdims=True)

    s0 = logits_of(0)
    s_ref[...] = s0
    lax.fori_loop(0, jnp.maximum(i - 1, 0), far_body, jnp.max(s0, axis=0, keepdims=True))
    toff = pl.multiple_of(jnp.where(i == 0, pair, 0), pair)
    s_ref[...] = s_ref[...] + tab_ref[0, pl.ds(toff, pair), :]
    _flash_step_t(s_ref, pair, values_of(jnp.maximum(i - 1, 0)), m_ref, acc_ref)

    @pl.when(i >= 1)
    def _():
        s_ref[...] = logits_of(i) + tab_ref[0, pair:2 * pair, :]
        _flash_step_t(s_ref, pair, values_of(i), m_ref, acc_ref)

    o = _flash_result_t(acc_ref)
    o_ref[0] = jnp.where(row < HEAD_DIM, o[:, :tq], o[:, tq:]).astype(BF16)


def _moba_prompt(qt, mkv16, vt, kmean_pad, kp, tab):
    b, _, t = qt.shape
    tq = MOBA_TQ
    nq = t // tq
    hp = H_MOBA // 2
    cols = 2 * tq
    return pl.pallas_call(
        functools.partial(_moba_kernel, n_blocks=t // MOBA_BLOCK),
        grid=(b, hp, nq),
        in_specs=[pl.BlockSpec((1, LANE, tq), lambda bb, h, i: (bb, h, i)),
                  pl.BlockSpec((1, t, LANE), lambda bb, h, i: (bb, 0, h)),
                  pl.BlockSpec((1, LANE, t), lambda bb, h, i: (bb, h, 0)),
                  pl.BlockSpec((1, LANE, LANE), lambda bb, h, i: (bb, 0, h)),
                  pl.BlockSpec((t, LANE), lambda bb, h, i: (0, 0), pipeline_mode=pl.Buffered(1)),
                  pl.BlockSpec((1,) + tab.shape[1:], lambda bb, h, i: (h, 0, 0), pipeline_mode=pl.Buffered(1))],
        out_specs=pl.BlockSpec((1, LANE, tq), lambda bb, h, i: (bb, h, i)),
        out_shape=jax.ShapeDtypeStruct((b, W_MOBA, t), BF16),
        scratch_shapes=[pltpu.VMEM((2 * LANE, cols), BF16),
                        pltpu.VMEM((1, cols), F32), pltpu.VMEM((LANE + ONES_ROWS, cols), F32),
                        pltpu.VMEM((2 * MOBA_BLOCK, cols), F32)],
        compiler_params=_cparams(("arbitrary", "arbitrary", "arbitrary")),
        name="moba_prompt",
    )(qt, mkv16, vt, kmean_pad, kp, tab)


def _cmp_kernel(pt_ref, *refs, pps):
    del pt_ref
    pages = refs[:pps]
    (pe_ref, kwa_ref, kwb_ref, kw2_ref, vwa_ref, vwb_ref, vw2_ref, kc_ref, vc_ref,
     uk_ref, uv_ref, sk_ref, sv_ref) = refs[pps:]
    j = pl.program_id(1)
    page = pages[0].shape[2]
    groups = pps * page // CMP_STRIDE
    for u in range(pps):
        xt = pages[u][0]
        sk_ref[u * page:(u + 1) * page, :] = xt[:LANE, :].T
        sv_ref[u * page:(u + 1) * page, :] = xt[LANE:, :].T
    row0 = pl.multiple_of(j * groups, groups)
    for l in range(CMP_STRIDE):
        rows_l = pl.ds(l, groups, stride=CMP_STRIDE)
        uk_ref[pl.ds(row0, groups), l * LANE:(l + 1) * LANE] = sk_ref[rows_l, :]
        uv_ref[pl.ds(row0, groups), l * LANE:(l + 1) * LANE] = sv_ref[rows_l, :]

    @pl.when(j == pl.num_programs(1) - 1)
    def _():
        nc = uk_ref.shape[0]
        last = lax.broadcasted_iota(jnp.int32, (nc, LANE), 0) == nc - 1

        def compress(u_ref, pe, wa_ref, wb_ref, w2_ref, out_ref):
            u = u_ref[...]
            pa = _dot((u + pe[0]).astype(BF16), wa_ref[...])
            pb = _dot((u + pe[1]).astype(BF16), wb_ref[...])
            pre = pa + pltpu.roll(pb, nc - 1, 0)
            hid = pre * _sigmoid(pre)
            out = _dot(hid.astype(BF16), w2_ref[...])
            out_ref[0] = jnp.where(last, 0.0, out).astype(BF16)

        compress(uk_ref, pe_ref[0], kwa_ref, kwb_ref, kw2_ref, kc_ref)
        compress(uv_ref, pe_ref[1], vwa_ref, vwb_ref, vw2_ref, vc_ref)


def _compress(pages_arr, pt, page, pe, kwa, kwb, kw2, vwa, vwb, vw2, paged):
    b, n_pg = pt.shape
    pps = min(PAGES_PER_STEP, n_pg)
    nc = n_pg * page // CMP_STRIDE
    if paged:
        page_spec = lambda u: pl.BlockSpec((1, 2 * LANE, page), lambda bb, j, p: (p[bb, j * pps + u], 0, 0))
    else:
        page_spec = lambda u: pl.BlockSpec((1, 2 * LANE, page), lambda bb, j, p: (bb, 0, j * pps + u))
    full = lambda a: pl.BlockSpec(a.shape, lambda bb, j, p: (0,) * a.ndim)
    consts = (pe, kwa, kwb, kw2, vwa, vwb, vw2)
    grid_spec = pltpu.PrefetchScalarGridSpec(
        num_scalar_prefetch=1,
        grid=(b, n_pg // pps),
        in_specs=[page_spec(u) for u in range(pps)] + [full(a) for a in consts],
        out_specs=[pl.BlockSpec((1, nc, LANE), lambda bb, j, p: (bb, 0, 0))] * 2,
        scratch_shapes=[pltpu.VMEM((nc, CMP_STRIDE * LANE), F32)] * 2 + [pltpu.VMEM((pps * page, LANE), F32)] * 2)
    return pl.pallas_call(
        functools.partial(_cmp_kernel, pps=pps),
        grid_spec=grid_spec,
        out_shape=[jax.ShapeDtypeStruct((b, nc, LANE), BF16)] * 2,
        compiler_params=_cparams(("arbitrary", "arbitrary")),
        name="nsa_compress",
    )(pt, *([pages_arr] * pps), *consts)


def _nsa_kernel(qt_ref, kc_ref, vct_ref, ovt_ref, ks_ref, vst_ref, kw_ref, vwt_ref, kp_ref, ts_ref, tw_ref, gt_ref,
                o_ref, qaug_ref, m_ref, acc_ref, ocmp_ref, oslc_ref, s_ref):
    k = pl.program_id(1)
    i = pl.program_id(2)
    tq = qt_ref.shape[2]
    q4 = jnp.concatenate([qt_ref[0, g * LANE:(g + 1) * LANE, :] for g in range(NSA_GROUP)], axis=1)
    qpos1 = i * tq + lax.broadcasted_iota(jnp.int32, (1, tq), 1)
    qpos = jnp.concatenate([qpos1] * NSA_GROUP, axis=1)

    nc = kc_ref.shape[1]
    lc = _dot(kc_ref[0], q4)
    tok = lax.broadcasted_iota(jnp.int32, lc.shape, 0)
    lm = jnp.where(tok * CMP_STRIDE + (CMP_LEN - 1) <= qpos, lc, NEG)
    mx = jnp.max(lm, axis=0, keepdims=True)
    s_ref[0:nc, :] = lm
    s_ref[0:nc, :] = jnp.exp2(s_ref[0:nc, :] - mx)
    norm = jnp.where(qpos >= CMP_LEN - 1,
                     1.0 / jnp.maximum(jnp.sum(s_ref[0:nc, :], axis=0, keepdims=True), TINY), 0.0)
    ocmp_ref[...] = _dot(vct_ref[0], s_ref[0:nc, :].astype(BF16)) * norm

    pcs = sum(s_ref[0:nc, g * tq:(g + 1) * tq] * norm[:, g * tq:(g + 1) * tq] for g in range(NSA_GROUP))
    ph, plo = _split_bf16(pcs)
    imp = _dot(ovt_ref[...], ph) + _dot(ovt_ref[...], plo)
    jb = lax.broadcasted_iota(jnp.int32, imp.shape, 0)
    cur = qpos1 >> int(math.log2(SLC_BLOCK))
    avail = jb <= cur
    forced = (jb == 0) | (jb == cur) | (jb == cur - 1)
    imp = jnp.where(avail, jnp.where(forced, FORCE, imp), NEG)
    sel = (_topk_mask(imp, SLC_TOPN, axis=0) > 0.0) & avail
    selb = jnp.where(sel, 0.0, NEG).astype(BF16)
    qaug_ref[...] = jnp.concatenate([q4, jnp.concatenate([selb] * NSA_GROUP, axis=1)], axis=0)

    cd = (i * tq) // KEY_TILE
    a0 = pl.multiple_of((i * tq) % KEY_TILE, tq)

    def table(t_ref, r0, n):
        return jnp.concatenate([t_ref[0, pl.ds(r0, n), pl.ds(g * KEY_TILE + a0, tq)] for g in range(NSA_GROUP)],
                               axis=1)

    _flash_init_t(m_ref, acc_ref)

    def slc_logits_at(off, n):
        kaug = jnp.concatenate([ks_ref[0, pl.ds(off, n), :], kp_ref[pl.ds(off, n), :]], axis=1)
        return _dot(kaug, qaug_ref[...])

    _sweep_tiles(slc_logits_at, lambda off, n: vst_ref[0, :, pl.ds(off, n)], functools.partial(table, ts_ref),
                 cd, m_ref, acc_ref, s_ref)
    oslc_ref[...] = _flash_result_t(acc_ref)

    _flash_init_t(m_ref, acc_ref)
    n_win = WINDOW // KEY_TILE
    span = WINDOW + KEY_TILE
    first = jnp.maximum(cd - n_win, 0)
    off = pl.multiple_of(first * KEY_TILE, KEY_TILE)
    toff = pl.multiple_of((first - (cd - n_win)) * KEY_TILE, KEY_TILE)
    s_ref[0:span, :] = _dot(kw_ref[0, pl.ds(off, span), :], q4) + table(tw_ref, toff, span)
    _flash_step_t(s_ref, span, vwt_ref[0, :, pl.ds(off, span)], m_ref, acc_ref)
    owin = _flash_result_t(acc_ref)

    ocmp = ocmp_ref[...]
    oslc = oslc_ref[...]
    gts = gt_ref[0]
    for g in range(NSA_GROUP):
        c = slice(g * tq, (g + 1) * tq)
        og = (gts[g:g + 1, :] * ocmp[:, c] + gts[NSA_GROUP + g:NSA_GROUP + g + 1, :] * oslc[:, c]
              + gts[2 * NSA_GROUP + g:2 * NSA_GROUP + g + 1, :] * owin[:, c])
        o_ref[0, g * HEAD_DIM:(g + 1) * HEAD_DIM, :] = jnp.where(k == 0, og[:HEAD_DIM], og[HEAD_DIM:]).astype(BF16)


def _nsa_prompt(qt, kc, vct, ovt, ksw, vst, vwt, kps, ts, tw, gt):
    b, _, t = qt.shape
    tq = min(NSA_TQ, t)
    nq = t // tq
    cols = NSA_GROUP * tq
    nc = kc.shape[1]
    keys = lambda col: pl.BlockSpec((1, t, LANE), lambda bb, k, i: (bb, 0, col))
    vals = pl.BlockSpec((1, LANE, t), lambda bb, k, i: (bb, 0, 0))
    return pl.pallas_call(
        _nsa_kernel,
        grid=(b, H_NSA_KV, nq),
        in_specs=[pl.BlockSpec((1, NSA_GROUP * LANE, tq), lambda bb, k, i: (bb, k, i)),
                  pl.BlockSpec((1, nc, LANE), lambda bb, k, i: (bb, 0, 0)),
                  pl.BlockSpec((1, LANE, nc), lambda bb, k, i: (bb, 0, 0)),
                  pl.BlockSpec(ovt.shape, lambda bb, k, i: (0, 0)),
                  keys(0), vals, keys(1), vals,
                  pl.BlockSpec((t, LANE), lambda bb, k, i: (0, 0), pipeline_mode=pl.Buffered(1)),
                  pl.BlockSpec((1,) + ts.shape[1:], lambda bb, k, i: (k, 0, 0), pipeline_mode=pl.Buffered(1)),
                  pl.BlockSpec((1,) + tw.shape[1:], lambda bb, k, i: (k, 0, 0), pipeline_mode=pl.Buffered(1)),
                  pl.BlockSpec((1, LANE, tq), lambda bb, k, i: (bb, k, i))],
        out_specs=pl.BlockSpec((1, NSA_GROUP * HEAD_DIM, tq), lambda bb, k, i: (bb, k, i)),
        out_shape=jax.ShapeDtypeStruct((b, W_NSA, t), BF16),
        scratch_shapes=[pltpu.VMEM((2 * LANE, cols), BF16),
                        pltpu.VMEM((1, cols), F32), pltpu.VMEM((LANE + ONES_ROWS, cols), F32),
                        pltpu.VMEM((LANE, cols), F32), pltpu.VMEM((LANE, cols), F32),
                        pltpu.VMEM((max(SWEEP_TILES * KEY_TILE, WINDOW + KEY_TILE, nc), cols), F32)],
        compiler_params=_cparams(("arbitrary", "arbitrary", "arbitrary")),
        name="nsa_prompt",
    )(qt, kc, vct, ovt, ksw, vst, ksw, vwt, kps, ts, tw, gt)


def _moba_dec_kernel(pt_ref, qbd_ref, new_ref, td_ref, *refs, pps, nblk):
    del pt_ref
    pages = refs[:pps]
    o_ref, sc_ref, mall_ref, lall_ref, oall_ref = refs[pps:]
    j = pl.program_id(1)
    qbd = qbd_ref[0]
    rows = qbd.shape[0]
    lane = lax.broadcasted_iota(jnp.int32, (rows, LANE), 1)
    ppb = MOBA_BLOCK // pages[0].shape[2]

    @pl.when(j == 0)
    def _():
        sc_ref[...] = jnp.zeros(sc_ref.shape, F32)
        mall_ref[...] = jnp.full(mall_ref.shape, NEG, F32)
        lall_ref[...] = jnp.zeros(lall_ref.shape, F32)

    for u in range(pps // ppb):
        blk = j * (pps // ppb) + u
        kt = jnp.concatenate([pages[u * ppb + w][0, :W_MOBA, :] for w in range(ppb)], axis=1)
        vt = jnp.concatenate([pages[u * ppb + w][0, W_MOBA:, :] for w in range(ppb)], axis=1)
        s = _dot(qbd, kt.astype(BF16))
        sc_ref[...] = jnp.where(lane == blk, jnp.sum(s, axis=1, keepdims=True), sc_ref[...])
        near = (blk == nblk - 1).astype(F32)
        s = s + near * td_ref[:, 0:MOBA_BLOCK]
        mb = jnp.max(s, axis=1, keepdims=True)
        p = jnp.exp2(s - mb)
        mall_ref[...] = jnp.where(lane == blk, mb, mall_ref[...])
        lall_ref[...] = jnp.where(lane == blk, jnp.sum(p, axis=1, keepdims=True), lall_ref[...])
        oall_ref[pl.ds(blk, 1)] = _dot_nt(p.astype(BF16), vt.astype(BF16))[None]

    @pl.when(j == pl.num_programs(1) - 1)
    def _():
        kn = new_ref[0, :, :W_MOBA].astype(BF16)
        vn = new_ref[0, :, W_MOBA:].astype(BF16)
        s_own = _dot_nt(qbd, kn) + td_ref[:, MOBA_BLOCK:MOBA_BLOCK + NEW_PAD]
        m_own = jnp.max(s_own, axis=1, keepdims=True)
        p_own = jnp.exp2(s_own - m_own)
        l_own = jnp.sum(p_own, axis=1, keepdims=True)
        o_own = _dot(p_own.astype(BF16), vn)
        sc = sc_ref[...]
        cand = lane < nblk
        sel = (_topk_mask(jnp.where(cand, sc, NEG), MOBA_TOPK) > 0.0) & cand
        mall = mall_ref[...]
        m_fin = jnp.maximum(jnp.max(jnp.where(sel, mall, NEG), axis=1, keepdims=True), m_own)
        w = jnp.where(sel, jnp.exp2(mall - m_fin), 0.0)
        w_own = jnp.exp2(m_own - m_fin)
        l = jnp.sum(w * lall_ref[...], axis=1, keepdims=True) + w_own * l_own
        acc = w_own * o_own
        for n in range(nblk):
            acc = acc + w[:, n:n + 1] * oall_ref[n]
        o = acc / jnp.maximum(l, TINY)
        rowh = lax.broadcasted_iota(jnp.int32, o.shape, 0) & (H_MOBA - 1)
        laneh = lax.broadcasted_iota(jnp.int32, o.shape, 1) >> int(math.log2(HEAD_DIM))
        o = jnp.where(rowh == laneh, o, 0.0)
        o_ref[0] = jnp.sum(o.reshape(rows // H_MOBA, H_MOBA, W_MOBA), axis=1)


def _moba_decode(cache_m, pt, qbd, new_pad, td):
    b, n_pg = pt.shape
    page = cache_m.shape[2]
    pps = min(PAGES_PER_STEP, n_pg)
    nblk = n_pg * page // MOBA_BLOCK
    rows = qbd.shape[1]
    s_new = rows // H_MOBA
    page_spec = lambda u: pl.BlockSpec((1, 2 * W_MOBA, page), lambda bb, j, p: (p[bb, j * pps + u], 0, 0))
    grid_spec = pltpu.PrefetchScalarGridSpec(
        num_scalar_prefetch=1,
        grid=(b, n_pg // pps),
        in_specs=[pl.BlockSpec((1, rows, W_MOBA), lambda bb, j, p: (bb, 0, 0)),
                  pl.BlockSpec((1, NEW_PAD, 2 * W_MOBA), lambda bb, j, p: (bb, 0, 0)),
                  pl.BlockSpec(td.shape, lambda bb, j, p: (0, 0))] + [page_spec(u) for u in range(pps)],
        out_specs=pl.BlockSpec((1, s_new, W_MOBA), lambda bb, j, p: (bb, 0, 0)),
        scratch_shapes=[pltpu.VMEM((rows, LANE), F32), pltpu.VMEM((rows, LANE), F32), pltpu.VMEM((rows, LANE), F32),
                        pltpu.VMEM((nblk, rows, W_MOBA), F32)])
    return pl.pallas_call(
        functools.partial(_moba_dec_kernel, pps=pps, nblk=nblk),
        grid_spec=grid_spec,
        out_shape=jax.ShapeDtypeStruct((b, s_new, W_MOBA), F32),
        compiler_params=_cparams(("arbitrary", "arbitrary")),
        name="moba_decode",
    )(pt, qbd, new_pad, td, *([cache_m] * pps))


def _nsa_dec_kernel(pt_ref, qd_ref, kc_ref, vc_ref, ov_ref, kp_ref, new_ref, win_ref, neww_ref, g_ref,
                    ts_ref, tso_ref, tw_ref, two_ref, *refs, pps, past):
    del pt_ref
    pages = refs[:pps]
    o_ref, qaug_ref, m_ref, l_ref, acc_ref, ocmp_ref, selown_ref = refs[pps:]
    j = pl.program_id(1)
    nj = pl.num_programs(1)
    qd = qd_ref[0]
    rows = qd.shape[0]
    page = pages[0].shape[2]
    grp = rows // NSA_GROUP
    s_new = grp // H_NSA_KV

    @pl.when(j == 0)
    def _():
        lc = _dot_nt(qd, kc_ref[0])
        tok = lax.broadcasted_iota(jnp.int32, lc.shape, 1)
        qpos = past + (lax.broadcasted_iota(jnp.int32, (rows, 1), 0) & (s_new - 1))
        pc = _masked_softmax(lc, tok * CMP_STRIDE + (CMP_LEN - 1) <= qpos)
        ocmp_ref[...] = _dot(pc.astype(BF16), vc_ref[0])
        pcs = pc[0:grp] + pc[grp:2 * grp] + pc[2 * grp:3 * grp] + pc[3 * grp:4 * grp]
        ph, plo = _split_bf16(pcs)
        imp = _dot(ph, ov_ref[...]) + _dot(plo, ov_ref[...])
        jb = lax.broadcasted_iota(jnp.int32, imp.shape, 1)
        cur = qpos[0:grp] >> int(math.log2(SLC_BLOCK))
        avail = jb <= cur
        forced = (jb == 0) | (jb == cur) | (jb == cur - 1)
        imp = jnp.where(avail, jnp.where(forced, FORCE, imp), NEG)
        sel = (_topk_mask(imp, SLC_TOPN) > 0.0) & avail
        selb = jnp.concatenate([jnp.where(sel, 0.0, NEG)] * NSA_GROUP, axis=0)
        qaug_ref[...] = jnp.concatenate([qd, selb[:, :LANE].astype(BF16)], axis=1)
        n_own = past // SLC_BLOCK
        selown_ref[...] = jnp.broadcast_to(selb[:, n_own:n_own + 1], selown_ref.shape)
        _flash_init(m_ref, l_ref, acc_ref)

    span = pps * page
    off = pl.multiple_of(j * span, span)
    kst = jnp.concatenate([pages[u][0, :LANE, :] for u in range(pps)], axis=1).astype(BF16)
    vst = jnp.concatenate([pages[u][0, LANE:, :] for u in range(pps)], axis=1).astype(BF16)
    kaug = jnp.concatenate([kst, kp_ref[:, pl.ds(off, span)]], axis=0)
    near = (j == nj - 1).astype(F32)
    s = _dot(qaug_ref[...], kaug) + near * ts_ref[...]
    _flash_step(s, vst, m_ref, l_ref, acc_ref, feature_major=True)

    @pl.when(j == nj - 1)
    def _():
        new = new_ref[0]
        s_own = _dot_nt(qd, new[:, 2 * LANE:3 * LANE].astype(BF16)) + tso_ref[...] + selown_ref[:, 0:1]
        _flash_step(s_own, new[:, 3 * LANE:].astype(BF16), m_ref, l_ref, acc_ref)
        oslc = acc_ref[...] / jnp.maximum(l_ref[...], TINY)
        _flash_init(m_ref, l_ref, acc_ref)
        win = win_ref[0]
        _flash_step(_dot(qd, win[:LANE, :].astype(BF16)) + tw_ref[...], win[LANE:, :].astype(BF16),
                    m_ref, l_ref, acc_ref, feature_major=True)
        neww = neww_ref[0]
        _flash_step(_dot_nt(qd, neww[:, :LANE].astype(BF16)) + two_ref[...], neww[:, LANE:].astype(BF16),
                    m_ref, l_ref, acc_ref)
        owin = acc_ref[...] / jnp.maximum(l_ref[...], TINY)
        gts = g_ref[0]
        o_ref[0] = gts[:, 0:1] * ocmp_ref[...] + gts[:, 1:2] * oslc + gts[:, 2:3] * owin


def _nsa_decode(cache_n, pt, qd, kc, vc, ov, kps, new_pad, win, neww, gts, ts, tso, tw, two):
    b, n_pg = pt.shape
    page = cache_n.shape[2]
    pps = min(PAGES_PER_STEP, n_pg)
    rows = qd.shape[1]
    nc = kc.shape[1]
    full = lambda a: pl.BlockSpec(a.shape, lambda bb, j, p: (0,) * a.ndim)
    per_b = lambda a: pl.BlockSpec((1,) + a.shape[1:], lambda bb, j, p: (bb,) + (0,) * (a.ndim - 1))
    page_spec = lambda u: pl.BlockSpec((1, 2 * LANE, page), lambda bb, j, p: (p[bb, j * pps + u], 1, 0))
    grid_spec = pltpu.PrefetchScalarGridSpec(
        num_scalar_prefetch=1,
        grid=(b, n_pg // pps),
        in_specs=[per_b(qd), per_b(kc), per_b(vc), full(ov), full(kps), per_b(new_pad), per_b(win), per_b(neww),
                  per_b(gts), full(ts), full(tso), full(tw), full(two)] + [page_spec(u) for u in range(pps)],
        out_specs=pl.BlockSpec((1, rows, LANE), lambda bb, j, p: (bb, 0, 0)),
        scratch_shapes=[pltpu.VMEM((rows, 2 * LANE), BF16),
                        pltpu.VMEM((rows, 1), F32), pltpu.VMEM((rows, 1), F32), pltpu.VMEM((rows, LANE), F32),
                        pltpu.VMEM((rows, LANE), F32), pltpu.VMEM((rows, LANE), F32)])
    return pl.pallas_call(
        functools.partial(_nsa_dec_kernel, pps=pps, past=n_pg * page),
        grid_spec=grid_spec,
        out_shape=jax.ShapeDtypeStruct((b, rows, LANE), F32),
        compiler_params=_cparams(("arbitrary", "arbitrary")),
        name="nsa_decode",
    )(pt, qd, kc, vc, ov, kps, new_pad, win, neww, gts, ts, tso, tw, two, *([cache_n] * pps))


def _out_kernel(x_ref, om_ref, zm_ref, on_ref, zn_ref, gate_ref, w_ref, fg_ref, y_ref, *, feature_major):
    zm = zm_ref[...].astype(F32)
    zn = zn_ref[...].astype(F32)
    if feature_major:
        om = om_ref[0].astype(F32).T
        on = on_ref[0].astype(F32).T
    else:
        om = om_ref[...].astype(F32)
        on = on_ref[...].astype(F32)
    mm = (om * (zm * _sigmoid(zm))).astype(BF16)
    mn = (on * (zn * _sigmoid(zn))).astype(BF16)
    mixed = _dot(mm, w_ref[:W_MOBA, :]) + _dot(mn, w_ref[W_MOBA:, :])
    xn = x_ref[...] + gate_ref[0] * mixed
    inv = lax.rsqrt(jnp.mean(xn * xn, axis=-1, keepdims=True) + RMS_EPS)
    y_ref[...] = (xn * inv) * fg_ref[...]


def _out_proj(x2d, om, zm, on, zn, gate, w_out, fgain, tm, tiles_per_mod):
    r, d = x2d.shape
    mrows = gate.shape[1]
    feature_major = om.ndim == 3
    row = lambda width: pl.BlockSpec((tm, width), lambda i: (i, 0))
    if feature_major:
        mixer = pl.BlockSpec((1, om.shape[1], tm), lambda i: (i // tiles_per_mod, 0, i % tiles_per_mod))
    else:
        mixer = row(W_MOBA)
    return pl.pallas_call(
        functools.partial(_out_kernel, feature_major=feature_major),
        grid=(r // tm,),
        in_specs=[row(d), mixer, row(W_MOBA), mixer, row(W_NSA),
                  pl.BlockSpec((1, mrows, d), lambda i: (i // tiles_per_mod, 0, 0)),
                  pl.BlockSpec((d, d), lambda i: (0, 0)),
                  pl.BlockSpec((1, d), lambda i: (0, 0))],
        out_specs=row(d),
        out_shape=jax.ShapeDtypeStruct((r, d), F32),
        compiler_params=_cparams(("arbitrary",)),
        name="out_proj",
    )(x2d, om, zm, on, zn, gate, w_out, fgain)


def _t5_bucket(rel):
    n = np.maximum(rel, 0)
    exact = N_BUCKETS // 2
    x = np.log(np.maximum(n, 1) / exact) / math.log(MAX_DISTANCE / exact) * (N_BUCKETS - exact)
    near_boundary = (np.abs(x - np.round(x)) < 1e-3) & (n > exact) & (n < MAX_DISTANCE)
    assert not near_boundary.any()
    large = exact + np.floor(x + 1e-6).astype(np.int64)
    return np.where(n < exact, n, np.minimum(large, N_BUCKETS - 1)).astype(np.int32)


def _bias_of_rel(tab, rel, shift_far):
    onehot = (jnp.asarray(_t5_bucket(rel))[..., None] == jnp.arange(N_BUCKETS, dtype=jnp.int32)).astype(F32)
    val = jnp.moveaxis(jnp.dot(onehot, tab, precision=lax.Precision.HIGHEST), -1, 0)
    if shift_far:
        val = val - tab[N_BUCKETS - 1].reshape((-1,) + (1,) * rel.ndim)
    return jnp.where(jnp.asarray(rel >= 0)[None], val * LOG2E, NEG)


def _layout_w_in(w_in):
    d = w_in.shape[0]
    sc = HEAD_DIM ** -0.5 * LOG2E
    o = 4 * W_MOBA
    q_m, rest_m = w_in[:, :W_MOBA] * sc, w_in[:, W_MOBA:o]
    q_n = (w_in[:, o:o + W_NSA] * sc).reshape(d, H_NSA_KV, NSA_GROUP, HEAD_DIM)
    qn_exp = jnp.concatenate([jnp.pad(q_n[:, 0], ((0, 0), (0, 0), (0, HEAD_DIM))),
                              jnp.pad(q_n[:, 1], ((0, 0), (0, 0), (HEAD_DIM, 0)))], axis=1).reshape(d, H_NSA * LANE)
    o += W_NSA
    kv_n = w_in[:, o:o + 6 * W_NSA_KV]
    o += 6 * W_NSA_KV
    g_n = jnp.swapaxes(w_in[:, o:o + 3 * H_NSA].reshape(d, H_NSA_KV, NSA_GROUP, 3), 2, 3)
    g_n = jnp.pad(g_n.reshape(d, H_NSA_KV, 3 * NSA_GROUP), ((0, 0), (0, 0), (0, LANE - 3 * NSA_GROUP)))
    z_n = w_in[:, o + 3 * H_NSA:]
    w = jnp.concatenate([q_m, rest_m, qn_exp, kv_n, z_n, g_n.reshape(d, H_NSA_KV * LANE)], axis=1).astype(BF16)
    assert w.shape[1] == C_END
    wt = jnp.concatenate([w[:, C_QM:C_ZM], w[:, C_QN:C_ZN], w[:, C_G:C_END]], axis=1).T
    assert wt.shape[0] == R_END
    return w, wt


def _layout_cmp(pe, w1, w2):
    w = w1.reshape(2, CMP_STRIDE, HEAD_DIM, CMP_HIDDEN)
    z = jnp.zeros_like(w)
    full = jnp.stack([jnp.concatenate([w, z], axis=-1), jnp.concatenate([z, w], axis=-1)], axis=2)
    full = full.reshape(2, CMP_STRIDE * H_NSA_KV * HEAD_DIM, H_NSA_KV * CMP_HIDDEN).astype(BF16)
    zz = jnp.zeros_like(w2)
    w2bd = jnp.concatenate([jnp.concatenate([w2, zz], axis=1), jnp.concatenate([zz, w2], axis=1)], axis=0).astype(BF16)
    pe2 = jnp.broadcast_to(pe.reshape(2, CMP_STRIDE, 1, HEAD_DIM), (2, CMP_STRIDE, H_NSA_KV, HEAD_DIM))
    return pe2.reshape(2, 1, CMP_STRIDE * LANE), full[0], full[1], w2bd


def _block_onehot(t, block):
    return (jnp.arange(t)[:, None] // block == jnp.arange(LANE)[None, :]).astype(BF16)


def _overlap(nc, n_cmp, lanes):
    i = jnp.arange(nc)[:, None]
    s = jnp.arange(lanes)[None, :]
    start = i * CMP_STRIDE
    hit = (start < s * SLC_BLOCK + SLC_BLOCK) & (start + CMP_LEN - 1 >= s * SLC_BLOCK) & (i < n_cmp)
    return hit.astype(BF16)


def kernel(x_prompt, x_sample, c_prompt, c_sample, cache_moba_kv, cache_nsa_kv, state_nsa_win, page_table, w_ada, b_ada, norm_gain, w_in, cmp_pe, cmp_k_w1, cmp_k_w2, cmp_v_w1, cmp_v_w2, w_out, rel_bias, final_gain):
    depth = w_in.shape[0]
    assert depth == 1
    B, T, D = x_prompt.shape
    BS, S, _ = x_sample.shape
    n_pg = page_table.shape[1]
    page = cache_moba_kv.shape[2]
    P = n_pg * page
    WB = state_nsa_win.shape[2]
    assert D == (H_MOBA + H_NSA) * HEAD_DIM and T % (SWEEP_TILES * KEY_TILE) == 0 and P % MOBA_BLOCK == 0 and S <= NEW_PAD
    assert T // SLC_BLOCK <= LANE and P // SLC_BLOCK <= LANE and T >= WINDOW and WB == WINDOW
    assert (BS * S) % SUBLANE == 0 and S & (S - 1) == 0

    w, wt = _layout_w_in(w_in[0])
    w_out_b = w_out[0].astype(BF16)
    gain = norm_gain[0].reshape(1, D)
    fgain = final_gain.reshape(1, D)
    kcmp_w = _layout_cmp(cmp_pe[0, 0], cmp_k_w1[0], cmp_k_w2[0])
    vcmp_w = _layout_cmp(cmp_pe[0, 1], cmp_v_w1[0], cmp_v_w2[0])
    pe2 = jnp.stack([kcmp_w[0], vcmp_w[0]], axis=0)
    cmp_consts = (pe2,) + kcmp_w[1:] + vcmp_w[1:]
    bias_m = rel_bias[:, :H_MOBA]
    bias_n = rel_bias[:, H_MOBA:]

    m_all = B + BS
    m_pad = -(-m_all // SUBLANE) * SUBLANE
    c_all = jnp.pad(jnp.concatenate([c_prompt, c_sample], axis=0), ((0, m_pad - m_all), (0, 0)))
    mod = _ada(c_all, w_ada[0], b_ada[0])
    shift, scale, gate = mod[:, :D], mod[:, D:2 * D], mod[:, 2 * D:]

    a = np.arange(KEY_TILE)[None, :]
    jk = np.arange(2 * KEY_TILE)[:, None]
    rel_s = a + KEY_TILE - jk
    heads_major = lambda tb, n_grp: jnp.transpose(
        tb.reshape(n_grp, -1, tb.shape[1], tb.shape[2]), (0, 2, 1, 3)).reshape(n_grp, tb.shape[1], -1)
    rows_of = lambda tb, n, val: jnp.full((tb.shape[0], n, tb.shape[2]), val, F32)
    pad_rows = (SWEEP_TILES - 1) * KEY_TILE
    sweep_rows = lambda tb: jnp.concatenate([rows_of(tb, pad_rows, 0.0), tb, rows_of(tb, pad_rows, NEG)], axis=1)
    rel_m = MOBA_BLOCK + np.arange(MOBA_TQ)[None, :] - np.arange(MOBA_BLOCK + MOBA_TQ)[:, None]
    tab_m = heads_major(_bias_of_rel(bias_m, rel_m, True), H_MOBA // 2)
    tab_m = jnp.concatenate([rows_of(tab_m, MOBA_BLOCK, 0.0), tab_m], axis=1)
    tab_s = sweep_rows(heads_major(_bias_of_rel(bias_n, rel_s, True), H_NSA_KV))
    jw = np.arange(WINDOW + KEY_TILE)[:, None]
    rel_w = a + WINDOW - jw
    tab_w = heads_major(_bias_of_rel(bias_n, np.where(rel_w < WINDOW, rel_w, -1), False), H_NSA_KV)
    tab_w = jnp.concatenate([tab_w, rows_of(tab_w, WINDOW, NEG)], axis=1)

    tpm = T // PROJ_TM
    (qmt, mkvt, vmt, qnt, nkvt, vst, wkvt, vwt, gt, km, ksw, zm, zn, kmean) = _proj_fm(
        x_prompt.reshape(B * T, D), scale[:B].reshape(B, 1, D), shift[:B].reshape(B, 1, D), gain, w, wt, B, PROJ_TM)
    nb = T // MOBA_BLOCK
    kmean_pad = jnp.pad(kmean.reshape(B, nb, W_MOBA), ((0, 0), (0, LANE - nb), (0, 0)))
    o_m = _moba_prompt(qmt, km.reshape(B, T, W_MOBA), vmt, kmean_pad, _block_onehot(T, MOBA_BLOCK), tab_m)
    pt_prompt = jnp.zeros((B, T // page), jnp.int32)
    kc_p, vc_p = _compress(nkvt, pt_prompt, page, *cmp_consts, paged=False)
    nc_p = T // CMP_STRIDE
    o_n = _nsa_prompt(qnt, kc_p, jnp.swapaxes(vc_p, 1, 2), _overlap(nc_p, nc_p - 1, LANE).T,
                      ksw.reshape(B, T, 2 * LANE), vst, vwt, _block_onehot(T, SLC_BLOCK), tab_s, tab_w, gt)
    y_prompt = _out_proj(x_prompt.reshape(B * T, D), o_m, zm, o_n, zn,
                         gate[:B].reshape(B, 1, D), w_out_b, fgain, PROJ_TM, tpm).reshape(B, T, D)
    tokens_first = lambda m, c: jnp.transpose(m.reshape(1, B, c, -1, HEAD_DIM, m.shape[-1]), (0, 1, 5, 2, 3, 4))
    moba_kv_prompt = tokens_first(mkvt, 2)
    nsa_kv_prompt = tokens_first(nkvt, 4)
    win_prompt = tokens_first(wkvt[:, :, T - WINDOW:], 2)

    RS = BS * S
    rep = lambda m: jnp.repeat(m[B:B + BS], S, axis=0)
    (qm_s, mkv32_s, zm_s, qn_s, nkv32_s, wkv32_s, zn_s, gts_s) = _proj(
        x_sample.reshape(RS, D), rep(scale), rep(shift), gain, w)
    pad_new = lambda m: jnp.pad(m.reshape(BS, S, -1), ((0, 0), (0, NEW_PAD - S), (0, 0)))
    s_idx = np.arange(S)

    q_rep = jnp.repeat(qm_s.reshape(BS, S, 1, H_MOBA, HEAD_DIM), H_MOBA, axis=2)
    eye = (jnp.arange(H_MOBA)[:, None] == jnp.arange(H_MOBA)[None, :])[None, None, :, :, None]
    qbd = jnp.where(eye, q_rep, 0).reshape(BS, S * H_MOBA, W_MOBA)
    rel_d = np.concatenate([MOBA_BLOCK + s_idx[:, None] - np.arange(MOBA_BLOCK)[None, :],
                            s_idx[:, None] - np.arange(NEW_PAD)[None, :]], axis=1)
    td = jnp.moveaxis(_bias_of_rel(bias_m, rel_d, True), 0, 1).reshape(S * H_MOBA, -1)
    td = jnp.pad(td, ((0, 0), (0, 3 * LANE - td.shape[1])))
    feat_pages = lambda c: jnp.transpose(c, (0, 2, 3, 4, 1)).reshape(c.shape[0], -1, c.shape[1])
    o_m_s = _moba_decode(feat_pages(cache_moba_kv[0]), page_table, qbd, pad_new(mkv32_s), td)

    cache_n = feat_pages(cache_nsa_kv[0])
    kc_s, vc_s = _compress(cache_n, page_table, page, *cmp_consts, paged=True)
    nc_s = P // CMP_STRIDE
    n_cmp_s = (P + S - CMP_LEN) // CMP_STRIDE + 1
    order = lambda m: jnp.transpose(m, (0, 3, 2, 1) + tuple(range(4, m.ndim)))
    rows_n = NSA_GROUP * H_NSA_KV * S
    qd = order(qn_s.reshape(BS, S, H_NSA_KV, NSA_GROUP, LANE)).reshape(BS, rows_n, LANE)
    g3 = gts_s.reshape(BS, S, H_NSA_KV, LANE)[..., :3 * NSA_GROUP].reshape(BS, S, H_NSA_KV, 3, NSA_GROUP)
    gts_d = order(jnp.swapaxes(g3, 3, 4)).reshape(BS, rows_n, 3)
    bias_d = jnp.swapaxes(bias_n.reshape(N_BUCKETS, H_NSA_KV, NSA_GROUP), 1, 2).reshape(N_BUCKETS, H_NSA)

    def dec_table(rel, shift_far):
        return _bias_of_rel(bias_d, rel, shift_far).reshape(rows_n, rel.shape[1])

    ts = dec_table(page + s_idx[:, None] - np.arange(page)[None, :], True)
    ts = jnp.pad(ts, ((0, 0), (min(PAGES_PER_STEP, n_pg) * page - page, 0)))
    rel_own = s_idx[:, None] - np.arange(NEW_PAD)[None, :]
    tso = dec_table(rel_own, True)
    rel_win = WB + s_idx[:, None] - np.arange(WB)[None, :]
    tw = dec_table(np.where(rel_win < WINDOW, rel_win, -1), False)
    two = dec_table(rel_own, False)
    win_state = feat_pages(state_nsa_win[0])
    o_n_raw = _nsa_decode(cache_n, page_table, qd, kc_s, vc_s, _overlap(nc_s, n_cmp_s, 2 * LANE),
                          _block_onehot(P, SLC_BLOCK).T, pad_new(nkv32_s), win_state, pad_new(wkv32_s), gts_d,
                          ts, tso, tw, two)
    o5 = o_n_raw.reshape(BS, NSA_GROUP, H_NSA_KV, S, H_NSA_KV, HEAD_DIM)
    o_n_s = jnp.stack([o5[:, :, k, :, k] for k in range(H_NSA_KV)], axis=1)
    o_n_s = jnp.transpose(o_n_s, (0, 3, 1, 2, 4)).reshape(RS, W_NSA)
    y_sample = _out_proj(x_sample.reshape(RS, D), o_m_s.reshape(RS, W_MOBA).astype(BF16), zm_s,
                         o_n_s.astype(BF16), zn_s, rep(gate).reshape(1, RS, D), w_out_b, fgain, RS, 1).reshape(BS, S, D)
    moba_kv_sample = mkv32_s.reshape(1, BS, S, 2, H_MOBA, HEAD_DIM)
    nsa_kv_sample = nkv32_s.reshape(1, BS, S, 4, H_NSA_KV, HEAD_DIM)
    win_sample = jnp.concatenate([win_state[:, :, S:], jnp.swapaxes(wkv32_s.reshape(BS, S, 2 * LANE), 1, 2)], axis=2)
    win_sample = jnp.transpose(win_sample.reshape(1, BS, 2, H_NSA_KV, HEAD_DIM, WB), (0, 1, 5, 2, 3, 4))
    return (y_prompt, y_sample, moba_kv_prompt, moba_kv_sample, nsa_kv_prompt, nsa_kv_sample, win_prompt, win_sample)
```

```python
import functools
import math

import jax
import jax.numpy as jnp
import numpy as np
from jax import lax
from jax.experimental import pallas as pl
from jax.experimental.pallas import tpu as pltpu

F32 = jnp.float32
BF16 = jnp.bfloat16

HEAD_DIM = 64
H_MOBA = 8
H_NSA = 8
H_NSA_KV = 2
NSA_GROUP = 4
W_MOBA = H_MOBA * HEAD_DIM
W_NSA = H_NSA * HEAD_DIM
W_NSA_KV = H_NSA_KV * HEAD_DIM
MOBA_BLOCK = 256
MOBA_TOPK = 3
CMP_LEN = 32
CMP_STRIDE = 16
CMP_HIDDEN = 2 * HEAD_DIM
SLC_BLOCK = 64
SLC_TOPN = 16
WINDOW = 512
N_BUCKETS = 32
MAX_DISTANCE = 128
RMS_EPS = 1e-6
NEG = -1e30
FORCE = 1e9
TINY = 1e-30

LANE = 128
SUBLANE = 8
KEY_TILE = 256
SWEEP_TILES = 2
NSA_TQ = 256
MOBA_TQ = 2 * MOBA_BLOCK
PROJ_TM = 256
MOBA_PAGES_PER_STEP = 32
NSA_PAGES_PER_STEP = 64
NEW_PAD = 8
ONES_ROWS = 16
LOG2E = math.log2(math.e)
VMEM_LIMIT = 56 * 1024 * 1024

C_QM, C_MKV, C_ZM, C_QN, C_NKV, C_WKV, C_ZN, C_G, C_END = 0, 512, 1536, 2048, 3072, 3584, 3840, 4352, 4608

_NT = (((1,), (1,)), ((), ()))


def _dot(a, b):
    return jnp.dot(a, b, preferred_element_type=F32)


def _dot_nt(a, b):
    return lax.dot_general(a, b, _NT, preferred_element_type=F32)


def _split_bf16(a):
    hi = a.astype(BF16)
    lo = (a - hi.astype(F32)).astype(BF16)
    return hi, lo


def _sigmoid(x):
    return 1.0 / (1.0 + jnp.exp(-x))


def _cparams(sem):
    return pltpu.CompilerParams(dimension_semantics=sem, vmem_limit_bytes=VMEM_LIMIT)


def _ada_kernel(c_ref, w_ref, b_ref, o_ref):
    ch, cl = _split_bf16(c_ref[...])
    wh, wl = _split_bf16(w_ref[...])
    o_ref[...] = _dot(ch, wh) + _dot(ch, wl) + _dot(cl, wh) + b_ref[...]


def _ada(c_all, w_ada, b_ada):
    m, d = c_all.shape
    n = w_ada.shape[1]
    tn = 512
    return pl.pallas_call(
        _ada_kernel,
        grid=(n // tn,),
        in_specs=[pl.BlockSpec((m, d), lambda j: (0, 0)),
                  pl.BlockSpec((d, tn), lambda j: (0, j)),
                  pl.BlockSpec((1, tn), lambda j: (0, j))],
        out_specs=pl.BlockSpec((m, tn), lambda j: (0, j)),
        out_shape=jax.ShapeDtypeStruct((m, n), F32),
        compiler_params=_cparams(("arbitrary",)),
        name="ada",
    )(c_all, w_ada, b_ada.reshape(1, n))


def _modulated_norm(x, gain, scale, shift):
    inv = lax.rsqrt(jnp.mean(x * x, axis=-1, keepdims=True) + RMS_EPS)
    return (x * inv) * gain * (1.0 + scale) + shift


def _proj_kernel(x_ref, sc_ref, sh_ref, gain_ref, w_ref, qm_ref, mkv_ref, zm_ref, qn_ref, nkv_ref, wkv_ref, zn_ref, g_ref):
    hb = _modulated_norm(x_ref[...], gain_ref[...], sc_ref[...], sh_ref[...]).astype(BF16)
    col = lambda a, b: _dot(hb, w_ref[:, a:b])
    qm_ref[...] = col(C_QM, C_MKV).astype(BF16)
    mkv_ref[...] = col(C_MKV, C_ZM)
    zm_ref[...] = col(C_ZM, C_QN).astype(BF16)
    qn_ref[...] = col(C_QN, C_NKV).astype(BF16)
    nkv_ref[...] = col(C_NKV, C_WKV)
    wkv_ref[...] = col(C_WKV, C_ZN)
    zn_ref[...] = col(C_ZN, C_G).astype(BF16)
    g_ref[...] = _sigmoid(col(C_G, C_END))


def _proj(x2d, sc, sh, gain, w):
    r, d = x2d.shape
    full = lambda width: pl.BlockSpec((r, width), lambda i: (0, 0))
    widths = [(512, BF16), (1024, F32), (512, BF16), (1024, BF16), (512, F32), (256, F32), (512, BF16), (256, F32)]
    return pl.pallas_call(
        _proj_kernel,
        grid=(1,),
        in_specs=[full(d), full(d), full(d), pl.BlockSpec((1, d), lambda i: (0, 0)),
                  pl.BlockSpec((d, C_END), lambda i: (0, 0))],
        out_specs=[full(wd) for wd, _ in widths],
        out_shape=[jax.ShapeDtypeStruct((r, wd), dt) for wd, dt in widths],
        compiler_params=_cparams(("arbitrary",)),
        name="proj_decode",
    )(x2d, sc, sh, gain, w)


R_QM, R_MKV, R_QN, R_NKV, R_WKV, R_G, R_END = 0, 512, 1536, 2560, 3072, 3328, 3584


def _proj_fm_kernel(x_ref, sc_ref, sh_ref, gain_ref, w_ref, wt_ref,
                    qmt_ref, mkvt_ref, vmt_ref, qnt_ref, nkvt_ref, vst_ref, wkvt_ref, vwt_ref, gt_ref,
                    km_ref, ksw_ref, zm_ref, zn_ref, kmean_ref):
    h = _modulated_norm(x_ref[...], gain_ref[...], sc_ref[0], sh_ref[0])
    hb = h.astype(BF16)
    ht = h.T.astype(BF16)
    frow = lambda a, b: _dot(wt_ref[a:b, :], ht)
    col = lambda a, b: _dot(hb, w_ref[:, a:b])
    qmt_ref[0] = frow(R_QM, R_MKV).astype(BF16)
    mkvt = frow(R_MKV, R_QN)
    mkvt_ref[0] = mkvt
    vmt_ref[0] = mkvt[W_MOBA:].astype(BF16)
    qnt_ref[0] = frow(R_QN, R_NKV).astype(BF16)
    nkvt = frow(R_NKV, R_WKV)
    nkvt_ref[0] = nkvt
    vst_ref[0] = nkvt[3 * W_NSA_KV:].astype(BF16)
    wkvt = frow(R_WKV, R_G)
    wkvt_ref[0] = wkvt
    vwt_ref[0] = wkvt[W_NSA_KV:].astype(BF16)
    gt_ref[0] = _sigmoid(frow(R_G, R_END))
    km = col(C_MKV, C_MKV + W_MOBA)
    km_ref[...] = km.astype(BF16)
    kmean_ref[0] = jnp.mean(km, axis=0, keepdims=True)
    ksw_ref[...] = jnp.concatenate([col(C_NKV + 2 * W_NSA_KV, C_NKV + 3 * W_NSA_KV), col(C_WKV, C_WKV + W_NSA_KV)],
                                   axis=1).astype(BF16)
    zm_ref[...] = col(C_ZM, C_QN).astype(BF16)
    zn_ref[...] = col(C_ZN, C_G).astype(BF16)


def _proj_fm(x2d, sc, sh, gain, w, wt, b, tm):
    r, d = x2d.shape
    t = r // b
    tpb = t // tm
    nt = r // tm
    row = lambda width: pl.BlockSpec((tm, width), lambda i: (i, 0))
    fm = lambda rows: pl.BlockSpec((1, rows, tm), lambda i: (i // tpb, 0, i % tpb))
    mod = pl.BlockSpec((1, 1, d), lambda i: (i // tpb, 0, 0))
    const = lambda a: pl.BlockSpec(a.shape, lambda i: (0, 0), pipeline_mode=pl.Buffered(1))
    fm_outs = [(512, BF16), (1024, F32), (512, BF16), (1024, BF16), (512, F32), (128, BF16), (256, F32), (128, BF16),
               (256, F32)]
    tm_outs = [(512, BF16), (256, BF16), (512, BF16), (512, BF16)]
    return pl.pallas_call(
        _proj_fm_kernel,
        grid=(nt,),
        in_specs=[row(d), mod, mod, pl.BlockSpec((1, d), lambda i: (0, 0)), const(w), const(wt)],
        out_specs=[fm(rows) for rows, _ in fm_outs] + [row(wd) for wd, _ in tm_outs]
        + [pl.BlockSpec((1, 1, W_MOBA), lambda i: (i, 0, 0))],
        out_shape=[jax.ShapeDtypeStruct((b, rows, t), dt) for rows, dt in fm_outs]
        + [jax.ShapeDtypeStruct((r, wd), dt) for wd, dt in tm_outs] + [jax.ShapeDtypeStruct((nt, 1, W_MOBA), F32)],
        compiler_params=_cparams(("arbitrary",)),
        name="proj_prompt",
    )(x2d, sc, sh, gain, w, wt)


def _flash_step(s, v, m_ref, l_ref, acc_ref, feature_major=False):
    m_old = m_ref[...]
    m_new = jnp.maximum(m_old, jnp.max(s, axis=1, keepdims=True))
    alpha = jnp.exp2(m_old - m_new)
    p = jnp.exp2(s - m_new)
    l_ref[...] = alpha * l_ref[...] + jnp.sum(p, axis=1, keepdims=True)
    pv = _dot_nt(p.astype(BF16), v) if feature_major else _dot(p.astype(BF16), v)
    acc_ref[...] = alpha * acc_ref[...] + pv
    m_ref[...] = m_new


def _flash_init(m_ref, l_ref, acc_ref):
    m_ref[...] = jnp.full(m_ref.shape, NEG, F32)
    l_ref[...] = jnp.zeros(l_ref.shape, F32)
    acc_ref[...] = jnp.zeros(acc_ref.shape, F32)


def _flash_step_t(s_ref, n_keys, vt, m_ref, acc_ref):
    s_max = jnp.max(s_ref[0:n_keys, :], axis=0, keepdims=True)
    alpha, p = _softmax_weights_t(s_ref[0:n_keys, :], s_max, m_ref)
    acc_ref[...] = alpha * acc_ref[...] + _weighted_values_t(vt, p)


def _softmax_weights_t(s, s_max, m_ref):
    m_old = m_ref[...]
    m_new = jnp.maximum(m_old, s_max)
    m_ref[...] = m_new
    return jnp.exp2(m_old - m_new), jnp.exp2(s - m_new).astype(BF16)


def _weighted_values_t(vt, p):
    vt_aug = jnp.concatenate([vt, jnp.ones((ONES_ROWS, vt.shape[1]), BF16)], axis=0)
    return _dot(vt_aug, p)


def _flash_init_t(m_ref, acc_ref):
    m_ref[...] = jnp.full(m_ref.shape, NEG, F32)
    acc_ref[...] = jnp.zeros(acc_ref.shape, F32)


def _flash_result_t(acc_ref):
    width = acc_ref.shape[0] - ONES_ROWS
    return acc_ref[0:width, :] / jnp.maximum(acc_ref[width:width + 1, :], TINY)


def _sweep_tiles(logits_at, values_at, table_at, i_diag, m_ref, acc_ref, s_ref):
    g = SWEEP_TILES
    big = g * KEY_TILE
    n_far = jnp.maximum(i_diag - 1, 0)
    n_big = n_far // g
    rem = n_far - n_big * g

    def far_body(n, s_max):
        alpha, p = _softmax_weights_t(s_ref[0:big, :], s_max, m_ref)
        acc_ref[...] = alpha * acc_ref[...] + _weighted_values_t(values_at(pl.multiple_of(n * big, big), big), p)
        nxt = logits_at(pl.multiple_of((n + 1) * big, big), big)
        s_ref[0:big, :] = nxt
        return jnp.max(nxt, axis=0, keepdims=True)

    s0 = logits_at(0, big)
    s_ref[0:big, :] = s0
    lax.fori_loop(0, n_big, far_body, jnp.max(s0, axis=0, keepdims=True))
    first_block = jnp.where(i_diag == 0, g, g - 1 - rem)
    s_ref[0:big, :] = s_ref[0:big, :] + table_at(pl.multiple_of(first_block * KEY_TILE, KEY_TILE), big)
    _flash_step_t(s_ref, big, values_at(pl.multiple_of(n_big * big, big), big), m_ref, acc_ref)

    @pl.when((i_diag >= 1) & (rem == g - 1))
    def _():
        off = pl.multiple_of(i_diag * KEY_TILE, KEY_TILE)
        s_ref[0:KEY_TILE, :] = logits_at(off, KEY_TILE) + table_at(g * KEY_TILE, KEY_TILE)
        _flash_step_t(s_ref, KEY_TILE, values_at(off, KEY_TILE), m_ref, acc_ref)


def _topk_mask(vals, k, axis=1):
    idxf = lax.broadcasted_iota(jnp.int32, vals.shape, axis).astype(F32)

    def body(_, taken):
        cur = jnp.where(taken > 0.0, -jnp.inf, vals)
        mx = jnp.max(cur, axis=axis, keepdims=True)
        first = jnp.min(jnp.where(cur == mx, idxf, 1e9), axis=axis, keepdims=True)
        return jnp.where(idxf == first, 1.0, taken)

    return lax.fori_loop(0, k, body, jnp.zeros(vals.shape, F32))


def _masked_softmax(logits, valid, axis=1):
    lm = jnp.where(valid, logits, NEG)
    p = jnp.exp2(lm - jnp.max(lm, axis=axis, keepdims=True)) * valid.astype(F32)
    return p / jnp.maximum(jnp.sum(p, axis=axis, keepdims=True), TINY)


def _moba_kernel(qt_ref, k_ref, vt_ref, km_ref, kp_ref, tab_ref, o_ref, qaug_ref, m_ref, acc_ref, s_ref, *, n_blocks):
    i = pl.program_id(2)
    tq = qt_ref.shape[2]
    pair = 2 * MOBA_BLOCK
    qt = qt_ref[0]
    row = lax.broadcasted_iota(jnp.int32, qt.shape, 0)
    zero = jnp.zeros_like(qt)
    q2 = jnp.concatenate([jnp.where(row < HEAD_DIM, qt, zero), jnp.where(row >= HEAD_DIM, qt, zero)], axis=1)
    nbp = -(-n_blocks // ONES_ROWS) * ONES_ROWS
    sc = _dot(km_ref[0, :nbp, :].astype(BF16), q2)
    blk = lax.broadcasted_iota(jnp.int32, sc.shape, 0)
    col = lax.broadcasted_iota(jnp.int32, (1, sc.shape[1]), 1)
    cur = 2 * i + ((col & (tq - 1)) >> int(math.log2(MOBA_BLOCK)))
    cand = blk < cur
    taken = _topk_mask(jnp.where(cand, sc, NEG), MOBA_TOPK, axis=0)
    sel = ((taken > 0.0) & cand) | (blk == cur)
    selb = jnp.concatenate([jnp.where(sel, 0.0, NEG), jnp.full((LANE - nbp, sc.shape[1]), NEG, F32)], axis=0)
    qaug_ref[...] = jnp.concatenate([q2, selb.astype(BF16)], axis=0)
    _flash_init_t(m_ref, acc_ref)

    def logits_of(p):
        off = pl.multiple_of(p * pair, pair)
        kaug = jnp.concatenate([k_ref[0, pl.ds(off, pair), :], kp_ref[pl.ds(off, pair), :]], axis=1)
        return _dot(kaug, qaug_ref[...])

    values_of = lambda p: vt_ref[0, :, pl.ds(pl.multiple_of(p * pair, pair), pair)]

    def far_body(p, s_max):
        alpha, w = _softmax_weights_t(s_ref[...], s_max, m_ref)
        acc_ref[...] = alpha * acc_ref[...] + _weighted_values_t(values_of(p), w)
        nxt = logits_of(p + 1)
        s_ref[...] = nxt
        return jnp.max(nxt, axis=0, keepdims=True)

    s0 = logits_of(0)
    s_ref[...] = s0
    lax.fori_loop(0, jnp.maximum(i - 1, 0), far_body, jnp.max(s0, axis=0, keepdims=True))
    toff = pl.multiple_of(jnp.where(i == 0, pair, 0), pair)
    s_ref[...] = s_ref[...] + tab_ref[0, pl.ds(toff, pair), :]
    _flash_step_t(s_ref, pair, values_of(jnp.maximum(i - 1, 0)), m_ref, acc_ref)

    @pl.when(i >= 1)
    def _():
        s_ref[...] = logits_of(i) + tab_ref[0, pair:2 * pair, :]
        _flash_step_t(s_ref, pair, values_of(i), m_ref, acc_ref)

    o = _flash_result_t(acc_ref)
    o_ref[0] = jnp.where(row < HEAD_DIM, o[:, :tq], o[:, tq:]).astype(BF16)


def _moba_prompt(qt, mkv16, vt, kmean_pad, kp, tab):
    b, _, t = qt.shape
    tq = MOBA_TQ
    nq = t // tq
    hp = H_MOBA // 2
    cols = 2 * tq
    return pl.pallas_call(
        functools.partial(_moba_kernel, n_blocks=t // MOBA_BLOCK),
        grid=(b, hp, nq),
        in_specs=[pl.BlockSpec((1, LANE, tq), lambda bb, h, i: (bb, h, i)),
                  pl.BlockSpec((1, t, LANE), lambda bb, h, i: (bb, 0, h)),
                  pl.BlockSpec((1, LANE, t), lambda bb, h, i: (bb, h, 0)),
                  pl.BlockSpec((1, LANE, LANE), lambda bb, h, i: (bb, 0, h)),
                  pl.BlockSpec((t, LANE), lambda bb, h, i: (0, 0), pipeline_mode=pl.Buffered(1)),
                  pl.BlockSpec((1,) + tab.shape[1:], lambda bb, h, i: (h, 0, 0), pipeline_mode=pl.Buffered(1))],
        out_specs=pl.BlockSpec((1, LANE, tq), lambda bb, h, i: (bb, h, i)),
        out_shape=jax.ShapeDtypeStruct((b, W_MOBA, t), BF16),
        scratch_shapes=[pltpu.VMEM((2 * LANE, cols), BF16),
                        pltpu.VMEM((1, cols), F32), pltpu.VMEM((LANE + ONES_ROWS, cols), F32),
                        pltpu.VMEM((2 * MOBA_BLOCK, cols), F32)],
        compiler_params=_cparams(("arbitrary", "arbitrary", "arbitrary")),
        name="moba_prompt",
    )(qt, mkv16, vt, kmean_pad, kp, tab)


def _cmp_kernel(pt_ref, *refs, pps):
    del pt_ref
    pages = refs[:pps]
    (pe_ref, kwa_ref, kwb_ref, kw2_ref, vwa_ref, vwb_ref, vw2_ref, kc_ref, vc_ref,
     uk_ref, uv_ref, sk_ref, sv_ref) = refs[pps:]
    j = pl.program_id(1)
    page = pages[0].shape[2]
    groups = pps * page // CMP_STRIDE
    for u in range(pps):
        xt = pages[u][0]
        sk_ref[u * page:(u + 1) * page, :] = xt[:LANE, :].T
        sv_ref[u * page:(u + 1) * page, :] = xt[LANE:, :].T
    row0 = pl.multiple_of(j * groups, groups)
    for l in range(CMP_STRIDE):
        rows_l = pl.ds(l, groups, stride=CMP_STRIDE)
        uk_ref[pl.ds(row0, groups), l * LANE:(l + 1) * LANE] = sk_ref[rows_l, :]
        uv_ref[pl.ds(row0, groups), l * LANE:(l + 1) * LANE] = sv_ref[rows_l, :]

    @pl.when(j == pl.num_programs(1) - 1)
    def _():
        nc = uk_ref.shape[0]
        last = lax.broadcasted_iota(jnp.int32, (nc, LANE), 0) == nc - 1

        def compress(u_ref, pe, wa_ref, wb_ref, w2_ref, out_ref):
            u = u_ref[...]
            pa = _dot((u + pe[0]).astype(BF16), wa_ref[...])
            pb = _dot((u + pe[1]).astype(BF16), wb_ref[...])
            pre = pa + pltpu.roll(pb, nc - 1, 0)
            hid = pre * _sigmoid(pre)
            out = _dot(hid.astype(BF16), w2_ref[...])
            out_ref[0] = jnp.where(last, 0.0, out).astype(BF16)

        compress(uk_ref, pe_ref[0], kwa_ref, kwb_ref, kw2_ref, kc_ref)
        compress(uv_ref, pe_ref[1], vwa_ref, vwb_ref, vw2_ref, vc_ref)


def _compress(pages_arr, pt, page, pe, kwa, kwb, kw2, vwa, vwb, vw2, paged):
    b, n_pg = pt.shape
    pps = min(NSA_PAGES_PER_STEP, n_pg)
    nc = n_pg * page // CMP_STRIDE
    if paged:
        page_spec = lambda u: pl.BlockSpec((1, 2 * LANE, page), lambda bb, j, p: (p[bb, j * pps + u], 0, 0))
    else:
        page_spec = lambda u: pl.BlockSpec((1, 2 * LANE, page), lambda bb, j, p: (bb, 0, j * pps + u))
    full = lambda a: pl.BlockSpec(a.shape, lambda bb, j, p: (0,) * a.ndim)
    consts = (pe, kwa, kwb, kw2, vwa, vwb, vw2)
    grid_spec = pltpu.PrefetchScalarGridSpec(
        num_scalar_prefetch=1,
        grid=(b, n_pg // pps),
        in_specs=[page_spec(u) for u in range(pps)] + [full(a) for a in consts],
        out_specs=[pl.BlockSpec((1, nc, LANE), lambda bb, j, p: (bb, 0, 0))] * 2,
        scratch_shapes=[pltpu.VMEM((nc, CMP_STRIDE * LANE), F32)] * 2 + [pltpu.VMEM((pps * page, LANE), F32)] * 2)
    return pl.pallas_call(
        functools.partial(_cmp_kernel, pps=pps),
        grid_spec=grid_spec,
        out_shape=[jax.ShapeDtypeStruct((b, nc, LANE), BF16)] * 2,
        compiler_params=_cparams(("arbitrary", "arbitrary")),
        name="nsa_compress",
    )(pt, *([pages_arr] * pps), *consts)


def _nsa_kernel(qt_ref, kc_ref, vct_ref, ovt_ref, ks_ref, vst_ref, kw_ref, vwt_ref, kp_ref, ts_ref, tw_ref, gt_ref,
                o_ref, qaug_ref, m_ref, acc_ref, ocmp_ref, oslc_ref, s_ref):
    k = pl.program_id(1)
    i = pl.program_id(2)
    tq = qt_ref.shape[2]
    q4 = jnp.concatenate([qt_ref[0, g * LANE:(g + 1) * LANE, :] for g in range(NSA_GROUP)], axis=1)
    qpos1 = i * tq + lax.broadcasted_iota(jnp.int32, (1, tq), 1)
    qpos = jnp.concatenate([qpos1] * NSA_GROUP, axis=1)

    nc = kc_ref.shape[1]
    lc = _dot(kc_ref[0], q4)
    tok = lax.broadcasted_iota(jnp.int32, lc.shape, 0)
    lm = jnp.where(tok * CMP_STRIDE + (CMP_LEN - 1) <= qpos, lc, NEG)
    mx = jnp.max(lm, axis=0, keepdims=True)
    s_ref[0:nc, :] = lm
    s_ref[0:nc, :] = jnp.exp2(s_ref[0:nc, :] - mx)
    norm = jnp.where(qpos >= CMP_LEN - 1,
                     1.0 / jnp.maximum(jnp.sum(s_ref[0:nc, :], axis=0, keepdims=True), TINY), 0.0)
    ocmp_ref[...] = _dot(vct_ref[0], s_ref[0:nc, :].astype(BF16)) * norm

    pcs = sum(s_ref[0:nc, g * tq:(g + 1) * tq] * norm[:, g * tq:(g + 1) * tq] for g in range(NSA_GROUP))
    ph, plo = _split_bf16(pcs)
    imp = _dot(ovt_ref[...], ph) + _dot(ovt_ref[...], plo)
    jb = lax.broadcasted_iota(jnp.int32, imp.shape, 0)
    cur = qpos1 >> int(math.log2(SLC_BLOCK))
    avail = jb <= cur
    forced = (jb == 0) | (jb == cur) | (jb == cur - 1)
    imp = jnp.where(avail, jnp.where(forced, FORCE, imp), NEG)
    sel = (_topk_mask(imp, SLC_TOPN, axis=0) > 0.0) & avail
    selb = jnp.where(sel, 0.0, NEG).astype(BF16)
    qaug_ref[...] = jnp.concatenate([q4, jnp.concatenate([selb] * NSA_GROUP, axis=1)], axis=0)

    cd = (i * tq) // KEY_TILE
    a0 = pl.multiple_of((i * tq) % KEY_TILE, tq)

    def table(t_ref, r0, n):
        return jnp.concatenate([t_ref[0, pl.ds(r0, n), pl.ds(g * KEY_TILE + a0, tq)] for g in range(NSA_GROUP)],
                               axis=1)

    _flash_init_t(m_ref, acc_ref)

    def slc_logits_at(off, n):
        kaug = jnp.concatenate([ks_ref[0, pl.ds(off, n), :], kp_ref[pl.ds(off, n), :]], axis=1)
        return _dot(kaug, qaug_ref[...])

    _sweep_tiles(slc_logits_at, lambda off, n: vst_ref[0, :, pl.ds(off, n)], functools.partial(table, ts_ref),
                 cd, m_ref, acc_ref, s_ref)
    oslc_ref[...] = _flash_result_t(acc_ref)

    _flash_init_t(m_ref, acc_ref)
    n_win = WINDOW // KEY_TILE
    span = WINDOW + KEY_TILE
    first = jnp.maximum(cd - n_win, 0)
    off = pl.multiple_of(first * KEY_TILE, KEY_TILE)
    toff = pl.multiple_of((first - (cd - n_win)) * KEY_TILE, KEY_TILE)
    s_ref[0:span, :] = _dot(kw_ref[0, pl.ds(off, span), :], q4) + table(tw_ref, toff, span)
    _flash_step_t(s_ref, span, vwt_ref[0, :, pl.ds(off, span)], m_ref, acc_ref)
    owin = _flash_result_t(acc_ref)

    ocmp = ocmp_ref[...]
    oslc = oslc_ref[...]
    gts = gt_ref[0]
    for g in range(NSA_GROUP):
        c = slice(g * tq, (g + 1) * tq)
        og = (gts[g:g + 1, :] * ocmp[:, c] + gts[NSA_GROUP + g:NSA_GROUP + g + 1, :] * oslc[:, c]
              + gts[2 * NSA_GROUP + g:2 * NSA_GROUP + g + 1, :] * owin[:, c])
        o_ref[0, g * HEAD_DIM:(g + 1) * HEAD_DIM, :] = jnp.where(k == 0, og[:HEAD_DIM], og[HEAD_DIM:]).astype(BF16)


def _nsa_prompt(qt, kc, vct, ovt, ksw, vst, vwt, kps, ts, tw, gt):
    b, _, t = qt.shape
    tq = min(NSA_TQ, t)
    nq = t // tq
    cols = NSA_GROUP * tq
    nc = kc.shape[1]
    keys = lambda col: pl.BlockSpec((1, t, LANE), lambda bb, k, i: (bb, 0, col))
    vals = pl.BlockSpec((1, LANE, t), lambda bb, k, i: (bb, 0, 0))
    return pl.pallas_call(
        _nsa_kernel,
        grid=(b, H_NSA_KV, nq),
        in_specs=[pl.BlockSpec((1, NSA_GROUP * LANE, tq), lambda bb, k, i: (bb, k, i)),
                  pl.BlockSpec((1, nc, LANE), lambda bb, k, i: (bb, 0, 0)),
                  pl.BlockSpec((1, LANE, nc), lambda bb, k, i: (bb, 0, 0)),
                  pl.BlockSpec(ovt.shape, lambda bb, k, i: (0, 0)),
                  keys(0), vals, keys(1), vals,
                  pl.BlockSpec((t, LANE), lambda bb, k, i: (0, 0), pipeline_mode=pl.Buffered(1)),
                  pl.BlockSpec((1,) + ts.shape[1:], lambda bb, k, i: (k, 0, 0), pipeline_mode=pl.Buffered(1)),
                  pl.BlockSpec((1,) + tw.shape[1:], lambda bb, k, i: (k, 0, 0), pipeline_mode=pl.Buffered(1)),
                  pl.BlockSpec((1, LANE, tq), lambda bb, k, i: (bb, k, i))],
        out_specs=pl.BlockSpec((1, NSA_GROUP * HEAD_DIM, tq), lambda bb, k, i: (bb, k, i)),
        out_shape=jax.ShapeDtypeStruct((b, W_NSA, t), BF16),
        scratch_shapes=[pltpu.VMEM((2 * LANE, cols), BF16),
                        pltpu.VMEM((1, cols), F32), pltpu.VMEM((LANE + ONES_ROWS, cols), F32),
                        pltpu.VMEM((LANE, cols), F32), pltpu.VMEM((LANE, cols), F32),
                        pltpu.VMEM((max(SWEEP_TILES * KEY_TILE, WINDOW + KEY_TILE, nc), cols), F32)],
        compiler_params=_cparams(("arbitrary", "arbitrary", "arbitrary")),
        name="nsa_prompt",
    )(qt, kc, vct, ovt, ksw, vst, ksw, vwt, kps, ts, tw, gt)


def _moba_dec_kernel(pt_ref, qbd_ref, new_ref, td_ref, *refs, pps, nblk):
    del pt_ref
    pages = refs[:pps]
    o_ref, sc_ref, mall_ref, lall_ref, oall_ref = refs[pps:]
    j = pl.program_id(1)
    qbd = qbd_ref[0]
    rows = qbd.shape[0]
    lane = lax.broadcasted_iota(jnp.int32, (rows, LANE), 1)
    ppb = MOBA_BLOCK // pages[0].shape[2]

    @pl.when(j == 0)
    def _():
        sc_ref[...] = jnp.zeros(sc_ref.shape, F32)
        mall_ref[...] = jnp.full(mall_ref.shape, NEG, F32)
        lall_ref[...] = jnp.zeros(lall_ref.shape, F32)

    for u in range(pps // ppb):
        blk = j * (pps // ppb) + u
        kt = jnp.concatenate([pages[u * ppb + w][0, :W_MOBA, :] for w in range(ppb)], axis=1)
        vt = jnp.concatenate([pages[u * ppb + w][0, W_MOBA:, :] for w in range(ppb)], axis=1)
        s = _dot(qbd, kt.astype(BF16))
        sc_ref[...] = jnp.where(lane == blk, jnp.sum(s, axis=1, keepdims=True), sc_ref[...])
        near = (blk == nblk - 1).astype(F32)
        s = s + near * td_ref[:, 0:MOBA_BLOCK]
        mb = jnp.max(s, axis=1, keepdims=True)
        p = jnp.exp2(s - mb)
        mall_ref[...] = jnp.where(lane == blk, mb, mall_ref[...])
        lall_ref[...] = jnp.where(lane == blk, jnp.sum(p, axis=1, keepdims=True), lall_ref[...])
        oall_ref[pl.ds(blk, 1)] = _dot_nt(p.astype(BF16), vt.astype(BF16))[None]

    @pl.when(j == pl.num_programs(1) - 1)
    def _():
        kn = new_ref[0, :, :W_MOBA].astype(BF16)
        vn = new_ref[0, :, W_MOBA:].astype(BF16)
        s_own = _dot_nt(qbd, kn) + td_ref[:, MOBA_BLOCK:MOBA_BLOCK + NEW_PAD]
        m_own = jnp.max(s_own, axis=1, keepdims=True)
        p_own = jnp.exp2(s_own - m_own)
        l_own = jnp.sum(p_own, axis=1, keepdims=True)
        o_own = _dot(p_own.astype(BF16), vn)
        sc = sc_ref[...]
        cand = lane < nblk
        sel = (_topk_mask(jnp.where(cand, sc, NEG), MOBA_TOPK) > 0.0) & cand
        mall = mall_ref[...]
        m_fin = jnp.maximum(jnp.max(jnp.where(sel, mall, NEG), axis=1, keepdims=True), m_own)
        w = jnp.where(sel, jnp.exp2(mall - m_fin), 0.0)
        w_own = jnp.exp2(m_own - m_fin)
        l = jnp.sum(w * lall_ref[...], axis=1, keepdims=True) + w_own * l_own
        acc = w_own * o_own
        for n in range(nblk):
            acc = acc + w[:, n:n + 1] * oall_ref[n]
        o = acc / jnp.maximum(l, TINY)
        rowh = lax.broadcasted_iota(jnp.int32, o.shape, 0) & (H_MOBA - 1)
        laneh = lax.broadcasted_iota(jnp.int32, o.shape, 1) >> int(math.log2(HEAD_DIM))
        o = jnp.where(rowh == laneh, o, 0.0)
        o_ref[0] = jnp.sum(o.reshape(rows // H_MOBA, H_MOBA, W_MOBA), axis=1)


def _moba_decode(cache_m, pt, qbd, new_pad, td):
    b, n_pg = pt.shape
    page = cache_m.shape[2]
    pps = min(MOBA_PAGES_PER_STEP, n_pg)
    nblk =n_pg * page // MOBA_BLOCK
    rows = qbd.shape[1]
    s_new = rows // H_MOBA
    page_spec = lambda u: pl.BlockSpec((1, 2 * W_MOBA, page), lambda bb, j, p: (p[bb, j * pps + u], 0, 0))
    grid_spec = pltpu.PrefetchScalarGridSpec(
        num_scalar_prefetch=1,
        grid=(b, n_pg // pps),
        in_specs=[pl.BlockSpec((1, rows, W_MOBA), lambda bb, j, p: (bb, 0, 0)),
                  pl.BlockSpec((1, NEW_PAD, 2 * W_MOBA), lambda bb, j, p: (bb, 0, 0)),
                  pl.BlockSpec(td.shape, lambda bb, j, p: (0, 0))] + [page_spec(u) for u in range(pps)],
        out_specs=pl.BlockSpec((1, s_new, W_MOBA), lambda bb, j, p: (bb, 0, 0)),
        scratch_shapes=[pltpu.VMEM((rows, LANE), F32), pltpu.VMEM((rows, LANE), F32), pltpu.VMEM((rows, LANE), F32),
                        pltpu.VMEM((nblk, rows, W_MOBA), F32)])
    return pl.pallas_call(
        functools.partial(_moba_dec_kernel, pps=pps, nblk=nblk),
        grid_spec=grid_spec,
        out_shape=jax.ShapeDtypeStruct((b, s_new, W_MOBA), F32),
        compiler_params=_cparams(("arbitrary", "arbitrary")),
        name="moba_decode",
    )(pt, qbd, new_pad, td, *([cache_m] * pps))


def _nsa_dec_kernel(pt_ref, qd_ref, kc_ref, vc_ref, ov_ref, kp_ref, new_ref, win_ref, neww_ref, g_ref,
                    ts_ref, tso_ref, tw_ref, two_ref, *refs, pps, past):
    del pt_ref
    pages = refs[:pps]
    o_ref, qaug_ref, m_ref, l_ref, acc_ref, ocmp_ref, selown_ref = refs[pps:]
    j = pl.program_id(1)
    nj = pl.num_programs(1)
    qd = qd_ref[0]
    rows = qd.shape[0]
    page = pages[0].shape[2]
    grp = rows // NSA_GROUP
    s_new = grp // H_NSA_KV

    @pl.when(j == 0)
    def _():
        lc = _dot_nt(qd, kc_ref[0])
        tok = lax.broadcasted_iota(jnp.int32, lc.shape, 1)
        qpos = past + (lax.broadcasted_iota(jnp.int32, (rows, 1), 0) & (s_new - 1))
        pc = _masked_softmax(lc, tok * CMP_STRIDE + (CMP_LEN - 1) <= qpos)
        ocmp_ref[...] = _dot(pc.astype(BF16), vc_ref[0])
        pcs = pc[0:grp] + pc[grp:2 * grp] + pc[2 * grp:3 * grp] + pc[3 * grp:4 * grp]
        ph, plo = _split_bf16(pcs)
        imp = _dot(ph, ov_ref[...]) + _dot(plo, ov_ref[...])
        jb = lax.broadcasted_iota(jnp.int32, imp.shape, 1)
        cur = qpos[0:grp] >> int(math.log2(SLC_BLOCK))
        avail = jb <= cur
        forced = (jb == 0) | (jb == cur) | (jb == cur - 1)
        imp = jnp.where(avail, jnp.where(forced, FORCE, imp), NEG)
        sel = (_topk_mask(imp, SLC_TOPN) > 0.0) & avail
        selb = jnp.concatenate([jnp.where(sel, 0.0, NEG)] * NSA_GROUP, axis=0)
        qaug_ref[...] = jnp.concatenate([qd, selb[:, :LANE].astype(BF16)], axis=1)
        n_own = past // SLC_BLOCK
        selown_ref[...] = jnp.broadcast_to(selb[:, n_own:n_own + 1], selown_ref.shape)
        _flash_init(m_ref, l_ref, acc_ref)

    span = pps * page
    off = pl.multiple_of(j * span, span)
    kst = jnp.concatenate([pages[u][0, :LANE, :] for u in range(pps)], axis=1).astype(BF16)
    vst = jnp.concatenate([pages[u][0, LANE:, :] for u in range(pps)], axis=1).astype(BF16)
    kaug = jnp.concatenate([kst, kp_ref[:, pl.ds(off, span)]], axis=0)
    near = (j == nj - 1).astype(F32)
    s = _dot(qaug_ref[...], kaug) + near * ts_ref[...]
    _flash_step(s, vst, m_ref, l_ref, acc_ref, feature_major=True)

    @pl.when(j == nj - 1)
    def _():
        new = new_ref[0]
        s_own = _dot_nt(qd, new[:, 2 * LANE:3 * LANE].astype(BF16)) + tso_ref[...] + selown_ref[:, 0:1]
        _flash_step(s_own, new[:, 3 * LANE:].astype(BF16), m_ref, l_ref, acc_ref)
        oslc = acc_ref[...] / jnp.maximum(l_ref[...], TINY)
        _flash_init(m_ref, l_ref, acc_ref)
        win = win_ref[0]
        _flash_step(_dot(qd, win[:LANE, :].astype(BF16)) + tw_ref[...], win[LANE:, :].astype(BF16),
                    m_ref, l_ref, acc_ref, feature_major=True)
        neww = neww_ref[0]
        _flash_step(_dot_nt(qd, neww[:, :LANE].astype(BF16)) + two_ref[...], neww[:, LANE:].astype(BF16),
                    m_ref, l_ref, acc_ref)
        owin = acc_ref[...] / jnp.maximum(l_ref[...], TINY)
        gts = g_ref[0]
        o_ref[0] = gts[:, 0:1] * ocmp_ref[...] + gts[:, 1:2] * oslc + gts[:, 2:3] * owin


def _nsa_decode(cache_n, pt, qd, kc, vc, ov, kps, new_pad, win, neww, gts, ts, tso, tw, two):
    b, n_pg = pt.shape
    page = cache_n.shape[2]
    pps = min(NSA_PAGES_PER_STEP, n_pg)
    rows = qd.shape[1]
    nc = kc.shape[1]
    full = lambda a: pl.BlockSpec(a.shape, lambda bb, j, p: (0,) * a.ndim)
    per_b = lambda a: pl.BlockSpec((1,) + a.shape[1:], lambda bb, j, p: (bb,) + (0,) * (a.ndim - 1))
    page_spec = lambda u: pl.BlockSpec((1, 2 * LANE, page), lambda bb, j, p: (p[bb, j * pps + u], 1, 0))
    grid_spec = pltpu.PrefetchScalarGridSpec(
        num_scalar_prefetch=1,
        grid=(b, n_pg // pps),
        in_specs=[per_b(qd), per_b(kc), per_b(vc), full(ov), full(kps), per_b(new_pad), per_b(win), per_b(neww),
                  per_b(gts), full(ts), full(tso), full(tw), full(two)] + [page_spec(u) for u in range(pps)],
        out_specs=pl.BlockSpec((1, rows, LANE), lambda bb, j, p: (bb, 0, 0)),
        scratch_shapes=[pltpu.VMEM((rows, 2 * LANE), BF16),
                        pltpu.VMEM((rows, 1), F32), pltpu.VMEM((rows, 1), F32), pltpu.VMEM((rows, LANE), F32),
                        pltpu.VMEM((rows, LANE), F32), pltpu.VMEM((rows, LANE), F32)])
    return pl.pallas_call(
        functools.partial(_nsa_dec_kernel, pps=pps, past=n_pg * page),
        grid_spec=grid_spec,
        out_shape=jax.ShapeDtypeStruct((b, rows, LANE), F32),
        compiler_params=_cparams(("arbitrary", "arbitrary")),
        name="nsa_decode",
    )(pt, qd, kc, vc, ov, kps, new_pad, win, neww, gts, ts, tso, tw, two, *([cache_n] * pps))


def _out_kernel(x_ref, om_ref, zm_ref, on_ref, zn_ref, gate_ref, w_ref, fg_ref, y_ref, *, feature_major):
    zm = zm_ref[...].astype(F32)
    zn = zn_ref[...].astype(F32)
    if feature_major:
        om = om_ref[0].astype(F32).T
        on = on_ref[0].astype(F32).T
    else:
        om = om_ref[...].astype(F32)
        on = on_ref[...].astype(F32)
    mm = (om * (zm * _sigmoid(zm))).astype(BF16)
    mn = (on * (zn * _sigmoid(zn))).astype(BF16)
    mixed = _dot(mm, w_ref[:W_MOBA, :]) + _dot(mn, w_ref[W_MOBA:, :])
    xn = x_ref[...] + gate_ref[0] * mixed
    inv = lax.rsqrt(jnp.mean(xn * xn, axis=-1, keepdims=True) + RMS_EPS)
    y_ref[...] = (xn * inv) * fg_ref[...]


def _out_proj(x2d, om, zm, on, zn, gate, w_out, fgain, tm, tiles_per_mod):
    r, d = x2d.shape
    mrows = gate.shape[1]
    feature_major = om.ndim == 3
    row = lambda width: pl.BlockSpec((tm, width), lambda i: (i, 0))
    if feature_major:
        mixer = pl.BlockSpec((1, om.shape[1], tm), lambda i: (i // tiles_per_mod, 0, i % tiles_per_mod))
    else:
        mixer = row(W_MOBA)
    return pl.pallas_call(
        functools.partial(_out_kernel, feature_major=feature_major),
        grid=(r // tm,),
        in_specs=[row(d), mixer, row(W_MOBA), mixer, row(W_NSA),
                  pl.BlockSpec((1, mrows, d), lambda i: (i // tiles_per_mod, 0, 0)),
                  pl.BlockSpec((d, d), lambda i: (0, 0)),
                  pl.BlockSpec((1, d), lambda i: (0, 0))],
        out_specs=row(d),
        out_shape=jax.ShapeDtypeStruct((r, d), F32),
        compiler_params=_cparams(("arbitrary",)),
        name="out_proj",
    )(x2d, om, zm, on, zn, gate, w_out, fgain)


def _t5_bucket(rel):
    n = np.maximum(rel, 0)
    exact = N_BUCKETS // 2
    x = np.log(np.maximum(n, 1) / exact) / math.log(MAX_DISTANCE / exact) * (N_BUCKETS - exact)
    near_boundary = (np.abs(x - np.round(x)) < 1e-3) & (n > exact) & (n < MAX_DISTANCE)
    assert not near_boundary.any()
    large = exact + np.floor(x + 1e-6).astype(np.int64)
    return np.where(n < exact, n, np.minimum(large, N_BUCKETS - 1)).astype(np.int32)


def _bias_of_rel(tab, rel, shift_far):
    onehot = (jnp.asarray(_t5_bucket(rel))[..., None] == jnp.arange(N_BUCKETS, dtype=jnp.int32)).astype(F32)
    val = jnp.moveaxis(jnp.dot(onehot, tab, precision=lax.Precision.HIGHEST), -1, 0)
    if shift_far:
        val = val - tab[N_BUCKETS - 1].reshape((-1,) + (1,) * rel.ndim)
    return jnp.where(jnp.asarray(rel >= 0)[None], val * LOG2E, NEG)


def _layout_w_in(w_in):
    d = w_in.shape[0]
    sc = HEAD_DIM ** -0.5 * LOG2E
    o = 4 * W_MOBA
    q_m, rest_m = w_in[:, :W_MOBA] * sc, w_in[:, W_MOBA:o]
    q_n = (w_in[:, o:o + W_NSA] * sc).reshape(d, H_NSA_KV, NSA_GROUP, HEAD_DIM)
    qn_exp = jnp.concatenate([jnp.pad(q_n[:, 0], ((0, 0), (0, 0), (0, HEAD_DIM))),
                              jnp.pad(q_n[:, 1], ((0, 0), (0, 0), (HEAD_DIM, 0)))], axis=1).reshape(d, H_NSA * LANE)
    o += W_NSA
    kv_n = w_in[:, o:o + 6 * W_NSA_KV]
    o += 6 * W_NSA_KV
    g_n = jnp.swapaxes(w_in[:, o:o + 3 * H_NSA].reshape(d, H_NSA_KV, NSA_GROUP, 3), 2, 3)
    g_n = jnp.pad(g_n.reshape(d, H_NSA_KV, 3 * NSA_GROUP), ((0, 0), (0, 0), (0, LANE - 3 * NSA_GROUP)))
    z_n = w_in[:, o + 3 * H_NSA:]
    w = jnp.concatenate([q_m, rest_m, qn_exp, kv_n, z_n, g_n.reshape(d, H_NSA_KV * LANE)], axis=1).astype(BF16)
    assert w.shape[1] == C_END
    wt = jnp.concatenate([w[:, C_QM:C_ZM], w[:, C_QN:C_ZN], w[:, C_G:C_END]], axis=1).T
    assert wt.shape[0] == R_END
    return w, wt


def _layout_cmp(pe, w1, w2):
    w = w1.reshape(2, CMP_STRIDE, HEAD_DIM, CMP_HIDDEN)
    z = jnp.zeros_like(w)
    full = jnp.stack([jnp.concatenate([w, z], axis=-1), jnp.concatenate([z, w], axis=-1)], axis=2)
    full = full.reshape(2, CMP_STRIDE * H_NSA_KV * HEAD_DIM, H_NSA_KV * CMP_HIDDEN).astype(BF16)
    zz = jnp.zeros_like(w2)
    w2bd = jnp.concatenate([jnp.concatenate([w2, zz], axis=1), jnp.concatenate([zz, w2], axis=1)], axis=0).astype(BF16)
    pe2 = jnp.broadcast_to(pe.reshape(2, CMP_STRIDE, 1, HEAD_DIM), (2, CMP_STRIDE, H_NSA_KV, HEAD_DIM))
    return pe2.reshape(2, 1, CMP_STRIDE * LANE), full[0], full[1], w2bd


def _block_onehot(t, block):
    return (jnp.arange(t)[:, None] // block == jnp.arange(LANE)[None, :]).astype(BF16)


def _overlap(nc, n_cmp, lanes):
    i = jnp.arange(nc)[:, None]
    s = jnp.arange(lanes)[None, :]
    start = i * CMP_STRIDE
    hit = (start < s * SLC_BLOCK + SLC_BLOCK) & (start + CMP_LEN - 1 >= s * SLC_BLOCK) & (i < n_cmp)
    return hit.astype(BF16)


def kernel(x_prompt, x_sample, c_prompt, c_sample, cache_moba_kv, cache_nsa_kv, state_nsa_win, page_table, w_ada, b_ada, norm_gain, w_in, cmp_pe, cmp_k_w1, cmp_k_w2, cmp_v_w1, cmp_v_w2, w_out, rel_bias, final_gain):
    depth = w_in.shape[0]
    assert depth == 1
    B, T, D = x_prompt.shape
    BS, S, _ = x_sample.shape
    n_pg = page_table.shape[1]
    page = cache_moba_kv.shape[2]
    P = n_pg * page
    WB = state_nsa_win.shape[2]
    assert D == (H_MOBA + H_NSA) * HEAD_DIM and T % (SWEEP_TILES * KEY_TILE) == 0 and P % MOBA_BLOCK == 0 and S <= NEW_PAD
    assert T // SLC_BLOCK <= LANE and P // SLC_BLOCK <= LANE and T >= WINDOW and WB == WINDOW
    assert (BS * S) % SUBLANE == 0 and S & (S - 1) == 0

    w, wt = _layout_w_in(w_in[0])
    w_out_b = w_out[0].astype(BF16)
    gain = norm_gain[0].reshape(1, D)
    fgain = final_gain.reshape(1, D)
    kcmp_w = _layout_cmp(cmp_pe[0, 0], cmp_k_w1[0], cmp_k_w2[0])
    vcmp_w = _layout_cmp(cmp_pe[0, 1], cmp_v_w1[0], cmp_v_w2[0])
    pe2 = jnp.stack([kcmp_w[0], vcmp_w[0]], axis=0)
    cmp_consts = (pe2,) + kcmp_w[1:] + vcmp_w[1:]
    bias_m = rel_bias[:, :H_MOBA]
    bias_n = rel_bias[:, H_MOBA:]

    m_all = B + BS
    m_pad = -(-m_all // SUBLANE) * SUBLANE
    c_all = jnp.pad(jnp.concatenate([c_prompt, c_sample], axis=0), ((0, m_pad - m_all), (0, 0)))
    mod = _ada(c_all, w_ada[0], b_ada[0])
    shift, scale, gate = mod[:, :D], mod[:, D:2 * D], mod[:, 2 * D:]

    a = np.arange(KEY_TILE)[None, :]
    jk = np.arange(2 * KEY_TILE)[:, None]
    rel_s = a + KEY_TILE - jk
    heads_major = lambda tb, n_grp: jnp.transpose(
        tb.reshape(n_grp, -1, tb.shape[1], tb.shape[2]), (0, 2, 1, 3)).reshape(n_grp, tb.shape[1], -1)
    rows_of = lambda tb, n, val: jnp.full((tb.shape[0], n, tb.shape[2]), val, F32)
    pad_rows = (SWEEP_TILES - 1) * KEY_TILE
    sweep_rows = lambda tb: jnp.concatenate([rows_of(tb, pad_rows, 0.0), tb, rows_of(tb, pad_rows, NEG)], axis=1)
    rel_m = MOBA_BLOCK + np.arange(MOBA_TQ)[None, :] - np.arange(MOBA_BLOCK + MOBA_TQ)[:, None]
    tab_m = heads_major(_bias_of_rel(bias_m, rel_m, True), H_MOBA // 2)
    tab_m = jnp.concatenate([rows_of(tab_m, MOBA_BLOCK, 0.0), tab_m], axis=1)
    tab_s = sweep_rows(heads_major(_bias_of_rel(bias_n, rel_s, True), H_NSA_KV))
    jw = np.arange(WINDOW + KEY_TILE)[:, None]
    rel_w = a + WINDOW - jw
    tab_w = heads_major(_bias_of_rel(bias_n, np.where(rel_w < WINDOW, rel_w, -1), False), H_NSA_KV)
    tab_w = jnp.concatenate([tab_w, rows_of(tab_w, WINDOW, NEG)], axis=1)

    tpm = T // PROJ_TM
    (qmt, mkvt, vmt, qnt, nkvt, vst, wkvt, vwt, gt, km, ksw, zm, zn, kmean) = _proj_fm(
        x_prompt.reshape(B * T, D), scale[:B].reshape(B, 1, D), shift[:B].reshape(B, 1, D), gain, w, wt, B, PROJ_TM)
    nb = T // MOBA_BLOCK
    kmean_pad = jnp.pad(kmean.reshape(B, nb, W_MOBA), ((0, 0), (0, LANE - nb), (0, 0)))
    o_m = _moba_prompt(qmt, km.reshape(B, T, W_MOBA), vmt, kmean_pad, _block_onehot(T, MOBA_BLOCK), tab_m)
    pt_prompt = jnp.zeros((B, T // page), jnp.int32)
    kc_p, vc_p = _compress(nkvt, pt_prompt, page, *cmp_consts, paged=False)
    nc_p = T // CMP_STRIDE
    o_n = _nsa_prompt(qnt, kc_p, jnp.swapaxes(vc_p, 1, 2), _overlap(nc_p, nc_p - 1, LANE).T,
                      ksw.reshape(B, T, 2 * LANE), vst, vwt, _block_onehot(T, SLC_BLOCK), tab_s, tab_w, gt)
    y_prompt = _out_proj(x_prompt.reshape(B * T, D), o_m, zm, o_n, zn,
                         gate[:B].reshape(B, 1, D), w_out_b, fgain, PROJ_TM, tpm).reshape(B, T, D)
    tokens_first = lambda m, c: jnp.transpose(m.reshape(1, B, c, -1, HEAD_DIM, m.shape[-1]), (0, 1, 5, 2, 3, 4))
    moba_kv_prompt = tokens_first(mkvt, 2)
    nsa_kv_prompt = tokens_first(nkvt, 4)
    win_prompt = tokens_first(wkvt[:, :, T - WINDOW:], 2)

    RS = BS * S
    rep = lambda m: jnp.repeat(m[B:B + BS], S, axis=0)
    (qm_s, mkv32_s, zm_s, qn_s, nkv32_s, wkv32_s, zn_s, gts_s) = _proj(
        x_sample.reshape(RS, D), rep(scale), rep(shift), gain, w)
    pad_new = lambda m: jnp.pad(m.reshape(BS, S, -1), ((0, 0), (0, NEW_PAD - S), (0, 0)))
    s_idx = np.arange(S)

    q_rep = jnp.repeat(qm_s.reshape(BS, S, 1, H_MOBA, HEAD_DIM), H_MOBA, axis=2)
    eye = (jnp.arange(H_MOBA)[:, None] == jnp.arange(H_MOBA)[None, :])[None, None, :, :, None]
    qbd = jnp.where(eye, q_rep, 0).reshape(BS, S * H_MOBA, W_MOBA)
    rel_d = np.concatenate([MOBA_BLOCK + s_idx[:, None] - np.arange(MOBA_BLOCK)[None, :],
                            s_idx[:, None] - np.arange(NEW_PAD)[None, :]], axis=1)
    td = jnp.moveaxis(_bias_of_rel(bias_m, rel_d, True), 0, 1).reshape(S * H_MOBA, -1)
    td = jnp.pad(td, ((0, 0), (0, 3 * LANE - td.shape[1])))
    feat_pages = lambda c: jnp.transpose(c, (0, 2, 3, 4, 1)).reshape(c.shape[0], -1, c.shape[1])
    o_m_s = _moba_decode(feat_pages(cache_moba_kv[0]), page_table, qbd, pad_new(mkv32_s), td)

    cache_n = feat_pages(cache_nsa_kv[0])
    kc_s, vc_s = _compress(cache_n, page_table, page, *cmp_consts, paged=True)
    nc_s = P // CMP_STRIDE
    n_cmp_s = (P + S - CMP_LEN) // CMP_STRIDE + 1
    order = lambda m: jnp.transpose(m, (0, 3, 2, 1) + tuple(range(4, m.ndim)))
    rows_n = NSA_GROUP * H_NSA_KV * S
    qd = order(qn_s.reshape(BS, S, H_NSA_KV, NSA_GROUP, LANE)).reshape(BS, rows_n, LANE)
    g3 = gts_s.reshape(BS, S, H_NSA_KV, LANE)[..., :3 * NSA_GROUP].reshape(BS, S, H_NSA_KV, 3, NSA_GROUP)
    gts_d = order(jnp.swapaxes(g3, 3, 4)).reshape(BS, rows_n, 3)
    bias_d = jnp.swapaxes(bias_n.reshape(N_BUCKETS, H_NSA_KV, NSA_GROUP), 1, 2).reshape(N_BUCKETS, H_NSA)

    def dec_table(rel, shift_far):
        return _bias_of_rel(bias_d, rel, shift_far).reshape(rows_n, rel.shape[1])

    ts = dec_table(page + s_idx[:, None] - np.arange(page)[None, :], True)
    ts = jnp.pad(ts, ((0, 0), (min(NSA_PAGES_PER_STEP, n_pg) * page - page, 0)))
    rel_own = s_idx[:, None] - np.arange(NEW_PAD)[None, :]
    tso = dec_table(rel_own, True)
    rel_win = WB + s_idx[:, None] - np.arange(WB)[None, :]
    tw = dec_table(np.where(rel_win < WINDOW, rel_win, -1), False)
    two = dec_table(rel_own, False)
    win_state = feat_pages(state_nsa_win[0])
    o_n_raw = _nsa_decode(cache_n, page_table, qd, kc_s, vc_s, _overlap(nc_s, n_cmp_s, 2 * LANE),
                          _block_onehot(P, SLC_BLOCK).T, pad_new(nkv32_s), win_state, pad_new(wkv32_s), gts_d,
                          ts, tso, tw, two)
    o5 = o_n_raw.reshape(BS, NSA_GROUP, H_NSA_KV, S, H_NSA_KV, HEAD_DIM)
    o_n_s = jnp.stack([o5[:, :, k, :, k] for k in range(H_NSA_KV)], axis=1)
    o_n_s = jnp.transpose(o_n_s, (0, 3, 1, 2, 4)).reshape(RS, W_NSA)
    y_sample = _out_proj(x_sample.reshape(RS, D), o_m_s.reshape(RS, W_MOBA).astype(BF16), zm_s,
                         o_n_s.astype(BF16), zn_s, rep(gate).reshape(1, RS, D), w_out_b, fgain, RS, 1).reshape(BS, S, D)
    moba_kv_sample = mkv32_s.reshape(1, BS, S, 2, H_MOBA, HEAD_DIM)
    nsa_kv_sample = nkv32_s.reshape(1, BS, S, 4, H_NSA_KV, HEAD_DIM)
    win_sample = jnp.concatenate([win_state[:, :, S:], jnp.swapaxes(wkv32_s.reshape(BS, S, 2 * LANE), 1, 2)], axis=2)
    win_sample = jnp.transpose(win_sample.reshape(1, BS, 2, H_NSA_KV, HEAD_DIM, WB), (0, 1, 5, 2, 3, 4))
    return (y_prompt, y_sample, moba_kv_prompt, moba_kv_sample, nsa_kv_prompt, nsa_kv_sample, win_prompt, win_sample)
```

```python
import functools
import math

import jax
import jax.numpy as jnp
import numpy as np
from jax import lax
from jax.experimental import pallas as pl
from jax.experimental.pallas import tpu as pltpu

F32 = jnp.float32
BF16 = jnp.bfloat16

HEAD_DIM = 64
H_MOBA = 8
H_NSA = 8
H_NSA_KV = 2
NSA_GROUP = 4
W_MOBA = H_MOBA * HEAD_DIM
W_NSA = H_NSA * HEAD_DIM
W_NSA_KV = H_NSA_KV * HEAD_DIM
MOBA_BLOCK = 256
MOBA_TOPK = 3
CMP_LEN = 32
CMP_STRIDE = 16
CMP_HIDDEN = 2 * HEAD_DIM
SLC_BLOCK = 64
SLC_TOPN = 16
WINDOW = 512
N_BUCKETS = 32
MAX_DISTANCE = 128
RMS_EPS = 1e-6
NEG = -1e30
FORCE = 1e9
TINY = 1e-30

LANE = 128
SUBLANE = 8
KEY_TILE = 256
SWEEP_TILES = 2
NSA_TQ = 256
MOBA_TQ = 2 * MOBA_BLOCK
PROJ_TM = 256
MOBA_PAGES_PER_STEP = 32
NSA_PAGES_PER_STEP = 64
NEW_PAD = 8
ONES_ROWS = 16
LOG2E = math.log2(math.e)
VMEM_LIMIT = 56 * 1024 * 1024

C_QM, C_MKV, C_ZM, C_QN, C_NKV, C_WKV, C_ZN, C_G, C_END = 0, 512, 1536, 2048, 3072, 3584, 3840, 4352, 4608

_NT = (((1,), (1,)), ((), ()))


def _dot(a, b):
    return jnp.dot(a, b, preferred_element_type=F32)


def _dot_nt(a, b):
    return lax.dot_general(a, b, _NT, preferred_element_type=F32)


def _split_bf16(a):
    hi = a.astype(BF16)
    lo = (a - hi.astype(F32)).astype(BF16)
    return hi, lo


def _sigmoid(x):
    return 1.0 / (1.0 + jnp.exp(-x))


def _cparams(sem):
    return pltpu.CompilerParams(dimension_semantics=sem, vmem_limit_bytes=VMEM_LIMIT)


def _ada_kernel(c_ref, w_ref, b_ref, o_ref):
    ch, cl = _split_bf16(c_ref[...])
    wh, wl = _split_bf16(w_ref[...])
    o_ref[...] = _dot(ch, wh) + _dot(ch, wl) + _dot(cl, wh) + b_ref[...]


def _ada(c_all, w_ada, b_ada):
    m, d = c_all.shape
    n = w_ada.shape[1]
    tn = 512
    return pl.pallas_call(
        _ada_kernel,
        grid=(n // tn,),
        in_specs=[pl.BlockSpec((m, d), lambda j: (0, 0)),
                  pl.BlockSpec((d, tn), lambda j: (0, j)),
                  pl.BlockSpec((1, tn), lambda j: (0, j))],
        out_specs=pl.BlockSpec((m, tn), lambda j: (0, j)),
        out_shape=jax.ShapeDtypeStruct((m, n), F32),
        compiler_params=_cparams(("arbitrary",)),
        name="ada",
    )(c_all, w_ada, b_ada.reshape(1, n))


def _modulated_norm(x, gain, scale, shift):
    inv = lax.rsqrt(jnp.mean(x * x, axis=-1, keepdims=True) + RMS_EPS)
    return (x * inv) * gain * (1.0 + scale) + shift


def _proj_kernel(x_ref, sc_ref, sh_ref, gain_ref, w_ref, qm_ref, mkv_ref, zm_ref, qn_ref, nkv_ref, wkv_ref, zn_ref, g_ref):
    hb = _modulated_norm(x_ref[...], gain_ref[...], sc_ref[...], sh_ref[...]).astype(BF16)
    col = lambda a, b: _dot(hb, w_ref[:, a:b])
    qm_ref[...] = col(C_QM, C_MKV).astype(BF16)
    mkv_ref[...] = col(C_MKV, C_ZM)
    zm_ref[...] = col(C_ZM, C_QN).astype(BF16)
    qn_ref[...] = col(C_QN, C_NKV).astype(BF16)
    nkv_ref[...] = col(C_NKV, C_WKV)
    wkv_ref[...] = col(C_WKV, C_ZN)
    zn_ref[...] = col(C_ZN, C_G).astype(BF16)
    g_ref[...] = _sigmoid(col(C_G, C_END))


def _proj(x2d, sc, sh, gain, w):
    r, d = x2d.shape
    full = lambda width: pl.BlockSpec((r, width), lambda i: (0, 0))
    widths = [(512, BF16), (1024, F32), (512, BF16), (1024, BF16), (512, F32), (256, F32), (512, BF16), (256, F32)]
    return pl.pallas_call(
        _proj_kernel,
        grid=(1,),
        in_specs=[full(d), full(d), full(d), pl.BlockSpec((1, d), lambda i: (0, 0)),
                  pl.BlockSpec((d, C_END), lambda i: (0, 0))],
        out_specs=[full(wd) for wd, _ in widths],
        out_shape=[jax.ShapeDtypeStruct((r, wd), dt) for wd, dt in widths],
        compiler_params=_cparams(("arbitrary",)),
        name="proj_decode",
    )(x2d, sc, sh, gain, w)


R_QM, R_MKV, R_QN, R_NKV, R_WKV, R_G, R_END = 0, 512, 1536, 2560, 3072, 3328, 3584


def _proj_fm_kernel(x_ref, sc_ref, sh_ref, gain_ref, w_ref, wt_ref,
                    qmt_ref, mkvt_ref, vmt_ref, qnt_ref, nkvt_ref, vst_ref, wkvt_ref, vwt_ref, gt_ref,
                    km_ref, ksw_ref, zm_ref, zn_ref, kmean_ref):
    h = _modulated_norm(x_ref[...], gain_ref[...], sc_ref[0], sh_ref[0])
    hb = h.astype(BF16)
    ht = h.T.astype(BF16)
    frow = lambda a, b: _dot(wt_ref[a:b, :], ht)
    col = lambda a, b: _dot(hb, w_ref[:, a:b])
    qmt_ref[0] = frow(R_QM, R_MKV).astype(BF16)
    mkvt = frow(R_MKV, R_QN)
    mkvt_ref[0] = mkvt
    vmt_ref[0] = mkvt[W_MOBA:].astype(BF16)
    qnt_ref[0] = frow(R_QN, R_NKV).astype(BF16)
    nkvt = frow(R_NKV, R_WKV)
    nkvt_ref[0] = nkvt
    vst_ref[0] = nkvt[3 * W_NSA_KV:].astype(BF16)
    wkvt = frow(R_WKV, R_G)
    wkvt_ref[0] = wkvt
    vwt_ref[0] = wkvt[W_NSA_KV:].astype(BF16)
    gt_ref[0] = _sigmoid(frow(R_G, R_END))
    km = col(C_MKV, C_MKV + W_MOBA)
    km_ref[...] = km.astype(BF16)
    kmean_ref[0] = jnp.mean(km, axis=0, keepdims=True)
    ksw_ref[...] = jnp.concatenate([col(C_NKV + 2 * W_NSA_KV, C_NKV + 3 * W_NSA_KV), col(C_WKV, C_WKV + W_NSA_KV)],
                                   axis=1).astype(BF16)
    zm_ref[...] = col(C_ZM, C_QN).astype(BF16)
    zn_ref[...] = col(C_ZN, C_G).astype(BF16)


def _proj_fm(x2d, sc, sh, gain, w, wt, b, tm):
    r, d = x2d.shape
    t = r // b
    tpb = t // tm
    nt = r // tm
    row = lambda width: pl.BlockSpec((tm, width), lambda i: (i, 0))
    fm = lambda rows: pl.BlockSpec((1, rows, tm), lambda i: (i // tpb, 0, i % tpb))
    mod = pl.BlockSpec((1, 1, d), lambda i: (i // tpb, 0, 0))
    const = lambda a: pl.BlockSpec(a.shape, lambda i: (0, 0), pipeline_mode=pl.Buffered(1))
    fm_outs = [(512, BF16), (1024, F32), (512, BF16), (1024, BF16), (512, F32), (128, BF16), (256, F32), (128, BF16),
               (256, F32)]
    tm_outs = [(512, BF16), (256, BF16), (512, BF16), (512, BF16)]
    return pl.pallas_call(
        _proj_fm_kernel,
        grid=(nt,),
        in_specs=[row(d), mod, mod, pl.BlockSpec((1, d), lambda i: (0, 0)), const(w), const(wt)],
        out_specs=[fm(rows) for rows, _ in fm_outs] + [row(wd) for wd, _ in tm_outs]
        + [pl.BlockSpec((1, 1, W_MOBA), lambda i: (i, 0, 0))],
        out_shape=[jax.ShapeDtypeStruct((b, rows, t), dt) for rows, dt in fm_outs]
        + [jax.ShapeDtypeStruct((r, wd), dt) for wd, dt in tm_outs] + [jax.ShapeDtypeStruct((nt, 1, W_MOBA), F32)],
        compiler_params=_cparams(("arbitrary",)),
        name="proj_prompt",
    )(x2d, sc, sh, gain, w, wt)


def _flash_step(s, v, m_ref, l_ref, acc_ref, feature_major=False):
    m_old = m_ref[...]
    m_new = jnp.maximum(m_old, jnp.max(s, axis=1, keepdims=True))
    alpha = jnp.exp2(m_old - m_new)
    p = jnp.exp2(s - m_new)
    l_ref[...] = alpha * l_ref[...] + jnp.sum(p, axis=1, keepdims=True)
    pv = _dot_nt(p.astype(BF16), v) if feature_major else _dot(p.astype(BF16), v)
    acc_ref[...] = alpha * acc_ref[...] + pv
    m_ref[...] = m_new


def _flash_init(m_ref, l_ref, acc_ref):
    m_ref[...] = jnp.full(m_ref.shape, NEG, F32)
    l_ref[...] = jnp.zeros(l_ref.shape, F32)
    acc_ref[...] = jnp.zeros(acc_ref.shape, F32)


def _flash_step_t(s_ref, n_keys, vt, m_ref, acc_ref):
    s_max = jnp.max(s_ref[0:n_keys, :], axis=0, keepdims=True)
    alpha, p = _softmax_weights_t(s_ref[0:n_keys, :], s_max, m_ref)
    acc_ref[...] = alpha * acc_ref[...] + _weighted_values_t(vt, p)


def _softmax_weights_t(s, s_max, m_ref):
    m_old = m_ref[...]
    m_new = jnp.maximum(m_old, s_max)
    m_ref[...] = m_new
    return jnp.exp2(m_old - m_new), jnp.exp2(s - m_new).astype(BF16)


def _weighted_values_t(vt, p):
    vt_aug = jnp.concatenate([vt, jnp.ones((ONES_ROWS, vt.shape[1]), BF16)], axis=0)
    return _dot(vt_aug, p)


def _flash_init_t(m_ref, acc_ref):
    m_ref[...] = jnp.full(m_ref.shape, NEG, F32)
    acc_ref[...] = jnp.zeros(acc_ref.shape, F32)


def _flash_result_t(acc_ref):
    width = acc_ref.shape[0] - ONES_ROWS
    return acc_ref[0:width, :] / jnp.maximum(acc_ref[width:width + 1, :], TINY)


def _sweep_tiles(logits_at, values_at, table_at, i_diag, m_ref, acc_ref, s_ref):
    g = SWEEP_TILES
    big = g * KEY_TILE
    n_far = jnp.maximum(i_diag - 1, 0)
    n_big = n_far // g
    rem = n_far - n_big * g

    def far_body(n, s_max):
        alpha, p = _softmax_weights_t(s_ref[0:big, :], s_max, m_ref)
        acc_ref[...] = alpha * acc_ref[...] + _weighted_values_t(values_at(pl.multiple_of(n * big, big), big), p)
        nxt = logits_at(pl.multiple_of((n + 1) * big, big), big)
        s_ref[0:big, :] = nxt
        return jnp.max(nxt, axis=0, keepdims=True)

    s0 = logits_at(0, big)
    s_ref[0:big, :] = s0
    lax.fori_loop(0, n_big, far_body, jnp.max(s0, axis=0, keepdims=True))
    first_block = jnp.where(i_diag == 0, g, g - 1 - rem)
    s_ref[0:big, :] = s_ref[0:big, :] + table_at(pl.multiple_of(first_block * KEY_TILE, KEY_TILE), big)
    _flash_step_t(s_ref, big, values_at(pl.multiple_of(n_big * big, big), big), m_ref, acc_ref)

    @pl.when((i_diag >= 1) & (rem == g - 1))
    def _():
        off = pl.multiple_of(i_diag * KEY_TILE, KEY_TILE)
        s_ref[0:KEY_TILE, :] = logits_at(off, KEY_TILE) + table_at(g * KEY_TILE, KEY_TILE)
        _flash_step_t(s_ref, KEY_TILE, values_at(off, KEY_TILE), m_ref, acc_ref)


def _topk_mask(vals, k, axis=1):
    idxf = lax.broadcasted_iota(jnp.int32, vals.shape, axis).astype(F32)

    def body(_, taken):
        cur = jnp.where(taken > 0.0, -jnp.inf, vals)
        mx = jnp.max(cur, axis=axis, keepdims=True)
        first = jnp.min(jnp.where(cur == mx, idxf, 1e9), axis=axis, keepdims=True)
        return jnp.where(idxf == first, 1.0, taken)

    return lax.fori_loop(0, k, body, jnp.zeros(vals.shape, F32))


def _masked_softmax(logits, valid, axis=1):
    lm = jnp.where(valid, logits, NEG)
    p = jnp.exp2(lm - jnp.max(lm, axis=axis, keepdims=True)) * valid.astype(F32)
    return p / jnp.maximum(jnp.sum(p, axis=axis, keepdims=True), TINY)


def _moba_kernel(qt_ref, k_ref, vt_ref, km_ref, kp_ref, tab_ref, o_ref, qaug_ref, m_ref, acc_ref, s_ref, *, n_blocks):
    i = pl.program_id(2)
    tq = qt_ref.shape[2]
    pair = 2 * MOBA_BLOCK
    qt = qt_ref[0]
    row = lax.broadcasted_iota(jnp.int32, qt.shape, 0)
    zero = jnp.zeros_like(qt)
    q2 = jnp.concatenate([jnp.where(row < HEAD_DIM, qt, zero), jnp.where(row >= HEAD_DIM, qt, zero)], axis=1)
    nbp = -(-n_blocks // ONES_ROWS) * ONES_ROWS
    sc = _dot(km_ref[0, :nbp, :].astype(BF16), q2)
    blk = lax.broadcasted_iota(jnp.int32, sc.shape, 0)
    col = lax.broadcasted_iota(jnp.int32, (1, sc.shape[1]), 1)
    cur = 2 * i + ((col & (tq - 1)) >> int(math.log2(MOBA_BLOCK)))
    cand = blk < cur
    taken = _topk_mask(jnp.where(cand, sc, NEG), MOBA_TOPK, axis=0)
    sel = ((taken > 0.0) & cand) | (blk == cur)
    selb = jnp.concatenate([jnp.where(sel, 0.0, NEG), jnp.full((LANE - nbp, sc.shape[1]), NEG, F32)], axis=0)
    qaug_ref[...] = jnp.concatenate([q2, selb.astype(BF16)], axis=0)
    _flash_init_t(m_ref, acc_ref)

    def logits_of(p):
        off = pl.multiple_of(p * pair, pair)
        kaug = jnp.concatenate([k_ref[0, pl.ds(off, pair), :], kp_ref[pl.ds(off, pair), :]], axis=1)
        return _dot(kaug, qaug_ref[...])

    values_of = lambda p: vt_ref[0, :, pl.ds(pl.multiple_of(p * pair, pair), pair)]

    def far_body(p, s_max):
        alpha, w = _softmax_weights_t(s_ref[...], s_max, m_ref)
        acc_ref[...] = alpha * acc_ref[...] + _weighted_values_t(values_of(p), w)
        nxt = logits_of(p + 1)
        s_ref[...] = nxt
        return jnp.max(nxt, axis=0, keepdims=True)

    s0 = logits_of(0)
    s_ref[...] = s0
    lax.fori_loop(0, jnp.maximum(i - 1, 0), far_body, jnp.max(s0, axis=0, keepdims=True))
    toff = pl.multiple_of(jnp.where(i == 0, pair, 0), pair)
    s_ref[...] = s_ref[...] + tab_ref[0, pl.ds(toff, pair), :]
    _flash_step_t(s_ref, pair, values_of(jnp.maximum(i - 1, 0)), m_ref, acc_ref)

    p_own = jnp.maximum(i, 1)
    t_own = pl.multiple_of(jnp.where(i == 0, 2 * pair, pair), pair)
    s_ref[...] = logits_of(p_own) + tab_ref[0, pl.ds(t_own, pair), :]
    _flash_step_t(s_ref, pair, values_of(p_own), m_ref, acc_ref)

    o = _flash_result_t(acc_ref)
    o_ref[0] = jnp.where(row < HEAD_DIM, o[:, :tq], o[:, tq:]).astype(BF16)


def _moba_prompt(qt, mkv16, vt, kmean_pad, kp, tab):
    b, _, t = qt.shape
    tq = MOBA_TQ
    nq = t // tq
    hp = H_MOBA // 2
    cols = 2 * tq
    return pl.pallas_call(
        functools.partial(_moba_kernel, n_blocks=t // MOBA_BLOCK),
        grid=(b, hp, nq),
        in_specs=[pl.BlockSpec((1, LANE, tq), lambda bb, h, i: (bb, h, i)),
                  pl.BlockSpec((1, t, LANE), lambda bb, h, i: (bb, 0, h)),
                  pl.BlockSpec((1, LANE, t), lambda bb, h, i: (bb, h, 0)),
                  pl.BlockSpec((1, LANE, LANE), lambda bb, h, i: (bb, 0, h)),
                  pl.BlockSpec((t, LANE), lambda bb, h, i: (0, 0), pipeline_mode=pl.Buffered(1)),
                  pl.BlockSpec((1,) + tab.shape[1:], lambda bb, h, i: (h, 0, 0), pipeline_mode=pl.Buffered(1))],
        out_specs=pl.BlockSpec((1, LANE, tq), lambda bb, h, i: (bb, h, i)),
        out_shape=jax.ShapeDtypeStruct((b, W_MOBA, t), BF16),
        scratch_shapes=[pltpu.VMEM((2 * LANE, cols), BF16),
                        pltpu.VMEM((1, cols), F32), pltpu.VMEM((LANE + ONES_ROWS, cols), F32),
                        pltpu.VMEM((2 * MOBA_BLOCK, cols), F32)],
        compiler_params=_cparams(("arbitrary", "arbitrary", "arbitrary")),
        name="moba_prompt",
    )(qt, mkv16, vt, kmean_pad, kp, tab)


def _cmp_kernel(pt_ref, *refs, pps):
    del pt_ref
    pages = refs[:pps]
    (pe_ref, kwa_ref, kwb_ref, kw2_ref, vwa_ref, vwb_ref, vw2_ref, kc_ref, vc_ref,
     uk_ref, uv_ref, sk_ref, sv_ref) = refs[pps:]
    j = pl.program_id(1)
    page = pages[0].shape[2]
    groups = pps * page // CMP_STRIDE
    for u in range(pps):
        xt = pages[u][0]
        sk_ref[u * page:(u + 1) * page, :] = xt[:LANE, :].T
        sv_ref[u * page:(u + 1) * page, :] = xt[LANE:, :].T
    row0 = pl.multiple_of(j * groups, groups)
    for l in range(CMP_STRIDE):
        rows_l = pl.ds(l, groups, stride=CMP_STRIDE)
        uk_ref[pl.ds(row0, groups), l * LANE:(l + 1) * LANE] = sk_ref[rows_l, :]
        uv_ref[pl.ds(row0, groups), l * LANE:(l + 1) * LANE] = sv_ref[rows_l, :]

    @pl.when(j == pl.num_programs(1) - 1)
    def _():
        nc = uk_ref.shape[0]
        last = lax.broadcasted_iota(jnp.int32, (nc, LANE), 0) == nc - 1

        def compress(u_ref, pe, wa_ref, wb_ref, w2_ref, out_ref):
            u = u_ref[...]
            pa = _dot((u + pe[0]).astype(BF16), wa_ref[...])
            pb = _dot((u + pe[1]).astype(BF16), wb_ref[...])
            pre = pa + pltpu.roll(pb, nc - 1, 0)
            hid = pre * _sigmoid(pre)
            out = _dot(hid.astype(BF16), w2_ref[...])
            out_ref[0] = jnp.where(last, 0.0, out).astype(BF16)

        compress(uk_ref, pe_ref[0], kwa_ref, kwb_ref, kw2_ref, kc_ref)
        compress(uv_ref, pe_ref[1], vwa_ref, vwb_ref, vw2_ref, vc_ref)


def _compress(pages_arr, pt, page, pe, kwa, kwb, kw2, vwa, vwb, vw2, paged):
    b, n_pg = pt.shape
    pps = min(NSA_PAGES_PER_STEP, n_pg)
    nc = n_pg * page // CMP_STRIDE
    if paged:
        page_spec = lambda u: pl.BlockSpec((1, 2 * LANE, page), lambda bb, j, p: (p[bb, j * pps + u], 0, 0))
    else:
        page_spec = lambda u: pl.BlockSpec((1, 2 * LANE, page), lambda bb, j, p: (bb, 0, j * pps + u))
    full = lambda a: pl.BlockSpec(a.shape, lambda bb, j, p: (0,) * a.ndim)
    consts = (pe, kwa, kwb, kw2, vwa, vwb, vw2)
    grid_spec = pltpu.PrefetchScalarGridSpec(
        num_scalar_prefetch=1,
        grid=(b, n_pg // pps),
        in_specs=[page_spec(u) for u in range(pps)] + [full(a) for a in consts],
        out_specs=[pl.BlockSpec((1, nc, LANE), lambda bb, j, p: (bb, 0, 0))] * 2,
        scratch_shapes=[pltpu.VMEM((nc, CMP_STRIDE * LANE), F32)] * 2 + [pltpu.VMEM((pps * page, LANE), F32)] * 2)
    return pl.pallas_call(
        functools.partial(_cmp_kernel, pps=pps),
        grid_spec=grid_spec,
        out_shape=[jax.ShapeDtypeStruct((b, nc, LANE), BF16)] * 2,
        compiler_params=_cparams(("arbitrary", "arbitrary")),
        name="nsa_compress",
    )(pt, *([pages_arr] * pps), *consts)


def _nsa_kernel(qt_ref, kc_ref, vct_ref, ovt_ref, ks_ref, vst_ref, kw_ref, vwt_ref, kp_ref, ts_ref, tw_ref, gt_ref,
                o_ref, qaug_ref, m_ref, acc_ref, ocmp_ref, oslc_ref, s_ref):
    k = pl.program_id(1)
    i = pl.program_id(2)
    tq = qt_ref.shape[2]
    q4 = jnp.concatenate([qt_ref[0, g * LANE:(g + 1) * LANE, :] for g in range(NSA_GROUP)], axis=1)
    qpos1 = i * tq + lax.broadcasted_iota(jnp.int32, (1, tq), 1)
    qpos = jnp.concatenate([qpos1] * NSA_GROUP, axis=1)

    nc = kc_ref.shape[1]
    lc = _dot(kc_ref[0], q4)
    tok = lax.broadcasted_iota(jnp.int32, lc.shape, 0)
    lm = jnp.where(tok * CMP_STRIDE + (CMP_LEN - 1) <= qpos, lc, NEG)
    mx = jnp.max(lm, axis=0, keepdims=True)
    s_ref[0:nc, :] = lm
    s_ref[0:nc, :] = jnp.exp2(s_ref[0:nc, :] - mx)
    norm = jnp.where(qpos >= CMP_LEN - 1,
                     1.0 / jnp.maximum(jnp.sum(s_ref[0:nc, :], axis=0, keepdims=True), TINY), 0.0)
    ocmp_ref[...] = _dot(vct_ref[0], s_ref[0:nc, :].astype(BF16)) * norm

    pcs = sum(s_ref[0:nc, g * tq:(g + 1) * tq] * norm[:, g * tq:(g + 1) * tq] for g in range(NSA_GROUP))
    ph, plo = _split_bf16(pcs)
    imp = _dot(ovt_ref[...], ph) + _dot(ovt_ref[...], plo)
    jb = lax.broadcasted_iota(jnp.int32, imp.shape, 0)
    cur = qpos1 >> int(math.log2(SLC_BLOCK))
    avail = jb <= cur
    forced = (jb == 0) | (jb == cur) | (jb == cur - 1)
    imp = jnp.where(avail, jnp.where(forced, FORCE, imp), NEG)
    sel = (_topk_mask(imp, SLC_TOPN, axis=0) > 0.0) & avail
    selb = jnp.where(sel, 0.0, NEG).astype(BF16)
    qaug_ref[...] = jnp.concatenate([q4, jnp.concatenate([selb] * NSA_GROUP, axis=1)], axis=0)

    cd = (i * tq) // KEY_TILE
    a0 = pl.multiple_of((i * tq) % KEY_TILE, tq)

    def table(t_ref, r0, n):
        return jnp.concatenate([t_ref[0, pl.ds(r0, n), pl.ds(g * KEY_TILE + a0, tq)] for g in range(NSA_GROUP)],
                               axis=1)

    _flash_init_t(m_ref, acc_ref)

    def slc_logits_at(off, n):
        kaug = jnp.concatenate([ks_ref[0, pl.ds(off, n), :], kp_ref[pl.ds(off, n), :]], axis=1)
        return _dot(kaug, qaug_ref[...])

    _sweep_tiles(slc_logits_at, lambda off, n: vst_ref[0, :, pl.ds(off, n)], functools.partial(table, ts_ref),
                 cd, m_ref, acc_ref, s_ref)
    oslc_ref[...] = _flash_result_t(acc_ref)

    _flash_init_t(m_ref, acc_ref)
    n_win = WINDOW // KEY_TILE
    span = WINDOW + KEY_TILE
    first = jnp.maximum(cd - n_win, 0)
    off = pl.multiple_of(first * KEY_TILE, KEY_TILE)
    toff = pl.multiple_of((first - (cd - n_win)) * KEY_TILE, KEY_TILE)
    s_ref[0:span, :] = _dot(kw_ref[0, pl.ds(off, span), :], q4) + table(tw_ref, toff, span)
    _flash_step_t(s_ref, span, vwt_ref[0, :, pl.ds(off, span)], m_ref, acc_ref)
    owin = _flash_result_t(acc_ref)

    ocmp = ocmp_ref[...]
    oslc = oslc_ref[...]
    gts = gt_ref[0]
    for g in range(NSA_GROUP):
        c = slice(g * tq, (g + 1) * tq)
        og = (gts[g:g + 1, :] * ocmp[:, c] + gts[NSA_GROUP + g:NSA_GROUP + g + 1, :] * oslc[:, c]
              + gts[2 * NSA_GROUP + g:2 * NSA_GROUP + g + 1, :] * owin[:, c])
        o_ref[0, g * HEAD_DIM:(g + 1) * HEAD_DIM, :] = jnp.where(k == 0, og[:HEAD_DIM], og[HEAD_DIM:]).astype(BF16)


def _nsa_prompt(qt, kc, vct, ovt, ksw, vst, vwt, kps, ts, tw, gt):
    b, _, t = qt.shape
    tq = min(NSA_TQ, t)
    nq = t // tq
    cols = NSA_GROUP * tq
    nc = kc.shape[1]
    keys = lambda col: pl.BlockSpec((1, t, LANE), lambda bb, k, i: (bb, 0, col))
    vals = pl.BlockSpec((1, LANE, t), lambda bb, k, i: (bb, 0, 0))
    return pl.pallas_call(
        _nsa_kernel,
        grid=(b, H_NSA_KV, nq),
        in_specs=[pl.BlockSpec((1, NSA_GROUP * LANE, tq), lambda bb, k, i: (bb, k, i)),
                  pl.BlockSpec((1, nc, LANE), lambda bb, k, i: (bb, 0, 0)),
                  pl.BlockSpec((1, LANE, nc), lambda bb, k, i: (bb, 0, 0)),
                  pl.BlockSpec(ovt.shape, lambda bb, k, i: (0, 0)),
                  keys(0), vals, keys(1), vals,
                  pl.BlockSpec((t, LANE), lambda bb, k, i: (0, 0), pipeline_mode=pl.Buffered(1)),
                  pl.BlockSpec((1,) + ts.shape[1:], lambda bb, k, i: (k, 0, 0), pipeline_mode=pl.Buffered(1)),
                  pl.BlockSpec((1,) + tw.shape[1:], lambda bb, k, i: (k, 0, 0), pipeline_mode=pl.Buffered(1)),
                  pl.BlockSpec((1, LANE, tq), lambda bb, k, i: (bb, k, i))],
        out_specs=pl.BlockSpec((1, NSA_GROUP * HEAD_DIM, tq), lambda bb, k, i: (bb, k, i)),
        out_shape=jax.ShapeDtypeStruct((b, W_NSA, t), BF16),
        scratch_shapes=[pltpu.VMEM((2 * LANE, cols), BF16),
                        pltpu.VMEM((1, cols), F32), pltpu.VMEM((LANE + ONES_ROWS, cols), F32),
                        pltpu.VMEM((LANE, cols), F32), pltpu.VMEM((LANE, cols), F32),
                        pltpu.VMEM((max(SWEEP_TILES * KEY_TILE, WINDOW + KEY_TILE, nc), cols), F32)],
        compiler_params=_cparams(("arbitrary", "arbitrary", "arbitrary")),
        name="nsa_prompt",
    )(qt, kc, vct, ovt, ksw, vst, ksw, vwt, kps, ts, tw, gt)


def _moba_dec_kernel(pt_ref, qbd_ref, new_ref, td_ref, *refs, pps, nblk):
    del pt_ref
    pages = refs[:pps]
    o_ref, sc_ref, mall_ref, lall_ref, oall_ref = refs[pps:]
    j = pl.program_id(1)
    qbd = qbd_ref[0]
    rows = qbd.shape[0]
    lane = lax.broadcasted_iota(jnp.int32, (rows, LANE), 1)
    ppb = MOBA_BLOCK // pages[0].shape[2]

    @pl.when(j == 0)
    def _():
        sc_ref[...] = jnp.zeros(sc_ref.shape, F32)
        mall_ref[...] = jnp.full(mall_ref.shape, NEG, F32)
        lall_ref[...] = jnp.zeros(lall_ref.shape, F32)

    for u in range(pps // ppb):
        blk = j * (pps // ppb) + u
        kt = jnp.concatenate([pages[u * ppb + w][0, :W_MOBA, :] for w in range(ppb)], axis=1)
        vt = jnp.concatenate([pages[u * ppb + w][0, W_MOBA:, :] for w in range(ppb)], axis=1)
        s = _dot(qbd, kt.astype(BF16))
        sc_ref[...] = jnp.where(lane == blk, jnp.sum(s, axis=1, keepdims=True), sc_ref[...])
        near = (blk == nblk - 1).astype(F32)
        s = s + near * td_ref[:, 0:MOBA_BLOCK]
        mb = jnp.max(s, axis=1, keepdims=True)
        p = jnp.exp2(s - mb)
        mall_ref[...] = jnp.where(lane == blk, mb, mall_ref[...])
        lall_ref[...] = jnp.where(lane == blk, jnp.sum(p, axis=1, keepdims=True), lall_ref[...])
        oall_ref[pl.ds(blk, 1)] = _dot_nt(p.astype(BF16), vt.astype(BF16))[None]

    @pl.when(j == pl.num_programs(1) - 1)
    def _():
        kn = new_ref[0, :, :W_MOBA].astype(BF16)
        vn = new_ref[0, :, W_MOBA:].astype(BF16)
        s_own = _dot_nt(qbd, kn) + td_ref[:, MOBA_BLOCK:MOBA_BLOCK + NEW_PAD]
        m_own = jnp.max(s_own, axis=1, keepdims=True)
        p_own = jnp.exp2(s_own - m_own)
        l_own = jnp.sum(p_own, axis=1, keepdims=True)
        o_own = _dot(p_own.astype(BF16), vn)
        sc = sc_ref[...]
        cand = lane < nblk
        sel = (_topk_mask(jnp.where(cand, sc, NEG), MOBA_TOPK) > 0.0) & cand
        mall = mall_ref[...]
        m_fin = jnp.maximum(jnp.max(jnp.where(sel, mall, NEG), axis=1, keepdims=True), m_own)
        w = jnp.where(sel, jnp.exp2(mall - m_fin), 0.0)
        w_own = jnp.exp2(m_own - m_fin)
        l = jnp.sum(w * lall_ref[...], axis=1, keepdims=True) + w_own * l_own
        acc = w_own * o_own
        for n in range(nblk):
            acc = acc + w[:, n:n + 1] * oall_ref[n]
        o = acc / jnp.maximum(l, TINY)
        rowh = lax.broadcasted_iota(jnp.int32, o.shape, 0) & (H_MOBA - 1)
        laneh = lax.broadcasted_iota(jnp.int32, o.shape, 1) >> int(math.log2(HEAD_DIM))
        o = jnp.where(rowh == laneh, o, 0.0)
        o_ref[0] = jnp.sum(o.reshape(rows // H_MOBA, H_MOBA, W_MOBA), axis=1)


def _moba_decode(cache_m, pt, qbd, new_pad, td):
    b, n_pg = pt.shape
    page = cache_m.shape[2]
    pps = min(MOBA_PAGES_PER_STEP, n_pg)
    nblk =n_pg * page // MOBA_BLOCK
    rows = qbd.shape[1]
    s_new = rows // H_MOBA
    page_spec = lambda u: pl.BlockSpec((1, 2 * W_MOBA, page), lambda bb, j, p: (p[bb, j * pps + u], 0, 0))
    grid_spec = pltpu.PrefetchScalarGridSpec(
        num_scalar_prefetch=1,
        grid=(b, n_pg // pps),
        in_specs=[pl.BlockSpec((1, rows, W_MOBA), lambda bb, j, p: (bb, 0, 0)),
                  pl.BlockSpec((1, NEW_PAD, 2 * W_MOBA), lambda bb, j, p: (bb, 0, 0)),
                  pl.BlockSpec(td.shape, lambda bb, j, p: (0, 0))] + [page_spec(u) for u in range(pps)],
        out_specs=pl.BlockSpec((1, s_new, W_MOBA), lambda bb, j, p: (bb, 0, 0)),
        scratch_shapes=[pltpu.VMEM((rows, LANE), F32), pltpu.VMEM((rows, LANE), F32), pltpu.VMEM((rows, LANE), F32),
                        pltpu.VMEM((nblk, rows, W_MOBA), F32)])
    return pl.pallas_call(
        functools.partial(_moba_dec_kernel, pps=pps, nblk=nblk),
        grid_spec=grid_spec,
        out_shape=jax.ShapeDtypeStruct((b, s_new, W_MOBA), F32),
        compiler_params=_cparams(("arbitrary", "arbitrary")),
        name="moba_decode",
    )(pt, qbd, new_pad, td, *([cache_m] * pps))


def _nsa_dec_kernel(pt_ref, qd_ref, kc_ref, vc_ref, ov_ref, kp_ref, new_ref, win_ref, neww_ref, g_ref,
                    ts_ref, tso_ref, tw_ref, two_ref, *refs, pps, past):
    del pt_ref
    pages = refs[:pps]
    o_ref, qaug_ref, m_ref, l_ref, acc_ref, ocmp_ref, selown_ref = refs[pps:]
    j = pl.program_id(1)
    nj = pl.num_programs(1)
    qd = qd_ref[0]
    rows = qd.shape[0]
    page = pages[0].shape[2]
    grp = rows // NSA_GROUP
    s_new = grp // H_NSA_KV

    @pl.when(j == 0)
    def _():
        lc = _dot_nt(qd, kc_ref[0])
        tok = lax.broadcasted_iota(jnp.int32, lc.shape, 1)
        qpos = past + (lax.broadcasted_iota(jnp.int32, (rows, 1), 0) & (s_new - 1))
        pc = _masked_softmax(lc, tok * CMP_STRIDE + (CMP_LEN - 1) <= qpos)
        ocmp_ref[...] = _dot(pc.astype(BF16), vc_ref[0])
        pcs = pc[0:grp] + pc[grp:2 * grp] + pc[2 * grp:3 * grp] + pc[3 * grp:4 * grp]
        ph, plo = _split_bf16(pcs)
        imp = _dot(ph, ov_ref[...]) + _dot(plo, ov_ref[...])
        jb = lax.broadcasted_iota(jnp.int32, imp.shape, 1)
        cur = qpos[0:grp] >> int(math.log2(SLC_BLOCK))
        avail = jb <= cur
        forced = (jb == 0) | (jb == cur) | (jb == cur - 1)
        imp = jnp.where(avail, jnp.where(forced, FORCE, imp), NEG)
        sel = (_topk_mask(imp, SLC_TOPN) > 0.0) & avail
        selb = jnp.concatenate([jnp.where(sel, 0.0, NEG)] * NSA_GROUP, axis=0)
        qaug_ref[...] = jnp.concatenate([qd, selb[:, :LANE].astype(BF16)], axis=1)
        n_own = past // SLC_BLOCK
        selown_ref[...] = jnp.broadcast_to(selb[:, n_own:n_own + 1], selown_ref.shape)
        _flash_init(m_ref, l_ref, acc_ref)

    span = pps * page
    off = pl.multiple_of(j * span, span)
    kst = jnp.concatenate([pages[u][0, :LANE, :] for u in range(pps)], axis=1).astype(BF16)
    vst = jnp.concatenate([pages[u][0, LANE:, :] for u in range(pps)], axis=1).astype(BF16)
    kaug = jnp.concatenate([kst, kp_ref[:, pl.ds(off, span)]], axis=0)
    near = (j == nj - 1).astype(F32)
    s = _dot(qaug_ref[...], kaug) + near * ts_ref[...]
    _flash_step(s, vst, m_ref, l_ref, acc_ref, feature_major=True)

    @pl.when(j == nj - 1)
    def _():
        new = new_ref[0]
        s_own = _dot_nt(qd, new[:, 2 * LANE:3 * LANE].astype(BF16)) + tso_ref[...] + selown_ref[:, 0:1]
        _flash_step(s_own, new[:, 3 * LANE:].astype(BF16), m_ref, l_ref, acc_ref)
        oslc = acc_ref[...] / jnp.maximum(l_ref[...], TINY)
        _flash_init(m_ref, l_ref, acc_ref)
        win = win_ref[0]
        _flash_step(_dot(qd, win[:LANE, :].astype(BF16)) + tw_ref[...], win[LANE:, :].astype(BF16),
                    m_ref, l_ref, acc_ref, feature_major=True)
        neww = neww_ref[0]
        _flash_step(_dot_nt(qd, neww[:, :LANE].astype(BF16)) + two_ref[...], neww[:, LANE:].astype(BF16),
                    m_ref, l_ref, acc_ref)
        owin = acc_ref[...] / jnp.maximum(l_ref[...], TINY)
        gts = g_ref[0]
        o_ref[0] = gts[:, 0:1] * ocmp_ref[...] + gts[:, 1:2] * oslc + gts[:, 2:3] * owin


def _nsa_decode(cache_n, pt, qd, kc, vc, ov, kps, new_pad, win, neww, gts, ts, tso, tw, two):
    b, n_pg = pt.shape
    page = cache_n.shape[2]
    pps = min(NSA_PAGES_PER_STEP, n_pg)
    rows = qd.shape[1]
    nc = kc.shape[1]
    full = lambda a: pl.BlockSpec(a.shape, lambda bb, j, p: (0,) * a.ndim)
    per_b = lambda a: pl.BlockSpec((1,) + a.shape[1:], lambda bb, j, p: (bb,) + (0,) * (a.ndim - 1))
    page_spec = lambda u: pl.BlockSpec((1, 2 * LANE, page), lambda bb, j, p: (p[bb, j * pps + u], 1, 0))
    grid_spec = pltpu.PrefetchScalarGridSpec(
        num_scalar_prefetch=1,
        grid=(b, n_pg // pps),
        in_specs=[per_b(qd), per_b(kc), per_b(vc), full(ov), full(kps), per_b(new_pad), per_b(win), per_b(neww),
                  per_b(gts), full(ts), full(tso), full(tw), full(two)] + [page_spec(u) for u in range(pps)],
        out_specs=pl.BlockSpec((1, rows, LANE), lambda bb, j, p: (bb, 0, 0)),
        scratch_shapes=[pltpu.VMEM((rows, 2 * LANE), BF16),
                        pltpu.VMEM((rows, 1), F32), pltpu.VMEM((rows, 1), F32), pltpu.VMEM((rows, LANE), F32),
                        pltpu.VMEM((rows, LANE), F32), pltpu.VMEM((rows, LANE), F32)])
    return pl.pallas_call(
        functools.partial(_nsa_dec_kernel, pps=pps, past=n_pg * page),
        grid_spec=grid_spec,
        out_shape=jax.ShapeDtypeStruct((b, rows, LANE), F32),
        compiler_params=_cparams(("arbitrary", "arbitrary")),
        name="nsa_decode",
    )(pt, qd, kc, vc, ov, kps, new_pad, win, neww, gts, ts, tso, tw, two, *([cache_n] * pps))


def _out_kernel(x_ref, om_ref, zm_ref, on_ref, zn_ref, gate_ref, w_ref, fg_ref, y_ref, *, feature_major):
    zm = zm_ref[...].astype(F32)
    zn = zn_ref[...].astype(F32)
    if feature_major:
        om = om_ref[0].astype(F32).T
        on = on_ref[0].astype(F32).T
    else:
        om = om_ref[...].astype(F32)
        on = on_ref[...].astype(F32)
    mm = (om * (zm * _sigmoid(zm))).astype(BF16)
    mn = (on * (zn * _sigmoid(zn))).astype(BF16)
    mixed = _dot(mm, w_ref[:W_MOBA, :]) + _dot(mn, w_ref[W_MOBA:, :])
    xn = x_ref[...] + gate_ref[0] * mixed
    inv = lax.rsqrt(jnp.mean(xn * xn, axis=-1, keepdims=True) + RMS_EPS)
    y_ref[...] = (xn * inv) * fg_ref[...]


def _out_proj(x2d, om, zm, on, zn, gate, w_out, fgain, tm, tiles_per_mod):
    r, d = x2d.shape
    mrows = gate.shape[1]
    feature_major = om.ndim == 3
    row = lambda width: pl.BlockSpec((tm, width), lambda i: (i, 0))
    if feature_major:
        mixer = pl.BlockSpec((1, om.shape[1], tm), lambda i: (i // tiles_per_mod, 0, i % tiles_per_mod))
    else:
        mixer = row(W_MOBA)
    return pl.pallas_call(
        functools.partial(_out_kernel, feature_major=feature_major),
        grid=(r // tm,),
        in_specs=[row(d), mixer, row(W_MOBA), mixer, row(W_NSA),
                  pl.BlockSpec((1, mrows, d), lambda i: (i // tiles_per_mod, 0, 0)),
                  pl.BlockSpec((d, d), lambda i: (0, 0)),
                  pl.BlockSpec((1, d), lambda i: (0, 0))],
        out_specs=row(d),
        out_shape=jax.ShapeDtypeStruct((r, d), F32),
        compiler_params=_cparams(("arbitrary",)),
        name="out_proj",
    )(x2d, om, zm, on, zn, gate, w_out, fgain)


def _t5_bucket(rel):
    n = np.maximum(rel, 0)
    exact = N_BUCKETS // 2
    x = np.log(np.maximum(n, 1) / exact) / math.log(MAX_DISTANCE / exact) * (N_BUCKETS - exact)
    near_boundary = (np.abs(x - np.round(x)) < 1e-3) & (n > exact) & (n < MAX_DISTANCE)
    assert not near_boundary.any()
    large = exact + np.floor(x + 1e-6).astype(np.int64)
    return np.where(n < exact, n, np.minimum(large, N_BUCKETS - 1)).astype(np.int32)


def _bias_of_rel(tab, rel, shift_far):
    onehot = (jnp.asarray(_t5_bucket(rel))[..., None] == jnp.arange(N_BUCKETS, dtype=jnp.int32)).astype(F32)
    val = jnp.moveaxis(jnp.dot(onehot, tab, precision=lax.Precision.HIGHEST), -1, 0)
    if shift_far:
        val = val - tab[N_BUCKETS - 1].reshape((-1,) + (1,) * rel.ndim)
    return jnp.where(jnp.asarray(rel >= 0)[None], val * LOG2E, NEG)


def _layout_w_in(w_in):
    d = w_in.shape[0]
    sc = HEAD_DIM ** -0.5 * LOG2E
    o = 4 * W_MOBA
    q_m, rest_m = w_in[:, :W_MOBA] * sc, w_in[:, W_MOBA:o]
    q_n = (w_in[:, o:o + W_NSA] * sc).reshape(d, H_NSA_KV, NSA_GROUP, HEAD_DIM)
    qn_exp = jnp.concatenate([jnp.pad(q_n[:, 0], ((0, 0), (0, 0), (0, HEAD_DIM))),
                              jnp.pad(q_n[:, 1], ((0, 0), (0, 0), (HEAD_DIM, 0)))], axis=1).reshape(d, H_NSA * LANE)
    o += W_NSA
    kv_n = w_in[:, o:o + 6 * W_NSA_KV]
    o += 6 * W_NSA_KV
    g_n = jnp.swapaxes(w_in[:, o:o + 3 * H_NSA].reshape(d, H_NSA_KV, NSA_GROUP, 3), 2, 3)
    g_n = jnp.pad(g_n.reshape(d, H_NSA_KV, 3 * NSA_GROUP), ((0, 0), (0, 0), (0, LANE - 3 * NSA_GROUP)))
    z_n = w_in[:, o + 3 * H_NSA:]
    w = jnp.concatenate([q_m, rest_m, qn_exp, kv_n, z_n, g_n.reshape(d, H_NSA_KV * LANE)], axis=1).astype(BF16)
    assert w.shape[1] == C_END
    wt = jnp.concatenate([w[:, C_QM:C_ZM], w[:, C_QN:C_ZN], w[:, C_G:C_END]], axis=1).T
    assert wt.shape[0] == R_END
    return w, wt


def _layout_cmp(pe, w1, w2):
    w = w1.reshape(2, CMP_STRIDE, HEAD_DIM, CMP_HIDDEN)
    z = jnp.zeros_like(w)
    full = jnp.stack([jnp.concatenate([w, z], axis=-1), jnp.concatenate([z, w], axis=-1)], axis=2)
    full = full.reshape(2, CMP_STRIDE * H_NSA_KV * HEAD_DIM, H_NSA_KV * CMP_HIDDEN).astype(BF16)
    zz = jnp.zeros_like(w2)
    w2bd = jnp.concatenate([jnp.concatenate([w2, zz], axis=1), jnp.concatenate([zz, w2], axis=1)], axis=0).astype(BF16)
    pe2 = jnp.broadcast_to(pe.reshape(2, CMP_STRIDE, 1, HEAD_DIM), (2, CMP_STRIDE, H_NSA_KV, HEAD_DIM))
    return pe2.reshape(2, 1, CMP_STRIDE * LANE), full[0], full[1], w2bd


def _block_onehot(t, block):
    return (jnp.arange(t)[:, None] // block == jnp.arange(LANE)[None, :]).astype(BF16)


def _overlap(nc, n_cmp, lanes):
    i = jnp.arange(nc)[:, None]
    s = jnp.arange(lanes)[None, :]
    start = i * CMP_STRIDE
    hit = (start < s * SLC_BLOCK + SLC_BLOCK) & (start + CMP_LEN - 1 >= s * SLC_BLOCK) & (i < n_cmp)
    return hit.astype(BF16)


def kernel(x_prompt, x_sample, c_prompt, c_sample, cache_moba_kv, cache_nsa_kv, state_nsa_win, page_table, w_ada, b_ada, norm_gain, w_in, cmp_pe, cmp_k_w1, cmp_k_w2, cmp_v_w1, cmp_v_w2, w_out, rel_bias, final_gain):
    depth = w_in.shape[0]
    assert depth == 1
    B, T, D = x_prompt.shape
    BS, S, _ = x_sample.shape
    n_pg = page_table.shape[1]
    page = cache_moba_kv.shape[2]
    P = n_pg * page
    WB = state_nsa_win.shape[2]
    assert D == (H_MOBA + H_NSA) * HEAD_DIM and T % (SWEEP_TILES * KEY_TILE) == 0 and P % MOBA_BLOCK == 0 and S <= NEW_PAD
    assert T // SLC_BLOCK <= LANE and P // SLC_BLOCK <= LANE and T >= WINDOW and WB == WINDOW
    assert (BS * S) % SUBLANE == 0 and S & (S - 1) == 0

    w, wt = _layout_w_in(w_in[0])
    w_out_b = w_out[0].astype(BF16)
    gain = norm_gain[0].reshape(1, D)
    fgain = final_gain.reshape(1, D)
    kcmp_w = _layout_cmp(cmp_pe[0, 0], cmp_k_w1[0], cmp_k_w2[0])
    vcmp_w = _layout_cmp(cmp_pe[0, 1], cmp_v_w1[0], cmp_v_w2[0])
    pe2 = jnp.stack([kcmp_w[0], vcmp_w[0]], axis=0)
    cmp_consts = (pe2,) + kcmp_w[1:] + vcmp_w[1:]
    bias_m = rel_bias[:, :H_MOBA]
    bias_n = rel_bias[:, H_MOBA:]

    m_all = B + BS
    m_pad = -(-m_all // SUBLANE) * SUBLANE
    c_all = jnp.pad(jnp.concatenate([c_prompt, c_sample], axis=0), ((0, m_pad - m_all), (0, 0)))
    mod = _ada(c_all, w_ada[0], b_ada[0])
    shift, scale, gate = mod[:, :D], mod[:, D:2 * D], mod[:, 2 * D:]

    a = np.arange(KEY_TILE)[None, :]
    jk = np.arange(2 * KEY_TILE)[:, None]
    rel_s = a + KEY_TILE - jk
    heads_major = lambda tb, n_grp: jnp.transpose(
        tb.reshape(n_grp, -1, tb.shape[1], tb.shape[2]), (0, 2, 1, 3)).reshape(n_grp, tb.shape[1], -1)
    rows_of = lambda tb, n, val: jnp.full((tb.shape[0], n, tb.shape[2]), val, F32)
    pad_rows = (SWEEP_TILES - 1) * KEY_TILE
    sweep_rows = lambda tb: jnp.concatenate([rows_of(tb, pad_rows, 0.0), tb, rows_of(tb, pad_rows, NEG)], axis=1)
    rel_m = MOBA_BLOCK + np.arange(MOBA_TQ)[None, :] - np.arange(MOBA_BLOCK + MOBA_TQ)[:, None]
    tab_m = heads_major(_bias_of_rel(bias_m, rel_m, True), H_MOBA // 2)
    tab_m = jnp.concatenate([rows_of(tab_m, MOBA_BLOCK, 0.0), tab_m, rows_of(tab_m, MOBA_TQ, NEG)], axis=1)
    tab_s = sweep_rows(heads_major(_bias_of_rel(bias_n, rel_s, True), H_NSA_KV))
    jw = np.arange(WINDOW + KEY_TILE)[:, None]
    rel_w = a + WINDOW - jw
    tab_w = heads_major(_bias_of_rel(bias_n, np.where(rel_w < WINDOW, rel_w, -1), False), H_NSA_KV)
    tab_w = jnp.concatenate([tab_w, rows_of(tab_w, WINDOW, NEG)], axis=1)

    tpm = T // PROJ_TM
    (qmt, mkvt, vmt, qnt, nkvt, vst, wkvt, vwt, gt, km, ksw, zm, zn, kmean) = _proj_fm(
        x_prompt.reshape(B * T, D), scale[:B].reshape(B, 1, D), shift[:B].reshape(B, 1, D), gain, w, wt, B, PROJ_TM)
    nb = T // MOBA_BLOCK
    kmean_pad = jnp.pad(kmean.reshape(B, nb, W_MOBA), ((0, 0), (0, LANE - nb), (0, 0)))
    o_m = _moba_prompt(qmt, km.reshape(B, T, W_MOBA), vmt, kmean_pad, _block_onehot(T, MOBA_BLOCK), tab_m)
    pt_prompt = jnp.zeros((B, T // page), jnp.int32)
    kc_p, vc_p = _compress(nkvt, pt_prompt, page, *cmp_consts, paged=False)
    nc_p = T // CMP_STRIDE
    o_n = _nsa_prompt(qnt, kc_p, jnp.swapaxes(vc_p, 1, 2), _overlap(nc_p, nc_p - 1, LANE).T,
                      ksw.reshape(B, T, 2 * LANE), vst, vwt, _block_onehot(T, SLC_BLOCK), tab_s, tab_w, gt)
    y_prompt = _out_proj(x_prompt.reshape(B * T, D), o_m, zm, o_n, zn,
                         gate[:B].reshape(B, 1, D), w_out_b, fgain, PROJ_TM, tpm).reshape(B, T, D)
    tokens_first = lambda m, c: jnp.transpose(m.reshape(1, B, c, -1, HEAD_DIM, m.shape[-1]), (0, 1, 5, 2, 3, 4))
    moba_kv_prompt = tokens_first(mkvt, 2)
    nsa_kv_prompt = tokens_first(nkvt, 4)
    win_prompt = tokens_first(wkvt[:, :, T - WINDOW:], 2)

    RS = BS * S
    rep = lambda m: jnp.repeat(m[B:B + BS], S, axis=0)
    (qm_s, mkv32_s, zm_s, qn_s, nkv32_s, wkv32_s, zn_s, gts_s) = _proj(
        x_sample.reshape(RS, D), rep(scale), rep(shift), gain, w)
    pad_new = lambda m: jnp.pad(m.reshape(BS, S, -1), ((0, 0), (0, NEW_PAD - S), (0, 0)))
    s_idx = np.arange(S)

    q_rep = jnp.repeat(qm_s.reshape(BS, S, 1, H_MOBA, HEAD_DIM), H_MOBA, axis=2)
    eye = (jnp.arange(H_MOBA)[:, None] == jnp.arange(H_MOBA)[None, :])[None, None, :, :, None]
    qbd = jnp.where(eye, q_rep, 0).reshape(BS, S * H_MOBA, W_MOBA)
    rel_d = np.concatenate([MOBA_BLOCK + s_idx[:, None] - np.arange(MOBA_BLOCK)[None, :],
                            s_idx[:, None] - np.arange(NEW_PAD)[None, :]], axis=1)
    td = jnp.moveaxis(_bias_of_rel(bias_m, rel_d, True), 0, 1).reshape(S * H_MOBA, -1)
    td = jnp.pad(td, ((0, 0), (0, 3 * LANE - td.shape[1])))
    feat_pages = lambda c: jnp.transpose(c, (0, 2, 3, 4, 1)).reshape(c.shape[0], -1, c.shape[1])
    o_m_s = _moba_decode(feat_pages(cache_moba_kv[0]), page_table, qbd, pad_new(mkv32_s), td)

    cache_n = feat_pages(cache_nsa_kv[0])
    kc_s, vc_s = _compress(cache_n, page_table, page, *cmp_consts, paged=True)
    nc_s = P // CMP_STRIDE
    n_cmp_s = (P + S - CMP_LEN) // CMP_STRIDE + 1
    order = lambda m: jnp.transpose(m, (0, 3, 2, 1) + tuple(range(4, m.ndim)))
    rows_n = NSA_GROUP * H_NSA_KV * S
    qd = order(qn_s.reshape(BS, S, H_NSA_KV, NSA_GROUP, LANE)).reshape(BS, rows_n, LANE)
    g3 = gts_s.reshape(BS, S, H_NSA_KV, LANE)[..., :3 * NSA_GROUP].reshape(BS, S, H_NSA_KV, 3, NSA_GROUP)
    gts_d = order(jnp.swapaxes(g3, 3, 4)).reshape(BS, rows_n, 3)
    bias_d = jnp.swapaxes(bias_n.reshape(N_BUCKETS, H_NSA_KV, NSA_GROUP), 1, 2).reshape(N_BUCKETS, H_NSA)

    def dec_table(rel, shift_far):
        return _bias_of_rel(bias_d, rel, shift_far).reshape(rows_n, rel.shape[1])

    ts = dec_table(page + s_idx[:, None] - np.arange(page)[None, :], True)
    ts = jnp.pad(ts, ((0, 0), (min(NSA_PAGES_PER_STEP, n_pg) * page - page, 0)))
    rel_own = s_idx[:, None] - np.arange(NEW_PAD)[None, :]
    tso = dec_table(rel_own, True)
    rel_win = WB + s_idx[:, None] - np.arange(WB)[None, :]
    tw = dec_table(np.where(rel_win < WINDOW, rel_win, -1), False)
    two = dec_table(rel_own, False)
    win_state = feat_pages(state_nsa_win[0])
    o_n_raw = _nsa_decode(cache_n, page_table, qd, kc_s, vc_s, _overlap(nc_s, n_cmp_s, 2 * LANE),
                          _block_onehot(P, SLC_BLOCK).T, pad_new(nkv32_s), win_state, pad_new(wkv32_s), gts_d,
                          ts, tso, tw, two)
    o5 = o_n_raw.reshape(BS, NSA_GROUP, H_NSA_KV, S, H_NSA_KV, HEAD_DIM)
    o_n_s = jnp.stack([o5[:, :, k, :, k] for k in range(H_NSA_KV)], axis=1)
    o_n_s = jnp.transpose(o_n_s, (0, 3, 1, 2, 4)).reshape(RS, W_NSA)
    y_sample = _out_proj(x_sample.reshape(RS, D), o_m_s.reshape(RS, W_MOBA).astype(BF16), zm_s,
                         o_n_s.astype(BF16), zn_s, rep(gate).reshape(1, RS, D), w_out_b, fgain, RS, 1).reshape(BS, S, D)
    moba_kv_sample = mkv32_s.reshape(1, BS, S, 2, H_MOBA, HEAD_DIM)
    nsa_kv_sample = nkv32_s.reshape(1, BS, S, 4, H_NSA_KV, HEAD_DIM)
    win_sample = jnp.concatenate([win_state[:, :, S:], jnp.swapaxes(wkv32_s.reshape(BS, S, 2 * LANE), 1, 2)], axis=2)
    win_sample = jnp.transpose(win_sample.reshape(1, BS, 2, H_NSA_KV, HEAD_DIM, WB), (0, 1, 5, 2, 3, 4))
    return (y_prompt, y_sample, moba_kv_prompt, moba_kv_sample, nsa_kv_prompt, nsa_kv_sample, win_prompt, win_sample)
```
